```python
import jax, jax.numpy as jnp
from jax import lax
import numpy as np

D_MODEL = 1024
BATCH = 2
SEQ = 8192
DEPTH = 1
DEC_BATCH = 16
DEC_SEQ = 32
PAST_LEN = 2048

CHUNK = 64
ATT_HEADS = 8
ATT_KV_HEADS = 2
HEAD_DIM = 64
ATT_WIDTH = ATT_HEADS * HEAD_DIM
KV_WIDTH = ATT_KV_HEADS * HEAD_DIM
IDX_HEADS = 8
IDX_DIM = 64
IDX_SCALE = (IDX_DIM ** -0.5) * (IDX_HEADS ** -0.5)
TOPK_MAX = 256
Q_BLOCK = 128
RWKV_HEADS = 8
RWKV_HEAD = 64
RWKV_WIDTH = RWKV_HEADS * RWKV_HEAD
DECAY_LORA = 64
AAA_LORA = 64
GATE_LORA = 128
RWKV_SPLIT = (RWKV_WIDTH, RWKV_WIDTH, RWKV_WIDTH, DECAY_LORA, AAA_LORA, GATE_LORA)
RWKV_COLS = 3 * RWKV_WIDTH + DECAY_LORA + AAA_LORA + GATE_LORA
GN_EPS = 64e-5
L2_EPS = 1e-24
IN_SPLIT = (ATT_WIDTH, KV_WIDTH, KV_WIDTH, IDX_HEADS * IDX_DIM, IDX_DIM, IDX_HEADS, RWKV_COLS, D_MODEL, D_MODEL)
IN_COLS = ATT_WIDTH + 2 * KV_WIDTH + IDX_HEADS * IDX_DIM + IDX_DIM + IDX_HEADS + RWKV_COLS + 2 * D_MODEL
N_EXPERTS = 32
TOP_K = 4
D_FF = D_MODEL
SWIGLU_LIMIT = 7.0
SWIGLU_ALPHA = 1.702
NORM_EPS = 1e-6
LN_EPS = 1e-6

kernel_name = "hybrid_dsa_rwkv7_moe_stream_step"


def rms_norm(x, g):
    x32 = x.astype(jnp.float32)
    y = x32 * lax.rsqrt(jnp.mean(x32 * x32, axis=-1, keepdims=True) + NORM_EPS)
    return (y * g.astype(jnp.float32)).astype(x.dtype)


def layer_norm(x, w, b):
    x32 = x.astype(jnp.float32)
    mu = jnp.mean(x32, axis=-1, keepdims=True)
    var = jnp.mean(jnp.square(x32 - mu), axis=-1, keepdims=True)
    y = (x32 - mu) * lax.rsqrt(var + LN_EPS)
    return (y * w.astype(jnp.float32) + b.astype(jnp.float32)).astype(x.dtype)


def split_cols(z, sizes):
    offsets = [int(o) for o in np.cumsum(sizes)[:-1]]
    return jnp.split(z, offsets, axis=-1)


def dsa_attention(q, iq, iw, k, v, ik, q_start):
    B, Sq = q.shape[0], q.shape[1]
    L = k.shape[1]
    topk = min(TOPK_MAX, L // 4)
    qb = Q_BLOCK if Sq % Q_BLOCK == 0 else Sq
    nb = Sq // qb
    rep = ATT_HEADS // ATT_KV_HEADS
    k_chunk = jnp.arange(L) // CHUNK

    def block(args):
        q_b, iq_b, iw_b, pos_b = args
        rel = jax.nn.relu(jnp.einsum('bqhd,bld->bqhl', iq_b, ik))
        score = jnp.einsum('bqhl,bqh->bql', rel, iw_b).astype(jnp.float32)
        q_chunk = pos_b // CHUNK
        admissible = k_chunk[None, :] <= q_chunk[:, None]
        score = jnp.where(admissible[None], score, -jnp.inf)
        _, idx = lax.top_k(score, topk)
        valid = (idx // CHUNK) <= q_chunk[None, :, None]
        k_sel = jax.vmap(lambda kb, ib: kb[ib])(k, idx)
        v_sel = jax.vmap(lambda vb, ib: vb[ib])(v, idx)
        qg = q_b.reshape(B, qb, ATT_KV_HEADS, rep, HEAD_DIM)
        logits = jnp.einsum('bqgrd,bqkgd->bqgrk', qg, k_sel).astype(jnp.float32) * (HEAD_DIM ** -0.5)
        logits = jnp.where(valid[:, :, None, None, :], logits, -jnp.inf)
        probs = jax.nn.softmax(logits, axis=-1).astype(v.dtype)
        o = jnp.einsum('bqgrk,bqkgd->bqgrd', probs, v_sel)
        return o.reshape(B, qb, ATT_WIDTH)

    def to_blocks(t):
        return jnp.moveaxis(t.reshape((B, nb, qb) + t.shape[2:]), 1, 0)

    pos = (q_start + jnp.arange(Sq)).reshape(nb, qb)
    out = lax.map(block, (to_blocks(q), to_blocks(iq), to_blocks(iw), pos))
    return jnp.moveaxis(out, 0, 1).reshape(B, Sq, ATT_WIDTH)


def rwkv7_time_mix(z, s0, shift0, p):
    B, T, _ = z.shape
    f32 = jnp.float32
    prev = jnp.concatenate([shift0.astype(z.dtype), z[:, :-1]], axis=1)
    zm = z + (prev - z) * p['mu']
    r, k, v, wd, ad, gd = split_cols(zm, RWKV_SPLIT)
    w_raw = (p['w0'] + jnp.tanh(wd) @ p['w_up']).astype(f32)
    decay = jnp.exp(-jnp.exp(-jax.nn.softplus(-w_raw) - 0.5))
    a = jax.nn.sigmoid((p['a0'] + ad @ p['a_up']).astype(f32))
    g = jax.nn.sigmoid(gd) @ p['g_up']

    def heads(t):
        return t.reshape(B, T, RWKV_HEADS, RWKV_HEAD)

    kk = heads(k.astype(f32) * p['k_k'].astype(f32))
    kk = kk * lax.rsqrt(jnp.maximum(jnp.sum(kk * kk, axis=-1, keepdims=True), L2_EPS))
    k_mod = k.astype(f32) * (1.0 + (a - 1.0) * p['k_a'].astype(f32))
    r_h, w_h, k_h, v_h, a_h = heads(r.astype(f32)), heads(decay), heads(k_mod), heads(v.astype(f32)), heads(a)

    def step(S, inp):
        r_t, w_t, k_t, v_t, kk_t, a_t = inp
        sa = jnp.einsum('bhvk,bhk->bhv', S, -kk_t)
        S = S * w_t[:, :, None, :] + sa[..., None] * (kk_t * a_t)[:, :, None, :] + v_t[..., None] * k_t[:, :, None, :]
        return S, jnp.einsum('bhvk,bhk->bhv', S, r_t)

    xs = tuple(jnp.moveaxis(t, 1, 0) for t in (r_h, w_h, k_h, v_h, kk, a_h))
    S_T, ys = lax.scan(step, s0.astype(f32), xs)
    y = jnp.moveaxis(ys, 0, 1)
    mu = jnp.mean(y, axis=-1, keepdims=True)
    var = jnp.mean(jnp.square(y - mu), axis=-1, keepdims=True)
    y = ((y - mu) * lax.rsqrt(var + GN_EPS)).reshape(B, T, RWKV_WIDTH)
    y = y * p['ln_w'].astype(f32) + p['ln_b'].astype(f32)
    bonus = jnp.sum(r_h * k_h * p['r_k'].astype(f32), axis=-1, keepdims=True) * v_h
    y = (y + bonus.reshape(B, T, RWKV_WIDTH)) * g.astype(f32)
    return y.astype(z.dtype), S_T.astype(s0.dtype), z[:, -1:]


def moe_ffn(h, p):
    B, T, D = h.shape
    t = h.reshape(B * T, D)
    logits = (t @ p['w_router'] + p['b_router']).astype(jnp.float32)
    top_val, top_idx = lax.top_k(logits, TOP_K)
    top_w = jax.nn.softmax(top_val, axis=-1)
    gate = jnp.einsum('nk,nke->ne', top_w, jax.nn.one_hot(top_idx, N_EXPERTS, dtype=jnp.float32)).astype(h.dtype)
    y = jnp.zeros_like(t)
    for e in range(N_EXPERTS):
        hgu = t @ p['w_gu'][e] + p['b_gu'][e]
        hg = jnp.minimum(hgu[:, :D_FF], SWIGLU_LIMIT)
        hl = jnp.clip(hgu[:, D_FF:], -SWIGLU_LIMIT, SWIGLU_LIMIT)
        act = hg * jax.nn.sigmoid(SWIGLU_ALPHA * hg) * (hl + 1.0)
        y = y + gate[:, e:e + 1] * (act @ p['w_down'][e] + p['b_down'][e])
    return y.reshape(B, T, D)


def trunk_layer(x, c, past_k, past_v, past_ik, s0, shift0, p):
    B, T, _ = x.shape
    ada = jax.nn.silu(c) @ p['w_ada'] + p['b_ada']
    sh1, sc1, gt1, sh2, sc2, gt2 = jnp.split(ada[:, None, :], 6, axis=-1)
    h = rms_norm(x, p['g_pre_mix']) * (1.0 + sc1) + sh1
    q, k, v, iq, ik, iw, zr, ga, gb = split_cols(h @ p['w_in'], IN_SPLIT)
    k = k.reshape(B, T, ATT_KV_HEADS, HEAD_DIM)
    v = v.reshape(B, T, ATT_KV_HEADS, HEAD_DIM)
    ik = layer_norm(ik, p['ik_ln_w'], p['ik_ln_b'])
    att = dsa_attention(q.reshape(B, T, ATT_HEADS, HEAD_DIM), iq.reshape(B, T, IDX_HEADS, IDX_DIM), iw * IDX_SCALE,
                        jnp.concatenate([past_k.astype(k.dtype), k], axis=1),
                        jnp.concatenate([past_v.astype(v.dtype), v], axis=1),
                        jnp.concatenate([past_ik.astype(ik.dtype), ik], axis=1), past_k.shape[1])
    rw, s_new, shift_new = rwkv7_time_mix(zr, s0, shift0, p)
    mix = jax.nn.sigmoid(ga) * (att @ p['w_o_att']) + jax.nn.sigmoid(gb) * (rw @ p['w_o_rwkv'])
    x = x + gt1 * rms_norm(mix @ p['w_out'], p['g_post_mix'])
    h = rms_norm(x, p['g_pre_ffn']) * (1.0 + sc2) + sh2
    x = x + gt2 * rms_norm(moe_ffn(h, p), p['g_post_ffn'])
    return x, k, v, ik, s_new, shift_new


def setup_inputs(seed: int = 0) -> dict:
    key = jax.random.key(seed)
    ks = iter(jax.random.split(key, 48))
    D = D_MODEL
    L = DEPTH

    def nrm(shape, s):
        return jax.random.normal(next(ks), shape, jnp.float32) * s

    def gain(shape):
        return 1.0 + nrm(shape, 0.05)

    def unif(shape, lo, hi):
        return jax.random.uniform(next(ks), shape, jnp.float32, lo, hi)

    return {
        'x_prompt': nrm((BATCH, SEQ, D), 1.0),
        'x_sample': nrm((DEC_BATCH, DEC_SEQ, D), 1.0),
        'c_prompt': nrm((BATCH, D), 1.0),
        'c_sample': nrm((DEC_BATCH, D), 1.0),
        'cache_k': nrm((L, DEC_BATCH, PAST_LEN, ATT_KV_HEADS, HEAD_DIM), 1.0),
        'cache_v': nrm((L, DEC_BATCH, PAST_LEN, ATT_KV_HEADS, HEAD_DIM), 1.0),
        'cache_idx_k': nrm((L, DEC_BATCH, PAST_LEN, IDX_DIM), 1.0),
        'state_rwkv': nrm((L, DEC_BATCH, RWKV_HEADS, RWKV_HEAD, RWKV_HEAD), 0.5),
        'state_shift': nrm((L, DEC_BATCH, 1, RWKV_COLS), 1.0),
        'w_ada': nrm((L, D, 6 * D), 0.5 * D ** -0.5),
        'b_ada': nrm((L, 6 * D), 0.02),
        'g_pre_mix': gain((L, D)),
        'g_post_mix': gain((L, D)),
        'g_pre_ffn': gain((L, D)),
        'g_post_ffn': gain((L, D)),
        'w_in': nrm((L, D, IN_COLS), D ** -0.5),
        'ik_ln_w': gain((L, IDX_DIM)),
        'ik_ln_b': nrm((L, IDX_DIM), 0.02),
        'mu_rwkv': unif((L, RWKV_COLS), 0.0, 1.0),
        'w0': unif((L, RWKV_WIDTH), -6.0, -1.0),
        'w_up': nrm((L, DECAY_LORA, RWKV_WIDTH), 0.1),
        'a0': nrm((L, RWKV_WIDTH), 0.1),
        'a_up': nrm((L, AAA_LORA, RWKV_WIDTH), 0.1),
        'g_up': nrm((L, GATE_LORA, RWKV_WIDTH), GATE_LORA ** -0.5),
        'k_k': 0.85 + nrm((L, RWKV_WIDTH), 0.05),
        'k_a': 1.0 + nrm((L, RWKV_WIDTH), 0.05),
        'r_k': nrm((L, RWKV_HEADS, RWKV_HEAD), 0.1),
        'ln_x_w': gain((L, RWKV_WIDTH)),
        'ln_x_b': nrm((L, RWKV_WIDTH), 0.02),
        'w_o_att': nrm((L, ATT_WIDTH, D), ATT_WIDTH ** -0.5),
        'w_o_rwkv': nrm((L, RWKV_WIDTH, D), RWKV_WIDTH ** -0.5),
        'w_out': nrm((L, D, D), D ** -0.5),
        'w_router': nrm((L, D, N_EXPERTS), D ** -0.5),
        'b_router': nrm((L, N_EXPERTS), 0.01),
        'w_gu': nrm((L, N_EXPERTS, D, 2 * D_FF), D ** -0.5),
        'b_gu': nrm((L, N_EXPERTS, 2 * D_FF), 0.02),
        'w_down': nrm((L, N_EXPERTS, D_FF, D), D_FF ** -0.5),
        'b_down': nrm((L, N_EXPERTS, D), 0.02),
    }


def reference(x_prompt, x_sample, c_prompt, c_sample, cache_k, cache_v, cache_idx_k, state_rwkv, state_shift,
              w_ada, b_ada, g_pre_mix, g_post_mix, g_pre_ffn, g_post_ffn, w_in, ik_ln_w, ik_ln_b,
              mu_rwkv, w0, w_up, a0, a_up, g_up, k_k, k_a, r_k, ln_x_w, ln_x_b,
              w_o_att, w_o_rwkv, w_out, w_router, b_router, w_gu, b_gu, w_down, b_down):
    bp = x_prompt.shape[0]
    dt = x_prompt.dtype
    y_prompt, y_sample = x_prompt, x_sample
    st_p = [[], [], [], [], []]
    st_s = [[], [], [], [], []]
    for l in range(DEPTH):
        p = dict(w_ada=w_ada[l], b_ada=b_ada[l], g_pre_mix=g_pre_mix[l], g_post_mix=g_post_mix[l],
                 g_pre_ffn=g_pre_ffn[l], g_post_ffn=g_post_ffn[l], w_in=w_in[l], ik_ln_w=ik_ln_w[l],
                 ik_ln_b=ik_ln_b[l], mu=mu_rwkv[l], w0=w0[l], w_up=w_up[l], a0=a0[l], a_up=a_up[l],
                 g_up=g_up[l], k_k=k_k[l], k_a=k_a[l], r_k=r_k[l], ln_w=ln_x_w[l], ln_b=ln_x_b[l],
                 w_o_att=w_o_att[l], w_o_rwkv=w_o_rwkv[l], w_out=w_out[l], w_router=w_router[l],
                 b_router=b_router[l], w_gu=w_gu[l], b_gu=b_gu[l], w_down=w_down[l], b_down=b_down[l])
        y_prompt, kp, vp, ikp, sp, shp = trunk_layer(
            y_prompt, c_prompt,
            jnp.zeros((bp, 0, ATT_KV_HEADS, HEAD_DIM), dt), jnp.zeros((bp, 0, ATT_KV_HEADS, HEAD_DIM), dt),
            jnp.zeros((bp, 0, IDX_DIM), dt), jnp.zeros((bp, RWKV_HEADS, RWKV_HEAD, RWKV_HEAD), dt),
            jnp.zeros((bp, 1, RWKV_COLS), dt), p)
        y_sample, ks_, vs_, iks, ss, shs = trunk_layer(
            y_sample, c_sample, cache_k[l], cache_v[l], cache_idx_k[l], state_rwkv[l], state_shift[l], p)
        for lst, val in zip(st_p, (kp, vp, ikp, sp, shp)):
            lst.append(val)
        for lst, val in zip(st_s, (ks_, vs_, iks, ss, shs)):
            lst.append(val)
    new_k_p, new_v_p, new_ik_p, rwkv_p, shift_p = [jnp.stack(v, axis=0) for v in st_p]
    new_k_s, new_v_s, new_ik_s, rwkv_s, shift_s = [jnp.stack(v, axis=0) for v in st_s]
    return (y_prompt, y_sample, new_k_p, new_v_p, new_ik_p, rwkv_p, shift_p,
            new_k_s, new_v_s, new_ik_s, rwkv_s, shift_s)
```

```python
import functools

import numpy as np
import jax
import jax.numpy as jnp
from jax import lax
from jax.experimental import pallas as pl
from jax.experimental.pallas import tpu as pltpu

F32 = jnp.float32
BF16 = jnp.bfloat16
HI = lax.Precision.HIGHEST

CHUNK = 64
ATT_HEADS = 8
ATT_KV_HEADS = 2
HEAD_DIM = 64
IDX_HEADS = 8
IDX_DIM = 64
IDX_SCALE = (IDX_DIM ** -0.5) * (IDX_HEADS ** -0.5)
TOPK_MAX = 256
RWKV_HEADS = 8
RWKV_HEAD = 64
RWKV_WIDTH = RWKV_HEADS * RWKV_HEAD
DECAY_LORA = 64
AAA_LORA = 64
GATE_LORA = 128
RWKV_COLS = 3 * RWKV_WIDTH + DECAY_LORA + AAA_LORA + GATE_LORA
GN_EPS = 64e-5
L2_EPS = 1e-24
N_EXPERTS = 32
TOP_K = 4
SWIGLU_LIMIT = 7.0
SWIGLU_ALPHA = 1.702
NORM_EPS = 1e-6
LN_EPS = 1e-6

LANES = 128
VMEM_LIMIT = 56 * 1024 * 1024

HP = 8 * LANES
RW_CHUNK = 128
NEG_BIG = -1e30
KEY_NEG_INF = -2139095041
INT_MIN = -2147483648


def _params(sem):
    return pltpu.CompilerParams(dimension_semantics=sem, vmem_limit_bytes=VMEM_LIMIT)


def _nt(a, b, precision=None):
    return lax.dot_general(a, b, (((1,), (1,)), ((), ())), precision=precision,
                           preferred_element_type=F32)


def _mm(a, b, precision=None):
    return jnp.dot(a, b, precision=precision, preferred_element_type=F32)


def _rms(x, g):
    return x * lax.rsqrt(jnp.mean(x * x, axis=-1, keepdims=True) + NORM_EPS) * g


def _headpad_idx(seg_off, lane_off_fn=lambda h: 0):
    idx = np.zeros((HP,), np.int32)
    ok = np.zeros((HP,), bool)
    for h in range(8):
        d0 = h * LANES + lane_off_fn(h)
        idx[d0:d0 + 64] = seg_off + h * 64 + np.arange(64)
        ok[d0:d0 + 64] = True
    return idx, ok


def _plain_idx(seg_off, n, width):
    idx = np.zeros((width,), np.int32)
    ok = np.zeros((width,), bool)
    idx[:n] = seg_off + np.arange(n)
    ok[:n] = True
    return idx, ok


def _cat(parts):
    return np.concatenate([p[0] for p in parts]), np.concatenate([p[1] for p in parts])


_O_Q, _O_K, _O_V, _O_IQ, _O_IK, _O_IW, _O_ZR, _O_GA, _O_GB = 0, 512, 640, 768, 1280, 1344, 1352, 3144, 4168
_COLS_A = _cat([_headpad_idx(_O_Q, lambda h: (h // 4) * 64), _plain_idx(_O_K, 128, 128), _plain_idx(_O_V, 128, 128),
                _headpad_idx(_O_IQ), _plain_idx(_O_IK, 64, 128), _plain_idx(_O_IW, 8, 128)])
_COLS_ZR = _cat([_headpad_idx(0), _headpad_idx(512), _headpad_idx(1024), _plain_idx(1536, 128, 128),
                 _plain_idx(1664, 128, 128)])
ZRW = _COLS_ZR[0].shape[0]
_COLS_B = _cat([(_COLS_ZR[0] + _O_ZR, _COLS_ZR[1]), _plain_idx(_O_GA, 1024, 1024), _plain_idx(_O_GB, 1024, 1024)])
NA = _COLS_A[0].shape[0]
NB = _COLS_B[0].shape[0]
_HEAD_IDX = _headpad_idx(0)
_ATT_ROW_IDX = _headpad_idx(0, lambda h: (h // 4) * 64)


def _take_cols(w, cols):
    idx, ok = cols
    return jnp.where(jnp.asarray(ok)[None, :], jnp.take(w, jnp.asarray(idx), axis=1), 0.0)


def _take_rows(w, cols):
    idx, ok = cols
    return jnp.where(jnp.asarray(ok)[:, None], jnp.take(w, jnp.asarray(idx), axis=0), 0.0)


def _mod_spec(m, tm):
    assert m.shape[1] == 1
    return pl.BlockSpec((1, 1, m.shape[2]), lambda b, i: (b, 0, 0))


def _ada_kernel(c_ref, w_ref, b_ref, o_ref):
    c = c_ref[...]
    s = c * (1.0 / (1.0 + jnp.exp(-c)))
    o_ref[...] = _mm(s, w_ref[...], HI) + b_ref[...]


def _ada(c, w, b):
    n, d = c.shape
    nout = w.shape[1]
    bn = 768
    return pl.pallas_call(
        _ada_kernel,
        grid=(nout // bn,),
        in_specs=[pl.BlockSpec((n, d), lambda j: (0, 0)),
                  pl.BlockSpec((d, bn), lambda j: (0, j)),
                  pl.BlockSpec((1, bn), lambda j: (0, j))],
        out_specs=pl.BlockSpec((n, bn), lambda j: (0, j)),
        out_shape=jax.ShapeDtypeStruct((n, nout), F32),
        compiler_params=_params(("arbitrary",)),
        name="ada",
    )(c, w, b.reshape(1, nout))


def _modulated(x_ref, sc_ref, sh_ref, g_ref):
    h = _rms(x_ref[0], g_ref[...])
    return (h * (1.0 + sc_ref[0]) + sh_ref[0]).astype(BF16)


def _proj_a_kernel(x_ref, sc_ref, sh_ref, g_ref, w_ref, lnw_ref, lnb_ref,
                   q_ref, k_ref, v_ref, kb_ref, vb_ref, iq_ref, ik_ref, ikb_ref, iw_ref):
    hb = _modulated(x_ref, sc_ref, sh_ref, g_ref)
    for h in range(8):
        zq = _mm(hb, w_ref[:, h * LANES:(h + 1) * LANES])
        q_ref[0, h] = (zq * (HEAD_DIM ** -0.5)).astype(BF16)
    kv = _mm(hb, w_ref[:, 1024:1280])
    k = kv[:, :128]
    v = kv[:, 128:]
    k_ref[0] = k
    v_ref[0] = v
    kb_ref[0] = k.astype(BF16)
    vb_ref[0] = v.astype(BF16)
    for h in range(8):
        zi = _mm(hb, w_ref[:, 1280 + h * LANES:1280 + (h + 1) * LANES])
        iq_ref[0, h] = zi.astype(BF16)
    t = _mm(hb, w_ref[:, 2304:2560])
    ik = t[:, :128]
    lane = lax.broadcasted_iota(jnp.int32, ik.shape, 1)
    valid = lane < IDX_DIM
    mu = jnp.sum(ik, axis=-1, keepdims=True) * (1.0 / IDX_DIM)
    d = jnp.where(valid, ik - mu, 0.0)
    var = jnp.sum(d * d, axis=-1, keepdims=True) * (1.0 / IDX_DIM)
    ikn = d * lax.rsqrt(var + LN_EPS) * lnw_ref[...] + lnb_ref[...]
    ik_ref[0] = ikn[:, :IDX_DIM]
    ikb_ref[0] = ikn.astype(BF16)
    iw_ref[0] = t[:, 128:136] * IDX_SCALE


def _proj_a(x, sc, sh, g, w, lnw, lnb, tm):
    B, T, D = x.shape
    nt = T // tm
    tok = lambda n, dt: jax.ShapeDtypeStruct((B, T, n), dt)
    hm = lambda dt: jax.ShapeDtypeStruct((B, 8, T, LANES), dt)
    tspec = lambda n: pl.BlockSpec((1, tm, n), lambda b, i: (b, i, 0))
    hspec = pl.BlockSpec((1, 8, tm, LANES), lambda b, i: (b, 0, i, 0))
    full = lambda a: pl.BlockSpec(a.shape, lambda b, i: (0,) * a.ndim)
    return pl.pallas_call(
        _proj_a_kernel,
        grid=(B, nt),
        in_specs=[tspec(D), _mod_spec(sc, tm), _mod_spec(sh, tm), full(g), full(w), full(lnw), full(lnb)],
        out_specs=[hspec, tspec(128), tspec(128), tspec(128), tspec(128), hspec, tspec(IDX_DIM), tspec(128),
                   tspec(8)],
        out_shape=[hm(BF16), tok(128, F32), tok(128, F32), tok(128, BF16), tok(128, BF16), hm(BF16),
                   tok(IDX_DIM, F32), tok(128, BF16), tok(8, F32)],
        compiler_params=_params(("arbitrary", "arbitrary")),
        name="proj_att",
    )(x, sc, sh, g, w, lnw, lnb)


def _proj_b_kernel(x_ref, sc_ref, sh_ref, g_ref, w_ref, zr_ref, ga_ref, gb_ref):
    hb = _modulated(x_ref, sc_ref, sh_ref, g_ref)
    for j in range(ZRW // 256):
        zr_ref[0, :, j * 256:(j + 1) * 256] = _mm(hb, w_ref[:, j * 256:(j + 1) * 256])
    for j in range(4):
        za = _mm(hb, w_ref[:, ZRW + j * 256:ZRW + (j + 1) * 256])
        ga_ref[0, :, j * 256:(j + 1) * 256] = 1.0 / (1.0 + jnp.exp(-za))
        zb = _mm(hb, w_ref[:, ZRW + 1024 + j * 256:ZRW + 1024 + (j + 1) * 256])
        gb_ref[0, :, j * 256:(j + 1) * 256] = 1.0 / (1.0 + jnp.exp(-zb))


def _proj_b(x, sc, sh, g, w, tm):
    B, T, D = x.shape
    nt = T // tm
    tok = lambda n: jax.ShapeDtypeStruct((B, T, n), F32)
    tspec = lambda n: pl.BlockSpec((1, tm, n), lambda b, i: (b, i, 0))
    full = lambda a: pl.BlockSpec(a.shape, lambda b, i: (0,) * a.ndim)
    return pl.pallas_call(
        _proj_b_kernel,
        grid=(B, nt),
        in_specs=[tspec(D), _mod_spec(sc, tm), _mod_spec(sh, tm), full(g), full(w)],
        out_specs=[tspec(ZRW), tspec(1024), tspec(1024)],
        out_shape=[tok(ZRW), tok(1024), tok(1024)],
        compiler_params=_params(("arbitrary", "arbitrary")),
        name="proj_rwkv",
    )(x, sc, sh, g, w)


def _sortable_to_float(t):
    bits = t ^ (lax.shift_right_arithmetic(t, 31) & 0x7FFFFFFF)
    return lax.bitcast_convert_type(bits, F32)


def _attn_kernel(iq_ref, iw_ref, q_ref, ik_ref, k_ref, v_ref, o_ref, s_ref, m_ref, l_ref, acc_ref,
                 *, R, KB, ltot, q_off, topk):
    j = pl.program_id(1)
    q0 = q_off + j * R
    pos = q0 + lax.broadcasted_iota(jnp.int32, (R, 1), 0)
    qchunk = lax.shift_right_logical(pos, 6)
    kend = jnp.minimum(ltot, (lax.shift_right_logical(q0 + R - 1, 6) + 1) * CHUNK)
    nkb = lax.shift_right_logical(kend + KB - 1, KB.bit_length() - 1)
    lane_k = lax.broadcasted_iota(jnp.int32, (R, KB), 1)
    nch = KB // LANES

    iq_all = iq_ref[0].reshape(8 * R, LANES)
    iw = iw_ref[0]
    iwb = [jnp.broadcast_to(iw[:, h:h + 1], (R, KB)) for h in range(8)]

    def score_body(kb, carry):
        off = pl.multiple_of(kb * KB, KB)
        ikb = ik_ref[0, pl.ds(off, KB), :]
        s_all = _nt(iq_all, ikb)
        acc = jnp.zeros((R, KB), F32)
        for h in range(8):
            acc = acc + jnp.maximum(s_all[h * R:(h + 1) * R], 0.0) * iwb[h]
        kidx = off + lane_k
        adm = (lax.shift_right_logical(kidx, 6) <= qchunk) & (kidx < ltot)
        s_ref[kb] = jnp.where(adm, acc, -jnp.inf)
        return carry

    lax.fori_loop(0, nkb, score_body, 0)

    def count_ge(cand_f):
        def body(kb, part):
            blk = s_ref[kb]
            m = jnp.where(blk >= cand_f, 1.0, 0.0)
            for c in range(nch):
                part = part + m[:, c * LANES:(c + 1) * LANES]
            return part
        part = lax.fori_loop(0, nkb, body, jnp.zeros((R, LANES), F32))
        return jnp.sum(part, axis=1, keepdims=True)

    kf = float(topk)
    c0 = count_ge(jnp.zeros((R, 1), F32))
    t0 = jnp.where(c0 >= kf, 0, INT_MIN).astype(jnp.int32)

    def bit_body(i, t):
        cand = t + lax.shift_left(jnp.int32(1), 30 - i)
        cnt = count_ge(_sortable_to_float(cand))
        return jnp.where(cnt >= kf, cand, t)

    t = lax.fori_loop(0, 31, bit_body, t0)
    all_finite = t <= KEY_NEG_INF
    tau = jnp.where(all_finite, -jnp.inf, _sortable_to_float(jnp.maximum(t, KEY_NEG_INF)))

    def count_gt_eq(_):
        def body(kb, carry):
            pg, pe = carry
            blk = s_ref[kb]
            mg = jnp.where(blk > tau, 1.0, 0.0)
            me = jnp.where(blk == tau, 1.0, 0.0)
            for c in range(nch):
                pg = pg + mg[:, c * LANES:(c + 1) * LANES]
                pe = pe + me[:, c * LANES:(c + 1) * LANES]
            return pg, pe
        z = jnp.zeros((R, LANES), F32)
        pg, pe = lax.fori_loop(0, nkb, body, (z, z))
        return jnp.sum(pg, axis=1, keepdims=True), jnp.sum(pe, axis=1, keepdims=True)

    cnt_gt, cnt_eq = count_gt_eq(0)
    need = kf - cnt_gt
    tie = (cnt_eq > need) & jnp.logical_not(all_finite)
    any_tie = jnp.max(jnp.where(tie, 1.0, 0.0)) > 0.0

    def tie_bound():
        def count_eq_below(x):
            def body(kb, part):
                blk = s_ref[kb]
                kidx = kb * KB + lane_k
                m = jnp.where((blk == tau) & (kidx < x), 1.0, 0.0)
                for c in range(nch):
                    part = part + m[:, c * LANES:(c + 1) * LANES]
                return part
            part = lax.fori_loop(0, nkb, body, jnp.zeros((R, LANES), F32))
            return jnp.sum(part, axis=1, keepdims=True)

        nbits = int(ltot).bit_length()

        def body(i, x):
            cand = x + lax.shift_left(jnp.int32(1), nbits - 1 - i)
            ok = count_eq_below(cand) < need
            return jnp.where(ok, cand, x)
        x = lax.fori_loop(0, nbits, body, jnp.zeros((R, 1), jnp.int32))
        return x + 1

    big = jnp.full((R, 1), 1 << 30, jnp.int32)
    bound = lax.cond(any_tie, lambda: jnp.where(tie, tie_bound(), big), lambda: big)

    def bias_body(kb, carry):
        blk = s_ref[kb]
        kidx = kb * KB + lane_k
        sel = (blk > tau) | ((blk == tau) & (kidx < bound))
        sel = sel & (blk > -jnp.inf)
        s_ref[kb] = jnp.where(sel, 0.0, NEG_BIG)
        return carry

    lax.fori_loop(0, nkb, bias_body, 0)

    q_all = q_ref[0].reshape(8 * R, LANES)
    m_ref[...] = jnp.full(m_ref.shape, NEG_BIG, F32)
    l_ref[...] = jnp.zeros(l_ref.shape, F32)
    acc_ref[...] = jnp.zeros(acc_ref.shape, F32)

    def att_body(kb, carry):
        off = pl.multiple_of(kb * KB, KB)
        kblk = k_ref[0, pl.ds(off, KB), :]
        vblk = v_ref[0, pl.ds(off, KB), :]
        bias = s_ref[kb]
        logits = _nt(q_all, kblk)
        for h in range(8):
            lg = logits[h * R:(h + 1) * R] + bias
            m_old = m_ref[h]
            m_new = jnp.maximum(m_old, jnp.max(lg, axis=1, keepdims=True))
            alpha = jnp.exp(m_old - m_new)
            p = jnp.exp(lg - m_new)
            l_ref[h] = alpha * l_ref[h] + jnp.sum(p, axis=1, keepdims=True)
            acc_ref[h] = alpha * acc_ref[h] + _mm(p.astype(BF16), vblk)
            m_ref[h] = m_new
        return carry

    lax.fori_loop(0, nkb, att_body, 0)
    for h in range(8):
        o_ref[0, h] = (acc_ref[h] / l_ref[h]).astype(BF16)


def _attention(iq, iw, q, ikb, kb, vb, *, R, KB, ltot, q_off):
    B, _, Sq, _ = q.shape
    Lp = kb.shape[1]
    topk = min(TOPK_MAX, ltot // 4)
    assert Lp % KB == 0 and KB >= topk and Sq % R == 0
    hspec = pl.BlockSpec((1, 8, R, LANES), lambda b, i: (b, 0, i, 0))
    kspec = pl.BlockSpec((1, Lp, LANES), lambda b, i: (b, 0, 0))
    kern = functools.partial(_attn_kernel, R=R, KB=KB, ltot=ltot, q_off=q_off, topk=topk)
    return pl.pallas_call(
        kern,
        grid=(B, Sq // R),
        in_specs=[hspec, pl.BlockSpec((1, R, 8), lambda b, i: (b, i, 0)), hspec, kspec, kspec, kspec],
        out_specs=hspec,
        out_shape=jax.ShapeDtypeStruct((B, 8, Sq, LANES), BF16),
        scratch_shapes=[pltpu.VMEM((Lp // KB, R, KB), F32), pltpu.VMEM((8, R, 1), F32),
                        pltpu.VMEM((8, R, 1), F32), pltpu.VMEM((8, R, LANES), F32)],
        compiler_params=_params(("arbitrary", "arbitrary")),
        name="dsa_attention",
    )(iq, iw, q, ikb, kb, vb)


def _head_sum(x):
    parts = []
    for h in range(8):
        s = jnp.sum(x[:, h * LANES:(h + 1) * LANES], axis=1, keepdims=True)
        parts.append(jnp.broadcast_to(s, (x.shape[0], LANES)))
    return jnp.concatenate(parts, axis=1)


def _rwkv_prep_kernel(z_ref, sh0_ref, mu_ref, w0_ref, wup_ref, a0_ref, aup_ref, gup_ref, kk_ref, ka_ref, rk_ref,
                      r_o, lw_o, k_o, v_o, kkn_o, b_o, g_o, bg_o, lwt_o, kt_o, bt_o, carry_ref,
                      *, tm, t_valid):
    i = pl.program_id(1)

    @pl.when(i == 0)
    def _():
        carry_ref[...] = sh0_ref[0]

    z = z_ref[0]
    row = lax.broadcasted_iota(jnp.int32, (tm, 1), 0)
    prev = jnp.where(row == 0, carry_ref[...], pltpu.roll(z, 1, axis=0))
    carry_ref[...] = z[tm - 1:tm, :]
    zm = z + (prev - z) * mu_ref[...]
    live = i * tm + row < t_valid
    zm = jnp.where(live, zm, 0.0)
    r = zm[:, 0:HP]
    k = zm[:, HP:2 * HP]
    v = zm[:, 2 * HP:3 * HP]
    wa = zm[:, 3 * HP:3 * HP + LANES]
    gd = zm[:, 3 * HP + LANES:3 * HP + 2 * LANES]
    w_raw = w0_ref[...] + _mm(jnp.tanh(wa), wup_ref[...], HI)
    lw = (-float(np.exp(-0.5))) / (1.0 + jnp.exp(-w_raw))
    lw = jnp.where(live, lw, 0.0)
    a = 1.0 / (1.0 + jnp.exp(-(a0_ref[...] + _mm(wa, aup_ref[...], HI))))
    g = _mm(1.0 / (1.0 + jnp.exp(-gd)), gup_ref[...], HI)
    kk = k * kk_ref[...]
    kk = kk * lax.rsqrt(jnp.maximum(_head_sum(kk * kk), L2_EPS))
    k_mod = k * (1.0 + (a - 1.0) * ka_ref[...])
    b = kk * a
    bonus = _head_sum(r * k_mod * rk_ref[...]) * v
    r_o[0] = r
    lw_o[0] = lw
    k_o[0] = k_mod
    v_o[0] = v
    kkn_o[0] = kk
    b_o[0] = b
    g_o[0] = g
    bg_o[0] = bonus * g
    lwt_o[0] = lw.T
    kt_o[0] = k_mod.T
    bt_o[0] = b.T


def _rwkv_prep(zr, shift0, prm, tm, t_valid):
    B, T, _ = zr.shape
    tspec = pl.BlockSpec((1, tm, HP), lambda b, i: (b, i, 0))
    fspec = pl.BlockSpec((1, HP, tm), lambda b, i: (b, 0, i))
    full = lambda a: pl.BlockSpec(a.shape, lambda b, i: (0,) * a.ndim)
    tok = jax.ShapeDtypeStruct((B, T, HP), F32)
    feat = jax.ShapeDtypeStruct((B, HP, T), F32)
    names = ("mu", "w0", "w_up", "a0", "a_up", "g_up", "k_k", "k_a", "r_k")
    ws = [prm[n] for n in names]
    return pl.pallas_call(
        functools.partial(_rwkv_prep_kernel, tm=tm, t_valid=t_valid),
        grid=(B, T // tm),
        in_specs=[pl.BlockSpec((1, tm, ZRW), lambda b, i: (b, i, 0)),
                  pl.BlockSpec((1, 1, ZRW), lambda b, i: (b, 0, 0))] + [full(w) for w in ws],
        out_specs=[tspec] * 8 + [fspec] * 3,
        out_shape=[tok] * 8 + [feat] * 3,
        scratch_shapes=[pltpu.VMEM((1, ZRW), F32)],
        compiler_params=_params(("arbitrary", "arbitrary")),
        name="rwkv_prep",
    )(zr, shift0, *ws)


def _rwkv_chunk_kernel(r_ref, lw_ref, k_ref, v_ref, kk_ref, b_ref, lwt_ref, kt_ref, bt_ref, a1_ref, a2_ref,
                       *, C, hb):
    ri = lax.broadcasted_iota(jnp.int32, (C, C), 0)
    ci = lax.broadcasted_iota(jnp.int32, (C, C), 1)
    low_incl = jnp.where(ri >= ci, 1.0, 0.0)
    up_incl = jnp.where(ri <= ci, 1.0, 0.0)
    strict = ri > ci
    incl = ri >= ci
    eye64 = jnp.where(lax.broadcasted_iota(jnp.int32, (64, 64), 0) == lax.broadcasted_iota(jnp.int32, (64, 64), 1),
                      1.0, 0.0)
    for u in range(hb):
        sl = slice(u * LANES, (u + 1) * LANES)
        lw = lw_ref[0, :, sl]
        r = r_ref[0, :, sl]
        k = k_ref[0, :, sl]
        v = v_ref[0, :, sl]
        kk = kk_ref[0, :, sl]
        b = b_ref[0, :, sl]
        lwt = lwt_ref[0, sl, :][:64]
        kt = kt_ref[0, sl, :][:64]
        bt = bt_ref[0, sl, :][:64]
        cum = _mm(low_incl, lw, HI)
        cumt = _mm(lwt, up_incl, HI)
        e_neg = jnp.exp(-cum)
        at = -kk * jnp.exp(cum - lw)
        rt = r * jnp.exp(cum)
        g = _nt(jnp.concatenate([at, rt], axis=0), jnp.concatenate([b * e_neg, k * e_neg], axis=0), HI)
        n = jnp.where(strict, g[:C, :C], 0.0)
        aak = jnp.where(strict, g[:C, C:], 0.0)
        lrb = jnp.where(incl, g[C:, :C], 0.0)
        lrk = jnp.where(incl, g[C:, C:], 0.0)
        z = jnp.concatenate([aak, at], axis=1)
        steps = int(C).bit_length() - 1
        for s in range(steps):
            if s + 1 < steps:
                x = _mm(n, jnp.concatenate([n, z], axis=1), HI)
                n = x[:, :C]
                z = z + x[:, C:]
            else:
                z = z + _mm(n, z, HI)
        w2 = z[:, :C]
        w1 = z[:, C:]
        w2v = _mm(w2, v, HI)
        cl = cumt[:, C - 1:C]
        eb = jnp.exp(cl - cumt)
        top = jnp.concatenate([bt * eb, kt * eb], axis=1)
        bot = jnp.concatenate([lrb, lrk], axis=1)
        lhs = jnp.concatenate([top, bot], axis=0)
        left = _mm(lhs[:, :C], w1, HI)
        right = _mm(lhs, jnp.concatenate([w2v, v], axis=0), HI)
        diag = eye64 * jnp.exp(cl)
        a1_ref[0, u, 0] = jnp.concatenate([left[:64, :64] + diag, (left[64:] + rt)[:, :64]], axis=0)
        a2_ref[0, u, 0] = right


def _rwkv_chunks(r, lw, k, v, kk, b, lwt, kt, bt, C, hb):
    B, T, _ = r.shape
    nc = T // C
    tspec = pl.BlockSpec((1, C, hb * LANES), lambda bb, h, c: (bb, c, h))
    fspec = pl.BlockSpec((1, hb * LANES, C), lambda bb, h, c: (bb, h, c))
    return pl.pallas_call(
        functools.partial(_rwkv_chunk_kernel, C=C, hb=hb),
        grid=(B, 8 // hb, nc),
        in_specs=[tspec] * 6 + [fspec] * 3,
        out_specs=[pl.BlockSpec((1, hb, 1, 64 + C, 64), lambda bb, h, c: (bb, h, c, 0, 0)),
                   pl.BlockSpec((1, hb, 1, 64 + C, LANES), lambda bb, h, c: (bb, h, c, 0, 0))],
        out_shape=[jax.ShapeDtypeStruct((B, 8, nc, 64 + C, 64), F32),
                   jax.ShapeDtypeStruct((B, 8, nc, 64 + C, LANES), F32)],
        compiler_params=_params(("arbitrary", "arbitrary", "arbitrary")),
        name="rwkv_chunks",
    )(r, lw, k, v, kk, b, lwt, kt, bt)


def _rwkv_scan_kernel(a1_ref, a2_ref, s0_ref, y_ref, sT_ref, h_ref, *, C, cb):
    c = pl.program_id(1)

    @pl.when(c == 0)
    def _():
        h_ref[...] = s0_ref[0]

    for cc in range(cb):
        for h in range(8):
            res = _mm(a1_ref[0, h, cc], h_ref[h], HI) + a2_ref[0, h, cc]
            h_ref[h] = res[:64]
            y_ref[0, cc * C:(cc + 1) * C, h * LANES:(h + 1) * LANES] = res[64:]

    @pl.when(c == pl.num_programs(1) - 1)
    def _():
        sT_ref[0] = h_ref[...]


def _rwkv_scan(a1, a2, s0t, C, cb):
    B, _, nc, _, _ = a1.shape
    return pl.pallas_call(
        functools.partial(_rwkv_scan_kernel, C=C, cb=cb),
        grid=(B, nc // cb),
        in_specs=[pl.BlockSpec((1, 8, cb, 64 + C, 64), lambda b, c: (b, 0, c, 0, 0)),
                  pl.BlockSpec((1, 8, cb, 64 + C, LANES), lambda b, c: (b, 0, c, 0, 0)),
                  pl.BlockSpec((1, 8, 64, LANES), lambda b, c: (b, 0, 0, 0))],
        out_specs=[pl.BlockSpec((1, cb * C, HP), lambda b, c: (b, c, 0)),
                   pl.BlockSpec((1, 8, 64, LANES), lambda b, c: (b, 0, 0, 0))],
        out_shape=[jax.ShapeDtypeStruct((B, nc * C, HP), F32),
                   jax.ShapeDtypeStruct((B, 8, 64, LANES), F32)],
        scratch_shapes=[pltpu.VMEM((8, 64, LANES), F32)],
        compiler_params=_params(("arbitrary", "arbitrary")),
        name="rwkv_scan",
    )(a1, a2, s0t)


def _mix_kernel(att_ref, y_ref, g_ref, bg_ref, sga_ref, sgb_ref, x_ref, gt1_ref, sc2_ref, sh2_ref,
                woa_ref, wor_ref, wout_ref, lnw_ref, lnb_ref, gpost_ref, gpre_ref, wr_ref, br_ref,
                x1_ref, h2_ref, gate_ref, *, tm):
    att = jnp.concatenate([att_ref[0, h] for h in range(8)], axis=1)
    lane = lax.broadcasted_iota(jnp.int32, (tm, LANES), 1)
    valid = lane < RWKV_HEAD
    parts = []
    for h in range(8):
        y = y_ref[0, :, h * LANES:(h + 1) * LANES]
        mu = jnp.sum(y, axis=1, keepdims=True) * (1.0 / RWKV_HEAD)
        d = jnp.where(valid, y - mu, 0.0)
        var = jnp.sum(d * d, axis=1, keepdims=True) * (1.0 / RWKV_HEAD)
        parts.append(d * lax.rsqrt(var + GN_EPS))
    yn = jnp.concatenate(parts, axis=1)
    rw = (yn * lnw_ref[...] + lnb_ref[...]) * g_ref[0] + bg_ref[0]
    mix = sga_ref[0] * _mm(att, woa_ref[...]) + sgb_ref[0] * _mm(rw.astype(BF16), wor_ref[...])
    o = _mm(mix.astype(BF16), wout_ref[...])
    x1 = x_ref[0] + gt1_ref[0] * _rms(o, gpost_ref[...])
    x1_ref[0] = x1
    h2 = _rms(x1, gpre_ref[...]) * (1.0 + sc2_ref[0]) + sh2_ref[0]
    h2_ref[0] = h2.astype(BF16)
    logits = _mm(h2, wr_ref[...], HI) + br_ref[...]
    el = lax.broadcasted_iota(jnp.int32, logits.shape, 1).astype(F32)
    work = logits
    sel = jnp.zeros(logits.shape, jnp.bool_)
    vmax = None
    for kk in range(TOP_K):
        mx = jnp.max(work, axis=1, keepdims=True)
        if kk == 0:
            vmax = mx
        first = jnp.min(jnp.where(work == mx, el, float(N_EXPERTS)), axis=1, keepdims=True)
        hit = el == first
        sel = sel | hit
        work = jnp.where(hit, -jnp.inf, work)
    e = jnp.where(sel, jnp.exp(logits - vmax), 0.0)
    gate_ref[0] = e / jnp.sum(e, axis=1, keepdims=True)


def _mix(att, y, g, bg, sga, sgb, x, gt1, sc2, sh2, wts, tm):
    B, T, D = x.shape
    tspec = lambda n: pl.BlockSpec((1, tm, n), lambda b, i: (b, i, 0))
    full = lambda a: pl.BlockSpec(a.shape, lambda b, i: (0,) * a.ndim)
    tok = lambda n, dt: jax.ShapeDtypeStruct((B, T, n), dt)
    return pl.pallas_call(
        functools.partial(_mix_kernel, tm=tm),
        grid=(B, T // tm),
        in_specs=[pl.BlockSpec((1, 8, tm, LANES), lambda b, i: (b, 0, i, 0)), tspec(HP), tspec(HP), tspec(HP),
                  tspec(D), tspec(D), tspec(D), _mod_spec(gt1, tm), _mod_spec(sc2, tm), _mod_spec(sh2, tm)]
                 + [full(w) for w in wts],
        out_specs=[tspec(D), tspec(D), tspec(N_EXPERTS)],
        out_shape=[tok(D, F32), tok(D, BF16), tok(N_EXPERTS, F32)],
        compiler_params=_params(("arbitrary", "arbitrary")),
        name="mix_router",
    )(att, y, g, bg, sga, sgb, x, gt1, sc2, sh2, *wts)


def _moe_kernel(h_ref, gate_ref, x1_ref, gt2_ref, gpost_ref, wgu_ref, bgu_ref, wd_ref, bd_ref, o_ref, acc_ref,
                *, d_ff):
    e = pl.program_id(1)

    @pl.when(e == 0)
    def _():
        acc_ref[...] = jnp.zeros(acc_ref.shape, F32)

    hgu = _mm(h_ref[...], wgu_ref[0]) + bgu_ref[0]
    hg = jnp.minimum(hgu[:, :d_ff], SWIGLU_LIMIT)
    hl = jnp.clip(hgu[:, d_ff:], -SWIGLU_LIMIT, SWIGLU_LIMIT)
    act = hg * (1.0 / (1.0 + jnp.exp(-SWIGLU_ALPHA * hg))) * (hl + 1.0)
    contrib = _mm(act.astype(BF16), wd_ref[0]) + bd_ref[0]
    gate = gate_ref[...]
    el = lax.broadcasted_iota(jnp.int32, gate.shape, 1)
    ge = jnp.sum(jnp.where(el == e, gate, 0.0), axis=1, keepdims=True)
    acc_ref[...] += ge * contrib

    @pl.when(e == pl.num_programs(1) - 1)
    def _():
        o_ref[...] = x1_ref[...] + gt2_ref[0] * _rms(acc_ref[...], gpost_ref[...])


def _moe(h2, gate, x1, gt2, gpost, wgu, bgu, wd, bd, tm):
    N, D = h2.shape
    E, _, F2 = wgu.shape
    tspec = lambda n: pl.BlockSpec((tm, n), lambda i, e: (i, 0))
    if gt2.shape[1] == 1:
        tpb = N // gt2.shape[0]
        gspec = pl.BlockSpec((1, 1, D), lambda i, e: ((i * tm) // tpb, 0, 0))
    else:
        gspec = pl.BlockSpec((1, tm, D), lambda i, e: (i, 0, 0))
    return pl.pallas_call(
        functools.partial(_moe_kernel, d_ff=F2 // 2),
        grid=(N // tm, E),
        in_specs=[tspec(D), tspec(E), tspec(D), gspec, pl.BlockSpec((1, D), lambda i, e: (0, 0)),
                  pl.BlockSpec((1, D, F2), lambda i, e: (e, 0, 0)), pl.BlockSpec((1, 1, F2), lambda i, e: (e, 0, 0)),
                  pl.BlockSpec((1, F2 // 2, D), lambda i, e: (e, 0, 0)), pl.BlockSpec((1, 1, D), lambda i, e: (e, 0, 0))],
        out_specs=tspec(D),
        out_shape=jax.ShapeDtypeStruct((N, D), F32),
        scratch_shapes=[pltpu.VMEM((tm, D), F32)],
        compiler_params=_params(("arbitrary", "arbitrary")),
        name="moe",
    )(h2, gate, x1, gt2, gpost, wgu, bgu, wd, bd)


def _per_token(m, T):
    B, _, D = m.shape
    return jnp.broadcast_to(m, (B, T, D)).reshape(1, B * T, D)


def _layer(x, mods, past, s0, shift0, P, cfg):
    B, T, D = x.shape
    sh1, sc1, gt1, sh2, sc2, gt2 = mods
    tm = cfg["tm"]

    q, k, v, kb, vb, iq, ik, ikb, iw = _proj_a(x, sc1, sh1, P["g_pre_mix"], P["w_a"], P["ik_ln_w"], P["ik_ln_b"], tm)
    zr, sga, sgb = _proj_b(x, sc1, sh1, P["g_pre_mix"], P["w_b"], tm)

    if past is not None:
        pk, pv, pik = past
        plen = pk.shape[1]
        kb_all = jnp.concatenate([pk.reshape(B, plen, 128).astype(BF16), kb], axis=1)
        vb_all = jnp.concatenate([pv.reshape(B, plen, 128).astype(BF16), vb], axis=1)
        ik_all = jnp.concatenate([jnp.pad(pik, ((0, 0), (0, 0), (0, 64))).astype(BF16), ikb], axis=1)
    else:
        plen = 0
        kb_all, vb_all, ik_all = kb, vb, ikb
    ltot = plen + T
    KB = cfg["KB"]
    lp = -(-ltot // KB) * KB
    if lp != ltot:
        padk = ((0, 0), (0, lp - ltot), (0, 0))
        kb_all, vb_all, ik_all = jnp.pad(kb_all, padk), jnp.pad(vb_all, padk), jnp.pad(ik_all, padk)
    att = _attention(iq, iw, q, ik_all, kb_all, vb_all, R=cfg["R"], KB=KB, ltot=ltot, q_off=plen)

    C = RW_CHUNK
    tp = -(-T // C) * C
    zr_p = zr if tp == T else jnp.pad(zr, ((0, 0), (0, tp - T), (0, 0)))
    shift_pad = _take_cols(shift0.reshape(B, RWKV_COLS), _COLS_ZR).reshape(B, 1, ZRW)
    r, lw, km, vv, kkn, bb, g, bg, lwt, kt, bt = _rwkv_prep(zr_p, shift_pad, P, min(cfg["tm_rw"], tp), T)
    a1, a2 = _rwkv_chunks(r, lw, km, vv, kkn, bb, lwt, kt, bt, C, cfg["hb"])
    s0t = jnp.pad(jnp.swapaxes(s0, 2, 3), ((0, 0), (0, 0), (0, 0), (0, 64)))
    nc = tp // C
    cb = min(cfg["cb"], nc)
    y, sT = _rwkv_scan(a1, a2, s0t, C, cb)
    s_new = jnp.swapaxes(sT[..., :64], 2, 3)
    zlast = zr[:, T - 1]
    inv = np.zeros((RWKV_COLS,), np.int32)
    inv[_COLS_ZR[0][_COLS_ZR[1]]] = np.nonzero(_COLS_ZR[1])[0]
    shift_new = jnp.take(zlast, jnp.asarray(inv), axis=1).reshape(B, 1, RWKV_COLS)
    if tp != T:
        y, g, bg = y[:, :T], g[:, :T], bg[:, :T]

    wts = [P[n] for n in ("w_o_att", "w_o_rwkv", "w_out", "ln_x_w", "ln_x_b", "g_post_mix", "g_pre_ffn",
                          "w_router", "b_router")]
    x1, h2, gate = _mix(att, y, g, bg, sga, sgb, x, gt1, sc2, sh2, wts, tm)

    N = B * T
    tmm = min(cfg["tm_moe"], N)
    if T % tmm == 0:
        gt2m = gt2
    else:
        gt2m = _per_token(gt2, T).reshape(N // tmm, tmm, D)
    out = _moe(h2.reshape(N, D), gate.reshape(N, N_EXPERTS), x1.reshape(N, D), gt2m, P["g_post_ffn"],
               P["w_gu"], P["b_gu"], P["w_down"], P["b_down"], tmm)
    return (out.reshape(B, T, D), k.reshape(B, T, ATT_KV_HEADS, HEAD_DIM), v.reshape(B, T, ATT_KV_HEADS, HEAD_DIM),
            ik, s_new, shift_new)


def _prep_weights(l, w_in, ik_ln_w, ik_ln_b, mu_rwkv, w0, w_up, a0, a_up, g_up, k_k, k_a, r_k, ln_x_w, ln_x_b,
                  w_o_att, w_o_rwkv, w_out, w_router, b_router, w_gu, b_gu, w_down, b_down,
                  g_pre_mix, g_post_mix, g_pre_ffn, g_post_ffn):
    row = lambda a: a.reshape(1, -1)
    hp = lambda a: _take_cols(row(a), _HEAD_IDX)
    P = {}
    P["w_a"] = _take_cols(w_in[l], _COLS_A).astype(BF16)
    P["w_b"] = _take_cols(w_in[l], _COLS_B).astype(BF16)
    P["ik_ln_w"] = jnp.pad(row(ik_ln_w[l]), ((0, 0), (0, 64)))
    P["ik_ln_b"] = jnp.pad(row(ik_ln_b[l]), ((0, 0), (0, 64)))
    P["mu"] = _take_cols(row(mu_rwkv[l]), _COLS_ZR)
    P["w0"], P["a0"], P["k_k"], P["k_a"] = hp(w0[l]), hp(a0[l]), hp(k_k[l]), hp(k_a[l])
    P["r_k"] = hp(r_k[l].reshape(-1))
    P["ln_x_w"], P["ln_x_b"] = hp(ln_x_w[l]), hp(ln_x_b[l])
    z64 = jnp.zeros((64, HP), F32)
    P["w_up"] = jnp.concatenate([_take_cols(w_up[l], _HEAD_IDX), z64], axis=0)
    P["a_up"] = jnp.concatenate([z64, _take_cols(a_up[l], _HEAD_IDX)], axis=0)
    P["g_up"] = _take_cols(g_up[l], _HEAD_IDX)
    P["w_o_att"] = _take_rows(w_o_att[l], _ATT_ROW_IDX).astype(BF16)
    P["w_o_rwkv"] = _take_rows(w_o_rwkv[l], _HEAD_IDX).astype(BF16)
    P["w_out"] = w_out[l].astype(BF16)
    P["w_router"] = w_router[l]
    P["b_router"] = row(b_router[l])
    P["w_gu"] = w_gu[l].astype(BF16)
    P["b_gu"] = b_gu[l].reshape(N_EXPERTS, 1, -1)
    P["w_down"] = w_down[l].astype(BF16)
    P["b_down"] = b_down[l].reshape(N_EXPERTS, 1, -1)
    P["g_pre_mix"], P["g_post_mix"] = row(g_pre_mix[l]), row(g_post_mix[l])
    P["g_pre_ffn"], P["g_post_ffn"] = row(g_pre_ffn[l]), row(g_post_ffn[l])
    return P


CFG_PROMPT = dict(tm=256, R=128, KB=256, tm_rw=256, hb=2, cb=2, tm_moe=1024)
CFG_SAMPLE = dict(tm=32, R=32, KB=256, tm_rw=128, hb=2, cb=1, tm_moe=512)


def kernel(x_prompt, x_sample, c_prompt, c_sample, cache_k, cache_v, cache_idx_k, state_rwkv, state_shift, w_ada, b_ada, g_pre_mix, g_post_mix, g_pre_ffn, g_post_ffn, w_in, ik_ln_w, ik_ln_b, mu_rwkv, w0, w_up, a0, a_up, g_up, k_k, k_a, r_k, ln_x_w, ln_x_b, w_o_att, w_o_rwkv, w_out, w_router, b_router, w_gu, b_gu, w_down, b_down):
    depth = w_in.shape[0]
    bp, tp_, D = x_prompt.shape
    bs, ts, _ = x_sample.shape
    y_p, y_s = x_prompt, x_sample
    st_p = [[] for _ in range(5)]
    st_s = [[] for _ in range(5)]
    nc_all = bp + bs
    npad = -(-nc_all // 8) * 8
    c_all = jnp.pad(jnp.concatenate([c_prompt, c_sample], axis=0), ((0, npad - nc_all), (0, 0)))
    for l in range(depth):
        P = _prep_weights(l, w_in, ik_ln_w, ik_ln_b, mu_rwkv, w0, w_up, a0, a_up, g_up, k_k, k_a, r_k, ln_x_w,
                          ln_x_b, w_o_att, w_o_rwkv, w_out, w_router, b_router, w_gu, b_gu, w_down, b_down,
                          g_pre_mix, g_post_mix, g_pre_ffn, g_post_ffn)
        ada = _ada(c_all, w_ada[l], b_ada[l])
        mods_p = [m[:bp, None, :] for m in jnp.split(ada, 6, axis=-1)]
        mods_s = [m[bp:nc_all, None, :] for m in jnp.split(ada, 6, axis=-1)]
        zero_state = jnp.zeros((bp, RWKV_HEADS, RWKV_HEAD, RWKV_HEAD), F32)
        zero_shift = jnp.zeros((bp, 1, RWKV_COLS), F32)
        outs_p = _layer(y_p, mods_p, None, zero_state, zero_shift, P, CFG_PROMPT)
        outs_s = _layer(y_s, mods_s, (cache_k[l], cache_v[l], cache_idx_k[l]), state_rwkv[l], state_shift[l], P,
                        CFG_SAMPLE)
        y_p, y_s = outs_p[0], outs_s[0]
        for lst, val in zip(st_p, outs_p[1:]):
            lst.append(val)
        for lst, val in zip(st_s, outs_s[1:]):
            lst.append(val)
    sp = [jnp.stack(v, axis=0) for v in st_p]
    ss = [jnp.stack(v, axis=0) for v in st_s]
    return (y_p, y_s, sp[0], sp[1], sp[2], sp[3], sp[4], ss[0], ss[1], ss[2], ss[3], ss[4])
```

```python
import functools

import numpy as np
import jax
import jax.numpy as jnp
from jax import lax
from jax.experimental import pallas as pl
from jax.experimental.pallas import tpu as pltpu

F32 = jnp.float32
BF16 = jnp.bfloat16
HI = lax.Precision.HIGHEST

CHUNK = 64
ATT_HEADS = 8
ATT_KV_HEADS = 2
HEAD_DIM = 64
IDX_HEADS = 8
IDX_DIM = 64
IDX_SCALE = (IDX_DIM ** -0.5) * (IDX_HEADS ** -0.5)
TOPK_MAX = 256
RWKV_HEADS = 8
RWKV_HEAD = 64
RWKV_WIDTH = RWKV_HEADS * RWKV_HEAD
DECAY_LORA = 64
AAA_LORA = 64
GATE_LORA = 128
RWKV_COLS = 3 * RWKV_WIDTH + DECAY_LORA + AAA_LORA + GATE_LORA
GN_EPS = 64e-5
L2_EPS = 1e-24
N_EXPERTS = 32
TOP_K = 4
SWIGLU_LIMIT = 7.0
SWIGLU_ALPHA = 1.702
NORM_EPS = 1e-6
LN_EPS = 1e-6

LANES = 128
VMEM_LIMIT = 56 * 1024 * 1024

HP = 8 * LANES
RW_CHUNK = 128
NEG_BIG = -1e30
KEY_NEG_INF = -2139095041
INT_MIN = -2147483648
RW_PASSES = (1, 1, 1)


def _params(sem):
    return pltpu.CompilerParams(dimension_semantics=sem, vmem_limit_bytes=VMEM_LIMIT)


def _nt(a, b, precision=None):
    return lax.dot_general(a, b, (((1,), (1,)), ((), ())), precision=precision,
                           preferred_element_type=F32)


def _mm(a, b, precision=None):
    return jnp.dot(a, b, precision=precision, preferred_element_type=F32)


def _split_bf16(x, terms):
    out = []
    for _ in range(terms):
        p = x.astype(BF16)
        out.append(p)
        x = x - p.astype(F32)
    return out


def _mmp(a, b, passes, nt=False):
    dot = _nt if nt else _mm
    if passes == 1:
        return dot(a.astype(BF16), b.astype(BF16))
    ah, al = _split_bf16(a, 2)
    bh, bl = _split_bf16(b, 2)
    return dot(ah, bh) + (dot(ah, bl) + dot(al, bh))


def _rms(x, g):
    return x * lax.rsqrt(jnp.mean(x * x, axis=-1, keepdims=True) + NORM_EPS) * g


def _headpad_idx(seg_off, lane_off_fn=lambda h: 0):
    idx = np.zeros((HP,), np.int32)
    ok = np.zeros((HP,), bool)
    for h in range(8):
        d0 = h * LANES + lane_off_fn(h)
        idx[d0:d0 + 64] = seg_off + h * 64 + np.arange(64)
        ok[d0:d0 + 64] = True
    return idx, ok


def _plain_idx(seg_off, n, width):
    idx = np.zeros((width,), np.int32)
    ok = np.zeros((width,), bool)
    idx[:n] = seg_off + np.arange(n)
    ok[:n] = True
    return idx, ok


def _cat(parts):
    return np.concatenate([p[0] for p in parts]), np.concatenate([p[1] for p in parts])


_O_Q, _O_K, _O_V, _O_IQ, _O_IK, _O_IW, _O_ZR, _O_GA, _O_GB = 0, 512, 640, 768, 1280, 1344, 1352, 3144, 4168
_COLS_A = _cat([_headpad_idx(_O_Q, lambda h: (h // 4) * 64), _plain_idx(_O_K, 128, 128), _plain_idx(_O_V, 128, 128),
                _headpad_idx(_O_IQ), _plain_idx(_O_IK, 64, 128), _plain_idx(_O_IW, 8, 128)])
_COLS_ZR = _cat([_headpad_idx(0), _headpad_idx(512), _headpad_idx(1024), _plain_idx(1536, 128, 128),
                 _plain_idx(1664, 128, 128)])
ZRW = _COLS_ZR[0].shape[0]
_COLS_B = _cat([(_COLS_ZR[0] + _O_ZR, _COLS_ZR[1]), _plain_idx(_O_GA, 1024, 1024), _plain_idx(_O_GB, 1024, 1024)])
NA = _COLS_A[0].shape[0]
NB = _COLS_B[0].shape[0]
_HEAD_IDX = _headpad_idx(0)
_ATT_ROW_IDX = _headpad_idx(0, lambda h: (h // 4) * 64)


def _take_cols(w, cols):
    idx, ok = cols
    return jnp.where(jnp.asarray(ok)[None, :], jnp.take(w, jnp.asarray(idx), axis=1), 0.0)


def _take_rows(w, cols):
    idx, ok = cols
    return jnp.where(jnp.asarray(ok)[:, None], jnp.take(w, jnp.asarray(idx), axis=0), 0.0)


def _mod_spec(m, tm):
    assert m.shape[1] == 1
    return pl.BlockSpec((1, 1, m.shape[2]), lambda b, i: (b, 0, 0))


def _ada_kernel(c_ref, w_ref, b_ref, o_ref):
    c = c_ref[...]
    s = c * (1.0 / (1.0 + jnp.exp(-c)))
    o_ref[...] = _mm(s, w_ref[...], HI) + b_ref[...]


def _ada(c, w, b):
    n, d = c.shape
    nout = w.shape[1]
    bn = 768
    return pl.pallas_call(
        _ada_kernel,
        grid=(nout // bn,),
        in_specs=[pl.BlockSpec((n, d), lambda j: (0, 0)),
                  pl.BlockSpec((d, bn), lambda j: (0, j)),
                  pl.BlockSpec((1, bn), lambda j: (0, j))],
        out_specs=pl.BlockSpec((n, bn), lambda j: (0, j)),
        out_shape=jax.ShapeDtypeStruct((n, nout), F32),
        compiler_params=_params(("arbitrary",)),
        name="ada",
    )(c, w, b.reshape(1, nout))


def _modulated(x_ref, sc_ref, sh_ref, g_ref):
    h = _rms(x_ref[0], g_ref[...])
    return (h * (1.0 + sc_ref[0]) + sh_ref[0]).astype(BF16)


def _proj_a_kernel(x_ref, sc_ref, sh_ref, g_ref, w_ref, lnw_ref, lnb_ref,
                   q_ref, k_ref, v_ref, kb_ref, vb_ref, iq_ref, ik_ref, ikb_ref, iw_ref):
    hb = _modulated(x_ref, sc_ref, sh_ref, g_ref)
    for h in range(8):
        zq = _mm(hb, w_ref[:, h * LANES:(h + 1) * LANES])
        q_ref[0, h] = (zq * (HEAD_DIM ** -0.5)).astype(BF16)
    kv = _mm(hb, w_ref[:, 1024:1280])
    k = kv[:, :128]
    v = kv[:, 128:]
    k_ref[0] = k
    v_ref[0] = v
    kb_ref[0] = k.astype(BF16)
    vb_ref[0] = v.astype(BF16)
    for h in range(8):
        zi = _mm(hb, w_ref[:, 1280 + h * LANES:1280 + (h + 1) * LANES])
        iq_ref[0, h] = zi.astype(BF16)
    t = _mm(hb, w_ref[:, 2304:2560])
    ik = t[:, :128]
    lane = lax.broadcasted_iota(jnp.int32, ik.shape, 1)
    valid = lane < IDX_DIM
    mu = jnp.sum(ik, axis=-1, keepdims=True) * (1.0 / IDX_DIM)
    d = jnp.where(valid, ik - mu, 0.0)
    var = jnp.sum(d * d, axis=-1, keepdims=True) * (1.0 / IDX_DIM)
    ikn = d * lax.rsqrt(var + LN_EPS) * lnw_ref[...] + lnb_ref[...]
    ik_ref[0] = ikn[:, :IDX_DIM]
    ikb_ref[0] = ikn.astype(BF16)
    iw_ref[0] = t[:, 128:136] * IDX_SCALE


def _proj_a(x, sc, sh, g, w, lnw, lnb, tm):
    B, T, D = x.shape
    nt = T // tm
    tok = lambda n, dt: jax.ShapeDtypeStruct((B, T, n), dt)
    hm = lambda dt: jax.ShapeDtypeStruct((B, 8, T, LANES), dt)
    tspec = lambda n: pl.BlockSpec((1, tm, n), lambda b, i: (b, i, 0))
    hspec = pl.BlockSpec((1, 8, tm, LANES), lambda b, i: (b, 0, i, 0))
    full = lambda a: pl.BlockSpec(a.shape, lambda b, i: (0,) * a.ndim)
    return pl.pallas_call(
        _proj_a_kernel,
        grid=(B, nt),
        in_specs=[tspec(D), _mod_spec(sc, tm), _mod_spec(sh, tm), full(g), full(w), full(lnw), full(lnb)],
        out_specs=[hspec, tspec(128), tspec(128), tspec(128), tspec(128), hspec, tspec(IDX_DIM), tspec(128),
                   tspec(8)],
        out_shape=[hm(BF16), tok(128, F32), tok(128, F32), tok(128, BF16), tok(128, BF16), hm(BF16),
                   tok(IDX_DIM, F32), tok(128, BF16), tok(8, F32)],
        compiler_params=_params(("arbitrary", "arbitrary")),
        name="proj_att",
    )(x, sc, sh, g, w, lnw, lnb)


def _proj_b_kernel(x_ref, sc_ref, sh_ref, g_ref, w_ref, zr_ref, ga_ref, gb_ref):
    hb = _modulated(x_ref, sc_ref, sh_ref, g_ref)
    for j in range(ZRW // 256):
        zr_ref[0, :, j * 256:(j + 1) * 256] = _mm(hb, w_ref[:, j * 256:(j + 1) * 256])
    for j in range(4):
        za = _mm(hb, w_ref[:, ZRW + j * 256:ZRW + (j + 1) * 256])
        ga_ref[0, :, j * 256:(j + 1) * 256] = 1.0 / (1.0 + jnp.exp(-za))
        zb = _mm(hb, w_ref[:, ZRW + 1024 + j * 256:ZRW + 1024 + (j + 1) * 256])
        gb_ref[0, :, j * 256:(j + 1) * 256] = 1.0 / (1.0 + jnp.exp(-zb))


def _proj_b(x, sc, sh, g, w, tm):
    B, T, D = x.shape
    nt = T // tm
    tok = lambda n: jax.ShapeDtypeStruct((B, T, n), F32)
    tspec = lambda n: pl.BlockSpec((1, tm, n), lambda b, i: (b, i, 0))
    full = lambda a: pl.BlockSpec(a.shape, lambda b, i: (0,) * a.ndim)
    return pl.pallas_call(
        _proj_b_kernel,
        grid=(B, nt),
        in_specs=[tspec(D), _mod_spec(sc, tm), _mod_spec(sh, tm), full(g), full(w)],
        out_specs=[tspec(ZRW), tspec(1024), tspec(1024)],
        out_shape=[tok(ZRW), tok(1024), tok(1024)],
        compiler_params=_params(("arbitrary", "arbitrary")),
        name="proj_rwkv",
    )(x, sc, sh, g, w)


def _sortable_to_float(t):
    bits = t ^ (lax.shift_right_arithmetic(t, 31) & 0x7FFFFFFF)
    return lax.bitcast_convert_type(bits, F32)


def _attn_kernel(iq_ref, iw_ref, q_ref, ik_ref, k_ref, v_ref, o_ref, s_ref, m_ref, acc_ref,
                 *, R, KB, SB, ltot, q_off, topk):
    j = pl.program_id(1)
    q0 = q_off + j * R
    pos = q0 + lax.broadcasted_iota(jnp.int32, (R, 1), 0)
    qchunk = lax.shift_right_logical(pos, 6)
    kend = jnp.minimum(ltot, (lax.shift_right_logical(q0 + R - 1, 6) + 1) * CHUNK)
    nkb = lax.shift_right_logical(kend + KB - 1, KB.bit_length() - 1)
    lane_k = lax.broadcasted_iota(jnp.int32, (R, KB), 1)
    nch = KB // LANES

    iq_all = iq_ref[0].reshape(8 * R, LANES)
    iw = iw_ref[0]
    iwb = [jnp.broadcast_to(iw[:, h:h + 1], (R, SB)) for h in range(8)]
    lane_s = lax.broadcasted_iota(jnp.int32, (R, SB), 1)
    nsb = KB // SB

    def score_body(kb, carry):
        for u in range(nsb):
            off = pl.multiple_of(kb * KB + u * SB, SB)
            ikb = ik_ref[0, pl.ds(off, SB), :]
            s_all = _nt(iq_all, ikb)
            acc = jnp.zeros((R, SB), F32)
            for h in range(8):
                acc = acc + jnp.maximum(s_all[h * R:(h + 1) * R], 0.0) * iwb[h]
            kidx = off + lane_s
            adm = (lax.shift_right_logical(kidx, 6) <= qchunk) & (kidx < ltot)
            s_ref[kb, :, u * SB:(u + 1) * SB] = jnp.where(adm, acc, -jnp.inf)
        return carry

    lax.fori_loop(0, nkb, score_body, 0)

    def count_ge(cand_f):
        def body(kb, part):
            blk = s_ref[kb]
            m = jnp.where(blk >= cand_f, 1.0, 0.0)
            for c in range(nch):
                part = part + m[:, c * LANES:(c + 1) * LANES]
            return part
        part = lax.fori_loop(0, nkb, body, jnp.zeros((R, LANES), F32))
        return jnp.sum(part, axis=1, keepdims=True)

    kf = float(topk)
    c0 = count_ge(jnp.zeros((R, 1), F32))
    t0 = jnp.where(c0 >= kf, 0, INT_MIN).astype(jnp.int32)

    def bit_body(i, t):
        cand = t + lax.shift_left(jnp.int32(1), 30 - i)
        cnt = count_ge(_sortable_to_float(cand))
        return jnp.where(cnt >= kf, cand, t)

    t = lax.fori_loop(0, 31, bit_body, t0)
    all_finite = t <= KEY_NEG_INF
    tau = jnp.where(all_finite, -jnp.inf, _sortable_to_float(jnp.maximum(t, KEY_NEG_INF)))

    def count_gt_eq(_):
        def body(kb, carry):
            pg, pe = carry
            blk = s_ref[kb]
            mg = jnp.where(blk > tau, 1.0, 0.0)
            me = jnp.where(blk == tau, 1.0, 0.0)
            for c in range(nch):
                pg = pg + mg[:, c * LANES:(c + 1) * LANES]
                pe = pe + me[:, c * LANES:(c + 1) * LANES]
            return pg, pe
        z = jnp.zeros((R, LANES), F32)
        pg, pe = lax.fori_loop(0, nkb, body, (z, z))
        return jnp.sum(pg, axis=1, keepdims=True), jnp.sum(pe, axis=1, keepdims=True)

    cnt_gt, cnt_eq = count_gt_eq(0)
    need = kf - cnt_gt
    tie = (cnt_eq > need) & jnp.logical_not(all_finite)
    any_tie = jnp.max(jnp.where(tie, 1.0, 0.0)) > 0.0

    def tie_bound():
        def count_eq_below(x):
            def body(kb, part):
                blk = s_ref[kb]
                kidx = kb * KB + lane_k
                m = jnp.where((blk == tau) & (kidx < x), 1.0, 0.0)
                for c in range(nch):
                    part = part + m[:, c * LANES:(c + 1) * LANES]
                return part
            part = lax.fori_loop(0, nkb, body, jnp.zeros((R, LANES), F32))
            return jnp.sum(part, axis=1, keepdims=True)

        nbits = int(ltot).bit_length()

        def body(i, x):
            cand = x + lax.shift_left(jnp.int32(1), nbits - 1 - i)
            ok = count_eq_below(cand) < need
            return jnp.where(ok, cand, x)
        x = lax.fori_loop(0, nbits, body, jnp.zeros((R, 1), jnp.int32))
        return x + 1

    big = jnp.full((R, 1), 1 << 30, jnp.int32)
    bound = lax.cond(any_tie, lambda: jnp.where(tie, tie_bound(), big), lambda: big)

    def bias_body(kb, carry):
        blk = s_ref[kb]
        kidx = kb * KB + lane_k
        sel = (blk > tau) | ((blk == tau) & (kidx < bound))
        sel = sel & (blk > -jnp.inf)
        s_ref[kb] = jnp.where(sel, 0.0, NEG_BIG)
        return carry

    lax.fori_loop(0, nkb, bias_body, 0)

    q_all = q_ref[0].reshape(8 * R, LANES)
    m_ref[...] = jnp.full(m_ref.shape, NEG_BIG, F32)
    acc_ref[...] = jnp.zeros(acc_ref.shape, F32)

    def att_body(kb, carry):
        for u in range(nsb):
            off = pl.multiple_of(kb * KB + u * SB, SB)
            kblk = k_ref[0, pl.ds(off, SB), :]
            vblk = v_ref[0, pl.ds(off, SB), :]
            bias = s_ref[kb, :, u * SB:(u + 1) * SB]
            logits = _nt(q_all, kblk)
            for h in range(8):
                lg = logits[h * R:(h + 1) * R] + bias
                cm = lg[:, :LANES]
                for c in range(1, SB // LANES):
                    cm = jnp.maximum(cm, lg[:, c * LANES:(c + 1) * LANES])
                m_old = m_ref[h]
                m_new = jnp.maximum(m_old, jnp.max(cm, axis=1, keepdims=True))
                alpha = jnp.exp(m_old - m_new)
                p = jnp.exp(lg - jnp.concatenate([m_new] * (SB // LANES), axis=1))
                acc_ref[h] = jnp.concatenate([alpha, alpha], axis=1) * acc_ref[h] + _mm(p.astype(BF16), vblk)
                m_ref[h] = m_new
        return carry

    lax.fori_loop(0, nkb, att_body, 0)
    for h in range(8):
        a = acc_ref[h]
        o_ref[0, h] = (a[:, :LANES] / a[:, LANES:]).astype(BF16)


def _attention(iq, iw, q, ikb, kb, vb1, *, R, KB, SB, ltot, q_off):
    B, _, Sq, _ = q.shape
    Lp = kb.shape[1]
    topk = min(TOPK_MAX, ltot // 4)
    assert Lp % KB == 0 and KB % SB == 0 and KB >= topk and Sq % R == 0
    hspec = pl.BlockSpec((1, 8, R, LANES), lambda b, i: (b, 0, i, 0))
    kspec = pl.BlockSpec((1, Lp, LANES), lambda b, i: (b, 0, 0))
    vspec = pl.BlockSpec((1, Lp, 2 * LANES), lambda b, i: (b, 0, 0))
    kern = functools.partial(_attn_kernel, R=R, KB=KB, SB=SB, ltot=ltot, q_off=q_off, topk=topk)
    return pl.pallas_call(
        kern,
        grid=(B, Sq // R),
        in_specs=[hspec, pl.BlockSpec((1, R, 8), lambda b, i: (b, i, 0)), hspec, kspec, kspec, vspec],
        out_specs=hspec,
        out_shape=jax.ShapeDtypeStruct((B, 8, Sq, LANES), BF16),
        scratch_shapes=[pltpu.VMEM((Lp // KB, R, KB), F32), pltpu.VMEM((8, R, LANES), F32),
                        pltpu.VMEM((8, R, 2 * LANES), F32)],
        compiler_params=_params(("arbitrary", "arbitrary")),
        name="dsa_attention",
    )(iq, iw, q, ikb, kb, vb1)


def _head_sum(x):
    parts = []
    for h in range(8):
        s = jnp.sum(x[:, h * LANES:(h + 1) * LANES], axis=1, keepdims=True)
        parts.append(jnp.broadcast_to(s, (x.shape[0], LANES)))
    return jnp.concatenate(parts, axis=1)


def _rwkv_prep_kernel(z_ref, sh0_ref, mu_ref, w0_ref, wup_ref, a0_ref, aup_ref, gup_ref, kk_ref, ka_ref, rk_ref,
                      r_o, lw_o, k_o, v_o, kkn_o, b_o, g_o, bg_o, lwt_o, kt_o, bt_o, carry_ref,
                      *, tm, t_valid):
    i = pl.program_id(1)

    @pl.when(i == 0)
    def _():
        carry_ref[...] = sh0_ref[0]

    z = z_ref[0]
    row = lax.broadcasted_iota(jnp.int32, (tm, 1), 0)
    prev = jnp.where(row == 0, carry_ref[...], pltpu.roll(z, 1, axis=0))
    carry_ref[...] = z[tm - 1:tm, :]
    zm = z + (prev - z) * mu_ref[...]
    live = i * tm + row < t_valid
    zm = jnp.where(live, zm, 0.0)
    r = zm[:, 0:HP]
    k = zm[:, HP:2 * HP]
    v = zm[:, 2 * HP:3 * HP]
    wa = zm[:, 3 * HP:3 * HP + LANES]
    gd = zm[:, 3 * HP + LANES:3 * HP + 2 * LANES]
    w_raw = w0_ref[...] + _mm(jnp.tanh(wa), wup_ref[...], HI)
    lw = (-float(np.exp(-0.5))) / (1.0 + jnp.exp(-w_raw))
    lw = jnp.where(live, lw, 0.0)
    a = 1.0 / (1.0 + jnp.exp(-(a0_ref[...] + _mm(wa, aup_ref[...], HI))))
    g = _mm(1.0 / (1.0 + jnp.exp(-gd)), gup_ref[...], HI)
    kk = k * kk_ref[...]
    kk = kk * lax.rsqrt(jnp.maximum(_head_sum(kk * kk), L2_EPS))
    k_mod = k * (1.0 + (a - 1.0) * ka_ref[...])
    b = kk * a
    bonus = _head_sum(r * k_mod * rk_ref[...]) * v
    r_o[0] = r
    lw_o[0] = lw
    k_o[0] = k_mod
    v_o[0] = v
    kkn_o[0] = kk
    b_o[0] = b
    g_o[0] = g
    bg_o[0] = bonus * g
    lwt_o[0] = lw.T
    kt_o[0] = k_mod.T
    bt_o[0] = b.T


def _rwkv_prep(zr, shift0, prm, tm, t_valid):
    B, T, _ = zr.shape
    tspec = pl.BlockSpec((1, tm, HP), lambda b, i: (b, i, 0))
    fspec = pl.BlockSpec((1, HP, tm), lambda b, i: (b, 0, i))
    full = lambda a: pl.BlockSpec(a.shape, lambda b, i: (0,) * a.ndim)
    tok = jax.ShapeDtypeStruct((B, T, HP), F32)
    feat = jax.ShapeDtypeStruct((B, HP, T), F32)
    names = ("mu", "w0", "w_up", "a0", "a_up", "g_up", "k_k", "k_a", "r_k")
    ws = [prm[n] for n in names]
    return pl.pallas_call(
        functools.partial(_rwkv_prep_kernel, tm=tm, t_valid=t_valid),
        grid=(B, T // tm),
        in_specs=[pl.BlockSpec((1, tm, ZRW), lambda b, i: (b, i, 0)),
                  pl.BlockSpec((1, 1, ZRW), lambda b, i: (b, 0, 0))] + [full(w) for w in ws],
        out_specs=[tspec] * 8 + [fspec] * 3,
        out_shape=[tok] * 8 + [feat] * 3,
        scratch_shapes=[pltpu.VMEM((1, ZRW), F32)],
        compiler_params=_params(("arbitrary", "arbitrary")),
        name="rwkv_prep",
    )(zr, shift0, *ws)


def _rwkv_chunk_kernel(r_ref, lw_ref, k_ref, v_ref, kk_ref, b_ref, lwt_ref, kt_ref, bt_ref, a1_ref, a2_ref,
                       *, C, hb, pg, pc, po):
    ri = lax.broadcasted_iota(jnp.int32, (C, C), 0)
    ci = lax.broadcasted_iota(jnp.int32, (C, C), 1)
    low_incl = jnp.where(ri >= ci, 1.0, 0.0).astype(BF16)
    up_incl = jnp.where(ri <= ci, 1.0, 0.0).astype(BF16)
    strict = ri > ci
    incl = ri >= ci
    eye64 = jnp.where(lax.broadcasted_iota(jnp.int32, (64, 64), 0) == lax.broadcasted_iota(jnp.int32, (64, 64), 1),
                      1.0, 0.0)
    sls = [slice(u * LANES, (u + 1) * LANES) for u in range(hb)]
    each = lambda f, *ls: [f(*a) for a in zip(*ls)]
    lw = [lw_ref[0, :, sl] for sl in sls]
    r = [r_ref[0, :, sl] for sl in sls]
    k = [k_ref[0, :, sl] for sl in sls]
    v = [v_ref[0, :, sl] for sl in sls]
    kk = [kk_ref[0, :, sl] for sl in sls]
    b = [b_ref[0, :, sl] for sl in sls]
    lwt = [lwt_ref[0, sl, :][:64] for sl in sls]
    kt = [kt_ref[0, sl, :][:64] for sl in sls]
    bt = [bt_ref[0, sl, :][:64] for sl in sls]
    cum = each(lambda x: sum(_mm(low_incl, p) for p in _split_bf16(x, 3)), lw)
    cumt = each(lambda x: sum(_mm(p, up_incl) for p in _split_bf16(x, 3)), lwt)
    e_neg = each(lambda c: jnp.exp(-c), cum)
    at = each(lambda kk_, c, l: -kk_ * jnp.exp(c - l), kk, cum, lw)
    rt = each(lambda r_, c: r_ * jnp.exp(c), r, cum)
    g = each(lambda a, r_, b_, k_, e: _mmp(jnp.concatenate([a, r_], axis=0),
                                           jnp.concatenate([b_ * e, k_ * e], axis=0), pg, nt=True),
             at, rt, b, k, e_neg)
    n = each(lambda g_: jnp.where(strict, g_[:C, :C], 0.0), g)
    aak = each(lambda g_: jnp.where(strict, g_[:C, C:], 0.0), g)
    lrb = each(lambda g_: jnp.where(incl, g_[C:, :C], 0.0), g)
    lrk = each(lambda g_: jnp.where(incl, g_[C:, C:], 0.0), g)
    blk8 = lax.shift_right_logical(ri, 3) == lax.shift_right_logical(ci, 3)
    eye = jnp.where(ri == ci, 1.0, 0.0)
    n0 = each(lambda n_: jnp.where(blk8, n_, 0.0), n)
    x = each(lambda n_: eye + n_, n0)
    p = each(lambda n_: _mmp(n_, n_, pc), n0)
    x = each(lambda x_, p_: x_ + _mmp(x_, p_, pc), x, p)
    p = each(lambda p_: _mmp(p_, p_, pc), p)
    x = each(lambda x_, p_: x_ + _mmp(x_, p_, pc), x, p)
    for lv in range(3, int(C).bit_length() - 1):
        off = (lax.shift_right_logical(ri, lv + 1) == lax.shift_right_logical(ci, lv + 1)) & (
            lax.shift_right_logical(ri, lv) != lax.shift_right_logical(ci, lv))
        t = each(lambda n_, x_: _mmp(jnp.where(off, n_, 0.0), x_, pc), n, x)
        x = each(lambda x_, t_: x_ + _mmp(x_, t_, pc), x, t)
    z = each(lambda x_, a, b_: _mmp(x_, jnp.concatenate([a, b_], axis=1), pc), x, aak, at)
    w2v = each(lambda z_, v_: _mmp(z_[:, :C], v_, po), z, v)
    cl = each(lambda c: c[:, C - 1:C], cumt)
    eb = each(lambda c_, ct: jnp.exp(c_ - ct), cl, cumt)
    lhs = each(lambda bt_, kt_, e, lb, lk: jnp.concatenate(
        [jnp.concatenate([bt_ * e, kt_ * e], axis=1), jnp.concatenate([lb, lk], axis=1)], axis=0),
        bt, kt, eb, lrb, lrk)
    left = each(lambda l_, z_: _mmp(l_[:, :C], z_[:, C:], po), lhs, z)
    right = each(lambda l_, w, v_: _mmp(l_, jnp.concatenate([w, v_], axis=0), po), lhs, w2v, v)
    for u in range(hb):
        diag = eye64 * jnp.exp(cl[u])
        a1_ref[0, u, 0] = jnp.concatenate([left[u][:64, :64] + diag, (left[u][64:] + rt[u])[:, :64]], axis=0)
        a2_ref[0, u, 0] = right[u]


def _rwkv_chunks(r, lw, k, v, kk, b, lwt, kt, bt, C, hb, passes=RW_PASSES):
    B, T, _ = r.shape
    nc = T // C
    tspec = pl.BlockSpec((1, C, hb * LANES), lambda bb, h, c: (bb, c, h))
    fspec = pl.BlockSpec((1, hb * LANES, C), lambda bb, h, c: (bb, h, c))
    pg, pc, po = passes
    return pl.pallas_call(
        functools.partial(_rwkv_chunk_kernel, C=C, hb=hb, pg=pg, pc=pc, po=po),
        grid=(B, 8 // hb, nc),
        in_specs=[tspec] * 6 + [fspec] * 3,
        out_specs=[pl.BlockSpec((1, hb, 1, 64 + C, 64), lambda bb, h, c: (bb, h, c, 0, 0)),
                   pl.BlockSpec((1, hb, 1, 64 + C, LANES), lambda bb, h, c: (bb, h, c, 0, 0))],
        out_shape=[jax.ShapeDtypeStruct((B, 8, nc, 64 + C, 64), F32),
                   jax.ShapeDtypeStruct((B, 8, nc, 64 + C, LANES), F32)],
        compiler_params=_params(("arbitrary", "arbitrary", "arbitrary")),
        name="rwkv_chunks",
    )(r, lw, k, v, kk, b, lwt, kt, bt)


def _rwkv_scan_kernel(a1_ref, a2_ref, s0_ref, y_ref, sT_ref, h_ref, *, C, cb):
    c = pl.program_id(1)

    @pl.when(c == 0)
    def _():
        h_ref[...] = s0_ref[0]

    for cc in range(cb):
        for h in range(8):
            res = _mm(a1_ref[0, h, cc], h_ref[h], HI) + a2_ref[0, h, cc]
            h_ref[h] = res[:64]
            y_ref[0, cc * C:(cc + 1) * C, h * LANES:(h + 1) * LANES] = res[64:]

    @pl.when(c == pl.num_programs(1) - 1)
    def _():
        sT_ref[0] = h_ref[...]


def _rwkv_scan(a1, a2, s0t, C, cb):
    B, _, nc, _, _ = a1.shape
    return pl.pallas_call(
        functools.partial(_rwkv_scan_kernel, C=C, cb=cb),
        grid=(B, nc // cb),
        in_specs=[pl.BlockSpec((1, 8, cb, 64 + C, 64), lambda b, c: (b, 0, c, 0, 0)),
                  pl.BlockSpec((1, 8, cb, 64 + C, LANES), lambda b, c: (b, 0, c, 0, 0)),
                  pl.BlockSpec((1, 8, 64, LANES), lambda b, c: (b, 0, 0, 0))],
        out_specs=[pl.BlockSpec((1, cb * C, HP), lambda b, c: (b, c, 0)),
                   pl.BlockSpec((1, 8, 64, LANES), lambda b, c: (b, 0, 0, 0))],
        out_shape=[jax.ShapeDtypeStruct((B, nc * C, HP), F32),
                   jax.ShapeDtypeStruct((B, 8, 64, LANES), F32)],
        scratch_shapes=[pltpu.VMEM((8, 64, LANES), F32)],
        compiler_params=_params(("arbitrary", "arbitrary")),
        name="rwkv_scan",
    )(a1, a2, s0t)


def _mix_kernel(att_ref, y_ref, g_ref, bg_ref, sga_ref, sgb_ref, x_ref, gt1_ref, sc2_ref, sh2_ref,
                woa_ref, wor_ref, wout_ref, lnw_ref, lnb_ref, gpost_ref, gpre_ref, wr_ref, br_ref,
                x1_ref, h2_ref, gate_ref, *, tm):
    att = jnp.concatenate([att_ref[0, h] for h in range(8)], axis=1)
    lane = lax.broadcasted_iota(jnp.int32, (tm, LANES), 1)
    valid = lane < RWKV_HEAD
    parts = []
    for h in range(8):
        y = y_ref[0, :, h * LANES:(h + 1) * LANES]
        mu = jnp.sum(y, axis=1, keepdims=True) * (1.0 / RWKV_HEAD)
        d = jnp.where(valid, y - mu, 0.0)
        var = jnp.sum(d * d, axis=1, keepdims=True) * (1.0 / RWKV_HEAD)
        parts.append(d * lax.rsqrt(var + GN_EPS))
    yn = jnp.concatenate(parts, axis=1)
    rw = (yn * lnw_ref[...] + lnb_ref[...]) * g_ref[0] + bg_ref[0]
    mix = sga_ref[0] * _mm(att, woa_ref[...]) + sgb_ref[0] * _mm(rw.astype(BF16), wor_ref[...])
    o = _mm(mix.astype(BF16), wout_ref[...])
    x1 = x_ref[0] + gt1_ref[0] * _rms(o, gpost_ref[...])
    x1_ref[0] = x1
    h2 = _rms(x1, gpre_ref[...]) * (1.0 + sc2_ref[0]) + sh2_ref[0]
    h2_ref[0] = h2.astype(BF16)
    logits = _mm(h2, wr_ref[...], HI) + br_ref[...]
    el = lax.broadcasted_iota(jnp.int32, logits.shape, 1).astype(F32)
    work = logits
    sel = jnp.zeros(logits.shape, jnp.bool_)
    vmax = None
    for kk in range(TOP_K):
        mx = jnp.max(work, axis=1, keepdims=True)
        if kk == 0:
            vmax = mx
        first = jnp.min(jnp.where(work == mx, el, float(N_EXPERTS)), axis=1, keepdims=True)
        hit = el == first
        sel = sel | hit
        work = jnp.where(hit, -jnp.inf, work)
    e = jnp.where(sel, jnp.exp(logits - vmax), 0.0)
    gate_ref[0] = e / jnp.sum(e, axis=1, keepdims=True)


def _mix(att, y, g, bg, sga, sgb, x, gt1, sc2, sh2, wts, tm):
    B, T, D = x.shape
    tspec = lambda n: pl.BlockSpec((1, tm, n), lambda b, i: (b, i, 0))
    full = lambda a: pl.BlockSpec(a.shape, lambda b, i: (0,) * a.ndim)
    tok = lambda n, dt: jax.ShapeDtypeStruct((B, T, n), dt)
    return pl.pallas_call(
        functools.partial(_mix_kernel, tm=tm),
        grid=(B, T // tm),
        in_specs=[pl.BlockSpec((1, 8, tm, LANES), lambda b, i: (b, 0, i, 0)), tspec(HP), tspec(HP), tspec(HP),
                  tspec(D), tspec(D), tspec(D), _mod_spec(gt1, tm), _mod_spec(sc2, tm), _mod_spec(sh2, tm)]
                 + [full(w) for w in wts],
        out_specs=[tspec(D), tspec(D), tspec(N_EXPERTS)],
        out_shape=[tok(D, F32), tok(D, BF16), tok(N_EXPERTS, F32)],
        compiler_params=_params(("arbitrary", "arbitrary")),
        name="mix_router",
    )(att, y, g, bg, sga, sgb, x, gt1, sc2, sh2, *wts)


def _moe_kernel(h_ref, gate_ref, x1_ref, gt2_ref, gpost_ref, wgu_ref, bgu_ref, wd_ref, bd_ref, o_ref, acc_ref,
                *, d_ff):
    e = pl.program_id(1)

    @pl.when(e == 0)
    def _():
        acc_ref[...] = jnp.zeros(acc_ref.shape, F32)

    hgu = _mm(h_ref[...], wgu_ref[0]) + bgu_ref[0]
    hg = jnp.minimum(hgu[:, :d_ff], SWIGLU_LIMIT)
    hl = jnp.clip(hgu[:, d_ff:], -SWIGLU_LIMIT, SWIGLU_LIMIT)
    act = hg * (1.0 / (1.0 + jnp.exp(-SWIGLU_ALPHA * hg))) * (hl + 1.0)
    contrib = _mm(act.astype(BF16), wd_ref[0]) + bd_ref[0]
    gate = gate_ref[...]
    el = lax.broadcasted_iota(jnp.int32, gate.shape, 1)
    ge = jnp.sum(jnp.where(el == e, gate, 0.0), axis=1, keepdims=True)
    acc_ref[...] += ge * contrib

    @pl.when(e == pl.num_programs(1) - 1)
    def _():
        o_ref[...] = x1_ref[...] + gt2_ref[0] * _rms(acc_ref[...], gpost_ref[...])


def _moe(h2, gate, x1, gt2, gpost, wgu, bgu, wd, bd, tm):
    N, D = h2.shape
    E, _, F2 = wgu.shape
    tspec = lambda n: pl.BlockSpec((tm, n), lambda i, e: (i, 0))
    if gt2.shape[1] == 1:
        tpb = N // gt2.shape[0]
        gspec = pl.BlockSpec((1, 1, D), lambda i, e: ((i * tm) // tpb, 0, 0))
    else:
        gspec = pl.BlockSpec((1, tm, D), lambda i, e: (i, 0, 0))
    return pl.pallas_call(
        functools.partial(_moe_kernel, d_ff=F2 // 2),
        grid=(N // tm, E),
        in_specs=[tspec(D), tspec(E), tspec(D), gspec, pl.BlockSpec((1, D), lambda i, e: (0, 0)),
                  pl.BlockSpec((1, D, F2), lambda i, e: (e, 0, 0)), pl.BlockSpec((1, 1, F2), lambda i, e: (e, 0, 0)),
                  pl.BlockSpec((1, F2 // 2, D), lambda i, e: (e, 0, 0)), pl.BlockSpec((1, 1, D), lambda i, e: (e, 0, 0))],
        out_specs=tspec(D),
        out_shape=jax.ShapeDtypeStruct((N, D), F32),
        scratch_shapes=[pltpu.VMEM((tm, D), F32)],
        compiler_params=_params(("arbitrary", "arbitrary")),
        name="moe",
    )(h2, gate, x1, gt2, gpost, wgu, bgu, wd, bd)


def _per_token(m, T):
    B, _, D = m.shape
    return jnp.broadcast_to(m, (B, T, D)).reshape(1, B * T, D)


def _layer(x, mods, past, s0, shift0, P, cfg):
    B, T, D = x.shape
    sh1, sc1, gt1, sh2, sc2, gt2 = mods
    tm = cfg["tm"]

    q, k, v, kb, vb, iq, ik, ikb, iw = _proj_a(x, sc1, sh1, P["g_pre_mix"], P["w_a"], P["ik_ln_w"], P["ik_ln_b"], tm)
    zr, sga, sgb = _proj_b(x, sc1, sh1, P["g_pre_mix"], P["w_b"], tm)

    if past is not None:
        pk, pv, pik = past
        plen = pk.shape[1]
        kb_all = jnp.concatenate([pk.reshape(B, plen, 128).astype(BF16), kb], axis=1)
        vb_all = jnp.concatenate([pv.reshape(B, plen, 128).astype(BF16), vb], axis=1)
        ik_all = jnp.concatenate([jnp.pad(pik, ((0, 0), (0, 0), (0, 64))).astype(BF16), ikb], axis=1)
    else:
        plen = 0
        kb_all, vb_all, ik_all = kb, vb, ikb
    ltot = plen + T
    KB = cfg["KB"]
    lp = -(-ltot // KB) * KB
    if lp != ltot:
        padk = ((0, 0), (0, lp - ltot), (0, 0))
        kb_all, vb_all, ik_all = jnp.pad(kb_all, padk), jnp.pad(vb_all, padk), jnp.pad(ik_all, padk)
    vb1 = jnp.concatenate([vb_all, jnp.ones_like(vb_all)], axis=-1)
    att = _attention(iq, iw, q, ik_all, kb_all, vb1, R=cfg["R"], KB=KB, SB=cfg["SB"], ltot=ltot, q_off=plen)

    C = RW_CHUNK
    tp = -(-T // C) * C
    zr_p = zr if tp == T else jnp.pad(zr, ((0, 0), (0, tp - T), (0, 0)))
    shift_pad = _take_cols(shift0.reshape(B, RWKV_COLS), _COLS_ZR).reshape(B, 1, ZRW)
    r, lw, km, vv, kkn, bb, g, bg, lwt, kt, bt = _rwkv_prep(zr_p, shift_pad, P, min(cfg["tm_rw"], tp), T)
    a1, a2 = _rwkv_chunks(r, lw, km, vv, kkn, bb, lwt, kt, bt, C, cfg["hb"])
    s0t = jnp.pad(jnp.swapaxes(s0, 2, 3), ((0, 0), (0, 0), (0, 0), (0, 64)))
    nc = tp // C
    cb = min(cfg["cb"], nc)
    y, sT = _rwkv_scan(a1, a2, s0t, C, cb)
    s_new = jnp.swapaxes(sT[..., :64], 2, 3)
    zlast = zr[:, T - 1]
    inv = np.zeros((RWKV_COLS,), np.int32)
    inv[_COLS_ZR[0][_COLS_ZR[1]]] = np.nonzero(_COLS_ZR[1])[0]
    shift_new = jnp.take(zlast, jnp.asarray(inv), axis=1).reshape(B, 1, RWKV_COLS)
    if tp != T:
        y, g, bg = y[:, :T], g[:, :T], bg[:, :T]

    wts = [P[n] for n in ("w_o_att", "w_o_rwkv", "w_out", "ln_x_w", "ln_x_b", "g_post_mix", "g_pre_ffn",
                          "w_router", "b_router")]
    x1, h2, gate = _mix(att, y, g, bg, sga, sgb, x, gt1, sc2, sh2, wts, tm)

    N = B * T
    tmm = min(cfg["tm_moe"], N)
    if T % tmm == 0:
        gt2m = gt2
    else:
        gt2m = _per_token(gt2, T).reshape(N // tmm, tmm, D)
    out = _moe(h2.reshape(N, D), gate.reshape(N, N_EXPERTS), x1.reshape(N, D), gt2m, P["g_post_ffn"],
               P["w_gu"], P["b_gu"], P["w_down"], P["b_down"], tmm)
    return (out.reshape(B, T, D), k.reshape(B, T, ATT_KV_HEADS, HEAD_DIM), v.reshape(B, T, ATT_KV_HEADS, HEAD_DIM),
            ik, s_new, shift_new)


def _prep_weights(l, w_in, ik_ln_w, ik_ln_b, mu_rwkv, w0, w_up, a0, a_up, g_up, k_k, k_a, r_k, ln_x_w, ln_x_b,
                  w_o_att, w_o_rwkv, w_out, w_router, b_router, w_gu, b_gu, w_down, b_down,
                  g_pre_mix, g_post_mix, g_pre_ffn, g_post_ffn):
    row = lambda a: a.reshape(1, -1)
    hp = lambda a: _take_cols(row(a), _HEAD_IDX)
    P = {}
    P["w_a"] = _take_cols(w_in[l], _COLS_A).astype(BF16)
    P["w_b"] = _take_cols(w_in[l], _COLS_B).astype(BF16)
    P["ik_ln_w"] = jnp.pad(row(ik_ln_w[l]), ((0, 0), (0, 64)))
    P["ik_ln_b"] = jnp.pad(row(ik_ln_b[l]), ((0, 0), (0, 64)))
    P["mu"] = _take_cols(row(mu_rwkv[l]), _COLS_ZR)
    P["w0"], P["a0"], P["k_k"], P["k_a"] = hp(w0[l]), hp(a0[l]), hp(k_k[l]), hp(k_a[l])
    P["r_k"] = hp(r_k[l].reshape(-1))
    P["ln_x_w"], P["ln_x_b"] = hp(ln_x_w[l]), hp(ln_x_b[l])
    z64 = jnp.zeros((64, HP), F32)
    P["w_up"] = jnp.concatenate([_take_cols(w_up[l], _HEAD_IDX), z64], axis=0)
    P["a_up"] = jnp.concatenate([z64, _take_cols(a_up[l], _HEAD_IDX)], axis=0)
    P["g_up"] = _take_cols(g_up[l], _HEAD_IDX)
    P["w_o_att"] = _take_rows(w_o_att[l], _ATT_ROW_IDX).astype(BF16)
    P["w_o_rwkv"] = _take_rows(w_o_rwkv[l], _HEAD_IDX).astype(BF16)
    P["w_out"] = w_out[l].astype(BF16)
    P["w_router"] = w_router[l]
    P["b_router"] = row(b_router[l])
    P["w_gu"] = w_gu[l].astype(BF16)
    P["b_gu"] = b_gu[l].reshape(N_EXPERTS, 1, -1)
    P["w_down"] = w_down[l].astype(BF16)
    P["b_down"] = b_down[l].reshape(N_EXPERTS, 1, -1)
    P["g_pre_mix"], P["g_post_mix"] = row(g_pre_mix[l]), row(g_post_mix[l])
    P["g_pre_ffn"], P["g_post_ffn"] = row(g_pre_ffn[l]), row(g_post_ffn[l])
    return P


CFG_PROMPT = dict(tm=256, R=128, KB=512, SB=256, tm_rw=256, hb=8, cb=2, tm_moe=1024)
CFG_SAMPLE = dict(tm=32, R=32, KB=512, SB=256, tm_rw=128, hb=8, cb=1, tm_moe=512)


def kernel(x_prompt, x_sample, c_prompt, c_sample, cache_k, cache_v, cache_idx_k, state_rwkv, state_shift, w_ada, b_ada, g_pre_mix, g_post_mix, g_pre_ffn, g_post_ffn, w_in, ik_ln_w, ik_ln_b, mu_rwkv, w0, w_up, a0, a_up, g_up, k_k, k_a, r_k, ln_x_w, ln_x_b, w_o_att, w_o_rwkv, w_out, w_router, b_router, w_gu, b_gu, w_down, b_down):
    depth = w_in.shape[0]
    bp, tp_, D = x_prompt.shape
    bs, ts, _ = x_sample.shape
    y_p, y_s = x_prompt, x_sample
    st_p = [[] for _ in range(5)]
    st_s = [[] for _ in range(5)]
    nc_all = bp + bs
    npad = -(-nc_all // 8) * 8
    c_all = jnp.pad(jnp.concatenate([c_prompt, c_sample], axis=0), ((0, npad - nc_all), (0, 0)))
    for l in range(depth):
        P = _prep_weights(l, w_in, ik_ln_w, ik_ln_b, mu_rwkv, w0, w_up, a0, a_up, g_up, k_k, k_a, r_k, ln_x_w,
                          ln_x_b, w_o_att, w_o_rwkv, w_out, w_router, b_router, w_gu, b_gu, w_down, b_down,
                          g_pre_mix, g_post_mix, g_pre_ffn, g_post_ffn)
        ada = _ada(c_all, w_ada[l], b_ada[l])
        mods_p = [m[:bp, None, :] for m in jnp.split(ada, 6, axis=-1)]
        mods_s = [m[bp:nc_all, None, :] for m in jnp.split(ada, 6, axis=-1)]
        zero_state = jnp.zeros((bp, RWKV_HEADS, RWKV_HEAD, RWKV_HEAD), F32)
        zero_shift = jnp.zeros((bp, 1, RWKV_COLS), F32)
        outs_p = _layer(y_p, mods_p, None, zero_state, zero_shift, P, CFG_PROMPT)
        outs_s = _layer(y_s, mods_s, (cache_k[l], cache_v[l], cache_idx_k[l]), state_rwkv[l], state_shift[l], P,
                        CFG_SAMPLE)
        y_p, y_s = outs_p[0], outs_s[0]
        for lst, val in zip(st_p, outs_p[1:]):
            lst.append(val)
        for lst, val in zip(st_s, outs_s[1:]):
            lst.append(val)
    sp = [jnp.stack(v, axis=0) for v in st_p]
    ss = [jnp.stack(v, axis=0) for v in st_s]
    return (y_p, y_s, sp[0], sp[1], sp[2], sp[3], sp[4], ss[0], ss[1], ss[2], ss[3], ss[4])
```

```python
import functools

import numpy as np
import jax
import jax.numpy as jnp
from jax import lax
from jax.experimental import pallas as pl
from jax.experimental.pallas import tpu as pltpu
from jax.experimental.pallas import tpu_sc as plsc

F32 = jnp.float32
BF16 = jnp.bfloat16
HI = lax.Precision.HIGHEST

CHUNK = 64
ATT_HEADS = 8
ATT_KV_HEADS = 2
HEAD_DIM = 64
IDX_HEADS = 8
IDX_DIM = 64
IDX_SCALE = (IDX_DIM ** -0.5) * (IDX_HEADS ** -0.5)
TOPK_MAX = 256
RWKV_HEADS = 8
RWKV_HEAD = 64
RWKV_WIDTH = RWKV_HEADS * RWKV_HEAD
DECAY_LORA = 64
AAA_LORA = 64
GATE_LORA = 128
RWKV_COLS = 3 * RWKV_WIDTH + DECAY_LORA + AAA_LORA + GATE_LORA
GN_EPS = 64e-5
L2_EPS = 1e-24
N_EXPERTS = 32
TOP_K = 4
SWIGLU_LIMIT = 7.0
SWIGLU_ALPHA = 1.702
NORM_EPS = 1e-6
LN_EPS = 1e-6

LANES = 128
VMEM_LIMIT = 56 * 1024 * 1024

HP = 8 * LANES
RW_CHUNK = 128
NEG_BIG = -1e30
KEY_NEG_INF = -2139095041
INT_MIN = -2147483648
SC_WINDOW = 128
SC_SPLIT = 4
RW_PASSES = (1, 1, 1)


def _params(sem):
    return pltpu.CompilerParams(dimension_semantics=sem, vmem_limit_bytes=VMEM_LIMIT)


def _nt(a, b, precision=None):
    return lax.dot_general(a, b, (((1,), (1,)), ((), ())), precision=precision,
                           preferred_element_type=F32)


def _mm(a, b, precision=None):
    return jnp.dot(a, b, precision=precision, preferred_element_type=F32)


def _split_bf16(x, terms):
    out = []
    for _ in range(terms):
        p = x.astype(BF16)
        out.append(p)
        x = x - p.astype(F32)
    return out


def _mmp(a, b, passes, nt=False):
    dot = _nt if nt else _mm
    if passes == 1:
        return dot(a.astype(BF16), b.astype(BF16))
    ah, al = _split_bf16(a, 2)
    bh, bl = _split_bf16(b, 2)
    return dot(ah, bh) + (dot(ah, bl) + dot(al, bh))


def _rms(x, g):
    return x * lax.rsqrt(jnp.mean(x * x, axis=-1, keepdims=True) + NORM_EPS) * g


def _headpad_idx(seg_off, lane_off_fn=lambda h: 0):
    idx = np.zeros((HP,), np.int32)
    ok = np.zeros((HP,), bool)
    for h in range(8):
        d0 = h * LANES + lane_off_fn(h)
        idx[d0:d0 + 64] = seg_off + h * 64 + np.arange(64)
        ok[d0:d0 + 64] = True
    return idx, ok


def _plain_idx(seg_off, n, width):
    idx = np.zeros((width,), np.int32)
    ok = np.zeros((width,), bool)
    idx[:n] = seg_off + np.arange(n)
    ok[:n] = True
    return idx, ok


def _cat(parts):
    return np.concatenate([p[0] for p in parts]), np.concatenate([p[1] for p in parts])


_O_Q, _O_K, _O_V, _O_IQ, _O_IK, _O_IW, _O_ZR, _O_GA, _O_GB = 0, 512, 640, 768, 1280, 1344, 1352, 3144, 4168
_COLS_A = _cat([_headpad_idx(_O_Q, lambda h: (h // 4) * 64), _plain_idx(_O_K, 128, 128), _plain_idx(_O_V, 128, 128),
                _headpad_idx(_O_IQ), _plain_idx(_O_IK, 64, 128), _plain_idx(_O_IW, 8, 128)])
_COLS_ZR = _cat([_headpad_idx(0), _headpad_idx(512), _headpad_idx(1024), _plain_idx(1536, 128, 128),
                 _plain_idx(1664, 128, 128)])
ZRW = _COLS_ZR[0].shape[0]
_COLS_B = _cat([(_COLS_ZR[0] + _O_ZR, _COLS_ZR[1]), _plain_idx(_O_GA, 1024, 1024), _plain_idx(_O_GB, 1024, 1024)])
NA = _COLS_A[0].shape[0]
NB = _COLS_B[0].shape[0]
_HEAD_IDX = _headpad_idx(0)
_ATT_ROW_IDX = _headpad_idx(0, lambda h: (h // 4) * 64)


def _take_cols(w, cols):
    idx, ok = cols
    return jnp.where(jnp.asarray(ok)[None, :], jnp.take(w, jnp.asarray(idx), axis=1), 0.0)


def _take_rows(w, cols):
    idx, ok = cols
    return jnp.where(jnp.asarray(ok)[:, None], jnp.take(w, jnp.asarray(idx), axis=0), 0.0)


def _mod_spec(m, tm):
    assert m.shape[1] == 1
    return pl.BlockSpec((1, 1, m.shape[2]), lambda b, i: (b, 0, 0))


def _ada_kernel(c_ref, w_ref, b_ref, o_ref):
    c = c_ref[...]
    s = c * (1.0 / (1.0 + jnp.exp(-c)))
    o_ref[...] = _mm(s, w_ref[...], HI) + b_ref[...]


def _ada(c, w, b):
    n, d = c.shape
    nout = w.shape[1]
    bn = 768
    return pl.pallas_call(
        _ada_kernel,
        grid=(nout // bn,),
        in_specs=[pl.BlockSpec((n, d), lambda j: (0, 0)),
                  pl.BlockSpec((d, bn), lambda j: (0, j)),
                  pl.BlockSpec((1, bn), lambda j: (0, j))],
        out_specs=pl.BlockSpec((n, bn), lambda j: (0, j)),
        out_shape=jax.ShapeDtypeStruct((n, nout), F32),
        compiler_params=_params(("arbitrary",)),
        name="ada",
    )(c, w, b.reshape(1, nout))


def _modulated(x_ref, sc_ref, sh_ref, g_ref):
    h = _rms(x_ref[0], g_ref[...])
    return (h * (1.0 + sc_ref[0]) + sh_ref[0]).astype(BF16)


def _proj_a_kernel(x_ref, sc_ref, sh_ref, g_ref, w_ref, lnw_ref, lnb_ref,
                   q_ref, k_ref, v_ref, kb_ref, vb_ref, iq_ref, ik_ref, ikb_ref, iw_ref):
    hb = _modulated(x_ref, sc_ref, sh_ref, g_ref)
    for h in range(8):
        zq = _mm(hb, w_ref[:, h * LANES:(h + 1) * LANES])
        q_ref[0, h] = (zq * (HEAD_DIM ** -0.5)).astype(BF16)
    kv = _mm(hb, w_ref[:, 1024:1280])
    k = kv[:, :128]
    v = kv[:, 128:]
    k_ref[0] = k
    v_ref[0] = v
    kb_ref[0] = k.astype(BF16)
    vb_ref[0] = v.astype(BF16)
    for h in range(8):
        zi = _mm(hb, w_ref[:, 1280 + h * LANES:1280 + (h + 1) * LANES])
        iq_ref[0, h] = zi.astype(BF16)
    t = _mm(hb, w_ref[:, 2304:2560])
    ik = t[:, :128]
    lane = lax.broadcasted_iota(jnp.int32, ik.shape, 1)
    valid = lane < IDX_DIM
    mu = jnp.sum(ik, axis=-1, keepdims=True) * (1.0 / IDX_DIM)
    d = jnp.where(valid, ik - mu, 0.0)
    var = jnp.sum(d * d, axis=-1, keepdims=True) * (1.0 / IDX_DIM)
    ikn = d * lax.rsqrt(var + LN_EPS) * lnw_ref[...] + lnb_ref[...]
    ik_ref[0] = ikn[:, :IDX_DIM]
    ikb_ref[0] = ikn.astype(BF16)
    iw_ref[0] = t[:, 128:136] * IDX_SCALE


def _proj_a(x, sc, sh, g, w, lnw, lnb, tm):
    B, T, D = x.shape
    nt = T // tm
    tok = lambda n, dt: jax.ShapeDtypeStruct((B, T, n), dt)
    hm = lambda dt: jax.ShapeDtypeStruct((B, 8, T, LANES), dt)
    tspec = lambda n: pl.BlockSpec((1, tm, n), lambda b, i: (b, i, 0))
    hspec = pl.BlockSpec((1, 8, tm, LANES), lambda b, i: (b, 0, i, 0))
    full = lambda a: pl.BlockSpec(a.shape, lambda b, i: (0,) * a.ndim)
    return pl.pallas_call(
        _proj_a_kernel,
        grid=(B, nt),
        in_specs=[tspec(D), _mod_spec(sc, tm), _mod_spec(sh, tm), full(g), full(w), full(lnw), full(lnb)],
        out_specs=[hspec, tspec(128), tspec(128), tspec(128), tspec(128), hspec, tspec(IDX_DIM), tspec(128),
                   tspec(8)],
        out_shape=[hm(BF16), tok(128, F32), tok(128, F32), tok(128, BF16), tok(128, BF16), hm(BF16),
                   tok(IDX_DIM, F32), tok(128, BF16), tok(8, F32)],
        compiler_params=_params(("arbitrary", "arbitrary")),
        name="proj_att",
    )(x, sc, sh, g, w, lnw, lnb)


def _proj_b_kernel(x_ref, sc_ref, sh_ref, g_ref, w_ref, zr_ref, ga_ref, gb_ref):
    hb = _modulated(x_ref, sc_ref, sh_ref, g_ref)
    for j in range(ZRW // 256):
        zr_ref[0, :, j * 256:(j + 1) * 256] = _mm(hb, w_ref[:, j * 256:(j + 1) * 256])
    for j in range(4):
        za = _mm(hb, w_ref[:, ZRW + j * 256:ZRW + (j + 1) * 256])
        ga_ref[0, :, j * 256:(j + 1) * 256] = 1.0 / (1.0 + jnp.exp(-za))
        zb = _mm(hb, w_ref[:, ZRW + 1024 + j * 256:ZRW + 1024 + (j + 1) * 256])
        gb_ref[0, :, j * 256:(j + 1) * 256] = 1.0 / (1.0 + jnp.exp(-zb))


def _proj_b(x, sc, sh, g, w, tm):
    B, T, D = x.shape
    nt = T // tm
    tok = lambda n: jax.ShapeDtypeStruct((B, T, n), F32)
    tspec = lambda n: pl.BlockSpec((1, tm, n), lambda b, i: (b, i, 0))
    full = lambda a: pl.BlockSpec(a.shape, lambda b, i: (0,) * a.ndim)
    return pl.pallas_call(
        _proj_b_kernel,
        grid=(B, nt),
        in_specs=[tspec(D), _mod_spec(sc, tm), _mod_spec(sh, tm), full(g), full(w)],
        out_specs=[tspec(ZRW), tspec(1024), tspec(1024)],
        out_shape=[tok(ZRW), tok(1024), tok(1024)],
        compiler_params=_params(("arbitrary", "arbitrary")),
        name="proj_rwkv",
    )(x, sc, sh, g, w)


def _sortable_to_float(t):
    bits = t ^ (lax.shift_right_arithmetic(t, 31) & 0x7FFFFFFF)
    return lax.bitcast_convert_type(bits, F32)


def _attn_kernel(iq_ref, iw_ref, q_ref, ik_ref, k_ref, v_ref, o_ref, s_ref, m_ref, acc_ref,
                 *, R, KB, SB, ltot, q_off, topk):
    j = pl.program_id(1)
    q0 = q_off + j * R
    pos = q0 + lax.broadcasted_iota(jnp.int32, (R, 1), 0)
    qchunk = lax.shift_right_logical(pos, 6)
    kend = jnp.minimum(ltot, (lax.shift_right_logical(q0 + R - 1, 6) + 1) * CHUNK)
    nkb = lax.shift_right_logical(kend + KB - 1, KB.bit_length() - 1)
    lane_k = lax.broadcasted_iota(jnp.int32, (R, KB), 1)
    nch = KB // LANES

    iq_all = iq_ref[0].reshape(8 * R, LANES)
    iw = iw_ref[0]
    iwb = [jnp.broadcast_to(iw[:, h:h + 1], (R, SB)) for h in range(8)]
    lane_s = lax.broadcasted_iota(jnp.int32, (R, SB), 1)
    nsb = KB // SB

    def score_body(kb, carry):
        for u in range(nsb):
            off = pl.multiple_of(kb * KB + u * SB, SB)
            ikb = ik_ref[0, pl.ds(off, SB), :]
            s_all = _nt(iq_all, ikb)
            acc = jnp.zeros((R, SB), F32)
            for h in range(8):
                acc = acc + jnp.maximum(s_all[h * R:(h + 1) * R], 0.0) * iwb[h]
            kidx = off + lane_s
            adm = (lax.shift_right_logical(kidx, 6) <= qchunk) & (kidx < ltot)
            s_ref[kb, :, u * SB:(u + 1) * SB] = jnp.where(adm, acc, -jnp.inf)
        return carry

    lax.fori_loop(0, nkb, score_body, 0)

    def count_ge(cand_f):
        def body(kb, part):
            blk = s_ref[kb]
            m = jnp.where(blk >= cand_f, 1.0, 0.0)
            for c in range(nch):
                part = part + m[:, c * LANES:(c + 1) * LANES]
            return part
        part = lax.fori_loop(0, nkb, body, jnp.zeros((R, LANES), F32))
        return jnp.sum(part, axis=1, keepdims=True)

    kf = float(topk)
    c0 = count_ge(jnp.zeros((R, 1), F32))
    t0 = jnp.where(c0 >= kf, 0, INT_MIN).astype(jnp.int32)

    def bit_body(i, t):
        cand = t + lax.shift_left(jnp.int32(1), 30 - i)
        cnt = count_ge(_sortable_to_float(cand))
        return jnp.where(cnt >= kf, cand, t)

    t = lax.fori_loop(0, 31, bit_body, t0)
    all_finite = t <= KEY_NEG_INF
    tau = jnp.where(all_finite, -jnp.inf, _sortable_to_float(jnp.maximum(t, KEY_NEG_INF)))

    def count_gt_eq(_):
        def body(kb, carry):
            pg, pe = carry
            blk = s_ref[kb]
            mg = jnp.where(blk > tau, 1.0, 0.0)
            me = jnp.where(blk == tau, 1.0, 0.0)
            for c in range(nch):
                pg = pg + mg[:, c * LANES:(c + 1) * LANES]
                pe = pe + me[:, c * LANES:(c + 1) * LANES]
            return pg, pe
        z = jnp.zeros((R, LANES), F32)
        pg, pe = lax.fori_loop(0, nkb, body, (z, z))
        return jnp.sum(pg, axis=1, keepdims=True), jnp.sum(pe, axis=1, keepdims=True)

    cnt_gt, cnt_eq = count_gt_eq(0)
    need = kf - cnt_gt
    tie = (cnt_eq > need) & jnp.logical_not(all_finite)
    any_tie = jnp.max(jnp.where(tie, 1.0, 0.0)) > 0.0

    def tie_bound():
        def count_eq_below(x):
            def body(kb, part):
                blk = s_ref[kb]
                kidx = kb * KB + lane_k
                m = jnp.where((blk == tau) & (kidx < x), 1.0, 0.0)
                for c in range(nch):
                    part = part + m[:, c * LANES:(c + 1) * LANES]
                return part
            part = lax.fori_loop(0, nkb, body, jnp.zeros((R, LANES), F32))
            return jnp.sum(part, axis=1, keepdims=True)

        nbits = int(ltot).bit_length()

        def body(i, x):
            cand = x + lax.shift_left(jnp.int32(1), nbits - 1 - i)
            ok = count_eq_below(cand) < need
            return jnp.where(ok, cand, x)
        x = lax.fori_loop(0, nbits, body, jnp.zeros((R, 1), jnp.int32))
        return x + 1

    big = jnp.full((R, 1), 1 << 30, jnp.int32)
    bound = lax.cond(any_tie, lambda: jnp.where(tie, tie_bound(), big), lambda: big)

    def bias_body(kb, carry):
        blk = s_ref[kb]
        kidx = kb * KB + lane_k
        sel = (blk > tau) | ((blk == tau) & (kidx < bound))
        sel = sel & (blk > -jnp.inf)
        s_ref[kb] = jnp.where(sel, 0.0, NEG_BIG)
        return carry

    lax.fori_loop(0, nkb, bias_body, 0)

    q_all = q_ref[0].reshape(8 * R, LANES)
    m_ref[...] = jnp.full(m_ref.shape, NEG_BIG, F32)
    acc_ref[...] = jnp.zeros(acc_ref.shape, F32)

    def att_body(kb, carry):
        for u in range(nsb):
            off = pl.multiple_of(kb * KB + u * SB, SB)
            kblk = k_ref[0, pl.ds(off, SB), :]
            vblk = v_ref[0, pl.ds(off, SB), :]
            bias = s_ref[kb, :, u * SB:(u + 1) * SB]
            logits = _nt(q_all, kblk)
            for h in range(8):
                lg = logits[h * R:(h + 1) * R] + bias
                cm = lg[:, :LANES]
                for c in range(1, SB // LANES):
                    cm = jnp.maximum(cm, lg[:, c * LANES:(c + 1) * LANES])
                m_old = m_ref[h]
                m_new = jnp.maximum(m_old, jnp.max(cm, axis=1, keepdims=True))
                alpha = jnp.exp(m_old - m_new)
                p = jnp.exp(lg - jnp.concatenate([m_new] * (SB // LANES), axis=1))
                acc_ref[h] = jnp.concatenate([alpha, alpha], axis=1) * acc_ref[h] + _mm(p.astype(BF16), vblk)
                m_ref[h] = m_new
        return carry

    lax.fori_loop(0, nkb, att_body, 0)
    for h in range(8):
        a = acc_ref[h]
        o_ref[0, h] = (a[:, :LANES] / a[:, LANES:]).astype(BF16)


def _attention(iq, iw, q, ikb, kb, vb1, *, R, KB, SB, ltot, q_off):
    B, _, Sq, _ = q.shape
    Lp = kb.shape[1]
    topk = min(TOPK_MAX, ltot // 4)
    assert Lp % KB == 0 and KB % SB == 0 and KB >= topk and Sq % R == 0
    hspec = pl.BlockSpec((1, 8, R, LANES), lambda b, i: (b, 0, i, 0))
    kspec = pl.BlockSpec((1, Lp, LANES), lambda b, i: (b, 0, 0))
    vspec = pl.BlockSpec((1, Lp, 2 * LANES), lambda b, i: (b, 0, 0))
    kern = functools.partial(_attn_kernel, R=R, KB=KB, SB=SB, ltot=ltot, q_off=q_off, topk=topk)
    return pl.pallas_call(
        kern,
        grid=(B, Sq // R),
        in_specs=[hspec, pl.BlockSpec((1, R, 8), lambda b, i: (b, i, 0)), hspec, kspec, kspec, vspec],
        out_specs=hspec,
        out_shape=jax.ShapeDtypeStruct((B, 8, Sq, LANES), BF16),
        scratch_shapes=[pltpu.VMEM((Lp // KB, R, KB), F32), pltpu.VMEM((8, R, LANES), F32),
                        pltpu.VMEM((8, R, 2 * LANES), F32)],
        compiler_params=_params(("arbitrary", "arbitrary")),
        name="dsa_attention",
    )(iq, iw, q, ikb, kb, vb1)


def _head_sum(x):
    parts = []
    for h in range(8):
        s = jnp.sum(x[:, h * LANES:(h + 1) * LANES], axis=1, keepdims=True)
        parts.append(jnp.broadcast_to(s, (x.shape[0], LANES)))
    return jnp.concatenate(parts, axis=1)


def _rwkv_prep_kernel(z_ref, sh0_ref, mu_ref, w0_ref, wup_ref, a0_ref, aup_ref, gup_ref, kk_ref, ka_ref, rk_ref,
                      r_o, lw_o, k_o, v_o, kkn_o, b_o, g_o, bg_o, lwt_o, kt_o, bt_o, carry_ref,
                      *, tm, t_valid):
    i = pl.program_id(1)

    @pl.when(i == 0)
    def _():
        carry_ref[...] = sh0_ref[0]

    z = z_ref[0]
    row = lax.broadcasted_iota(jnp.int32, (tm, 1), 0)
    prev = jnp.where(row == 0, carry_ref[...], pltpu.roll(z, 1, axis=0))
    carry_ref[...] = z[tm - 1:tm, :]
    zm = z + (prev - z) * mu_ref[...]
    live = i * tm + row < t_valid
    zm = jnp.where(live, zm, 0.0)
    r = zm[:, 0:HP]
    k = zm[:, HP:2 * HP]
    v = zm[:, 2 * HP:3 * HP]
    wa = zm[:, 3 * HP:3 * HP + LANES]
    gd = zm[:, 3 * HP + LANES:3 * HP + 2 * LANES]
    w_raw = w0_ref[...] + _mm(jnp.tanh(wa), wup_ref[...], HI)
    lw = (-float(np.exp(-0.5))) / (1.0 + jnp.exp(-w_raw))
    lw = jnp.where(live, lw, 0.0)
    a = 1.0 / (1.0 + jnp.exp(-(a0_ref[...] + _mm(wa, aup_ref[...], HI))))
    g = _mm(1.0 / (1.0 + jnp.exp(-gd)), gup_ref[...], HI)
    kk = k * kk_ref[...]
    kk = kk * lax.rsqrt(jnp.maximum(_head_sum(kk * kk), L2_EPS))
    k_mod = k * (1.0 + (a - 1.0) * ka_ref[...])
    b = kk * a
    bonus = _head_sum(r * k_mod * rk_ref[...]) * v
    r_o[0] = r
    lw_o[0] = lw
    k_o[0] = k_mod
    v_o[0] = v
    kkn_o[0] = kk
    b_o[0] = b
    g_o[0] = g
    bg_o[0] = bonus * g
    lwt_o[0] = lw.T
    kt_o[0] = k_mod.T
    bt_o[0] = b.T


def _rwkv_prep(zr, shift0, prm, tm, t_valid):
    B, T, _ = zr.shape
    tspec = pl.BlockSpec((1, tm, HP), lambda b, i: (b, i, 0))
    fspec = pl.BlockSpec((1, HP, tm), lambda b, i: (b, 0, i))
    full = lambda a: pl.BlockSpec(a.shape, lambda b, i: (0,) * a.ndim)
    tok = jax.ShapeDtypeStruct((B, T, HP), F32)
    feat = jax.ShapeDtypeStruct((B, HP, T), F32)
    names = ("mu", "w0", "w_up", "a0", "a_up", "g_up", "k_k", "k_a", "r_k")
    ws = [prm[n] for n in names]
    return pl.pallas_call(
        functools.partial(_rwkv_prep_kernel, tm=tm, t_valid=t_valid),
        grid=(B, T // tm),
        in_specs=[pl.BlockSpec((1, tm, ZRW), lambda b, i: (b, i, 0)),
                  pl.BlockSpec((1, 1, ZRW), lambda b, i: (b, 0, 0))] + [full(w) for w in ws],
        out_specs=[tspec] * 8 + [fspec] * 3,
        out_shape=[tok] * 8 + [feat] * 3,
        scratch_shapes=[pltpu.VMEM((1, ZRW), F32)],
        compiler_params=_params(("arbitrary", "arbitrary")),
        name="rwkv_prep",
    )(zr, shift0, *ws)


def _rwkv_chunk_kernel(r_ref, lw_ref, k_ref, v_ref, kk_ref, b_ref, lwt_ref, kt_ref, bt_ref, a1_ref, a2_ref,
                       *, C, hb, pg, pc, po):
    ri = lax.broadcasted_iota(jnp.int32, (C, C), 0)
    ci = lax.broadcasted_iota(jnp.int32, (C, C), 1)
    low_incl = jnp.where(ri >= ci, 1.0, 0.0).astype(BF16)
    up_incl = jnp.where(ri <= ci, 1.0, 0.0).astype(BF16)
    strict = ri > ci
    incl = ri >= ci
    eye64 = jnp.where(lax.broadcasted_iota(jnp.int32, (64, 64), 0) == lax.broadcasted_iota(jnp.int32, (64, 64), 1),
                      1.0, 0.0)
    sls = [slice(u * LANES, (u + 1) * LANES) for u in range(hb)]
    each = lambda f, *ls: [f(*a) for a in zip(*ls)]
    lw = [lw_ref[0, :, sl] for sl in sls]
    r = [r_ref[0, :, sl] for sl in sls]
    k = [k_ref[0, :, sl] for sl in sls]
    v = [v_ref[0, :, sl] for sl in sls]
    kk = [kk_ref[0, :, sl] for sl in sls]
    b = [b_ref[0, :, sl] for sl in sls]
    lwt = [lwt_ref[0, sl, :][:64] for sl in sls]
    kt = [kt_ref[0, sl, :][:64] for sl in sls]
    bt = [bt_ref[0, sl, :][:64] for sl in sls]
    cum = each(lambda x: sum(_mm(low_incl, p) for p in _split_bf16(x, 3)), lw)
    cumt = each(lambda x: sum(_mm(p, up_incl) for p in _split_bf16(x, 3)), lwt)
    e_neg = each(lambda c: jnp.exp(-c), cum)
    at = each(lambda kk_, c, l: -kk_ * jnp.exp(c - l), kk, cum, lw)
    rt = each(lambda r_, c: r_ * jnp.exp(c), r, cum)
    g = each(lambda a, r_, b_, k_, e: _mmp(jnp.concatenate([a, r_], axis=0),
                                           jnp.concatenate([b_ * e, k_ * e], axis=0), pg, nt=True),
             at, rt, b, k, e_neg)
    n = each(lambda g_: jnp.where(strict, g_[:C, :C], 0.0), g)
    aak = each(lambda g_: jnp.where(strict, g_[:C, C:], 0.0), g)
    lrb = each(lambda g_: jnp.where(incl, g_[C:, :C], 0.0), g)
    lrk = each(lambda g_: jnp.where(incl, g_[C:, C:], 0.0), g)
    blk8 = lax.shift_right_logical(ri, 3) == lax.shift_right_logical(ci, 3)
    eye = jnp.where(ri == ci, 1.0, 0.0)
    n0 = each(lambda n_: jnp.where(blk8, n_, 0.0), n)
    x = each(lambda n_: eye + n_, n0)
    p = each(lambda n_: _mmp(n_, n_, pc), n0)
    x = each(lambda x_, p_: x_ + _mmp(x_, p_, pc), x, p)
    p = each(lambda p_: _mmp(p_, p_, pc), p)
    x = each(lambda x_, p_: x_ + _mmp(x_, p_, pc), x, p)
    for lv in range(3, int(C).bit_length() - 1):
        off = (lax.shift_right_logical(ri, lv + 1) == lax.shift_right_logical(ci, lv + 1)) & (
            lax.shift_right_logical(ri, lv) != lax.shift_right_logical(ci, lv))
        t = each(lambda n_, x_: _mmp(jnp.where(off, n_, 0.0), x_, pc), n, x)
        x = each(lambda x_, t_: x_ + _mmp(x_, t_, pc), x, t)
    z = each(lambda x_, a, b_: _mmp(x_, jnp.concatenate([a, b_], axis=1), pc), x, aak, at)
    w2v = each(lambda z_, v_: _mmp(z_[:, :C], v_, po), z, v)
    cl = each(lambda c: c[:, C - 1:C], cumt)
    eb = each(lambda c_, ct: jnp.exp(c_ - ct), cl, cumt)
    lhs = each(lambda bt_, kt_, e, lb, lk: jnp.concatenate(
        [jnp.concatenate([bt_ * e, kt_ * e], axis=1), jnp.concatenate([lb, lk], axis=1)], axis=0),
        bt, kt, eb, lrb, lrk)
    left = each(lambda l_, z_: _mmp(l_[:, :C], z_[:, C:], po), lhs, z)
    right = each(lambda l_, w, v_: _mmp(l_, jnp.concatenate([w, v_], axis=0), po), lhs, w2v, v)
    for u in range(hb):
        diag = eye64 * jnp.exp(cl[u])
        a1_ref[0, u, 0] = jnp.concatenate([left[u][:64, :64] + diag, (left[u][64:] + rt[u])[:, :64]], axis=0)
        a2_ref[0, u, 0] = right[u]


def _rwkv_chunks(r, lw, k, v, kk, b, lwt, kt, bt, C, hb, passes=RW_PASSES):
    B, T, _ = r.shape
    nc = T // C
    tspec = pl.BlockSpec((1, C, hb * LANES), lambda bb, h, c: (bb, c, h))
    fspec = pl.BlockSpec((1, hb * LANES, C), lambda bb, h, c: (bb, h, c))
    pg, pc, po = passes
    return pl.pallas_call(
        functools.partial(_rwkv_chunk_kernel, C=C, hb=hb, pg=pg, pc=pc, po=po),
        grid=(B, 8 // hb, nc),
        in_specs=[tspec] * 6 + [fspec] * 3,
        out_specs=[pl.BlockSpec((1, hb, 1, 64 + C, 64), lambda bb, h, c: (bb, h, c, 0, 0)),
                   pl.BlockSpec((1, hb, 1, 64 + C, LANES), lambda bb, h, c: (bb, h, c, 0, 0))],
        out_shape=[jax.ShapeDtypeStruct((B, 8, nc, 64 + C, 64), F32),
                   jax.ShapeDtypeStruct((B, 8, nc, 64 + C, LANES), F32)],
        compiler_params=_params(("arbitrary", "arbitrary", "arbitrary")),
        name="rwkv_chunks",
    )(r, lw, k, v, kk, b, lwt, kt, bt)


def _rwkv_scan_kernel(a1_ref, a2_ref, s0_ref, y_ref, sT_ref, h_ref, *, C, cb):
    c = pl.program_id(1)

    @pl.when(c == 0)
    def _():
        h_ref[...] = s0_ref[0]

    for cc in range(cb):
        for h in range(8):
            res = _mm(a1_ref[0, h, cc], h_ref[h], HI) + a2_ref[0, h, cc]
            h_ref[h] = res[:64]
            y_ref[0, cc * C:(cc + 1) * C, h * LANES:(h + 1) * LANES] = res[64:]

    @pl.when(c == pl.num_programs(1) - 1)
    def _():
        sT_ref[0] = h_ref[...]


def _rwkv_scan(a1, a2, s0t, C, cb):
    B, _, nc, _, _ = a1.shape
    return pl.pallas_call(
        functools.partial(_rwkv_scan_kernel, C=C, cb=cb),
        grid=(B, nc // cb),
        in_specs=[pl.BlockSpec((1, 8, cb, 64 + C, 64), lambda b, c: (b, 0, c, 0, 0)),
                  pl.BlockSpec((1, 8, cb, 64 + C, LANES), lambda b, c: (b, 0, c, 0, 0)),
                  pl.BlockSpec((1, 8, 64, LANES), lambda b, c: (b, 0, 0, 0))],
        out_specs=[pl.BlockSpec((1, cb * C, HP), lambda b, c: (b, c, 0)),
                   pl.BlockSpec((1, 8, 64, LANES), lambda b, c: (b, 0, 0, 0))],
        out_shape=[jax.ShapeDtypeStruct((B, nc * C, HP), F32),
                   jax.ShapeDtypeStruct((B, 8, 64, LANES), F32)],
        scratch_shapes=[pltpu.VMEM((8, 64, LANES), F32)],
        compiler_params=_params(("arbitrary", "arbitrary")),
        name="rwkv_scan",
    )(a1, a2, s0t)


def _mix_kernel(att_ref, y_ref, g_ref, bg_ref, sga_ref, sgb_ref, x_ref, gt1_ref, sc2_ref, sh2_ref,
                woa_ref, wor_ref, wout_ref, lnw_ref, lnb_ref, gpost_ref, gpre_ref, wr_ref, br_ref,
                x1_ref, h2_ref, gate_ref, eidx_ref, wgt_ref, rel_ref, cnt_out_ref, cnt_ref, *, tm):
    att = jnp.concatenate([att_ref[0, h] for h in range(8)], axis=1)
    lane = lax.broadcasted_iota(jnp.int32, (tm, LANES), 1)
    valid = lane < RWKV_HEAD
    parts = []
    for h in range(8):
        y = y_ref[0, :, h * LANES:(h + 1) * LANES]
        mu = jnp.sum(y, axis=1, keepdims=True) * (1.0 / RWKV_HEAD)
        d = jnp.where(valid, y - mu, 0.0)
        var = jnp.sum(d * d, axis=1, keepdims=True) * (1.0 / RWKV_HEAD)
        parts.append(d * lax.rsqrt(var + GN_EPS))
    yn = jnp.concatenate(parts, axis=1)
    rw = (yn * lnw_ref[...] + lnb_ref[...]) * g_ref[0] + bg_ref[0]
    mix = sga_ref[0] * _mm(att, woa_ref[...]) + sgb_ref[0] * _mm(rw.astype(BF16), wor_ref[...])
    o = _mm(mix.astype(BF16), wout_ref[...])
    x1 = x_ref[0] + gt1_ref[0] * _rms(o, gpost_ref[...])
    x1_ref[0] = x1
    h2 = _rms(x1, gpre_ref[...]) * (1.0 + sc2_ref[0]) + sh2_ref[0]
    h2_ref[0] = h2
    logits = _mm(h2, wr_ref[...], HI) + br_ref[...]
    el = lax.broadcasted_iota(jnp.int32, logits.shape, 1).astype(F32)
    work = logits
    sel = jnp.zeros(logits.shape, jnp.bool_)
    vmax = None
    firsts = []
    for kk in range(TOP_K):
        mx = jnp.max(work, axis=1, keepdims=True)
        if kk == 0:
            vmax = mx
        first = jnp.min(jnp.where(work == mx, el, float(N_EXPERTS)), axis=1, keepdims=True)
        hit = el == first
        firsts.append(first)
        sel = sel | hit
        work = jnp.where(hit, -jnp.inf, work)
    e = jnp.where(sel, jnp.exp(logits - vmax), 0.0)
    gate = e / jnp.sum(e, axis=1, keepdims=True)
    gate_ref[0] = gate
    @pl.when((pl.program_id(0) == 0) & (pl.program_id(1) == 0))
    def _():
        cnt_ref[...] = jnp.zeros(cnt_ref.shape, F32)

    self = jnp.where(sel, 1.0, 0.0)
    ti = lax.broadcasted_iota(jnp.int32, (tm, tm), 0)
    tj = lax.broadcasted_iota(jnp.int32, (tm, tm), 1)
    before = _mm(jnp.where(tj < ti, 1.0, 0.0).astype(BF16), self.astype(BF16)) + cnt_ref[...]
    cnt_ref[...] = cnt_ref[...] + jnp.sum(self, axis=0, keepdims=True)
    cnt_out_ref[...] = cnt_ref[...]
    ln = lax.broadcasted_iota(jnp.int32, (tm, LANES), 1)
    eo = jnp.zeros((tm, LANES), F32)
    go = jnp.zeros((tm, LANES), F32)
    ro = jnp.zeros((tm, LANES), F32)
    for kk in range(TOP_K):
        hit = el == firsts[kk]
        gk = jnp.sum(jnp.where(hit, gate, 0.0), axis=1, keepdims=True)
        rk = jnp.sum(jnp.where(hit, before, 0.0), axis=1, keepdims=True)
        eo = jnp.where(ln == kk, firsts[kk], eo)
        go = jnp.where(ln == kk, gk, go)
        ro = jnp.where(ln == kk, rk, ro)
    eidx_ref[0] = eo.astype(jnp.int32)
    wgt_ref[0] = go
    rel_ref[0] = ro.astype(jnp.int32)


def _mix(att, y, g, bg, sga, sgb, x, gt1, sc2, sh2, wts, tm):
    B, T, D = x.shape
    tspec = lambda n: pl.BlockSpec((1, tm, n), lambda b, i: (b, i, 0))
    full = lambda a: pl.BlockSpec(a.shape, lambda b, i: (0,) * a.ndim)
    tok = lambda n, dt: jax.ShapeDtypeStruct((B, T, n), dt)
    return pl.pallas_call(
        functools.partial(_mix_kernel, tm=tm),
        grid=(B, T // tm),
        in_specs=[pl.BlockSpec((1, 8, tm, LANES), lambda b, i: (b, 0, i, 0)), tspec(HP), tspec(HP), tspec(HP),
                  tspec(D), tspec(D), tspec(D), _mod_spec(gt1, tm), _mod_spec(sc2, tm), _mod_spec(sh2, tm)]
                 + [full(w) for w in wts],
        out_specs=[tspec(D), tspec(D), tspec(N_EXPERTS), tspec(LANES), tspec(LANES), tspec(LANES),
                   pl.BlockSpec((1, N_EXPERTS), lambda b, i: (0, 0))],
        out_shape=[tok(D, F32), tok(D, F32), tok(N_EXPERTS, F32), tok(LANES, jnp.int32), tok(LANES, F32),
                   tok(LANES, jnp.int32), jax.ShapeDtypeStruct((1, N_EXPERTS), F32)],
        scratch_shapes=[pltpu.VMEM((1, N_EXPERTS), F32)],
        compiler_params=_params(("arbitrary", "arbitrary")),
        name="mix_router",
    )(att, y, g, bg, sga, sgb, x, gt1, sc2, sh2, *wts)


def _expert(xb, wgu, bgu, wd, bd, d_ff):
    hgu = _mm(xb, wgu) + bgu
    hg = jnp.minimum(hgu[:, :d_ff], SWIGLU_LIMIT)
    hl = jnp.clip(hgu[:, d_ff:], -SWIGLU_LIMIT, SWIGLU_LIMIT)
    act = hg * (1.0 / (1.0 + jnp.exp(-SWIGLU_ALPHA * hg))) * (hl + 1.0)
    return _mm(act.astype(BF16), wd) + bd


def _sc_gather_rows(table, idx):
    V, D = table.shape
    M = idx.shape[0]
    dpiece = D // SC_SPLIT
    t2 = table.reshape(V * SC_SPLIT, dpiece)
    i2 = (idx[:, None] * SC_SPLIT + jnp.arange(SC_SPLIT, dtype=jnp.int32)[None, :]).reshape(1, M * SC_SPLIT)
    mesh = plsc.VectorSubcoreMesh(core_axis_name="core", subcore_axis_name="subcore")
    steps = M * SC_SPLIT // (mesh.num_cores * SC_WINDOW)
    assert steps * mesh.num_cores * SC_WINDOW == M * SC_SPLIT

    @pl.kernel(out_type=jax.ShapeDtypeStruct((M * SC_SPLIT, dpiece), table.dtype), mesh=mesh)
    def gather(x_hbm, i_hbm, o_hbm):
        base = lax.axis_index("core") * steps

        def body(i_vmem, o_vmem):
            pltpu.sync_copy(x_hbm.at[i_vmem.at[0]], o_vmem)

        pltpu.emit_pipeline(
            body,
            grid=(steps,),
            in_specs=[pl.BlockSpec((1, SC_WINDOW), index_map=lambda i: (0, base + i))],
            out_specs=[pl.BlockSpec((SC_WINDOW, dpiece), index_map=lambda i: (base + i, 0))],
            core_axis_name="subcore",
            dimension_semantics=(pltpu.PARALLEL,),
        )(i_hbm, o_hbm)

    return gather(t2, i2).reshape(M, D)


def _moe_grouped_kernel(te_ref, nt_ref, x_ref, wgu_ref, bgu_ref, wd_ref, bd_ref, o_ref, *, d_ff):
    @pl.when(pl.program_id(0) < nt_ref[0])
    def _():
        o_ref[...] = _expert(x_ref[...].astype(BF16), wgu_ref[0], bgu_ref[0], wd_ref[0], bd_ref[0], d_ff)


def _moe_grouped(xs, te, nt, wgu, bgu, wd, bd, TG):
    NP, D = xs.shape
    E, _, F2 = wgu.shape
    wmap = lambda g, te_, nt_: (te_[g], 0, 0)
    xmap = lambda g, te_, nt_: (jnp.minimum(g, nt_[0] - 1), 0)
    return pl.pallas_call(
        functools.partial(_moe_grouped_kernel, d_ff=F2 // 2),
        grid_spec=pltpu.PrefetchScalarGridSpec(
            num_scalar_prefetch=2,
            grid=(NP // TG,),
            in_specs=[pl.BlockSpec((TG, D), xmap),
                      pl.BlockSpec((1, D, F2), wmap), pl.BlockSpec((1, 1, F2), wmap),
                      pl.BlockSpec((1, F2 // 2, D), wmap), pl.BlockSpec((1, 1, D), wmap)],
            out_specs=pl.BlockSpec((TG, D), xmap)),
        out_shape=jax.ShapeDtypeStruct((NP, D), F32),
        compiler_params=_params(("arbitrary",)),
        name="moe_grouped",
    )(te, nt, xs, wgu, bgu, wd, bd)


def _moe_combine_kernel(y0_ref, y1_ref, y2_ref, y3_ref, w_ref, x1_ref, gt2_ref, gpost_ref, o_ref):
    w = w_ref[...]
    acc = w[:, 0:1] * y0_ref[...]
    acc = acc + w[:, 1:2] * y1_ref[...]
    acc = acc + w[:, 2:3] * y2_ref[...]
    acc = acc + w[:, 3:4] * y3_ref[...]
    o_ref[...] = x1_ref[...] + gt2_ref[0] * _rms(acc, gpost_ref[...])


def _moe_combine(yslot, wgt, x1, gt2, gpost, tm):
    N, D = x1.shape
    nb = N // tm
    tpb = N // gt2.shape[0]
    yspec = lambda k: pl.BlockSpec((tm, D), lambda i, k=k: (i + k * nb, 0))
    return pl.pallas_call(
        _moe_combine_kernel,
        grid=(nb,),
        in_specs=[yspec(0), yspec(1), yspec(2), yspec(3), pl.BlockSpec((tm, LANES), lambda i: (i, 0)),
                  pl.BlockSpec((tm, D), lambda i: (i, 0)), pl.BlockSpec((1, 1, D), lambda i: ((i * tm) // tpb, 0, 0)),
                  pl.BlockSpec((1, D), lambda i: (0, 0))],
        out_specs=pl.BlockSpec((tm, D), lambda i: (i, 0)),
        out_shape=jax.ShapeDtypeStruct((N, D), F32),
        compiler_params=_params(("arbitrary",)),
        name="moe_combine",
    )(yslot, yslot, yslot, yslot, wgt, x1, gt2, gpost)


def _moe_kernel(h_ref, gate_ref, x1_ref, gt2_ref, gpost_ref, wgu_ref, bgu_ref, wd_ref, bd_ref, o_ref, acc_ref,
                *, d_ff):
    e = pl.program_id(1)

    @pl.when(e == 0)
    def _():
        acc_ref[...] = jnp.zeros(acc_ref.shape, F32)

    contrib = _expert(h_ref[...].astype(BF16), wgu_ref[0], bgu_ref[0], wd_ref[0], bd_ref[0], d_ff)
    gate = gate_ref[...]
    el = lax.broadcasted_iota(jnp.int32, gate.shape, 1)
    ge = jnp.sum(jnp.where(el == e, gate, 0.0), axis=1, keepdims=True)
    acc_ref[...] += ge * contrib

    @pl.when(e == pl.num_programs(1) - 1)
    def _():
        o_ref[...] = x1_ref[...] + gt2_ref[0] * _rms(acc_ref[...], gpost_ref[...])


def _moe(h2, gate, x1, gt2, gpost, wgu, bgu, wd, bd, tm):
    N, D = h2.shape
    E, _, F2 = wgu.shape
    tspec = lambda n: pl.BlockSpec((tm, n), lambda i, e: (i, 0))
    if gt2.shape[1] == 1:
        tpb = N // gt2.shape[0]
        gspec = pl.BlockSpec((1, 1, D), lambda i, e: ((i * tm) // tpb, 0, 0))
    else:
        gspec = pl.BlockSpec((1, tm, D), lambda i, e: (i, 0, 0))
    return pl.pallas_call(
        functools.partial(_moe_kernel, d_ff=F2 // 2),
        grid=(N // tm, E),
        in_specs=[tspec(D), tspec(E), tspec(D), gspec, pl.BlockSpec((1, D), lambda i, e: (0, 0)),
                  pl.BlockSpec((1, D, F2), lambda i, e: (e, 0, 0)), pl.BlockSpec((1, 1, F2), lambda i, e: (e, 0, 0)),
                  pl.BlockSpec((1, F2 // 2, D), lambda i, e: (e, 0, 0)), pl.BlockSpec((1, 1, D), lambda i, e: (e, 0, 0))],
        out_specs=tspec(D),
        out_shape=jax.ShapeDtypeStruct((N, D), F32),
        scratch_shapes=[pltpu.VMEM((tm, D), F32)],
        compiler_params=_params(("arbitrary", "arbitrary")),
        name="moe",
    )(h2, gate, x1, gt2, gpost, wgu, bgu, wd, bd)


def _per_token(m, T):
    B, _, D = m.shape
    return jnp.broadcast_to(m, (B, T, D)).reshape(1, B * T, D)


def _layer(x, mods, past, s0, shift0, P, cfg):
    B, T, D = x.shape
    sh1, sc1, gt1, sh2, sc2, gt2 = mods
    tm = cfg["tm"]

    q, k, v, kb, vb, iq, ik, ikb, iw = _proj_a(x, sc1, sh1, P["g_pre_mix"], P["w_a"], P["ik_ln_w"], P["ik_ln_b"], tm)
    zr, sga, sgb = _proj_b(x, sc1, sh1, P["g_pre_mix"], P["w_b"], tm)

    if past is not None:
        pk, pv, pik = past
        plen = pk.shape[1]
        kb_all = jnp.concatenate([pk.reshape(B, plen, 128).astype(BF16), kb], axis=1)
        vb_all = jnp.concatenate([pv.reshape(B, plen, 128).astype(BF16), vb], axis=1)
        ik_all = jnp.concatenate([jnp.pad(pik, ((0, 0), (0, 0), (0, 64))).astype(BF16), ikb], axis=1)
    else:
        plen = 0
        kb_all, vb_all, ik_all = kb, vb, ikb
    ltot = plen + T
    KB = cfg["KB"]
    lp = -(-ltot // KB) * KB
    if lp != ltot:
        padk = ((0, 0), (0, lp - ltot), (0, 0))
        kb_all, vb_all, ik_all = jnp.pad(kb_all, padk), jnp.pad(vb_all, padk), jnp.pad(ik_all, padk)
    vb1 = jnp.concatenate([vb_all, jnp.ones_like(vb_all)], axis=-1)
    att = _attention(iq, iw, q, ik_all, kb_all, vb1, R=cfg["R"], KB=KB, SB=cfg["SB"], ltot=ltot, q_off=plen)

    C = RW_CHUNK
    tp = -(-T // C) * C
    zr_p = zr if tp == T else jnp.pad(zr, ((0, 0), (0, tp - T), (0, 0)))
    shift_pad = _take_cols(shift0.reshape(B, RWKV_COLS), _COLS_ZR).reshape(B, 1, ZRW)
    r, lw, km, vv, kkn, bb, g, bg, lwt, kt, bt = _rwkv_prep(zr_p, shift_pad, P, min(cfg["tm_rw"], tp), T)
    a1, a2 = _rwkv_chunks(r, lw, km, vv, kkn, bb, lwt, kt, bt, C, cfg["hb"])
    s0t = jnp.pad(jnp.swapaxes(s0, 2, 3), ((0, 0), (0, 0), (0, 0), (0, 64)))
    nc = tp // C
    cb = min(cfg["cb"], nc)
    y, sT = _rwkv_scan(a1, a2, s0t, C, cb)
    s_new = jnp.swapaxes(sT[..., :64], 2, 3)
    zlast = zr[:, T - 1]
    inv = np.zeros((RWKV_COLS,), np.int32)
    inv[_COLS_ZR[0][_COLS_ZR[1]]] = np.nonzero(_COLS_ZR[1])[0]
    shift_new = jnp.take(zlast, jnp.asarray(inv), axis=1).reshape(B, 1, RWKV_COLS)
    if tp != T:
        y, g, bg = y[:, :T], g[:, :T], bg[:, :T]

    wts = [P[n] for n in ("w_o_att", "w_o_rwkv", "w_out", "ln_x_w", "ln_x_b", "g_post_mix", "g_pre_ffn",
                          "w_router", "b_router")]
    x1, h2, gate, eidx, wgt, rel, cnt = _mix(att, y, g, bg, sga, sgb, x, gt1, sc2, sh2, wts, tm)

    N = B * T
    tmm = min(cfg["tm_moe"], N)
    if cfg["routed"]:
        TG = cfg["TG"]
        ntm = N * TOP_K // TG + N_EXPERTS
        cnt_i = cnt[0].astype(jnp.int32)
        padded = (cnt_i + TG - 1) // TG * TG
        ends = jnp.cumsum(padded)
        off = ends - padded
        nt = ends[-1] // TG
        gi = jnp.arange(ntm, dtype=jnp.int32)
        te = jnp.searchsorted(ends // TG, jnp.minimum(gi, nt - 1), side="right").astype(jnp.int32)
        e4 = eidx.reshape(N, LANES)[:, :TOP_K]
        pos = jnp.take(off, e4) + rel.reshape(N, LANES)[:, :TOP_K]
        tok = jnp.broadcast_to(jnp.arange(N, dtype=jnp.int32)[:, None], (N, TOP_K))
        src = jnp.zeros((ntm * TG,), jnp.int32).at[pos.reshape(-1)].set(tok.reshape(-1))
        xs = _sc_gather_rows(h2.reshape(N, D), src)
        ys = _moe_grouped(xs, te, nt.reshape(1), P["w_gu"], P["b_gu"], P["w_down"], P["b_down"], TG)
        yslot = _sc_gather_rows(ys, pos.T.reshape(-1))
        out = _moe_combine(yslot, wgt.reshape(N, LANES), x1.reshape(N, D), gt2, P["g_post_ffn"], tm)
    else:
        if T % tmm == 0:
            gt2m = gt2
        else:
            gt2m = _per_token(gt2, T).reshape(N // tmm, tmm, D)
        out = _moe(h2.reshape(N, D), gate.reshape(N, N_EXPERTS), x1.reshape(N, D), gt2m, P["g_post_ffn"],
                   P["w_gu"], P["b_gu"], P["w_down"], P["b_down"], tmm)
    return (out.reshape(B, T, D), k.reshape(B, T, ATT_KV_HEADS, HEAD_DIM), v.reshape(B, T, ATT_KV_HEADS, HEAD_DIM),
            ik, s_new, shift_new)


def _prep_weights(l, w_in, ik_ln_w, ik_ln_b, mu_rwkv, w0, w_up, a0, a_up, g_up, k_k, k_a, r_k, ln_x_w, ln_x_b,
                  w_o_att, w_o_rwkv, w_out, w_router, b_router, w_gu, b_gu, w_down, b_down,
                  g_pre_mix, g_post_mix, g_pre_ffn, g_post_ffn):
    row = lambda a: a.reshape(1, -1)
    hp = lambda a: _take_cols(row(a), _HEAD_IDX)
    P = {}
    P["w_a"] = _take_cols(w_in[l], _COLS_A).astype(BF16)
    P["w_b"] = _take_cols(w_in[l], _COLS_B).astype(BF16)
    P["ik_ln_w"] = jnp.pad(row(ik_ln_w[l]), ((0, 0), (0, 64)))
    P["ik_ln_b"] = jnp.pad(row(ik_ln_b[l]), ((0, 0), (0, 64)))
    P["mu"] = _take_cols(row(mu_rwkv[l]), _COLS_ZR)
    P["w0"], P["a0"], P["k_k"], P["k_a"] = hp(w0[l]), hp(a0[l]), hp(k_k[l]), hp(k_a[l])
    P["r_k"] = hp(r_k[l].reshape(-1))
    P["ln_x_w"], P["ln_x_b"] = hp(ln_x_w[l]), hp(ln_x_b[l])
    z64 = jnp.zeros((64, HP), F32)
    P["w_up"] = jnp.concatenate([_take_cols(w_up[l], _HEAD_IDX), z64], axis=0)
    P["a_up"] = jnp.concatenate([z64, _take_cols(a_up[l], _HEAD_IDX)], axis=0)
    P["g_up"] = _take_cols(g_up[l], _HEAD_IDX)
    P["w_o_att"] = _take_rows(w_o_att[l], _ATT_ROW_IDX).astype(BF16)
    P["w_o_rwkv"] = _take_rows(w_o_rwkv[l], _HEAD_IDX).astype(BF16)
    P["w_out"] = w_out[l].astype(BF16)
    P["w_router"] = w_router[l]
    P["b_router"] = row(b_router[l])
    P["w_gu"] = w_gu[l].astype(BF16)
    P["b_gu"] = b_gu[l].reshape(N_EXPERTS, 1, -1)
    P["w_down"] = w_down[l].astype(BF16)
    P["b_down"] = b_down[l].reshape(N_EXPERTS, 1, -1)
    P["g_pre_mix"], P["g_post_mix"] = row(g_pre_mix[l]), row(g_post_mix[l])
    P["g_pre_ffn"], P["g_post_ffn"] = row(g_pre_ffn[l]), row(g_post_ffn[l])
    return P


CFG_PROMPT = dict(tm=256, R=128, KB=512, SB=256, tm_rw=256, hb=8, cb=2, tm_moe=1024, routed=True, TG=512)
CFG_SAMPLE = dict(tm=32, R=32, KB=512, SB=256, tm_rw=128, hb=8, cb=1, tm_moe=512, routed=False)


def kernel(x_prompt, x_sample, c_prompt, c_sample, cache_k, cache_v, cache_idx_k, state_rwkv, state_shift, w_ada, b_ada, g_pre_mix, g_post_mix, g_pre_ffn, g_post_ffn, w_in, ik_ln_w, ik_ln_b, mu_rwkv, w0, w_up, a0, a_up, g_up, k_k, k_a, r_k, ln_x_w, ln_x_b, w_o_att, w_o_rwkv, w_out, w_router, b_router, w_gu, b_gu, w_down, b_down):
    depth = w_in.shape[0]
    bp, tp_, D = x_prompt.shape
    bs, ts, _ = x_sample.shape
    y_p, y_s = x_prompt, x_sample
    st_p = [[] for _ in range(5)]
    st_s = [[] for _ in range(5)]
    nc_all = bp + bs
    npad = -(-nc_all // 8) * 8
    c_all = jnp.pad(jnp.concatenate([c_prompt, c_sample], axis=0), ((0, npad - nc_all), (0, 0)))
    for l in range(depth):
        P = _prep_weights(l, w_in, ik_ln_w, ik_ln_b, mu_rwkv, w0, w_up, a0, a_up, g_up, k_k, k_a, r_k, ln_x_w,
                          ln_x_b, w_o_att, w_o_rwkv, w_out, w_router, b_router, w_gu, b_gu, w_down, b_down,
                          g_pre_mix, g_post_mix, g_pre_ffn, g_post_ffn)
        ada = _ada(c_all, w_ada[l], b_ada[l])
        mods_p = [m[:bp, None, :] for m in jnp.split(ada, 6, axis=-1)]
        mods_s = [m[bp:nc_all, None, :] for m in jnp.split(ada, 6, axis=-1)]
        zero_state = jnp.zeros((bp, RWKV_HEADS, RWKV_HEAD, RWKV_HEAD), F32)
        zero_shift = jnp.zeros((bp, 1, RWKV_COLS), F32)
        outs_p = _layer(y_p, mods_p, None, zero_state, zero_shift, P, CFG_PROMPT)
        outs_s = _layer(y_s, mods_s, (cache_k[l], cache_v[l], cache_idx_k[l]), state_rwkv[l], state_shift[l], P,
                        CFG_SAMPLE)
        y_p, y_s = outs_p[0], outs_s[0]
        for lst, val in zip(st_p, outs_p[1:]):
            lst.append(val)
        for lst, val in zip(st_s, outs_s[1:]):
            lst.append(val)
    sp = [jnp.stack(v, axis=0) for v in st_p]
    ss = [jnp.stack(v, axis=0) for v in st_s]
    return (y_p, y_s, sp[0], sp[1], sp[2], sp[3], sp[4], ss[0], ss[1], ss[2], ss[3], ss[4])
```

```python
import functools

import numpy as np
import jax
import jax.numpy as jnp
from jax import lax
from jax.experimental import pallas as pl
from jax.experimental.pallas import tpu as pltpu
from jax.experimental.pallas import tpu_sc as plsc

F32 = jnp.float32
BF16 = jnp.bfloat16
HI = lax.Precision.HIGHEST

CHUNK = 64
ATT_HEADS = 8
ATT_KV_HEADS = 2
HEAD_DIM = 64
IDX_HEADS = 8
IDX_DIM = 64
IDX_SCALE = (IDX_DIM ** -0.5) * (IDX_HEADS ** -0.5)
TOPK_MAX = 256
RWKV_HEADS = 8
RWKV_HEAD = 64
RWKV_WIDTH = RWKV_HEADS * RWKV_HEAD
DECAY_LORA = 64
AAA_LORA = 64
GATE_LORA = 128
RWKV_COLS = 3 * RWKV_WIDTH + DECAY_LORA + AAA_LORA + GATE_LORA
GN_EPS = 64e-5
L2_EPS = 1e-24
N_EXPERTS = 32
TOP_K = 4
SWIGLU_LIMIT = 7.0
SWIGLU_ALPHA = 1.702
NORM_EPS = 1e-6
LN_EPS = 1e-6

LANES = 128
VMEM_LIMIT = 56 * 1024 * 1024

HP = 8 * LANES
RW_CHUNK = 128
NEG_BIG = -1e30
KEY_NEG_INF = -2139095041
INT_MIN = -2147483648
SC_WINDOW = 128
SC_PIECE = 256
RW_PASSES = (1, 1, 1)


def _params(sem):
    return pltpu.CompilerParams(dimension_semantics=sem, vmem_limit_bytes=VMEM_LIMIT)


def _nt(a, b, precision=None):
    return lax.dot_general(a, b, (((1,), (1,)), ((), ())), precision=precision,
                           preferred_element_type=F32)


def _mm(a, b, precision=None):
    return jnp.dot(a, b, precision=precision, preferred_element_type=F32)


def _split_bf16(x, terms):
    out = []
    for _ in range(terms):
        p = x.astype(BF16)
        out.append(p)
        x = x - p.astype(F32)
    return out


def _mmp(a, b, passes, nt=False):
    dot = _nt if nt else _mm
    if passes == 1:
        return dot(a.astype(BF16), b.astype(BF16))
    ah, al = _split_bf16(a, 2)
    bh, bl = _split_bf16(b, 2)
    return dot(ah, bh) + (dot(ah, bl) + dot(al, bh))


def _rms(x, g):
    return x * lax.rsqrt(jnp.mean(x * x, axis=-1, keepdims=True) + NORM_EPS) * g


def _headpad_idx(seg_off, lane_off_fn=lambda h: 0):
    idx = np.zeros((HP,), np.int32)
    ok = np.zeros((HP,), bool)
    for h in range(8):
        d0 = h * LANES + lane_off_fn(h)
        idx[d0:d0 + 64] = seg_off + h * 64 + np.arange(64)
        ok[d0:d0 + 64] = True
    return idx, ok


def _plain_idx(seg_off, n, width):
    idx = np.zeros((width,), np.int32)
    ok = np.zeros((width,), bool)
    idx[:n] = seg_off + np.arange(n)
    ok[:n] = True
    return idx, ok


def _cat(parts):
    return np.concatenate([p[0] for p in parts]), np.concatenate([p[1] for p in parts])


_O_Q, _O_K, _O_V, _O_IQ, _O_IK, _O_IW, _O_ZR, _O_GA, _O_GB = 0, 512, 640, 768, 1280, 1344, 1352, 3144, 4168
_COLS_A = _cat([_headpad_idx(_O_Q, lambda h: (h // 4) * 64), _plain_idx(_O_K, 128, 128), _plain_idx(_O_V, 128, 128),
                _headpad_idx(_O_IQ), _plain_idx(_O_IK, 64, 128), _plain_idx(_O_IW, 8, 128)])
_COLS_ZR = _cat([_headpad_idx(0), _headpad_idx(512), _headpad_idx(1024), _plain_idx(1536, 128, 128),
                 _plain_idx(1664, 128, 128)])
ZRW = _COLS_ZR[0].shape[0]
_COLS_B = _cat([(_COLS_ZR[0] + _O_ZR, _COLS_ZR[1]), _plain_idx(_O_GA, 1024, 1024), _plain_idx(_O_GB, 1024, 1024)])
NA = _COLS_A[0].shape[0]
NB = _COLS_B[0].shape[0]
_HEAD_IDX = _headpad_idx(0)
_ATT_ROW_IDX = _headpad_idx(0, lambda h: (h // 4) * 64)


def _take_cols(w, cols):
    idx, ok = cols
    return jnp.where(jnp.asarray(ok)[None, :], jnp.take(w, jnp.asarray(idx), axis=1), 0.0)


def _take_rows(w, cols):
    idx, ok = cols
    return jnp.where(jnp.asarray(ok)[:, None], jnp.take(w, jnp.asarray(idx), axis=0), 0.0)


def _mod_spec(m, tm):
    assert m.shape[1] == 1
    return pl.BlockSpec((1, 1, m.shape[2]), lambda b, i: (b, 0, 0))


def _ada_kernel(c_ref, w_ref, b_ref, o_ref):
    c = c_ref[...]
    s = c * (1.0 / (1.0 + jnp.exp(-c)))
    o_ref[...] = _mm(s, w_ref[...], HI) + b_ref[...]


def _ada(c, w, b):
    n, d = c.shape
    nout = w.shape[1]
    bn = 768
    return pl.pallas_call(
        _ada_kernel,
        grid=(nout // bn,),
        in_specs=[pl.BlockSpec((n, d), lambda j: (0, 0)),
                  pl.BlockSpec((d, bn), lambda j: (0, j)),
                  pl.BlockSpec((1, bn), lambda j: (0, j))],
        out_specs=pl.BlockSpec((n, bn), lambda j: (0, j)),
        out_shape=jax.ShapeDtypeStruct((n, nout), F32),
        compiler_params=_params(("arbitrary",)),
        name="ada",
    )(c, w, b.reshape(1, nout))


def _modulated(x_ref, sc_ref, sh_ref, g_ref):
    h = _rms(x_ref[0], g_ref[...])
    return (h * (1.0 + sc_ref[0]) + sh_ref[0]).astype(BF16)


def _proj_a_kernel(x_ref, sc_ref, sh_ref, g_ref, w_ref, lnw_ref, lnb_ref,
                   q_ref, k_ref, v_ref, kb_ref, vb_ref, iq_ref, ik_ref, ikb_ref, iw_ref):
    hb = _modulated(x_ref, sc_ref, sh_ref, g_ref)
    for h in range(8):
        zq = _mm(hb, w_ref[:, h * LANES:(h + 1) * LANES])
        q_ref[0, h] = (zq * (HEAD_DIM ** -0.5)).astype(BF16)
    kv = _mm(hb, w_ref[:, 1024:1280])
    k = kv[:, :128]
    v = kv[:, 128:]
    k_ref[0] = k
    v_ref[0] = v
    kb_ref[0] = k.astype(BF16)
    vb_ref[0] = v.astype(BF16)
    for h in range(8):
        zi = _mm(hb, w_ref[:, 1280 + h * LANES:1280 + (h + 1) * LANES])
        iq_ref[0, h] = zi.astype(BF16)
    t = _mm(hb, w_ref[:, 2304:2560])
    ik = t[:, :128]
    lane = lax.broadcasted_iota(jnp.int32, ik.shape, 1)
    valid = lane < IDX_DIM
    mu = jnp.sum(ik, axis=-1, keepdims=True) * (1.0 / IDX_DIM)
    d = jnp.where(valid, ik - mu, 0.0)
    var = jnp.sum(d * d, axis=-1, keepdims=True) * (1.0 / IDX_DIM)
    ikn = d * lax.rsqrt(var + LN_EPS) * lnw_ref[...] + lnb_ref[...]
    ik_ref[0] = ikn[:, :IDX_DIM]
    ikb_ref[0] = ikn.astype(BF16)
    iw_ref[0] = t[:, 128:136] * IDX_SCALE


def _proj_a(x, sc, sh, g, w, lnw, lnb, tm):
    B, T, D = x.shape
    nt = T // tm
    tok = lambda n, dt: jax.ShapeDtypeStruct((B, T, n), dt)
    hm = lambda dt: jax.ShapeDtypeStruct((B, 8, T, LANES), dt)
    tspec = lambda n: pl.BlockSpec((1, tm, n), lambda b, i: (b, i, 0))
    hspec = pl.BlockSpec((1, 8, tm, LANES), lambda b, i: (b, 0, i, 0))
    full = lambda a: pl.BlockSpec(a.shape, lambda b, i: (0,) * a.ndim)
    return pl.pallas_call(
        _proj_a_kernel,
        grid=(B, nt),
        in_specs=[tspec(D), _mod_spec(sc, tm), _mod_spec(sh, tm), full(g), full(w), full(lnw), full(lnb)],
        out_specs=[hspec, tspec(128), tspec(128), tspec(128), tspec(128), hspec, tspec(IDX_DIM), tspec(128),
                   tspec(8)],
        out_shape=[hm(BF16), tok(128, F32), tok(128, F32), tok(128, BF16), tok(128, BF16), hm(BF16),
                   tok(IDX_DIM, F32), tok(128, BF16), tok(8, F32)],
        compiler_params=_params(("arbitrary", "arbitrary")),
        name="proj_att",
    )(x, sc, sh, g, w, lnw, lnb)


def _proj_b_kernel(x_ref, sc_ref, sh_ref, g_ref, w_ref, zr_ref, ga_ref, gb_ref):
    hb = _modulated(x_ref, sc_ref, sh_ref, g_ref)
    for j in range(ZRW // 256):
        zr_ref[0, :, j * 256:(j + 1) * 256] = _mm(hb, w_ref[:, j * 256:(j + 1) * 256])
    for j in range(4):
        za = _mm(hb, w_ref[:, ZRW + j * 256:ZRW + (j + 1) * 256])
        ga_ref[0, :, j * 256:(j + 1) * 256] = 1.0 / (1.0 + jnp.exp(-za))
        zb = _mm(hb, w_ref[:, ZRW + 1024 + j * 256:ZRW + 1024 + (j + 1) * 256])
        gb_ref[0, :, j * 256:(j + 1) * 256] = 1.0 / (1.0 + jnp.exp(-zb))


def _proj_b(x, sc, sh, g, w, tm):
    B, T, D = x.shape
    nt = T // tm
    tok = lambda n: jax.ShapeDtypeStruct((B, T, n), F32)
    tspec = lambda n: pl.BlockSpec((1, tm, n), lambda b, i: (b, i, 0))
    full = lambda a: pl.BlockSpec(a.shape, lambda b, i: (0,) * a.ndim)
    return pl.pallas_call(
        _proj_b_kernel,
        grid=(B, nt),
        in_specs=[tspec(D), _mod_spec(sc, tm), _mod_spec(sh, tm), full(g), full(w)],
        out_specs=[tspec(ZRW), tspec(1024), tspec(1024)],
        out_shape=[tok(ZRW), tok(1024), tok(1024)],
        compiler_params=_params(("arbitrary", "arbitrary")),
        name="proj_rwkv",
    )(x, sc, sh, g, w)


def _sortable_to_float(t):
    bits = t ^ (lax.shift_right_arithmetic(t, 31) & 0x7FFFFFFF)
    return lax.bitcast_convert_type(bits, F32)


def _attn_kernel(iq_ref, iw_ref, q_ref, ik_ref, k_ref, v_ref, o_ref, s_ref, m_ref, acc_ref,
                 *, R, KB, SB, ltot, q_off, topk):
    j = pl.program_id(1)
    q0 = q_off + j * R
    pos = q0 + lax.broadcasted_iota(jnp.int32, (R, 1), 0)
    qchunk = lax.shift_right_logical(pos, 6)
    kend = jnp.minimum(ltot, (lax.shift_right_logical(q0 + R - 1, 6) + 1) * CHUNK)
    nkb = lax.shift_right_logical(kend + KB - 1, KB.bit_length() - 1)
    lane_k = lax.broadcasted_iota(jnp.int32, (R, KB), 1)
    nch = KB // LANES

    iq_all = iq_ref[0].reshape(8 * R, LANES)
    iw = iw_ref[0]
    iwb = [jnp.broadcast_to(iw[:, h:h + 1], (R, SB)) for h in range(8)]
    lane_s = lax.broadcasted_iota(jnp.int32, (R, SB), 1)
    nsb = KB // SB

    def score_body(kb, carry):
        for u in range(nsb):
            off = pl.multiple_of(kb * KB + u * SB, SB)
            ikb = ik_ref[0, pl.ds(off, SB), :]
            s_all = _nt(iq_all, ikb)
            acc = jnp.zeros((R, SB), F32)
            for h in range(8):
                acc = acc + jnp.maximum(s_all[h * R:(h + 1) * R], 0.0) * iwb[h]
            kidx = off + lane_s
            adm = (lax.shift_right_logical(kidx, 6) <= qchunk) & (kidx < ltot)
            s_ref[kb, :, u * SB:(u + 1) * SB] = jnp.where(adm, acc, -jnp.inf)
        return carry

    lax.fori_loop(0, nkb, score_body, 0)

    def count_ge(cand_f):
        def body(kb, part):
            blk = s_ref[kb]
            m = jnp.where(blk >= cand_f, 1.0, 0.0)
            for c in range(nch):
                part = part + m[:, c * LANES:(c + 1) * LANES]
            return part
        part = lax.fori_loop(0, nkb, body, jnp.zeros((R, LANES), F32))
        return jnp.sum(part, axis=1, keepdims=True)

    kf = float(topk)
    c0 = count_ge(jnp.zeros((R, 1), F32))
    t0 = jnp.where(c0 >= kf, 0, INT_MIN).astype(jnp.int32)

    def bit_body(i, t):
        cand = t + lax.shift_left(jnp.int32(1), 30 - i)
        cnt = count_ge(_sortable_to_float(cand))
        return jnp.where(cnt >= kf, cand, t)

    t = lax.fori_loop(0, 31, bit_body, t0)
    all_finite = t <= KEY_NEG_INF
    tau = jnp.where(all_finite, -jnp.inf, _sortable_to_float(jnp.maximum(t, KEY_NEG_INF)))

    def count_gt_eq(_):
        def body(kb, carry):
            pg, pe = carry
            blk = s_ref[kb]
            mg = jnp.where(blk > tau, 1.0, 0.0)
            me = jnp.where(blk == tau, 1.0, 0.0)
            for c in range(nch):
                pg = pg + mg[:, c * LANES:(c + 1) * LANES]
                pe = pe + me[:, c * LANES:(c + 1) * LANES]
            return pg, pe
        z = jnp.zeros((R, LANES), F32)
        pg, pe = lax.fori_loop(0, nkb, body, (z, z))
        return jnp.sum(pg, axis=1, keepdims=True), jnp.sum(pe, axis=1, keepdims=True)

    cnt_gt, cnt_eq = count_gt_eq(0)
    need = kf - cnt_gt
    tie = (cnt_eq > need) & jnp.logical_not(all_finite)
    any_tie = jnp.max(jnp.where(tie, 1.0, 0.0)) > 0.0

    def tie_bound():
        def count_eq_below(x):
            def body(kb, part):
                blk = s_ref[kb]
                kidx = kb * KB + lane_k
                m = jnp.where((blk == tau) & (kidx < x), 1.0, 0.0)
                for c in range(nch):
                    part = part + m[:, c * LANES:(c + 1) * LANES]
                return part
            part = lax.fori_loop(0, nkb, body, jnp.zeros((R, LANES), F32))
            return jnp.sum(part, axis=1, keepdims=True)

        nbits = int(ltot).bit_length()

        def body(i, x):
            cand = x + lax.shift_left(jnp.int32(1), nbits - 1 - i)
            ok = count_eq_below(cand) < need
            return jnp.where(ok, cand, x)
        x = lax.fori_loop(0, nbits, body, jnp.zeros((R, 1), jnp.int32))
        return x + 1

    big = jnp.full((R, 1), 1 << 30, jnp.int32)
    bound = lax.cond(any_tie, lambda: jnp.where(tie, tie_bound(), big), lambda: big)

    def bias_body(kb, carry):
        blk = s_ref[kb]
        kidx = kb * KB + lane_k
        sel = (blk > tau) | ((blk == tau) & (kidx < bound))
        sel = sel & (blk > -jnp.inf)
        s_ref[kb] = jnp.where(sel, 0.0, NEG_BIG)
        return carry

    lax.fori_loop(0, nkb, bias_body, 0)

    q_all = q_ref[0].reshape(8 * R, LANES)
    m_ref[...] = jnp.full(m_ref.shape, NEG_BIG, F32)
    acc_ref[...] = jnp.zeros(acc_ref.shape, F32)

    def att_body(kb, carry):
        for u in range(nsb):
            off = pl.multiple_of(kb * KB + u * SB, SB)
            kblk = k_ref[0, pl.ds(off, SB), :]
            vblk = v_ref[0, pl.ds(off, SB), :]
            bias = s_ref[kb, :, u * SB:(u + 1) * SB]
            logits = _nt(q_all, kblk)
            for h in range(8):
                lg = logits[h * R:(h + 1) * R] + bias
                cm = lg[:, :LANES]
                for c in range(1, SB // LANES):
                    cm = jnp.maximum(cm, lg[:, c * LANES:(c + 1) * LANES])
                m_old = m_ref[h]
                m_new = jnp.maximum(m_old, jnp.max(cm, axis=1, keepdims=True))
                alpha = jnp.exp(m_old - m_new)
                p = jnp.exp(lg - jnp.concatenate([m_new] * (SB // LANES), axis=1))
                acc_ref[h] = jnp.concatenate([alpha, alpha], axis=1) * acc_ref[h] + _mm(p.astype(BF16), vblk)
                m_ref[h] = m_new
        return carry

    lax.fori_loop(0, nkb, att_body, 0)
    for h in range(8):
        a = acc_ref[h]
        o_ref[0, h] = (a[:, :LANES] / a[:, LANES:]).astype(BF16)


def _attention(iq, iw, q, ikb, kb, vb1, *, R, KB, SB, ltot, q_off):
    B, _, Sq, _ = q.shape
    Lp = kb.shape[1]
    topk = min(TOPK_MAX, ltot // 4)
    assert Lp % KB == 0 and KB % SB == 0 and KB >= topk and Sq % R == 0
    hspec = pl.BlockSpec((1, 8, R, LANES), lambda b, i: (b, 0, i, 0))
    kspec = pl.BlockSpec((1, Lp, LANES), lambda b, i: (b, 0, 0))
    vspec = pl.BlockSpec((1, Lp, 2 * LANES), lambda b, i: (b, 0, 0))
    kern = functools.partial(_attn_kernel, R=R, KB=KB, SB=SB, ltot=ltot, q_off=q_off, topk=topk)
    return pl.pallas_call(
        kern,
        grid=(B, Sq // R),
        in_specs=[hspec, pl.BlockSpec((1, R, 8), lambda b, i: (b, i, 0)), hspec, kspec, kspec, vspec],
        out_specs=hspec,
        out_shape=jax.ShapeDtypeStruct((B, 8, Sq, LANES), BF16),
        scratch_shapes=[pltpu.VMEM((Lp // KB, R, KB), F32), pltpu.VMEM((8, R, LANES), F32),
                        pltpu.VMEM((8, R, 2 * LANES), F32)],
        compiler_params=_params(("arbitrary", "arbitrary")),
        name="dsa_attention",
    )(iq, iw, q, ikb, kb, vb1)


def _head_sum(x):
    parts = []
    for h in range(8):
        s = jnp.sum(x[:, h * LANES:(h + 1) * LANES], axis=1, keepdims=True)
        parts.append(jnp.broadcast_to(s, (x.shape[0], LANES)))
    return jnp.concatenate(parts, axis=1)


def _rwkv_prep_kernel(z_ref, sh0_ref, mu_ref, w0_ref, wup_ref, a0_ref, aup_ref, gup_ref, kk_ref, ka_ref, rk_ref,
                      r_o, lw_o, k_o, v_o, kkn_o, b_o, g_o, bg_o, lwt_o, kt_o, bt_o, carry_ref,
                      *, tm, t_valid):
    i = pl.program_id(1)

    @pl.when(i == 0)
    def _():
        carry_ref[...] = sh0_ref[0]

    z = z_ref[0]
    row = lax.broadcasted_iota(jnp.int32, (tm, 1), 0)
    prev = jnp.where(row == 0, carry_ref[...], pltpu.roll(z, 1, axis=0))
    carry_ref[...] = z[tm - 1:tm, :]
    zm = z + (prev - z) * mu_ref[...]
    live = i * tm + row < t_valid
    zm = jnp.where(live, zm, 0.0)
    r = zm[:, 0:HP]
    k = zm[:, HP:2 * HP]
    v = zm[:, 2 * HP:3 * HP]
    wa = zm[:, 3 * HP:3 * HP + LANES]
    gd = zm[:, 3 * HP + LANES:3 * HP + 2 * LANES]
    w_raw = w0_ref[...] + _mm(jnp.tanh(wa), wup_ref[...], HI)
    lw = (-float(np.exp(-0.5))) / (1.0 + jnp.exp(-w_raw))
    lw = jnp.where(live, lw, 0.0)
    a = 1.0 / (1.0 + jnp.exp(-(a0_ref[...] + _mm(wa, aup_ref[...], HI))))
    g = _mm(1.0 / (1.0 + jnp.exp(-gd)), gup_ref[...], HI)
    kk = k * kk_ref[...]
    kk = kk * lax.rsqrt(jnp.maximum(_head_sum(kk * kk), L2_EPS))
    k_mod = k * (1.0 + (a - 1.0) * ka_ref[...])
    b = kk * a
    bonus = _head_sum(r * k_mod * rk_ref[...]) * v
    r_o[0] = r
    lw_o[0] = lw
    k_o[0] = k_mod
    v_o[0] = v
    kkn_o[0] = kk
    b_o[0] = b
    g_o[0] = g
    bg_o[0] = bonus * g
    lwt_o[0] = lw.T
    kt_o[0] = k_mod.T
    bt_o[0] = b.T


def _rwkv_prep(zr, shift0, prm, tm, t_valid):
    B, T, _ = zr.shape
    tspec = pl.BlockSpec((1, tm, HP), lambda b, i: (b, i, 0))
    fspec = pl.BlockSpec((1, HP, tm), lambda b, i: (b, 0, i))
    full = lambda a: pl.BlockSpec(a.shape, lambda b, i: (0,) * a.ndim)
    tok = jax.ShapeDtypeStruct((B, T, HP), F32)
    feat = jax.ShapeDtypeStruct((B, HP, T), F32)
    names = ("mu", "w0", "w_up", "a0", "a_up", "g_up", "k_k", "k_a", "r_k")
    ws = [prm[n] for n in names]
    return pl.pallas_call(
        functools.partial(_rwkv_prep_kernel, tm=tm, t_valid=t_valid),
        grid=(B, T // tm),
        in_specs=[pl.BlockSpec((1, tm, ZRW), lambda b, i: (b, i, 0)),
                  pl.BlockSpec((1, 1, ZRW), lambda b, i: (b, 0, 0))] + [full(w) for w in ws],
        out_specs=[tspec] * 8 + [fspec] * 3,
        out_shape=[tok] * 8 + [feat] * 3,
        scratch_shapes=[pltpu.VMEM((1, ZRW), F32)],
        compiler_params=_params(("arbitrary", "arbitrary")),
        name="rwkv_prep",
    )(zr, shift0, *ws)


def _rwkv_chunk_kernel(r_ref, lw_ref, k_ref, v_ref, kk_ref, b_ref, lwt_ref, kt_ref, bt_ref, a1_ref, a2_ref,
                       *, C, hb, pg, pc, po):
    ri = lax.broadcasted_iota(jnp.int32, (C, C), 0)
    ci = lax.broadcasted_iota(jnp.int32, (C, C), 1)
    low_incl = jnp.where(ri >= ci, 1.0, 0.0).astype(BF16)
    up_incl = jnp.where(ri <= ci, 1.0, 0.0).astype(BF16)
    strict = ri > ci
    incl = ri >= ci
    eye64 = jnp.where(lax.broadcasted_iota(jnp.int32, (64, 64), 0) == lax.broadcasted_iota(jnp.int32, (64, 64), 1),
                      1.0, 0.0)
    sls = [slice(u * LANES, (u + 1) * LANES) for u in range(hb)]
    each = lambda f, *ls: [f(*a) for a in zip(*ls)]
    lw = [lw_ref[0, :, sl] for sl in sls]
    r = [r_ref[0, :, sl] for sl in sls]
    k = [k_ref[0, :, sl] for sl in sls]
    v = [v_ref[0, :, sl] for sl in sls]
    kk = [kk_ref[0, :, sl] for sl in sls]
    b = [b_ref[0, :, sl] for sl in sls]
    lwt = [lwt_ref[0, sl, :][:64] for sl in sls]
    kt = [kt_ref[0, sl, :][:64] for sl in sls]
    bt = [bt_ref[0, sl, :][:64] for sl in sls]
    cum = each(lambda x: sum(_mm(low_incl, p) for p in _split_bf16(x, 3)), lw)
    cumt = each(lambda x: sum(_mm(p, up_incl) for p in _split_bf16(x, 3)), lwt)
    e_neg = each(lambda c: jnp.exp(-c), cum)
    at = each(lambda kk_, c, l: -kk_ * jnp.exp(c - l), kk, cum, lw)
    rt = each(lambda r_, c: r_ * jnp.exp(c), r, cum)
    g = each(lambda a, r_, b_, k_, e: _mmp(jnp.concatenate([a, r_], axis=0),
                                           jnp.concatenate([b_ * e, k_ * e], axis=0), pg, nt=True),
             at, rt, b, k, e_neg)
    n = each(lambda g_: jnp.where(strict, g_[:C, :C], 0.0), g)
    aak = each(lambda g_: jnp.where(strict, g_[:C, C:], 0.0), g)
    lrb = each(lambda g_: jnp.where(incl, g_[C:, :C], 0.0), g)
    lrk = each(lambda g_: jnp.where(incl, g_[C:, C:], 0.0), g)
    blk8 = lax.shift_right_logical(ri, 3) == lax.shift_right_logical(ci, 3)
    eye = jnp.where(ri == ci, 1.0, 0.0)
    n0 = each(lambda n_: jnp.where(blk8, n_, 0.0), n)
    x = each(lambda n_: eye + n_, n0)
    p = each(lambda n_: _mmp(n_, n_, pc), n0)
    x = each(lambda x_, p_: x_ + _mmp(x_, p_, pc), x, p)
    p = each(lambda p_: _mmp(p_, p_, pc), p)
    x = each(lambda x_, p_: x_ + _mmp(x_, p_, pc), x, p)
    for lv in range(3, int(C).bit_length() - 1):
        off = (lax.shift_right_logical(ri, lv + 1) == lax.shift_right_logical(ci, lv + 1)) & (
            lax.shift_right_logical(ri, lv) != lax.shift_right_logical(ci, lv))
        t = each(lambda n_, x_: _mmp(jnp.where(off, n_, 0.0), x_, pc), n, x)
        x = each(lambda x_, t_: x_ + _mmp(x_, t_, pc), x, t)
    z = each(lambda x_, a, b_: _mmp(x_, jnp.concatenate([a, b_], axis=1), pc), x, aak, at)
    w2v = each(lambda z_, v_: _mmp(z_[:, :C], v_, po), z, v)
    cl = each(lambda c: c[:, C - 1:C], cumt)
    eb = each(lambda c_, ct: jnp.exp(c_ - ct), cl, cumt)
    lhs = each(lambda bt_, kt_, e, lb, lk: jnp.concatenate(
        [jnp.concatenate([bt_ * e, kt_ * e], axis=1), jnp.concatenate([lb, lk], axis=1)], axis=0),
        bt, kt, eb, lrb, lrk)
    left = each(lambda l_, z_: _mmp(l_[:, :C], z_[:, C:], po), lhs, z)
    right = each(lambda l_, w, v_: _mmp(l_, jnp.concatenate([w, v_], axis=0), po), lhs, w2v, v)
    for u in range(hb):
        diag = eye64 * jnp.exp(cl[u])
        a1_ref[0, u, 0] = jnp.concatenate([left[u][:64, :64] + diag, (left[u][64:] + rt[u])[:, :64]], axis=0)
        a2_ref[0, u, 0] = right[u]


def _rwkv_chunks(r, lw, k, v, kk, b, lwt, kt, bt, C, hb, passes=RW_PASSES):
    B, T, _ = r.shape
    nc = T // C
    tspec = pl.BlockSpec((1, C, hb * LANES), lambda bb, h, c: (bb, c, h))
    fspec = pl.BlockSpec((1, hb * LANES, C), lambda bb, h, c: (bb, h, c))
    pg, pc, po = passes
    return pl.pallas_call(
        functools.partial(_rwkv_chunk_kernel, C=C, hb=hb, pg=pg, pc=pc, po=po),
        grid=(B, 8 // hb, nc),
        in_specs=[tspec] * 6 + [fspec] * 3,
        out_specs=[pl.BlockSpec((1, hb, 1, 64 + C, 64), lambda bb, h, c: (bb, h, c, 0, 0)),
                   pl.BlockSpec((1, hb, 1, 64 + C, LANES), lambda bb, h, c: (bb, h, c, 0, 0))],
        out_shape=[jax.ShapeDtypeStruct((B, 8, nc, 64 + C, 64), F32),
                   jax.ShapeDtypeStruct((B, 8, nc, 64 + C, LANES), F32)],
        compiler_params=_params(("arbitrary", "arbitrary", "arbitrary")),
        name="rwkv_chunks",
    )(r, lw, k, v, kk, b, lwt, kt, bt)


def _rwkv_scan_kernel(a1_ref, a2_ref, s0_ref, y_ref, sT_ref, h_ref, *, C, cb):
    c = pl.program_id(1)

    @pl.when(c == 0)
    def _():
        h_ref[...] = s0_ref[0]

    for cc in range(cb):
        for h in range(8):
            res = _mm(a1_ref[0, h, cc], h_ref[h], HI) + a2_ref[0, h, cc]
            h_ref[h] = res[:64]
            y_ref[0, cc * C:(cc + 1) * C, h * LANES:(h + 1) * LANES] = res[64:]

    @pl.when(c == pl.num_programs(1) - 1)
    def _():
        sT_ref[0] = h_ref[...]


def _rwkv_scan(a1, a2, s0t, C, cb):
    B, _, nc, _, _ = a1.shape
    return pl.pallas_call(
        functools.partial(_rwkv_scan_kernel, C=C, cb=cb),
        grid=(B, nc // cb),
        in_specs=[pl.BlockSpec((1, 8, cb, 64 + C, 64), lambda b, c: (b, 0, c, 0, 0)),
                  pl.BlockSpec((1, 8, cb, 64 + C, LANES), lambda b, c: (b, 0, c, 0, 0)),
                  pl.BlockSpec((1, 8, 64, LANES), lambda b, c: (b, 0, 0, 0))],
        out_specs=[pl.BlockSpec((1, cb * C, HP), lambda b, c: (b, c, 0)),
                   pl.BlockSpec((1, 8, 64, LANES), lambda b, c: (b, 0, 0, 0))],
        out_shape=[jax.ShapeDtypeStruct((B, nc * C, HP), F32),
                   jax.ShapeDtypeStruct((B, 8, 64, LANES), F32)],
        scratch_shapes=[pltpu.VMEM((8, 64, LANES), F32)],
        compiler_params=_params(("arbitrary", "arbitrary")),
        name="rwkv_scan",
    )(a1, a2, s0t)


def _mix_kernel(att_ref, y_ref, g_ref, bg_ref, sga_ref, sgb_ref, x_ref, gt1_ref, sc2_ref, sh2_ref,
                woa_ref, wor_ref, wout_ref, lnw_ref, lnb_ref, gpost_ref, gpre_ref, wr_ref, br_ref,
                x1_ref, h2_ref, gate_ref, eidx_ref, wgt_ref, rel_ref, cnt_out_ref, cnt_ref, *, tm):
    att = jnp.concatenate([att_ref[0, h] for h in range(8)], axis=1)
    lane = lax.broadcasted_iota(jnp.int32, (tm, LANES), 1)
    valid = lane < RWKV_HEAD
    parts = []
    for h in range(8):
        y = y_ref[0, :, h * LANES:(h + 1) * LANES]
        mu = jnp.sum(y, axis=1, keepdims=True) * (1.0 / RWKV_HEAD)
        d = jnp.where(valid, y - mu, 0.0)
        var = jnp.sum(d * d, axis=1, keepdims=True) * (1.0 / RWKV_HEAD)
        parts.append(d * lax.rsqrt(var + GN_EPS))
    yn = jnp.concatenate(parts, axis=1)
    rw = (yn * lnw_ref[...] + lnb_ref[...]) * g_ref[0] + bg_ref[0]
    mix = sga_ref[0] * _mm(att, woa_ref[...]) + sgb_ref[0] * _mm(rw.astype(BF16), wor_ref[...])
    o = _mm(mix.astype(BF16), wout_ref[...])
    x1 = x_ref[0] + gt1_ref[0] * _rms(o, gpost_ref[...])
    x1_ref[0] = x1
    h2 = _rms(x1, gpre_ref[...]) * (1.0 + sc2_ref[0]) + sh2_ref[0]
    h2_ref[0] = h2
    logits = _mm(h2, wr_ref[...], HI) + br_ref[...]
    el = lax.broadcasted_iota(jnp.int32, logits.shape, 1).astype(F32)
    work = logits
    sel = jnp.zeros(logits.shape, jnp.bool_)
    vmax = None
    firsts = []
    for kk in range(TOP_K):
        mx = jnp.max(work, axis=1, keepdims=True)
        if kk == 0:
            vmax = mx
        first = jnp.min(jnp.where(work == mx, el, float(N_EXPERTS)), axis=1, keepdims=True)
        hit = el == first
        firsts.append(first)
        sel = sel | hit
        work = jnp.where(hit, -jnp.inf, work)
    e = jnp.where(sel, jnp.exp(logits - vmax), 0.0)
    gate = e / jnp.sum(e, axis=1, keepdims=True)
    gate_ref[0] = gate
    @pl.when((pl.program_id(0) == 0) & (pl.program_id(1) == 0))
    def _():
        cnt_ref[...] = jnp.zeros(cnt_ref.shape, F32)

    self = jnp.where(sel, 1.0, 0.0)
    ti = lax.broadcasted_iota(jnp.int32, (tm, tm), 0)
    tj = lax.broadcasted_iota(jnp.int32, (tm, tm), 1)
    before = _mm(jnp.where(tj < ti, 1.0, 0.0).astype(BF16), self.astype(BF16)) + cnt_ref[...]
    cnt_ref[...] = cnt_ref[...] + jnp.sum(self, axis=0, keepdims=True)
    cnt_out_ref[...] = cnt_ref[...]
    ln = lax.broadcasted_iota(jnp.int32, (tm, LANES), 1)
    eo = jnp.zeros((tm, LANES), F32)
    go = jnp.zeros((tm, LANES), F32)
    ro = jnp.zeros((tm, LANES), F32)
    for kk in range(TOP_K):
        hit = el == firsts[kk]
        gk = jnp.sum(jnp.where(hit, gate, 0.0), axis=1, keepdims=True)
        rk = jnp.sum(jnp.where(hit, before, 0.0), axis=1, keepdims=True)
        eo = jnp.where(ln == kk, firsts[kk], eo)
        go = jnp.where(ln == kk, gk, go)
        ro = jnp.where(ln == kk, rk, ro)
    eidx_ref[0] = eo.astype(jnp.int32)
    wgt_ref[0] = go
    rel_ref[0] = ro.astype(jnp.int32)


def _mix(att, y, g, bg, sga, sgb, x, gt1, sc2, sh2, wts, tm):
    B, T, D = x.shape
    tspec = lambda n: pl.BlockSpec((1, tm, n), lambda b, i: (b, i, 0))
    full = lambda a: pl.BlockSpec(a.shape, lambda b, i: (0,) * a.ndim)
    tok = lambda n, dt: jax.ShapeDtypeStruct((B, T, n), dt)
    return pl.pallas_call(
        functools.partial(_mix_kernel, tm=tm),
        grid=(B, T // tm),
        in_specs=[pl.BlockSpec((1, 8, tm, LANES), lambda b, i: (b, 0, i, 0)), tspec(HP), tspec(HP), tspec(HP),
                  tspec(D), tspec(D), tspec(D), _mod_spec(gt1, tm), _mod_spec(sc2, tm), _mod_spec(sh2, tm)]
                 + [full(w) for w in wts],
        out_specs=[tspec(D), tspec(D), tspec(N_EXPERTS), tspec(LANES), tspec(LANES), tspec(LANES),
                   pl.BlockSpec((1, N_EXPERTS), lambda b, i: (0, 0))],
        out_shape=[tok(D, F32), tok(D, F32), tok(N_EXPERTS, F32), tok(LANES, jnp.int32), tok(LANES, F32),
                   tok(LANES, jnp.int32), jax.ShapeDtypeStruct((1, N_EXPERTS), F32)],
        scratch_shapes=[pltpu.VMEM((1, N_EXPERTS), F32)],
        compiler_params=_params(("arbitrary", "arbitrary")),
        name="mix_router",
    )(att, y, g, bg, sga, sgb, x, gt1, sc2, sh2, *wts)


def _expert(xb, wgu, bgu, wd, bd, d_ff):
    hgu = _mm(xb, wgu) + bgu
    hg = jnp.minimum(hgu[:, :d_ff], SWIGLU_LIMIT)
    hl = jnp.clip(hgu[:, d_ff:], -SWIGLU_LIMIT, SWIGLU_LIMIT)
    act = hg * (1.0 / (1.0 + jnp.exp(-SWIGLU_ALPHA * hg))) * (hl + 1.0)
    return _mm(act.astype(BF16), wd) + bd


def _sc_mesh():
    return plsc.VectorSubcoreMesh(core_axis_name="core", subcore_axis_name="subcore")


def _sc_dispatch(h, pos_slots, n_rows, q):
    N, D = h.shape
    mesh = _sc_mesh()
    steps = N // (mesh.num_cores * SC_WINDOW)
    assert steps * mesh.num_cores * SC_WINDOW == N

    @pl.kernel(out_type=jax.ShapeDtypeStruct((n_rows, SC_PIECE), h.dtype), mesh=mesh, scratch_types=[])
    def scatter(x_hbm, *refs):
        idx_hbm, o_hbm = refs[:-1], refs[-1]
        base = lax.axis_index("core") * steps

        def body(x_vmem, *i_vmem):
            for iv in i_vmem:
                pltpu.sync_copy(x_vmem, o_hbm.at[iv.at[0]])

        pltpu.emit_pipeline(
            body,
            grid=(steps,),
            in_specs=[pl.BlockSpec((SC_WINDOW, SC_PIECE), index_map=lambda i: (base + i, q))]
                     + [pl.BlockSpec((1, SC_WINDOW), index_map=lambda i: (0, base + i))] * len(idx_hbm),
            out_specs=[],
            core_axis_name="subcore",
            dimension_semantics=(pltpu.PARALLEL,),
        )(x_hbm, *idx_hbm)

    return scatter(h, *pos_slots)


def _sc_gather(table, idx):
    M = idx.shape[1]
    mesh = _sc_mesh()
    steps = M // (mesh.num_cores * SC_WINDOW)
    assert steps * mesh.num_cores * SC_WINDOW == M

    @pl.kernel(out_type=jax.ShapeDtypeStruct((M, SC_PIECE), table.dtype), mesh=mesh)
    def gather(x_hbm, i_hbm, o_hbm):
        base = lax.axis_index("core") * steps

        def body(i_vmem, o_vmem):
            pltpu.sync_copy(x_hbm.at[i_vmem.at[0]], o_vmem)

        pltpu.emit_pipeline(
            body,
            grid=(steps,),
            in_specs=[pl.BlockSpec((1, SC_WINDOW), index_map=lambda i: (0, base + i))],
            out_specs=[pl.BlockSpec((SC_WINDOW, SC_PIECE), index_map=lambda i: (base + i, 0))],
            core_axis_name="subcore",
            dimension_semantics=(pltpu.PARALLEL,),
        )(i_hbm, o_hbm)

    return gather(table, idx)


def _moe_grouped_kernel(te_ref, nt_ref, *refs, d_ff, npiece):
    x_refs, (wgu_ref, bgu_ref, wd_ref, bd_ref), o_refs = refs[:npiece], refs[npiece:npiece + 4], refs[npiece + 4:]

    @pl.when(pl.program_id(0) < nt_ref[0])
    def _():
        xb = jnp.concatenate([r[...] for r in x_refs], axis=1).astype(BF16)
        y = _expert(xb, wgu_ref[0], bgu_ref[0], wd_ref[0], bd_ref[0], d_ff)
        for q, o_ref in enumerate(o_refs):
            o_ref[...] = y[:, q * SC_PIECE:(q + 1) * SC_PIECE]


def _moe_grouped(xs, te, nt, wgu, bgu, wd, bd, TG):
    NP = xs[0].shape[0]
    E, D, F2 = wgu.shape
    npiece = len(xs)
    wmap = lambda g, te_, nt_: (te_[g], 0, 0)
    xmap = lambda g, te_, nt_: (jnp.minimum(g, nt_[0] - 1), 0)
    pspec = pl.BlockSpec((TG, SC_PIECE), xmap)
    return pl.pallas_call(
        functools.partial(_moe_grouped_kernel, d_ff=F2 // 2, npiece=npiece),
        grid_spec=pltpu.PrefetchScalarGridSpec(
            num_scalar_prefetch=2,
            grid=(NP // TG,),
            in_specs=[pspec] * npiece + [pl.BlockSpec((1, D, F2), wmap), pl.BlockSpec((1, 1, F2), wmap),
                                         pl.BlockSpec((1, F2 // 2, D), wmap), pl.BlockSpec((1, 1, D), wmap)],
            out_specs=[pspec] * npiece),
        out_shape=[jax.ShapeDtypeStruct((NP, SC_PIECE), F32)] * npiece,
        compiler_params=_params(("arbitrary",)),
        name="moe_grouped",
    )(te, nt, *xs, wgu, bgu, wd, bd)


def _moe_combine_kernel(*refs, npiece):
    y_refs = refs[:TOP_K * npiece]
    w_ref, x1_ref, gt2_ref, gpost_ref, o_ref = refs[TOP_K * npiece:]
    w = w_ref[...]
    acc = None
    for k in range(TOP_K):
        yk = jnp.concatenate([y_refs[k * npiece + q][...] for q in range(npiece)], axis=1)
        term = w[:, k:k + 1] * yk
        acc = term if acc is None else acc + term
    o_ref[...] = x1_ref[...] + gt2_ref[0] * _rms(acc, gpost_ref[...])


def _moe_combine(yslot, wgt, x1, gt2, gpost, tm):
    N, D = x1.shape
    nb = N // tm
    tpb = N // gt2.shape[0]
    npiece = len(yslot)
    yspec = lambda k: pl.BlockSpec((tm, SC_PIECE), lambda i, k=k: (i + k * nb, 0))
    return pl.pallas_call(
        functools.partial(_moe_combine_kernel, npiece=npiece),
        grid=(nb,),
        in_specs=[yspec(k) for k in range(TOP_K) for _ in range(npiece)]
                 + [pl.BlockSpec((tm, LANES), lambda i: (i, 0)), pl.BlockSpec((tm, D), lambda i: (i, 0)),
                    pl.BlockSpec((1, 1, D), lambda i: ((i * tm) // tpb, 0, 0)), pl.BlockSpec((1, D), lambda i: (0, 0))],
        out_specs=pl.BlockSpec((tm, D), lambda i: (i, 0)),
        out_shape=jax.ShapeDtypeStruct((N, D), F32),
        compiler_params=_params(("arbitrary",)),
        name="moe_combine",
    )(*[yslot[q] for _ in range(TOP_K) for q in range(npiece)], wgt, x1, gt2, gpost)


def _moe_kernel(h_ref, gate_ref, x1_ref, gt2_ref, gpost_ref, wgu_ref, bgu_ref, wd_ref, bd_ref, o_ref, acc_ref,
                *, d_ff):
    e = pl.program_id(1)

    @pl.when(e == 0)
    def _():
        acc_ref[...] = jnp.zeros(acc_ref.shape, F32)

    contrib = _expert(h_ref[...].astype(BF16), wgu_ref[0], bgu_ref[0], wd_ref[0], bd_ref[0], d_ff)
    gate = gate_ref[...]
    el = lax.broadcasted_iota(jnp.int32, gate.shape, 1)
    ge = jnp.sum(jnp.where(el == e, gate, 0.0), axis=1, keepdims=True)
    acc_ref[...] += ge * contrib

    @pl.when(e == pl.num_programs(1) - 1)
    def _():
        o_ref[...] = x1_ref[...] + gt2_ref[0] * _rms(acc_ref[...], gpost_ref[...])


def _moe(h2, gate, x1, gt2, gpost, wgu, bgu, wd, bd, tm):
    N, D = h2.shape
    E, _, F2 = wgu.shape
    tspec = lambda n: pl.BlockSpec((tm, n), lambda i, e: (i, 0))
    if gt2.shape[1] == 1:
        tpb = N // gt2.shape[0]
        gspec = pl.BlockSpec((1, 1, D), lambda i, e: ((i * tm) // tpb, 0, 0))
    else:
        gspec = pl.BlockSpec((1, tm, D), lambda i, e: (i, 0, 0))
    return pl.pallas_call(
        functools.partial(_moe_kernel, d_ff=F2 // 2),
        grid=(N // tm, E),
        in_specs=[tspec(D), tspec(E), tspec(D), gspec, pl.BlockSpec((1, D), lambda i, e: (0, 0)),
                  pl.BlockSpec((1, D, F2), lambda i, e: (e, 0, 0)), pl.BlockSpec((1, 1, F2), lambda i, e: (e, 0, 0)),
                  pl.BlockSpec((1, F2 // 2, D), lambda i, e: (e, 0, 0)), pl.BlockSpec((1, 1, D), lambda i, e: (e, 0, 0))],
        out_specs=tspec(D),
        out_shape=jax.ShapeDtypeStruct((N, D), F32),
        scratch_shapes=[pltpu.VMEM((tm, D), F32)],
        compiler_params=_params(("arbitrary", "arbitrary")),
        name="moe",
    )(h2, gate, x1, gt2, gpost, wgu, bgu, wd, bd)


def _per_token(m, T):
    B, _, D = m.shape
    return jnp.broadcast_to(m, (B, T, D)).reshape(1, B * T, D)


def _layer(x, mods, past, s0, shift0, P, cfg):
    B, T, D = x.shape
    sh1, sc1, gt1, sh2, sc2, gt2 = mods
    tm = cfg["tm"]

    q, k, v, kb, vb, iq, ik, ikb, iw = _proj_a(x, sc1, sh1, P["g_pre_mix"], P["w_a"], P["ik_ln_w"], P["ik_ln_b"], tm)
    zr, sga, sgb = _proj_b(x, sc1, sh1, P["g_pre_mix"], P["w_b"], tm)

    if past is not None:
        pk, pv, pik = past
        plen = pk.shape[1]
        kb_all = jnp.concatenate([pk.reshape(B, plen, 128).astype(BF16), kb], axis=1)
        vb_all = jnp.concatenate([pv.reshape(B, plen, 128).astype(BF16), vb], axis=1)
        ik_all = jnp.concatenate([jnp.pad(pik, ((0, 0), (0, 0), (0, 64))).astype(BF16), ikb], axis=1)
    else:
        plen = 0
        kb_all, vb_all, ik_all = kb, vb, ikb
    ltot = plen + T
    KB = cfg["KB"]
    lp = -(-ltot // KB) * KB
    if lp != ltot:
        padk = ((0, 0), (0, lp - ltot), (0, 0))
        kb_all, vb_all, ik_all = jnp.pad(kb_all, padk), jnp.pad(vb_all, padk), jnp.pad(ik_all, padk)
    vb1 = jnp.concatenate([vb_all, jnp.ones_like(vb_all)], axis=-1)
    att = _attention(iq, iw, q, ik_all, kb_all, vb1, R=cfg["R"], KB=KB, SB=cfg["SB"], ltot=ltot, q_off=plen)

    C = RW_CHUNK
    tp = -(-T // C) * C
    zr_p = zr if tp == T else jnp.pad(zr, ((0, 0), (0, tp - T), (0, 0)))
    shift_pad = _take_cols(shift0.reshape(B, RWKV_COLS), _COLS_ZR).reshape(B, 1, ZRW)
    r, lw, km, vv, kkn, bb, g, bg, lwt, kt, bt = _rwkv_prep(zr_p, shift_pad, P, min(cfg["tm_rw"], tp), T)
    a1, a2 = _rwkv_chunks(r, lw, km, vv, kkn, bb, lwt, kt, bt, C, cfg["hb"])
    s0t = jnp.pad(jnp.swapaxes(s0, 2, 3), ((0, 0), (0, 0), (0, 0), (0, 64)))
    nc = tp // C
    cb = min(cfg["cb"], nc)
    y, sT = _rwkv_scan(a1, a2, s0t, C, cb)
    s_new = jnp.swapaxes(sT[..., :64], 2, 3)
    zlast = zr[:, T - 1]
    inv = np.zeros((RWKV_COLS,), np.int32)
    inv[_COLS_ZR[0][_COLS_ZR[1]]] = np.nonzero(_COLS_ZR[1])[0]
    shift_new = jnp.take(zlast, jnp.asarray(inv), axis=1).reshape(B, 1, RWKV_COLS)
    if tp != T:
        y, g, bg = y[:, :T], g[:, :T], bg[:, :T]

    wts = [P[n] for n in ("w_o_att", "w_o_rwkv", "w_out", "ln_x_w", "ln_x_b", "g_post_mix", "g_pre_ffn",
                          "w_router", "b_router")]
    x1, h2, gate, eidx, wgt, rel, cnt = _mix(att, y, g, bg, sga, sgb, x, gt1, sc2, sh2, wts, tm)

    N = B * T
    tmm = min(cfg["tm_moe"], N)
    if cfg["routed"]:
        TG = cfg["TG"]
        ntm = N * TOP_K // TG + N_EXPERTS
        cnt_i = cnt[0].astype(jnp.int32)
        padded = (cnt_i + TG - 1) // TG * TG
        ends = jnp.cumsum(padded)
        off = ends - padded
        nt = ends[-1] // TG
        gi = jnp.minimum(jnp.arange(ntm, dtype=jnp.int32), nt - 1)
        te = jnp.sum((ends // TG)[None, :] <= gi[:, None], axis=1).astype(jnp.int32)
        e4 = eidx.reshape(N, LANES)[:, :TOP_K]
        pos = jnp.take(off, e4) + rel.reshape(N, LANES)[:, :TOP_K]
        pos_t = pos.T
        npiece = D // SC_PIECE
        h2f = h2.reshape(N, D)
        xs = [_sc_dispatch(h2f, [pos_t[k:k + 1] for k in range(TOP_K)], ntm * TG, q) for q in range(npiece)]
        ys = _moe_grouped(xs, te, nt.reshape(1), P["w_gu"], P["b_gu"], P["w_down"], P["b_down"], TG)
        idx_c = pos_t.reshape(1, TOP_K * N)
        yslot = [_sc_gather(ys[q], idx_c) for q in range(npiece)]
        out = _moe_combine(yslot, wgt.reshape(N, LANES), x1.reshape(N, D), gt2, P["g_post_ffn"], tm)
    else:
        if T % tmm == 0:
            gt2m = gt2
        else:
            gt2m = _per_token(gt2, T).reshape(N // tmm, tmm, D)
        out = _moe(h2.reshape(N, D), gate.reshape(N, N_EXPERTS), x1.reshape(N, D), gt2m, P["g_post_ffn"],
                   P["w_gu"], P["b_gu"], P["w_down"], P["b_down"], tmm)
    return (out.reshape(B, T, D), k.reshape(B, T, ATT_KV_HEADS, HEAD_DIM), v.reshape(B, T, ATT_KV_HEADS, HEAD_DIM),
            ik, s_new, shift_new)


def _prep_weights(l, w_in, ik_ln_w, ik_ln_b, mu_rwkv, w0, w_up, a0, a_up, g_up, k_k, k_a, r_k, ln_x_w, ln_x_b,
                  w_o_att, w_o_rwkv, w_out, w_router, b_router, w_gu, b_gu, w_down, b_down,
                  g_pre_mix, g_post_mix, g_pre_ffn, g_post_ffn):
    row = lambda a: a.reshape(1, -1)
    hp = lambda a: _take_cols(row(a), _HEAD_IDX)
    P = {}
    P["w_a"] = _take_cols(w_in[l], _COLS_A).astype(BF16)
    P["w_b"] = _take_cols(w_in[l], _COLS_B).astype(BF16)
    P["ik_ln_w"] = jnp.pad(row(ik_ln_w[l]), ((0, 0), (0, 64)))
    P["ik_ln_b"] = jnp.pad(row(ik_ln_b[l]), ((0, 0), (0, 64)))
    P["mu"] = _take_cols(row(mu_rwkv[l]), _COLS_ZR)
    P["w0"], P["a0"], P["k_k"], P["k_a"] = hp(w0[l]), hp(a0[l]), hp(k_k[l]), hp(k_a[l])
    P["r_k"] = hp(r_k[l].reshape(-1))
    P["ln_x_w"], P["ln_x_b"] = hp(ln_x_w[l]), hp(ln_x_b[l])
    z64 = jnp.zeros((64, HP), F32)
    P["w_up"] = jnp.concatenate([_take_cols(w_up[l], _HEAD_IDX), z64], axis=0)
    P["a_up"] = jnp.concatenate([z64, _take_cols(a_up[l], _HEAD_IDX)], axis=0)
    P["g_up"] = _take_cols(g_up[l], _HEAD_IDX)
    P["w_o_att"] = _take_rows(w_o_att[l], _ATT_ROW_IDX).astype(BF16)
    P["w_o_rwkv"] = _take_rows(w_o_rwkv[l], _HEAD_IDX).astype(BF16)
    P["w_out"] = w_out[l].astype(BF16)
    P["w_router"] = w_router[l]
    P["b_router"] = row(b_router[l])
    P["w_gu"] = w_gu[l].astype(BF16)
    P["b_gu"] = b_gu[l].reshape(N_EXPERTS, 1, -1)
    P["w_down"] = w_down[l].astype(BF16)
    P["b_down"] = b_down[l].reshape(N_EXPERTS, 1, -1)
    P["g_pre_mix"], P["g_post_mix"] = row(g_pre_mix[l]), row(g_post_mix[l])
    P["g_pre_ffn"], P["g_post_ffn"] = row(g_pre_ffn[l]), row(g_post_ffn[l])
    return P


CFG_PROMPT = dict(tm=256, R=128, KB=512, SB=256, tm_rw=256, hb=8, cb=2, tm_moe=1024, routed=True, TG=512)
CFG_SAMPLE = dict(tm=32, R=32, KB=512, SB=256, tm_rw=128, hb=8, cb=1, tm_moe=512, routed=False)


def kernel(x_prompt, x_sample, c_prompt, c_sample, cache_k, cache_v, cache_idx_k, state_rwkv, state_shift, w_ada, b_ada, g_pre_mix, g_post_mix, g_pre_ffn, g_post_ffn, w_in, ik_ln_w, ik_ln_b, mu_rwkv, w0, w_up, a0, a_up, g_up, k_k, k_a, r_k, ln_x_w, ln_x_b, w_o_att, w_o_rwkv, w_out, w_router, b_router, w_gu, b_gu, w_down, b_down):
    depth = w_in.shape[0]
    bp, tp_, D = x_prompt.shape
    bs, ts, _ = x_sample.shape
    y_p, y_s = x_prompt, x_sample
    st_p = [[] for _ in range(5)]
    st_s = [[] for _ in range(5)]
    nc_all = bp + bs
    npad = -(-nc_all // 8) * 8
    c_all = jnp.pad(jnp.concatenate([c_prompt, c_sample], axis=0), ((0, npad - nc_all), (0, 0)))
    for l in range(depth):
        P = _prep_weights(l, w_in, ik_ln_w, ik_ln_b, mu_rwkv, w0, w_up, a0, a_up, g_up, k_k, k_a, r_k, ln_x_w,
                          ln_x_b, w_o_att, w_o_rwkv, w_out, w_router, b_router, w_gu, b_gu, w_down, b_down,
                          g_pre_mix, g_post_mix, g_pre_ffn, g_post_ffn)
        ada = _ada(c_all, w_ada[l], b_ada[l])
        mods_p = [m[:bp, None, :] for m in jnp.split(ada, 6, axis=-1)]
        mods_s = [m[bp:nc_all, None, :] for m in jnp.split(ada, 6, axis=-1)]
        zero_state = jnp.zeros((bp, RWKV_HEADS, RWKV_HEAD, RWKV_HEAD), F32)
        zero_shift = jnp.zeros((bp, 1, RWKV_COLS), F32)
        outs_p = _layer(y_p, mods_p, None, zero_state, zero_shift, P, CFG_PROMPT)
        outs_s = _layer(y_s, mods_s, (cache_k[l], cache_v[l], cache_idx_k[l]), state_rwkv[l], state_shift[l], P,
                        CFG_SAMPLE)
        y_p, y_s = outs_p[0], outs_s[0]
        for lst, val in zip(st_p, outs_p[1:]):
            lst.append(val)
        for lst, val in zip(st_s, outs_s[1:]):
            lst.append(val)
    sp = [jnp.stack(v, axis=0) for v in st_p]
    ss = [jnp.stack(v, axis=0) for v in st_s]
    return (y_p, y_s, sp[0], sp[1], sp[2], sp[3], sp[4], ss[0], ss[1], ss[2], ss[3], ss[4])
```

```python
import functools

import numpy as np
import jax
import jax.numpy as jnp
from jax import lax
from jax.experimental import pallas as pl
from jax.experimental.pallas import tpu as pltpu
from jax.experimental.pallas import tpu_sc as plsc

F32 = jnp.float32
BF16 = jnp.bfloat16
HI = lax.Precision.HIGHEST

CHUNK = 64
ATT_HEADS = 8
ATT_KV_HEADS = 2
HEAD_DIM = 64
IDX_HEADS = 8
IDX_DIM = 64
IDX_SCALE = (IDX_DIM ** -0.5) * (IDX_HEADS ** -0.5)
TOPK_MAX = 256
RWKV_HEADS = 8
RWKV_HEAD = 64
RWKV_WIDTH = RWKV_HEADS * RWKV_HEAD
DECAY_LORA = 64
AAA_LORA = 64
GATE_LORA = 128
RWKV_COLS = 3 * RWKV_WIDTH + DECAY_LORA + AAA_LORA + GATE_LORA
GN_EPS = 64e-5
L2_EPS = 1e-24
N_EXPERTS = 32
TOP_K = 4
SWIGLU_LIMIT = 7.0
SWIGLU_ALPHA = 1.702
NORM_EPS = 1e-6
LN_EPS = 1e-6

LANES = 128
VMEM_LIMIT = 56 * 1024 * 1024

HP = 8 * LANES
RW_CHUNK = 128
NEG_BIG = -1e30
LOG2E = 1.4426950408889634
KEY_NEG_INF = -2139095041
INT_MIN = -2147483648
SC_WINDOW = 128
SC_PIECE = 256
RW_PASSES = (1, 1, 1)


def _params(sem):
    return pltpu.CompilerParams(dimension_semantics=sem, vmem_limit_bytes=VMEM_LIMIT)


def _nt(a, b, precision=None):
    return lax.dot_general(a, b, (((1,), (1,)), ((), ())), precision=precision,
                           preferred_element_type=F32)


def _mm(a, b, precision=None):
    return jnp.dot(a, b, precision=precision, preferred_element_type=F32)


def _split_bf16(x, terms):
    out = []
    for _ in range(terms):
        p = x.astype(BF16)
        out.append(p)
        x = x - p.astype(F32)
    return out


def _mmp(a, b, passes, nt=False):
    dot = _nt if nt else _mm
    if passes == 1:
        return dot(a.astype(BF16), b.astype(BF16))
    ah, al = _split_bf16(a, 2)
    bh, bl = _split_bf16(b, 2)
    return dot(ah, bh) + (dot(ah, bl) + dot(al, bh))


def _rms(x, g):
    return x * lax.rsqrt(jnp.mean(x * x, axis=-1, keepdims=True) + NORM_EPS) * g


def _headpad_idx(seg_off, lane_off_fn=lambda h: 0):
    idx = np.zeros((HP,), np.int32)
    ok = np.zeros((HP,), bool)
    for h in range(8):
        d0 = h * LANES + lane_off_fn(h)
        idx[d0:d0 + 64] = seg_off + h * 64 + np.arange(64)
        ok[d0:d0 + 64] = True
    return idx, ok


def _plain_idx(seg_off, n, width):
    idx = np.zeros((width,), np.int32)
    ok = np.zeros((width,), bool)
    idx[:n] = seg_off + np.arange(n)
    ok[:n] = True
    return idx, ok


def _cat(parts):
    return np.concatenate([p[0] for p in parts]), np.concatenate([p[1] for p in parts])


_O_Q, _O_K, _O_V, _O_IQ, _O_IK, _O_IW, _O_ZR, _O_GA, _O_GB = 0, 512, 640, 768, 1280, 1344, 1352, 3144, 4168
_COLS_A = _cat([_headpad_idx(_O_Q, lambda h: (h // 4) * 64), _plain_idx(_O_K, 128, 128), _plain_idx(_O_V, 128, 128),
                _headpad_idx(_O_IQ), _plain_idx(_O_IK, 64, 128), _plain_idx(_O_IW, 8, 128)])
_COLS_ZR = _cat([_headpad_idx(0), _headpad_idx(512), _headpad_idx(1024), _plain_idx(1536, 128, 128),
                 _plain_idx(1664, 128, 128)])
ZRW = _COLS_ZR[0].shape[0]
_COLS_B = _cat([(_COLS_ZR[0] + _O_ZR, _COLS_ZR[1]), _plain_idx(_O_GA, 1024, 1024), _plain_idx(_O_GB, 1024, 1024)])
NA = _COLS_A[0].shape[0]
NB = _COLS_B[0].shape[0]
_HEAD_IDX = _headpad_idx(0)
_ATT_ROW_IDX = _headpad_idx(0, lambda h: (h // 4) * 64)


def _take_cols(w, cols):
    idx, ok = cols
    return jnp.where(jnp.asarray(ok)[None, :], jnp.take(w, jnp.asarray(idx), axis=1), 0.0)


def _take_rows(w, cols):
    idx, ok = cols
    return jnp.where(jnp.asarray(ok)[:, None], jnp.take(w, jnp.asarray(idx), axis=0), 0.0)


def _mod_spec(m, tm):
    assert m.shape[1] == 1
    return pl.BlockSpec((1, 1, m.shape[2]), lambda b, i: (b, 0, 0))


def _ada_kernel(c_ref, w_ref, b_ref, o_ref):
    c = c_ref[...]
    s = c * (1.0 / (1.0 + jnp.exp(-c)))
    o_ref[...] = _mm(s, w_ref[...], HI) + b_ref[...]


def _ada(c, w, b):
    n, d = c.shape
    nout = w.shape[1]
    bn = 768
    return pl.pallas_call(
        _ada_kernel,
        grid=(nout // bn,),
        in_specs=[pl.BlockSpec((n, d), lambda j: (0, 0)),
                  pl.BlockSpec((d, bn), lambda j: (0, j)),
                  pl.BlockSpec((1, bn), lambda j: (0, j))],
        out_specs=pl.BlockSpec((n, bn), lambda j: (0, j)),
        out_shape=jax.ShapeDtypeStruct((n, nout), F32),
        compiler_params=_params(("arbitrary",)),
        name="ada",
    )(c, w, b.reshape(1, nout))


def _modulated(x_ref, sc_ref, sh_ref, g_ref):
    h = _rms(x_ref[0], g_ref[...])
    return (h * (1.0 + sc_ref[0]) + sh_ref[0]).astype(BF16)


def _proj_a_kernel(x_ref, sc_ref, sh_ref, g_ref, w_ref, lnw_ref, lnb_ref,
                   q_ref, k_ref, v_ref, kb_ref, vb_ref, iq_ref, ik_ref, ikb_ref, iw_ref):
    hb = _modulated(x_ref, sc_ref, sh_ref, g_ref)
    for h in range(8):
        zq = _mm(hb, w_ref[:, h * LANES:(h + 1) * LANES])
        q_ref[0, h] = (zq * (HEAD_DIM ** -0.5 * LOG2E)).astype(BF16)
    kv = _mm(hb, w_ref[:, 1024:1280])
    k = kv[:, :128]
    v = kv[:, 128:]
    k_ref[0] = k
    v_ref[0] = v
    kb_ref[0] = k.astype(BF16)
    vb_ref[0] = v.astype(BF16)
    for h in range(8):
        zi = _mm(hb, w_ref[:, 1280 + h * LANES:1280 + (h + 1) * LANES])
        iq_ref[0, h] = zi.astype(BF16)
    t = _mm(hb, w_ref[:, 2304:2560])
    ik = t[:, :128]
    lane = lax.broadcasted_iota(jnp.int32, ik.shape, 1)
    valid = lane < IDX_DIM
    mu = jnp.sum(ik, axis=-1, keepdims=True) * (1.0 / IDX_DIM)
    d = jnp.where(valid, ik - mu, 0.0)
    var = jnp.sum(d * d, axis=-1, keepdims=True) * (1.0 / IDX_DIM)
    ikn = d * lax.rsqrt(var + LN_EPS) * lnw_ref[...] + lnb_ref[...]
    ik_ref[0] = ikn[:, :IDX_DIM]
    ikb_ref[0] = ikn.astype(BF16)
    iw_ref[0] = t[:, 128:136] * IDX_SCALE


def _proj_a(x, sc, sh, g, w, lnw, lnb, tm):
    B, T, D = x.shape
    nt = T // tm
    tok = lambda n, dt: jax.ShapeDtypeStruct((B, T, n), dt)
    hm = lambda dt: jax.ShapeDtypeStruct((B, 8, T, LANES), dt)
    tspec = lambda n: pl.BlockSpec((1, tm, n), lambda b, i: (b, i, 0))
    hspec = pl.BlockSpec((1, 8, tm, LANES), lambda b, i: (b, 0, i, 0))
    full = lambda a: pl.BlockSpec(a.shape, lambda b, i: (0,) * a.ndim)
    return pl.pallas_call(
        _proj_a_kernel,
        grid=(B, nt),
        in_specs=[tspec(D), _mod_spec(sc, tm), _mod_spec(sh, tm), full(g), full(w), full(lnw), full(lnb)],
        out_specs=[hspec, tspec(128), tspec(128), tspec(128), tspec(128), hspec, tspec(IDX_DIM), tspec(128),
                   tspec(8)],
        out_shape=[hm(BF16), tok(128, F32), tok(128, F32), tok(128, BF16), tok(128, BF16), hm(BF16),
                   tok(IDX_DIM, F32), tok(128, BF16), tok(8, F32)],
        compiler_params=_params(("arbitrary", "arbitrary")),
        name="proj_att",
    )(x, sc, sh, g, w, lnw, lnb)


def _proj_b_kernel(x_ref, sc_ref, sh_ref, g_ref, w_ref, zr_ref, ga_ref, gb_ref):
    hb = _modulated(x_ref, sc_ref, sh_ref, g_ref)
    for j in range(ZRW // 256):
        zr_ref[0, :, j * 256:(j + 1) * 256] = _mm(hb, w_ref[:, j * 256:(j + 1) * 256])
    for j in range(4):
        za = _mm(hb, w_ref[:, ZRW + j * 256:ZRW + (j + 1) * 256])
        ga_ref[0, :, j * 256:(j + 1) * 256] = 1.0 / (1.0 + jnp.exp(-za))
        zb = _mm(hb, w_ref[:, ZRW + 1024 + j * 256:ZRW + 1024 + (j + 1) * 256])
        gb_ref[0, :, j * 256:(j + 1) * 256] = 1.0 / (1.0 + jnp.exp(-zb))


def _proj_b(x, sc, sh, g, w, tm):
    B, T, D = x.shape
    nt = T // tm
    tok = lambda n: jax.ShapeDtypeStruct((B, T, n), F32)
    tspec = lambda n: pl.BlockSpec((1, tm, n), lambda b, i: (b, i, 0))
    full = lambda a: pl.BlockSpec(a.shape, lambda b, i: (0,) * a.ndim)
    return pl.pallas_call(
        _proj_b_kernel,
        grid=(B, nt),
        in_specs=[tspec(D), _mod_spec(sc, tm), _mod_spec(sh, tm), full(g), full(w)],
        out_specs=[tspec(ZRW), tspec(1024), tspec(1024)],
        out_shape=[tok(ZRW), tok(1024), tok(1024)],
        compiler_params=_params(("arbitrary", "arbitrary")),
        name="proj_rwkv",
    )(x, sc, sh, g, w)


def _sortable_to_float(t):
    bits = t ^ (lax.shift_right_arithmetic(t, 31) & 0x7FFFFFFF)
    return lax.bitcast_convert_type(bits, F32)


def _top16(x):
    bits = lax.bitcast_convert_type(x, jnp.int32) & jnp.int32(-65536)
    return lax.bitcast_convert_type(bits, F32).astype(BF16)


def _attn_kernel(iq_ref, iw_ref, q_ref, ik_ref, k_ref, v_ref, o_ref, s_ref, sb_ref, m_ref, acc_ref,
                 *, R, KB, SB, HG, ltot, q_off, topk):
    j = pl.program_id(1)
    q0 = q_off + j * R
    pos = q0 + lax.broadcasted_iota(jnp.int32, (R, 1), 0)
    qchunk = lax.shift_right_logical(pos, 6)
    kend = jnp.minimum(ltot, (lax.shift_right_logical(q0 + R - 1, 6) + 1) * CHUNK)
    nkb = lax.shift_right_logical(kend + KB - 1, KB.bit_length() - 1)
    lane_k = lax.broadcasted_iota(jnp.int32, (R, KB), 1)
    nch = KB // LANES

    iq_all = iq_ref[0].reshape(8 * R, LANES)
    iw = iw_ref[0]
    iwb = [jnp.broadcast_to(iw[:, h:h + 1], (R, SB)) for h in range(8)]
    lane_s = lax.broadcasted_iota(jnp.int32, (R, SB), 1)
    nsb = KB // SB

    def score_body(kb, carry):
        for u in range(nsb):
            off = pl.multiple_of(kb * KB + u * SB, SB)
            ikb = ik_ref[0, pl.ds(off, SB), :]
            s_all = _nt(iq_all, ikb)
            acc = jnp.zeros((R, SB), F32)
            for h in range(8):
                acc = acc + jnp.maximum(s_all[h * R:(h + 1) * R], 0.0) * iwb[h]
            kidx = off + lane_s
            adm = (lax.shift_right_logical(kidx, 6) <= qchunk) & (kidx < ltot)
            sc = jnp.where(adm, acc, -jnp.inf)
            s_ref[kb, :, u * SB:(u + 1) * SB] = sc
            sb_ref[kb, :, u * SB:(u + 1) * SB] = _top16(sc)
        return carry

    lax.fori_loop(0, nkb, score_body, 0)

    def count_ge(ref, cand, dt):
        one, zero = jnp.ones((), dt), jnp.zeros((), dt)

        def body(kb, part):
            blk = ref[kb]
            m = jnp.where(blk >= cand, one, zero)
            for c in range(nch):
                part = part + m[:, c * LANES:(c + 1) * LANES]
            return part
        part = lax.fori_loop(0, nkb, body, jnp.zeros((R, LANES), dt))
        return jnp.sum(part.astype(F32), axis=1, keepdims=True)

    kf = float(topk)
    c0 = count_ge(sb_ref, jnp.zeros((R, 1), BF16), BF16)
    t0 = jnp.where(c0 >= kf, 0, INT_MIN).astype(jnp.int32)

    def hi_body(i, t):
        cand = t + lax.shift_left(jnp.int32(1), 30 - i)
        cnt = count_ge(sb_ref, _top16(_sortable_to_float(cand)), BF16)
        return jnp.where(cnt >= kf, cand, t)

    def lo_body(i, t):
        cand = t + lax.shift_left(jnp.int32(1), 15 - i)
        cnt = count_ge(s_ref, _sortable_to_float(cand), F32)
        return jnp.where(cnt >= kf, cand, t)

    t = lax.fori_loop(0, 15, hi_body, t0)
    t = lax.fori_loop(0, 16, lo_body, t)
    all_finite = t <= KEY_NEG_INF
    tau = jnp.where(all_finite, -jnp.inf, _sortable_to_float(jnp.maximum(t, KEY_NEG_INF)))

    def count_gt_eq(_):
        def body(kb, carry):
            pg, pe = carry
            blk = s_ref[kb]
            mg = jnp.where(blk > tau, 1.0, 0.0)
            me = jnp.where(blk == tau, 1.0, 0.0)
            for c in range(nch):
                pg = pg + mg[:, c * LANES:(c + 1) * LANES]
                pe = pe + me[:, c * LANES:(c + 1) * LANES]
            return pg, pe
        z = jnp.zeros((R, LANES), F32)
        pg, pe = lax.fori_loop(0, nkb, body, (z, z))
        return jnp.sum(pg, axis=1, keepdims=True), jnp.sum(pe, axis=1, keepdims=True)

    cnt_gt, cnt_eq = count_gt_eq(0)
    need = kf - cnt_gt
    tie = (cnt_eq > need) & jnp.logical_not(all_finite)
    any_tie = jnp.max(jnp.where(tie, 1.0, 0.0)) > 0.0

    def tie_bound():
        def count_eq_below(x):
            def body(kb, part):
                blk = s_ref[kb]
                kidx = kb * KB + lane_k
                m = jnp.where((blk == tau) & (kidx < x), 1.0, 0.0)
                for c in range(nch):
                    part = part + m[:, c * LANES:(c + 1) * LANES]
                return part
            part = lax.fori_loop(0, nkb, body, jnp.zeros((R, LANES), F32))
            return jnp.sum(part, axis=1, keepdims=True)

        nbits = int(ltot).bit_length()

        def body(i, x):
            cand = x + lax.shift_left(jnp.int32(1), nbits - 1 - i)
            ok = count_eq_below(cand) < need
            return jnp.where(ok, cand, x)
        x = lax.fori_loop(0, nbits, body, jnp.zeros((R, 1), jnp.int32))
        return x + 1

    big = jnp.full((R, 1), 1 << 30, jnp.int32)
    bound = lax.cond(any_tie, lambda: jnp.where(tie, tie_bound(), big), lambda: big)

    def bias_body(kb, carry):
        blk = s_ref[kb]
        kidx = kb * KB + lane_k
        sel = (blk > tau) | ((blk == tau) & (kidx < bound))
        sel = sel & (blk > -jnp.inf)
        s_ref[kb] = jnp.where(sel, 0.0, NEG_BIG)
        return carry

    lax.fori_loop(0, nkb, bias_body, 0)

    q_all = q_ref[0].reshape(8 * R, LANES)
    m_ref[...] = jnp.full(m_ref.shape, NEG_BIG, F32)
    acc_ref[...] = jnp.zeros(acc_ref.shape, F32)

    def att_body(kb, carry):
        for u in range(nsb):
            off = pl.multiple_of(kb * KB + u * SB, SB)
            kblk = k_ref[0, pl.ds(off, SB), :]
            vblk = v_ref[0, pl.ds(off, SB), :]
            bias = s_ref[kb, :, u * SB:(u + 1) * SB]
            logits = _nt(q_all, kblk)
            nch = SB // LANES
            for h0 in range(0, 8, HG):
                hs = list(range(h0, h0 + HG))
                lg = [logits[h * R:(h + 1) * R] + bias for h in hs]
                cm = [functools.reduce(jnp.maximum, [l[:, c * LANES:(c + 1) * LANES] for c in range(nch)])
                      for l in lg]
                m_old = [m_ref[h] for h in hs]
                m_new = [jnp.maximum(mo, jnp.max(c, axis=1, keepdims=True)) for mo, c in zip(m_old, cm)]
                alpha = [jnp.exp2(mo - mn) for mo, mn in zip(m_old, m_new)]
                p = [jnp.exp2(l - jnp.concatenate([mn] * nch, axis=1)).astype(BF16) for l, mn in zip(lg, m_new)]
                pv = [_mm(pp, vblk) for pp in p]
                for i, h in enumerate(hs):
                    acc_ref[h] = jnp.concatenate([alpha[i], alpha[i]], axis=1) * acc_ref[h] + pv[i]
                    m_ref[h] = m_new[i]
        return carry

    lax.fori_loop(0, nkb, att_body, 0)
    for h in range(8):
        a = acc_ref[h]
        o_ref[0, h] = (a[:, :LANES] / a[:, LANES:]).astype(BF16)


def _attention(iq, iw, q, ikb, kb, vb1, *, R, KB, SB, HG, ltot, q_off):
    B, _, Sq, _ = q.shape
    Lp = kb.shape[1]
    topk = min(TOPK_MAX, ltot // 4)
    assert Lp % KB == 0 and KB % SB == 0 and KB >= topk and Sq % R == 0
    hspec = pl.BlockSpec((1, 8, R, LANES), lambda b, i: (b, 0, i, 0))
    kspec = pl.BlockSpec((1, Lp, LANES), lambda b, i: (b, 0, 0))
    vspec = pl.BlockSpec((1, Lp, 2 * LANES), lambda b, i: (b, 0, 0))
    kern = functools.partial(_attn_kernel, R=R, KB=KB, SB=SB, HG=HG, ltot=ltot, q_off=q_off, topk=topk)
    return pl.pallas_call(
        kern,
        grid=(B, Sq // R),
        in_specs=[hspec, pl.BlockSpec((1, R, 8), lambda b, i: (b, i, 0)), hspec, kspec, kspec, vspec],
        out_specs=hspec,
        out_shape=jax.ShapeDtypeStruct((B, 8, Sq, LANES), BF16),
        scratch_shapes=[pltpu.VMEM((Lp // KB, R, KB), F32), pltpu.VMEM((Lp // KB, R, KB), BF16),
                        pltpu.VMEM((8, R, LANES), F32),
                        pltpu.VMEM((8, R, 2 * LANES), F32)],
        compiler_params=_params(("arbitrary", "arbitrary")),
        name="dsa_attention",
    )(iq, iw, q, ikb, kb, vb1)


def _head_sum(x):
    parts = []
    for h in range(8):
        s = jnp.sum(x[:, h * LANES:(h + 1) * LANES], axis=1, keepdims=True)
        parts.append(jnp.broadcast_to(s, (x.shape[0], LANES)))
    return jnp.concatenate(parts, axis=1)


def _rwkv_prep_kernel(z_ref, sh0_ref, mu_ref, w0_ref, wup_ref, a0_ref, aup_ref, gup_ref, kk_ref, ka_ref, rk_ref,
                      r_o, lw_o, k_o, v_o, kkn_o, b_o, g_o, bg_o, lwt_o, kt_o, bt_o, carry_ref,
                      *, tm, t_valid):
    i = pl.program_id(1)

    @pl.when(i == 0)
    def _():
        carry_ref[...] = sh0_ref[0]

    z = z_ref[0]
    row = lax.broadcasted_iota(jnp.int32, (tm, 1), 0)
    prev = jnp.where(row == 0, carry_ref[...], pltpu.roll(z, 1, axis=0))
    carry_ref[...] = z[tm - 1:tm, :]
    zm = z + (prev - z) * mu_ref[...]
    live = i * tm + row < t_valid
    zm = jnp.where(live, zm, 0.0)
    r = zm[:, 0:HP]
    k = zm[:, HP:2 * HP]
    v = zm[:, 2 * HP:3 * HP]
    wa = zm[:, 3 * HP:3 * HP + LANES]
    gd = zm[:, 3 * HP + LANES:3 * HP + 2 * LANES]
    w_raw = w0_ref[...] + _mm(jnp.tanh(wa), wup_ref[...], HI)
    lw = (-float(np.exp(-0.5))) / (1.0 + jnp.exp(-w_raw))
    lw = jnp.where(live, lw, 0.0)
    a = 1.0 / (1.0 + jnp.exp(-(a0_ref[...] + _mm(wa, aup_ref[...], HI))))
    g = _mm(1.0 / (1.0 + jnp.exp(-gd)), gup_ref[...], HI)
    kk = k * kk_ref[...]
    kk = kk * lax.rsqrt(jnp.maximum(_head_sum(kk * kk), L2_EPS))
    k_mod = k * (1.0 + (a - 1.0) * ka_ref[...])
    b = kk * a
    bonus = _head_sum(r * k_mod * rk_ref[...]) * v
    r_o[0] = r
    lw_o[0] = lw
    k_o[0] = k_mod
    v_o[0] = v
    kkn_o[0] = kk
    b_o[0] = b
    g_o[0] = g
    bg_o[0] = bonus * g
    lwt_o[0] = lw.T
    kt_o[0] = k_mod.T
    bt_o[0] = b.T


def _rwkv_prep(zr, shift0, prm, tm, t_valid):
    B, T, _ = zr.shape
    tspec = pl.BlockSpec((1, tm, HP), lambda b, i: (b, i, 0))
    fspec = pl.BlockSpec((1, HP, tm), lambda b, i: (b, 0, i))
    full = lambda a: pl.BlockSpec(a.shape, lambda b, i: (0,) * a.ndim)
    tok = jax.ShapeDtypeStruct((B, T, HP), F32)
    feat = jax.ShapeDtypeStruct((B, HP, T), F32)
    names = ("mu", "w0", "w_up", "a0", "a_up", "g_up", "k_k", "k_a", "r_k")
    ws = [prm[n] for n in names]
    return pl.pallas_call(
        functools.partial(_rwkv_prep_kernel, tm=tm, t_valid=t_valid),
        grid=(B, T // tm),
        in_specs=[pl.BlockSpec((1, tm, ZRW), lambda b, i: (b, i, 0)),
                  pl.BlockSpec((1, 1, ZRW), lambda b, i: (b, 0, 0))] + [full(w) for w in ws],
        out_specs=[tspec] * 8 + [fspec] * 3,
        out_shape=[tok] * 8 + [feat] * 3,
        scratch_shapes=[pltpu.VMEM((1, ZRW), F32)],
        compiler_params=_params(("arbitrary", "arbitrary")),
        name="rwkv_prep",
    )(zr, shift0, *ws)


def _rwkv_chunk_kernel(r_ref, lw_ref, k_ref, v_ref, kk_ref, b_ref, lwt_ref, kt_ref, bt_ref, a1_ref, a2_ref,
                       *, C, hb, pg, pc, po):
    ri = lax.broadcasted_iota(jnp.int32, (C, C), 0)
    ci = lax.broadcasted_iota(jnp.int32, (C, C), 1)
    low_incl = jnp.where(ri >= ci, 1.0, 0.0).astype(BF16)
    up_incl = jnp.where(ri <= ci, 1.0, 0.0).astype(BF16)
    strict = ri > ci
    incl = ri >= ci
    eye64 = jnp.where(lax.broadcasted_iota(jnp.int32, (64, 64), 0) == lax.broadcasted_iota(jnp.int32, (64, 64), 1),
                      1.0, 0.0)
    sls = [slice(u * LANES, (u + 1) * LANES) for u in range(hb)]
    each = lambda f, *ls: [f(*a) for a in zip(*ls)]
    lw = [lw_ref[0, :, sl] for sl in sls]
    r = [r_ref[0, :, sl] for sl in sls]
    k = [k_ref[0, :, sl] for sl in sls]
    v = [v_ref[0, :, sl] for sl in sls]
    kk = [kk_ref[0, :, sl] for sl in sls]
    b = [b_ref[0, :, sl] for sl in sls]
    lwt = [lwt_ref[0, sl, :][:64] for sl in sls]
    kt = [kt_ref[0, sl, :][:64] for sl in sls]
    bt = [bt_ref[0, sl, :][:64] for sl in sls]
    cum = each(lambda x: sum(_mm(low_incl, p) for p in _split_bf16(x, 3)), lw)
    cumt = each(lambda x: sum(_mm(p, up_incl) for p in _split_bf16(x, 3)), lwt)
    e_neg = each(lambda c: jnp.exp(-c), cum)
    at = each(lambda kk_, c, l: -kk_ * jnp.exp(c - l), kk, cum, lw)
    rt = each(lambda r_, c: r_ * jnp.exp(c), r, cum)
    g = each(lambda a, r_, b_, k_, e: _mmp(jnp.concatenate([a, r_], axis=0),
                                           jnp.concatenate([b_ * e, k_ * e], axis=0), pg, nt=True),
             at, rt, b, k, e_neg)
    n = each(lambda g_: jnp.where(strict, g_[:C, :C], 0.0), g)
    aak = each(lambda g_: jnp.where(strict, g_[:C, C:], 0.0), g)
    lrb = each(lambda g_: jnp.where(incl, g_[C:, :C], 0.0), g)
    lrk = each(lambda g_: jnp.where(incl, g_[C:, C:], 0.0), g)
    blk8 = lax.shift_right_logical(ri, 3) == lax.shift_right_logical(ci, 3)
    eye = jnp.where(ri == ci, 1.0, 0.0)
    n0 = each(lambda n_: jnp.where(blk8, n_, 0.0), n)
    x = each(lambda n_: eye + n_, n0)
    p = each(lambda n_: _mmp(n_, n_, pc), n0)
    x = each(lambda x_, p_: x_ + _mmp(x_, p_, pc), x, p)
    p = each(lambda p_: _mmp(p_, p_, pc), p)
    x = each(lambda x_, p_: x_ + _mmp(x_, p_, pc), x, p)
    for lv in range(3, int(C).bit_length() - 1):
        off = (lax.shift_right_logical(ri, lv + 1) == lax.shift_right_logical(ci, lv + 1)) & (
            lax.shift_right_logical(ri, lv) != lax.shift_right_logical(ci, lv))
        t = each(lambda n_, x_: _mmp(jnp.where(off, n_, 0.0), x_, pc), n, x)
        x = each(lambda x_, t_: x_ + _mmp(x_, t_, pc), x, t)
    z = each(lambda x_, a, b_: _mmp(x_, jnp.concatenate([a, b_], axis=1), pc), x, aak, at)
    w2v = each(lambda z_, v_: _mmp(z_[:, :C], v_, po), z, v)
    cl = each(lambda c: c[:, C - 1:C], cumt)
    eb = each(lambda c_, ct: jnp.exp(c_ - ct), cl, cumt)
    lhs = each(lambda bt_, kt_, e, lb, lk: jnp.concatenate(
        [jnp.concatenate([bt_ * e, kt_ * e], axis=1), jnp.concatenate([lb, lk], axis=1)], axis=0),
        bt, kt, eb, lrb, lrk)
    left = each(lambda l_, z_: _mmp(l_[:, :C], z_[:, C:], po), lhs, z)
    right = each(lambda l_, w, v_: _mmp(l_, jnp.concatenate([w, v_], axis=0), po), lhs, w2v, v)
    for u in range(hb):
        diag = eye64 * jnp.exp(cl[u])
        a1_ref[0, u, 0] = jnp.concatenate([left[u][:64, :64] + diag, (left[u][64:] + rt[u])[:, :64]], axis=0)
        a2_ref[0, u, 0] = right[u]


def _rwkv_chunks(r, lw, k, v, kk, b, lwt, kt, bt, C, hb, passes=RW_PASSES):
    B, T, _ = r.shape
    nc = T // C
    tspec = pl.BlockSpec((1, C, hb * LANES), lambda bb, h, c: (bb, c, h))
    fspec = pl.BlockSpec((1, hb * LANES, C), lambda bb, h, c: (bb, h, c))
    pg, pc, po = passes
    return pl.pallas_call(
        functools.partial(_rwkv_chunk_kernel, C=C, hb=hb, pg=pg, pc=pc, po=po),
        grid=(B, 8 // hb, nc),
        in_specs=[tspec] * 6 + [fspec] * 3,
        out_specs=[pl.BlockSpec((1, hb, 1, 64 + C, 64), lambda bb, h, c: (bb, h, c, 0, 0)),
                   pl.BlockSpec((1, hb, 1, 64 + C, LANES), lambda bb, h, c: (bb, h, c, 0, 0))],
        out_shape=[jax.ShapeDtypeStruct((B, 8, nc, 64 + C, 64), F32),
                   jax.ShapeDtypeStruct((B, 8, nc, 64 + C, LANES), F32)],
        compiler_params=_params(("arbitrary", "arbitrary", "arbitrary")),
        name="rwkv_chunks",
    )(r, lw, k, v, kk, b, lwt, kt, bt)


def _rwkv_scan_kernel(a1_ref, a2_ref, s0_ref, y_ref, sT_ref, h_ref, *, C, cb):
    c = pl.program_id(1)

    @pl.when(c == 0)
    def _():
        h_ref[...] = s0_ref[0]

    for cc in range(cb):
        for h in range(8):
            res = _mm(a1_ref[0, h, cc], h_ref[h], HI) + a2_ref[0, h, cc]
            h_ref[h] = res[:64]
            y_ref[0, cc * C:(cc + 1) * C, h * LANES:(h + 1) * LANES] = res[64:]

    @pl.when(c == pl.num_programs(1) - 1)
    def _():
        sT_ref[0] = h_ref[...]


def _rwkv_scan(a1, a2, s0t, C, cb):
    B, _, nc, _, _ = a1.shape
    return pl.pallas_call(
        functools.partial(_rwkv_scan_kernel, C=C, cb=cb),
        grid=(B, nc // cb),
        in_specs=[pl.BlockSpec((1, 8, cb, 64 + C, 64), lambda b, c: (b, 0, c, 0, 0)),
                  pl.BlockSpec((1, 8, cb, 64 + C, LANES), lambda b, c: (b, 0, c, 0, 0)),
                  pl.BlockSpec((1, 8, 64, LANES), lambda b, c: (b, 0, 0, 0))],
        out_specs=[pl.BlockSpec((1, cb * C, HP), lambda b, c: (b, c, 0)),
                   pl.BlockSpec((1, 8, 64, LANES), lambda b, c: (b, 0, 0, 0))],
        out_shape=[jax.ShapeDtypeStruct((B, nc * C, HP), F32),
                   jax.ShapeDtypeStruct((B, 8, 64, LANES), F32)],
        scratch_shapes=[pltpu.VMEM((8, 64, LANES), F32)],
        compiler_params=_params(("arbitrary", "arbitrary")),
        name="rwkv_scan",
    )(a1, a2, s0t)


def _mix_kernel(att_ref, y_ref, g_ref, bg_ref, sga_ref, sgb_ref, x_ref, gt1_ref, sc2_ref, sh2_ref,
                woa_ref, wor_ref, wout_ref, lnw_ref, lnb_ref, gpost_ref, gpre_ref, wr_ref, br_ref,
                x1_ref, h2_ref, gate_ref, eidx_ref, wgt_ref, rel_ref, cnt_out_ref, cnt_ref, *, tm):
    att = jnp.concatenate([att_ref[0, h] for h in range(8)], axis=1)
    lane = lax.broadcasted_iota(jnp.int32, (tm, LANES), 1)
    valid = lane < RWKV_HEAD
    parts = []
    for h in range(8):
        y = y_ref[0, :, h * LANES:(h + 1) * LANES]
        mu = jnp.sum(y, axis=1, keepdims=True) * (1.0 / RWKV_HEAD)
        d = jnp.where(valid, y - mu, 0.0)
        var = jnp.sum(d * d, axis=1, keepdims=True) * (1.0 / RWKV_HEAD)
        parts.append(d * lax.rsqrt(var + GN_EPS))
    yn = jnp.concatenate(parts, axis=1)
    rw = (yn * lnw_ref[...] + lnb_ref[...]) * g_ref[0] + bg_ref[0]
    mix = sga_ref[0] * _mm(att, woa_ref[...]) + sgb_ref[0] * _mm(rw.astype(BF16), wor_ref[...])
    o = _mm(mix.astype(BF16), wout_ref[...])
    x1 = x_ref[0] + gt1_ref[0] * _rms(o, gpost_ref[...])
    x1_ref[0] = x1
    h2 = _rms(x1, gpre_ref[...]) * (1.0 + sc2_ref[0]) + sh2_ref[0]
    h2_ref[0] = h2
    logits = _mm(h2, wr_ref[...], HI) + br_ref[...]
    el = lax.broadcasted_iota(jnp.int32, logits.shape, 1).astype(F32)
    work = logits
    sel = jnp.zeros(logits.shape, jnp.bool_)
    vmax = None
    firsts = []
    for kk in range(TOP_K):
        mx = jnp.max(work, axis=1, keepdims=True)
        if kk == 0:
            vmax = mx
        first = jnp.min(jnp.where(work == mx, el, float(N_EXPERTS)), axis=1, keepdims=True)
        hit = el == first
        firsts.append(first)
        sel = sel | hit
        work = jnp.where(hit, -jnp.inf, work)
    e = jnp.where(sel, jnp.exp(logits - vmax), 0.0)
    gate = e / jnp.sum(e, axis=1, keepdims=True)
    gate_ref[0] = gate
    @pl.when((pl.program_id(0) == 0) & (pl.program_id(1) == 0))
    def _():
        cnt_ref[...] = jnp.zeros(cnt_ref.shape, F32)

    self = jnp.where(sel, 1.0, 0.0)
    ti = lax.broadcasted_iota(jnp.int32, (tm, tm), 0)
    tj = lax.broadcasted_iota(jnp.int32, (tm, tm), 1)
    before = _mm(jnp.where(tj < ti, 1.0, 0.0).astype(BF16), self.astype(BF16)) + cnt_ref[...]
    cnt_ref[...] = cnt_ref[...] + jnp.sum(self, axis=0, keepdims=True)
    cnt_out_ref[...] = cnt_ref[...]
    ln = lax.broadcasted_iota(jnp.int32, (tm, LANES), 1)
    eo = jnp.zeros((tm, LANES), F32)
    go = jnp.zeros((tm, LANES), F32)
    ro = jnp.zeros((tm, LANES), F32)
    for kk in range(TOP_K):
        hit = el == firsts[kk]
        gk = jnp.sum(jnp.where(hit, gate, 0.0), axis=1, keepdims=True)
        rk = jnp.sum(jnp.where(hit, before, 0.0), axis=1, keepdims=True)
        eo = jnp.where(ln == kk, firsts[kk], eo)
        go = jnp.where(ln == kk, gk, go)
        ro = jnp.where(ln == kk, rk, ro)
    eidx_ref[0] = eo.astype(jnp.int32)
    wgt_ref[0] = go
    rel_ref[0] = ro.astype(jnp.int32)


def _mix(att, y, g, bg, sga, sgb, x, gt1, sc2, sh2, wts, tm):
    B, T, D = x.shape
    tspec = lambda n: pl.BlockSpec((1, tm, n), lambda b, i: (b, i, 0))
    full = lambda a: pl.BlockSpec(a.shape, lambda b, i: (0,) * a.ndim)
    tok = lambda n, dt: jax.ShapeDtypeStruct((B, T, n), dt)
    return pl.pallas_call(
        functools.partial(_mix_kernel, tm=tm),
        grid=(B, T // tm),
        in_specs=[pl.BlockSpec((1, 8, tm, LANES), lambda b, i: (b, 0, i, 0)), tspec(HP), tspec(HP), tspec(HP),
                  tspec(D), tspec(D), tspec(D), _mod_spec(gt1, tm), _mod_spec(sc2, tm), _mod_spec(sh2, tm)]
                 + [full(w) for w in wts],
        out_specs=[tspec(D), tspec(D), tspec(N_EXPERTS), tspec(LANES), tspec(LANES), tspec(LANES),
                   pl.BlockSpec((1, N_EXPERTS), lambda b, i: (0, 0))],
        out_shape=[tok(D, F32), tok(D, F32), tok(N_EXPERTS, F32), tok(LANES, jnp.int32), tok(LANES, F32),
                   tok(LANES, jnp.int32), jax.ShapeDtypeStruct((1, N_EXPERTS), F32)],
        scratch_shapes=[pltpu.VMEM((1, N_EXPERTS), F32)],
        compiler_params=_params(("arbitrary", "arbitrary")),
        name="mix_router",
    )(att, y, g, bg, sga, sgb, x, gt1, sc2, sh2, *wts)


def _expert(xb, wgu, bgu, wd, bd, d_ff):
    wgu = wgu.astype(BF16)
    wd = wd.astype(BF16)
    hgu = _mm(xb, wgu) + bgu
    hg = jnp.minimum(hgu[:, :d_ff], SWIGLU_LIMIT)
    hl = jnp.clip(hgu[:, d_ff:], -SWIGLU_LIMIT, SWIGLU_LIMIT)
    act = hg * (1.0 / (1.0 + jnp.exp(-SWIGLU_ALPHA * hg))) * (hl + 1.0)
    return _mm(act.astype(BF16), wd) + bd


def _sc_mesh():
    return plsc.VectorSubcoreMesh(core_axis_name="core", subcore_axis_name="subcore")


def _sc_dispatch(h, pos_slots, n_rows, q):
    N, D = h.shape
    mesh = _sc_mesh()
    steps = N // (mesh.num_cores * SC_WINDOW)
    assert steps * mesh.num_cores * SC_WINDOW == N

    @pl.kernel(out_type=jax.ShapeDtypeStruct((n_rows, SC_PIECE), h.dtype), mesh=mesh, scratch_types=[])
    def scatter(x_hbm, *refs):
        idx_hbm, o_hbm = refs[:-1], refs[-1]
        base = lax.axis_index("core") * steps

        def body(x_vmem, *i_vmem):
            for iv in i_vmem:
                pltpu.sync_copy(x_vmem, o_hbm.at[iv.at[0]])

        pltpu.emit_pipeline(
            body,
            grid=(steps,),
            in_specs=[pl.BlockSpec((SC_WINDOW, SC_PIECE), index_map=lambda i: (base + i, q))]
                     + [pl.BlockSpec((1, SC_WINDOW), index_map=lambda i: (0, base + i))] * len(idx_hbm),
            out_specs=[],
            core_axis_name="subcore",
            dimension_semantics=(pltpu.PARALLEL,),
        )(x_hbm, *idx_hbm)

    return scatter(h, *pos_slots)


def _sc_gather(table, idx):
    M = idx.shape[1]
    mesh = _sc_mesh()
    steps = M // (mesh.num_cores * SC_WINDOW)
    assert steps * mesh.num_cores * SC_WINDOW == M

    @pl.kernel(out_type=jax.ShapeDtypeStruct((M, SC_PIECE), table.dtype), mesh=mesh)
    def gather(x_hbm, i_hbm, o_hbm):
        base = lax.axis_index("core") * steps

        def body(i_vmem, o_vmem):
            pltpu.sync_copy(x_hbm.at[i_vmem.at[0]], o_vmem)

        pltpu.emit_pipeline(
            body,
            grid=(steps,),
            in_specs=[pl.BlockSpec((1, SC_WINDOW), index_map=lambda i: (0, base + i))],
            out_specs=[pl.BlockSpec((SC_WINDOW, SC_PIECE), index_map=lambda i: (base + i, 0))],
            core_axis_name="subcore",
            dimension_semantics=(pltpu.PARALLEL,),
        )(i_hbm, o_hbm)

    return gather(table, idx)


def _moe_grouped_kernel(te_ref, nt_ref, *refs, d_ff, npiece):
    x_refs, (wgu_ref, bgu_ref, wd_ref, bd_ref), o_refs = refs[:npiece], refs[npiece:npiece + 4], refs[npiece + 4:]

    @pl.when(pl.program_id(0) < nt_ref[0])
    def _():
        xb = jnp.concatenate([r[...] for r in x_refs], axis=1).astype(BF16)
        y = _expert(xb, wgu_ref[0], bgu_ref[0], wd_ref[0], bd_ref[0], d_ff)
        for q, o_ref in enumerate(o_refs):
            o_ref[...] = y[:, q * SC_PIECE:(q + 1) * SC_PIECE]


def _moe_grouped(xs, te, nt, wgu, bgu, wd, bd, TG):
    NP = xs[0].shape[0]
    E, D, F2 = wgu.shape
    npiece = len(xs)
    wmap = lambda g, te_, nt_: (te_[g], 0, 0)
    xmap = lambda g, te_, nt_: (jnp.minimum(g, nt_[0] - 1), 0)
    pspec = pl.BlockSpec((TG, SC_PIECE), xmap)
    return pl.pallas_call(
        functools.partial(_moe_grouped_kernel, d_ff=F2 // 2, npiece=npiece),
        grid_spec=pltpu.PrefetchScalarGridSpec(
            num_scalar_prefetch=2,
            grid=(NP // TG,),
            in_specs=[pspec] * npiece + [pl.BlockSpec((1, D, F2), wmap), pl.BlockSpec((1, 1, F2), wmap),
                                         pl.BlockSpec((1, F2 // 2, D), wmap), pl.BlockSpec((1, 1, D), wmap)],
            out_specs=[pspec] * npiece),
        out_shape=[jax.ShapeDtypeStruct((NP, SC_PIECE), F32)] * npiece,
        compiler_params=_params(("arbitrary",)),
        name="moe_grouped",
    )(te, nt, *xs, wgu, bgu, wd, bd)


def _moe_combine_kernel(*refs, npiece):
    y_refs = refs[:TOP_K * npiece]
    w_ref, x1_ref, gt2_ref, gpost_ref, o_ref = refs[TOP_K * npiece:]
    w = w_ref[...]
    acc = None
    for k in range(TOP_K):
        yk = jnp.concatenate([y_refs[k * npiece + q][...] for q in range(npiece)], axis=1)
        term = w[:, k:k + 1] * yk
        acc = term if acc is None else acc + term
    o_ref[...] = x1_ref[...] + gt2_ref[0] * _rms(acc, gpost_ref[...])


def _moe_combine(yslot, wgt, x1, gt2, gpost, tm):
    N, D = x1.shape
    nb = N // tm
    tpb = N // gt2.shape[0]
    npiece = len(yslot)
    yspec = lambda k: pl.BlockSpec((tm, SC_PIECE), lambda i, k=k: (i + k * nb, 0))
    return pl.pallas_call(
        functools.partial(_moe_combine_kernel, npiece=npiece),
        grid=(nb,),
        in_specs=[yspec(k) for k in range(TOP_K) for _ in range(npiece)]
                 + [pl.BlockSpec((tm, LANES), lambda i: (i, 0)), pl.BlockSpec((tm, D), lambda i: (i, 0)),
                    pl.BlockSpec((1, 1, D), lambda i: ((i * tm) // tpb, 0, 0)), pl.BlockSpec((1, D), lambda i: (0, 0))],
        out_specs=pl.BlockSpec((tm, D), lambda i: (i, 0)),
        out_shape=jax.ShapeDtypeStruct((N, D), F32),
        compiler_params=_params(("arbitrary",)),
        name="moe_combine",
    )(*[yslot[q] for _ in range(TOP_K) for q in range(npiece)], wgt, x1, gt2, gpost)


def _moe_kernel(h_ref, gate_ref, x1_ref, gt2_ref, gpost_ref, wgu_ref, bgu_ref, wd_ref, bd_ref, o_ref, acc_ref,
                *, d_ff):
    e = pl.program_id(1)

    @pl.when(e == 0)
    def _():
        acc_ref[...] = jnp.zeros(acc_ref.shape, F32)

    contrib = _expert(h_ref[...].astype(BF16), wgu_ref[0], bgu_ref[0], wd_ref[0], bd_ref[0], d_ff)
    gate = gate_ref[...]
    el = lax.broadcasted_iota(jnp.int32, gate.shape, 1)
    ge = jnp.sum(jnp.where(el == e, gate, 0.0), axis=1, keepdims=True)
    acc_ref[...] += ge * contrib

    @pl.when(e == pl.num_programs(1) - 1)
    def _():
        o_ref[...] = x1_ref[...] + gt2_ref[0] * _rms(acc_ref[...], gpost_ref[...])


def _moe(h2, gate, x1, gt2, gpost, wgu, bgu, wd, bd, tm):
    N, D = h2.shape
    E, _, F2 = wgu.shape
    tspec = lambda n: pl.BlockSpec((tm, n), lambda i, e: (i, 0))
    if gt2.shape[1] == 1:
        tpb = N // gt2.shape[0]
        gspec = pl.BlockSpec((1, 1, D), lambda i, e: ((i * tm) // tpb, 0, 0))
    else:
        gspec = pl.BlockSpec((1, tm, D), lambda i, e: (i, 0, 0))
    return pl.pallas_call(
        functools.partial(_moe_kernel, d_ff=F2 // 2),
        grid=(N // tm, E),
        in_specs=[tspec(D), tspec(E), tspec(D), gspec, pl.BlockSpec((1, D), lambda i, e: (0, 0)),
                  pl.BlockSpec((1, D, F2), lambda i, e: (e, 0, 0)), pl.BlockSpec((1, 1, F2), lambda i, e: (e, 0, 0)),
                  pl.BlockSpec((1, F2 // 2, D), lambda i, e: (e, 0, 0)), pl.BlockSpec((1, 1, D), lambda i, e: (e, 0, 0))],
        out_specs=tspec(D),
        out_shape=jax.ShapeDtypeStruct((N, D), F32),
        scratch_shapes=[pltpu.VMEM((tm, D), F32)],
        compiler_params=_params(("arbitrary", "arbitrary")),
        name="moe",
    )(h2, gate, x1, gt2, gpost, wgu, bgu, wd, bd)


def _per_token(m, T):
    B, _, D = m.shape
    return jnp.broadcast_to(m, (B, T, D)).reshape(1, B * T, D)


def _layer(x, mods, past, s0, shift0, P, cfg):
    B, T, D = x.shape
    sh1, sc1, gt1, sh2, sc2, gt2 = mods
    tm = cfg["tm"]

    q, k, v, kb, vb, iq, ik, ikb, iw = _proj_a(x, sc1, sh1, P["g_pre_mix"], P["w_a"], P["ik_ln_w"], P["ik_ln_b"], tm)
    zr, sga, sgb = _proj_b(x, sc1, sh1, P["g_pre_mix"], P["w_b"], tm)

    if past is not None:
        pk, pv, pik = past
        plen = pk.shape[1]
        kb_all = jnp.concatenate([pk.reshape(B, plen, 128).astype(BF16), kb], axis=1)
        vb_all = jnp.concatenate([pv.reshape(B, plen, 128).astype(BF16), vb], axis=1)
        ik_all = jnp.concatenate([jnp.pad(pik, ((0, 0), (0, 0), (0, 64))).astype(BF16), ikb], axis=1)
    else:
        plen = 0
        kb_all, vb_all, ik_all = kb, vb, ikb
    ltot = plen + T
    KB = cfg["KB"]
    lp = -(-ltot // KB) * KB
    if lp != ltot:
        padk = ((0, 0), (0, lp - ltot), (0, 0))
        kb_all, vb_all, ik_all = jnp.pad(kb_all, padk), jnp.pad(vb_all, padk), jnp.pad(ik_all, padk)
    vb1 = jnp.concatenate([vb_all, jnp.ones_like(vb_all)], axis=-1)
    att = _attention(iq, iw, q, ik_all, kb_all, vb1, R=cfg["R"], KB=KB, SB=cfg["SB"], HG=cfg["HG"], ltot=ltot,
                     q_off=plen)

    C = RW_CHUNK
    tp = -(-T // C) * C
    zr_p = zr if tp == T else jnp.pad(zr, ((0, 0), (0, tp - T), (0, 0)))
    shift_pad = _take_cols(shift0.reshape(B, RWKV_COLS), _COLS_ZR).reshape(B, 1, ZRW)
    r, lw, km, vv, kkn, bb, g, bg, lwt, kt, bt = _rwkv_prep(zr_p, shift_pad, P, min(cfg["tm_rw"], tp), T)
    a1, a2 = _rwkv_chunks(r, lw, km, vv, kkn, bb, lwt, kt, bt, C, cfg["hb"])
    s0t = jnp.pad(jnp.swapaxes(s0, 2, 3), ((0, 0), (0, 0), (0, 0), (0, 64)))
    nc = tp // C
    cb = min(cfg["cb"], nc)
    y, sT = _rwkv_scan(a1, a2, s0t, C, cb)
    s_new = jnp.swapaxes(sT[..., :64], 2, 3)
    zlast = zr[:, T - 1]
    inv = np.zeros((RWKV_COLS,), np.int32)
    inv[_COLS_ZR[0][_COLS_ZR[1]]] = np.nonzero(_COLS_ZR[1])[0]
    shift_new = jnp.take(zlast, jnp.asarray(inv), axis=1).reshape(B, 1, RWKV_COLS)
    if tp != T:
        y, g, bg = y[:, :T], g[:, :T], bg[:, :T]

    wts = [P[n] for n in ("w_o_att", "w_o_rwkv", "w_out", "ln_x_w", "ln_x_b", "g_post_mix", "g_pre_ffn",
                          "w_router", "b_router")]
    x1, h2, gate, eidx, wgt, rel, cnt = _mix(att, y, g, bg, sga, sgb, x, gt1, sc2, sh2, wts, tm)

    N = B * T
    tmm = min(cfg["tm_moe"], N)
    if cfg["routed"]:
        TG = cfg["TG"]
        ntm = N * TOP_K // TG + N_EXPERTS
        cnt_i = cnt[0].astype(jnp.int32)
        padded = (cnt_i + TG - 1) // TG * TG
        ends = jnp.cumsum(padded)
        off = ends - padded
        nt = ends[-1] // TG
        gi = jnp.minimum(jnp.arange(ntm, dtype=jnp.int32), nt - 1)
        te = jnp.sum((ends // TG)[None, :] <= gi[:, None], axis=1).astype(jnp.int32)
        e4 = eidx.reshape(N, LANES)[:, :TOP_K]
        pos = jnp.take(off, e4) + rel.reshape(N, LANES)[:, :TOP_K]
        pos_t = pos.T
        npiece = D // SC_PIECE
        h2f = h2.reshape(N, D)
        xs = [_sc_dispatch(h2f, [pos_t[k:k + 1] for k in range(TOP_K)], ntm * TG, q) for q in range(npiece)]
        ys = _moe_grouped(xs, te, nt.reshape(1), P["w_gu"], P["b_gu"], P["w_down"], P["b_down"], TG)
        idx_c = pos_t.reshape(1, TOP_K * N)
        yslot = [_sc_gather(ys[q], idx_c) for q in range(npiece)]
        out = _moe_combine(yslot, wgt.reshape(N, LANES), x1.reshape(N, D), gt2, P["g_post_ffn"], tm)
    else:
        if T % tmm == 0:
            gt2m = gt2
        else:
            gt2m = _per_token(gt2, T).reshape(N // tmm, tmm, D)
        out = _moe(h2.reshape(N, D), gate.reshape(N, N_EXPERTS), x1.reshape(N, D), gt2m, P["g_post_ffn"],
                   P["w_gu"], P["b_gu"], P["w_down"], P["b_down"], tmm)
    return (out.reshape(B, T, D), k.reshape(B, T, ATT_KV_HEADS, HEAD_DIM), v.reshape(B, T, ATT_KV_HEADS, HEAD_DIM),
            ik, s_new, shift_new)


def _prep_weights(l, w_in, ik_ln_w, ik_ln_b, mu_rwkv, w0, w_up, a0, a_up, g_up, k_k, k_a, r_k, ln_x_w, ln_x_b,
                  w_o_att, w_o_rwkv, w_out, w_router, b_router, w_gu, b_gu, w_down, b_down,
                  g_pre_mix, g_post_mix, g_pre_ffn, g_post_ffn):
    row = lambda a: a.reshape(1, -1)
    hp = lambda a: _take_cols(row(a), _HEAD_IDX)
    P = {}
    P["w_a"] = _take_cols(w_in[l], _COLS_A).astype(BF16)
    P["w_b"] = _take_cols(w_in[l], _COLS_B).astype(BF16)
    P["ik_ln_w"] = jnp.pad(row(ik_ln_w[l]), ((0, 0), (0, 64)))
    P["ik_ln_b"] = jnp.pad(row(ik_ln_b[l]), ((0, 0), (0, 64)))
    P["mu"] = _take_cols(row(mu_rwkv[l]), _COLS_ZR)
    P["w0"], P["a0"], P["k_k"], P["k_a"] = hp(w0[l]), hp(a0[l]), hp(k_k[l]), hp(k_a[l])
    P["r_k"] = hp(r_k[l].reshape(-1))
    P["ln_x_w"], P["ln_x_b"] = hp(ln_x_w[l]), hp(ln_x_b[l])
    z64 = jnp.zeros((64, HP), F32)
    P["w_up"] = jnp.concatenate([_take_cols(w_up[l], _HEAD_IDX), z64], axis=0)
    P["a_up"] = jnp.concatenate([z64, _take_cols(a_up[l], _HEAD_IDX)], axis=0)
    P["g_up"] = _take_cols(g_up[l], _HEAD_IDX)
    P["w_o_att"] = _take_rows(w_o_att[l], _ATT_ROW_IDX).astype(BF16)
    P["w_o_rwkv"] = _take_rows(w_o_rwkv[l], _HEAD_IDX).astype(BF16)
    P["w_out"] = w_out[l].astype(BF16)
    P["w_router"] = w_router[l]
    P["b_router"] = row(b_router[l])
    P["w_gu"] = w_gu[l]
    P["b_gu"] = b_gu[l].reshape(N_EXPERTS, 1, -1)
    P["w_down"] = w_down[l]
    P["b_down"] = b_down[l].reshape(N_EXPERTS, 1, -1)
    P["g_pre_mix"], P["g_post_mix"] = row(g_pre_mix[l]), row(g_post_mix[l])
    P["g_pre_ffn"], P["g_post_ffn"] = row(g_pre_ffn[l]), row(g_post_ffn[l])
    return P


CFG_PROMPT = dict(tm=256, R=128, KB=512, SB=256, HG=2,tm_rw=256, hb=8, cb=2, tm_moe=1024, routed=True, TG=512)
CFG_SAMPLE = dict(tm=32, R=32, KB=512, SB=256, HG=2,tm_rw=128, hb=8, cb=1, tm_moe=512, routed=False)


def kernel(x_prompt, x_sample, c_prompt, c_sample, cache_k, cache_v, cache_idx_k, state_rwkv, state_shift, w_ada, b_ada, g_pre_mix, g_post_mix, g_pre_ffn, g_post_ffn, w_in, ik_ln_w, ik_ln_b, mu_rwkv, w0, w_up, a0, a_up, g_up, k_k, k_a, r_k, ln_x_w, ln_x_b, w_o_att, w_o_rwkv, w_out, w_router, b_router, w_gu, b_gu, w_down, b_down):
    depth = w_in.shape[0]
    bp, tp_, D = x_prompt.shape
    bs, ts, _ = x_sample.shape
    y_p, y_s = x_prompt, x_sample
    st_p = [[] for _ in range(5)]
    st_s = [[] for _ in range(5)]
    nc_all = bp + bs
    npad = -(-nc_all // 8) * 8
    c_all = jnp.pad(jnp.concatenate([c_prompt, c_sample], axis=0), ((0, npad - nc_all), (0, 0)))
    for l in range(depth):
        P = _prep_weights(l, w_in, ik_ln_w, ik_ln_b, mu_rwkv, w0, w_up, a0, a_up, g_up, k_k, k_a, r_k, ln_x_w,
                          ln_x_b, w_o_att, w_o_rwkv, w_out, w_router, b_router, w_gu, b_gu, w_down, b_down,
                          g_pre_mix, g_post_mix, g_pre_ffn, g_post_ffn)
        ada = _ada(c_all, w_ada[l], b_ada[l])
        mods_p = [m[:bp, None, :] for m in jnp.split(ada, 6, axis=-1)]
        mods_s = [m[bp:nc_all, None, :] for m in jnp.split(ada, 6, axis=-1)]
        zero_state = jnp.zeros((bp, RWKV_HEADS, RWKV_HEAD, RWKV_HEAD), F32)
        zero_shift = jnp.zeros((bp, 1, RWKV_COLS), F32)
        outs_p = _layer(y_p, mods_p, None, zero_state, zero_shift, P, CFG_PROMPT)
        outs_s = _layer(y_s, mods_s, (cache_k[l], cache_v[l], cache_idx_k[l]), state_rwkv[l], state_shift[l], P,
                        CFG_SAMPLE)
        y_p, y_s = outs_p[0], outs_s[0]
        for lst, val in zip(st_p, outs_p[1:]):
            lst.append(val)
        for lst, val in zip(st_s, outs_s[1:]):
            lst.append(val)
    sp = [jnp.stack(v, axis=0) for v in st_p]
    ss = [jnp.stack(v, axis=0) for v in st_s]
    return (y_p, y_s, sp[0], sp[1], sp[2], sp[3], sp[4], ss[0], ss[1], ss[2], ss[3], ss[4])
```

```python
import functools

import numpy as np
import jax
import jax.numpy as jnp
from jax import lax
from jax.experimental import pallas as pl
from jax.experimental.pallas import tpu as pltpu
from jax.experimental.pallas import tpu_sc as plsc

F32 = jnp.float32
BF16 = jnp.bfloat16
HI = lax.Precision.HIGHEST

CHUNK = 64
ATT_HEADS = 8
ATT_KV_HEADS = 2
HEAD_DIM = 64
IDX_HEADS = 8
IDX_DIM = 64
IDX_SCALE = (IDX_DIM ** -0.5) * (IDX_HEADS ** -0.5)
TOPK_MAX = 256
RWKV_HEADS = 8
RWKV_HEAD = 64
RWKV_WIDTH = RWKV_HEADS * RWKV_HEAD
DECAY_LORA = 64
AAA_LORA = 64
GATE_LORA = 128
RWKV_COLS = 3 * RWKV_WIDTH + DECAY_LORA + AAA_LORA + GATE_LORA
GN_EPS = 64e-5
L2_EPS = 1e-24
N_EXPERTS = 32
TOP_K = 4
SWIGLU_LIMIT = 7.0
SWIGLU_ALPHA = 1.702
NORM_EPS = 1e-6
LN_EPS = 1e-6

LANES = 128
VMEM_LIMIT = 56 * 1024 * 1024

HP = 8 * LANES
RW_CHUNK = 128
NEG_BIG = -1e30
LOG2E = 1.4426950408889634
KEY_NEG_INF = -2139095041
INT_MIN = -2147483648
SC_WINDOW = 128
SC_PIECE = 256
RW_PASSES = (1, 1, 1)


def _params(sem):
    return pltpu.CompilerParams(dimension_semantics=sem, vmem_limit_bytes=VMEM_LIMIT)


def _nt(a, b, precision=None):
    return lax.dot_general(a, b, (((1,), (1,)), ((), ())), precision=precision,
                           preferred_element_type=F32)


def _mm(a, b, precision=None):
    return jnp.dot(a, b, precision=precision, preferred_element_type=F32)


def _split_bf16(x, terms):
    out = []
    for _ in range(terms):
        p = x.astype(BF16)
        out.append(p)
        x = x - p.astype(F32)
    return out


def _mmp(a, b, passes, nt=False):
    dot = _nt if nt else _mm
    if passes == 1:
        return dot(a.astype(BF16), b.astype(BF16))
    ah, al = _split_bf16(a, 2)
    bh, bl = _split_bf16(b, 2)
    return dot(ah, bh) + (dot(ah, bl) + dot(al, bh))


def _rms(x, g):
    return x * lax.rsqrt(jnp.mean(x * x, axis=-1, keepdims=True) + NORM_EPS) * g


def _headpad_idx(seg_off, lane_off_fn=lambda h: 0):
    idx = np.zeros((HP,), np.int32)
    ok = np.zeros((HP,), bool)
    for h in range(8):
        d0 = h * LANES + lane_off_fn(h)
        idx[d0:d0 + 64] = seg_off + h * 64 + np.arange(64)
        ok[d0:d0 + 64] = True
    return idx, ok


def _plain_idx(seg_off, n, width):
    idx = np.zeros((width,), np.int32)
    ok = np.zeros((width,), bool)
    idx[:n] = seg_off + np.arange(n)
    ok[:n] = True
    return idx, ok


def _cat(parts):
    return np.concatenate([p[0] for p in parts]), np.concatenate([p[1] for p in parts])


_O_Q, _O_K, _O_V, _O_IQ, _O_IK, _O_IW, _O_ZR, _O_GA, _O_GB = 0, 512, 640, 768, 1280, 1344, 1352, 3144, 4168
_COLS_A = _cat([_headpad_idx(_O_Q, lambda h: (h // 4) * 64), _plain_idx(_O_K, 128, 128), _plain_idx(_O_V, 128, 128),
                _headpad_idx(_O_IQ), _plain_idx(_O_IK, 64, 128), _plain_idx(_O_IW, 8, 128)])
_COLS_ZR = _cat([_headpad_idx(0), _headpad_idx(512), _headpad_idx(1024), _plain_idx(1536, 128, 128),
                 _plain_idx(1664, 128, 128)])
ZRW = _COLS_ZR[0].shape[0]
_COLS_B = _cat([(_COLS_ZR[0] + _O_ZR, _COLS_ZR[1]), _plain_idx(_O_GA, 1024, 1024), _plain_idx(_O_GB, 1024, 1024)])
NA = _COLS_A[0].shape[0]
NB = _COLS_B[0].shape[0]
_HEAD_IDX = _headpad_idx(0)
_ATT_ROW_IDX = _headpad_idx(0, lambda h: (h // 4) * 64)


def _take_cols(w, cols):
    idx, ok = cols
    return jnp.where(jnp.asarray(ok)[None, :], jnp.take(w, jnp.asarray(idx), axis=1), 0.0)


def _take_rows(w, cols):
    idx, ok = cols
    return jnp.where(jnp.asarray(ok)[:, None], jnp.take(w, jnp.asarray(idx), axis=0), 0.0)


def _mod_spec(m, tm):
    assert m.shape[1] == 1
    return pl.BlockSpec((1, 1, m.shape[2]), lambda b, i: (b, 0, 0))


def _ada_kernel(c_ref, w_ref, b_ref, o_ref):
    c = c_ref[...]
    s = c * (1.0 / (1.0 + jnp.exp(-c)))
    o_ref[...] = _mm(s, w_ref[...], HI) + b_ref[...]


def _ada(c, w, b):
    n, d = c.shape
    nout = w.shape[1]
    bn = 768
    return pl.pallas_call(
        _ada_kernel,
        grid=(nout // bn,),
        in_specs=[pl.BlockSpec((n, d), lambda j: (0, 0)),
                  pl.BlockSpec((d, bn), lambda j: (0, j)),
                  pl.BlockSpec((1, bn), lambda j: (0, j))],
        out_specs=pl.BlockSpec((n, bn), lambda j: (0, j)),
        out_shape=jax.ShapeDtypeStruct((n, nout), F32),
        compiler_params=_params(("arbitrary",)),
        name="ada",
    )(c, w, b.reshape(1, nout))


def _modulated(x_ref, sc_ref, sh_ref, g_ref):
    h = _rms(x_ref[0], g_ref[...])
    return (h * (1.0 + sc_ref[0]) + sh_ref[0]).astype(BF16)


def _proj_a_kernel(x_ref, sc_ref, sh_ref, g_ref, w_ref, lnw_ref, lnb_ref,
                   q_ref, k_ref, v_ref, kb_ref, vb_ref, iq_ref, ik_ref, ikb_ref, iw_ref):
    hb = _modulated(x_ref, sc_ref, sh_ref, g_ref)
    for h in range(8):
        zq = _mm(hb, w_ref[:, h * LANES:(h + 1) * LANES])
        q_ref[0, h] = (zq * (HEAD_DIM ** -0.5 * LOG2E)).astype(BF16)
    kv = _mm(hb, w_ref[:, 1024:1280])
    k = kv[:, :128]
    v = kv[:, 128:]
    k_ref[0] = k
    v_ref[0] = v
    kb_ref[0] = k.astype(BF16)
    vb_ref[0] = v.astype(BF16)
    for h in range(8):
        zi = _mm(hb, w_ref[:, 1280 + h * LANES:1280 + (h + 1) * LANES])
        iq_ref[0, h] = zi.astype(BF16)
    t = _mm(hb, w_ref[:, 2304:2560])
    ik = t[:, :128]
    lane = lax.broadcasted_iota(jnp.int32, ik.shape, 1)
    valid = lane < IDX_DIM
    mu = jnp.sum(ik, axis=-1, keepdims=True) * (1.0 / IDX_DIM)
    d = jnp.where(valid, ik - mu, 0.0)
    var = jnp.sum(d * d, axis=-1, keepdims=True) * (1.0 / IDX_DIM)
    ikn = d * lax.rsqrt(var + LN_EPS) * lnw_ref[...] + lnb_ref[...]
    ik_ref[0] = ikn[:, :IDX_DIM]
    ikb_ref[0] = ikn.astype(BF16)
    iw_ref[0] = t[:, 128:136] * IDX_SCALE


def _proj_a(x, sc, sh, g, w, lnw, lnb, tm):
    B, T, D = x.shape
    nt = T // tm
    tok = lambda n, dt: jax.ShapeDtypeStruct((B, T, n), dt)
    hm = lambda dt: jax.ShapeDtypeStruct((B, 8, T, LANES), dt)
    tspec = lambda n: pl.BlockSpec((1, tm, n), lambda b, i: (b, i, 0))
    hspec = pl.BlockSpec((1, 8, tm, LANES), lambda b, i: (b, 0, i, 0))
    full = lambda a: pl.BlockSpec(a.shape, lambda b, i: (0,) * a.ndim)
    return pl.pallas_call(
        _proj_a_kernel,
        grid=(B, nt),
        in_specs=[tspec(D), _mod_spec(sc, tm), _mod_spec(sh, tm), full(g), full(w), full(lnw), full(lnb)],
        out_specs=[hspec, tspec(128), tspec(128), tspec(128), tspec(128), hspec, tspec(IDX_DIM), tspec(128),
                   tspec(8)],
        out_shape=[hm(BF16), tok(128, F32), tok(128, F32), tok(128, BF16), tok(128, BF16), hm(BF16),
                   tok(IDX_DIM, F32), tok(128, BF16), tok(8, F32)],
        compiler_params=_params(("arbitrary", "arbitrary")),
        name="proj_att",
    )(x, sc, sh, g, w, lnw, lnb)


def _proj_b_kernel(x_ref, sc_ref, sh_ref, g_ref, w_ref, zr_ref, ga_ref, gb_ref):
    hb = _modulated(x_ref, sc_ref, sh_ref, g_ref)
    for j in range(ZRW // 256):
        zr_ref[0, :, j * 256:(j + 1) * 256] = _mm(hb, w_ref[:, j * 256:(j + 1) * 256])
    for j in range(4):
        za = _mm(hb, w_ref[:, ZRW + j * 256:ZRW + (j + 1) * 256])
        ga_ref[0, :, j * 256:(j + 1) * 256] = 1.0 / (1.0 + jnp.exp(-za))
        zb = _mm(hb, w_ref[:, ZRW + 1024 + j * 256:ZRW + 1024 + (j + 1) * 256])
        gb_ref[0, :, j * 256:(j + 1) * 256] = 1.0 / (1.0 + jnp.exp(-zb))


def _proj_b(x, sc, sh, g, w, tm):
    B, T, D = x.shape
    nt = T // tm
    tok = lambda n: jax.ShapeDtypeStruct((B, T, n), F32)
    tspec = lambda n: pl.BlockSpec((1, tm, n), lambda b, i: (b, i, 0))
    full = lambda a: pl.BlockSpec(a.shape, lambda b, i: (0,) * a.ndim)
    return pl.pallas_call(
        _proj_b_kernel,
        grid=(B, nt),
        in_specs=[tspec(D), _mod_spec(sc, tm), _mod_spec(sh, tm), full(g), full(w)],
        out_specs=[tspec(ZRW), tspec(1024), tspec(1024)],
        out_shape=[tok(ZRW), tok(1024), tok(1024)],
        compiler_params=_params(("arbitrary", "arbitrary")),
        name="proj_rwkv",
    )(x, sc, sh, g, w)


def _sortable_to_float(t):
    bits = t ^ (lax.shift_right_arithmetic(t, 31) & 0x7FFFFFFF)
    return lax.bitcast_convert_type(bits, F32)


def _top16(x):
    bits = lax.bitcast_convert_type(x, jnp.int32) & jnp.int32(-65536)
    return lax.bitcast_convert_type(bits, F32).astype(BF16)


def _attn_kernel(iq_ref, iw_ref, q_ref, ik_ref, k_ref, v_ref, o_ref, s_ref, sb_ref, m_ref, acc_ref,
                 *, R, KB, SB, HG, ltot, q_off, topk):
    j = pl.program_id(1)
    q0 = q_off + j * R
    pos = q0 + lax.broadcasted_iota(jnp.int32, (R, 1), 0)
    qchunk = lax.shift_right_logical(pos, 6)
    kend = jnp.minimum(ltot, (lax.shift_right_logical(q0 + R - 1, 6) + 1) * CHUNK)
    nkb = lax.shift_right_logical(kend + KB - 1, KB.bit_length() - 1)
    lane_k = lax.broadcasted_iota(jnp.int32, (R, KB), 1)
    nch = KB // LANES

    iq_all = iq_ref[0].reshape(8 * R, LANES)
    iw = iw_ref[0]
    iwb = [jnp.broadcast_to(iw[:, h:h + 1], (R, SB)) for h in range(8)]
    lane_s = lax.broadcasted_iota(jnp.int32, (R, SB), 1)
    nsb = KB // SB

    def score_body(kb, carry):
        for u in range(nsb):
            off = pl.multiple_of(kb * KB + u * SB, SB)
            ikb = ik_ref[0, pl.ds(off, SB), :]
            s_all = _nt(iq_all, ikb)
            acc = jnp.zeros((R, SB), F32)
            for h in range(8):
                acc = acc + jnp.maximum(s_all[h * R:(h + 1) * R], 0.0) * iwb[h]
            kidx = off + lane_s
            adm = (lax.shift_right_logical(kidx, 6) <= qchunk) & (kidx < ltot)
            sc = jnp.where(adm, acc, -jnp.inf)
            s_ref[kb, :, u * SB:(u + 1) * SB] = sc
            sb_ref[kb, :, u * SB:(u + 1) * SB] = _top16(sc)
        return carry

    lax.fori_loop(0, nkb, score_body, 0)

    def count_ge(ref, cand, dt):
        one, zero = jnp.ones((), dt), jnp.zeros((), dt)

        def body(kb, part):
            blk = ref[kb]
            m = jnp.where(blk >= cand, one, zero)
            for c in range(nch):
                part = part + m[:, c * LANES:(c + 1) * LANES]
            return part
        part = lax.fori_loop(0, nkb, body, jnp.zeros((R, LANES), dt))
        return jnp.sum(part.astype(F32), axis=1, keepdims=True)

    kf = float(topk)
    c0 = count_ge(sb_ref, jnp.zeros((R, 1), BF16), BF16)
    t0 = jnp.where(c0 >= kf, 0, INT_MIN).astype(jnp.int32)

    def hi_body(i, t):
        cand = t + lax.shift_left(jnp.int32(1), 30 - i)
        cnt = count_ge(sb_ref, _top16(_sortable_to_float(cand)), BF16)
        return jnp.where(cnt >= kf, cand, t)

    def lo_body(i, t):
        cand = t + lax.shift_left(jnp.int32(1), 15 - i)
        cnt = count_ge(s_ref, _sortable_to_float(cand), F32)
        return jnp.where(cnt >= kf, cand, t)

    t = lax.fori_loop(0, 15, hi_body, t0)
    t = lax.fori_loop(0, 16, lo_body, t)
    all_finite = t <= KEY_NEG_INF
    tau = jnp.where(all_finite, -jnp.inf, _sortable_to_float(jnp.maximum(t, KEY_NEG_INF)))

    def count_gt_eq(_):
        def body(kb, carry):
            pg, pe = carry
            blk = s_ref[kb]
            mg = jnp.where(blk > tau, 1.0, 0.0)
            me = jnp.where(blk == tau, 1.0, 0.0)
            for c in range(nch):
                pg = pg + mg[:, c * LANES:(c + 1) * LANES]
                pe = pe + me[:, c * LANES:(c + 1) * LANES]
            return pg, pe
        z = jnp.zeros((R, LANES), F32)
        pg, pe = lax.fori_loop(0, nkb, body, (z, z))
        return jnp.sum(pg, axis=1, keepdims=True), jnp.sum(pe, axis=1, keepdims=True)

    cnt_gt, cnt_eq = count_gt_eq(0)
    need = kf - cnt_gt
    tie = (cnt_eq > need) & jnp.logical_not(all_finite)
    any_tie = jnp.max(jnp.where(tie, 1.0, 0.0)) > 0.0

    def tie_bound():
        def count_eq_below(x):
            def body(kb, part):
                blk = s_ref[kb]
                kidx = kb * KB + lane_k
                m = jnp.where((blk == tau) & (kidx < x), 1.0, 0.0)
                for c in range(nch):
                    part = part + m[:, c * LANES:(c + 1) * LANES]
                return part
            part = lax.fori_loop(0, nkb, body, jnp.zeros((R, LANES), F32))
            return jnp.sum(part, axis=1, keepdims=True)

        nbits = int(ltot).bit_length()

        def body(i, x):
            cand = x + lax.shift_left(jnp.int32(1), nbits - 1 - i)
            ok = count_eq_below(cand) < need
            return jnp.where(ok, cand, x)
        x = lax.fori_loop(0, nbits, body, jnp.zeros((R, 1), jnp.int32))
        return x + 1

    big = jnp.full((R, 1), 1 << 30, jnp.int32)
    bound = lax.cond(any_tie, lambda: jnp.where(tie, tie_bound(), big), lambda: big)

    def bias_body(kb, carry):
        blk = s_ref[kb]
        kidx = kb * KB + lane_k
        sel = (blk > tau) | ((blk == tau) & (kidx < bound))
        sel = sel & (blk > -jnp.inf)
        s_ref[kb] = jnp.where(sel, 0.0, NEG_BIG)
        return carry

    lax.fori_loop(0, nkb, bias_body, 0)

    q_all = q_ref[0].reshape(8 * R, LANES)
    m_ref[...] = jnp.full(m_ref.shape, NEG_BIG, F32)
    acc_ref[...] = jnp.zeros(acc_ref.shape, F32)

    def att_body(kb, carry):
        for u in range(nsb):
            off = pl.multiple_of(kb * KB + u * SB, SB)
            kblk = k_ref[0, pl.ds(off, SB), :]
            vblk = v_ref[0, pl.ds(off, SB), :]
            bias = s_ref[kb, :, u * SB:(u + 1) * SB]
            logits = _nt(q_all, kblk)
            nch = SB // LANES
            for h0 in range(0, 8, HG):
                hs = list(range(h0, h0 + HG))
                lg = [logits[h * R:(h + 1) * R] + bias for h in hs]
                cm = [functools.reduce(jnp.maximum, [l[:, c * LANES:(c + 1) * LANES] for c in range(nch)])
                      for l in lg]
                m_old = [m_ref[h] for h in hs]
                m_new = [jnp.maximum(mo, jnp.max(c, axis=1, keepdims=True)) for mo, c in zip(m_old, cm)]
                alpha = [jnp.exp2(mo - mn) for mo, mn in zip(m_old, m_new)]
                p = [jnp.exp2(l - jnp.concatenate([mn] * nch, axis=1)).astype(BF16) for l, mn in zip(lg, m_new)]
                pv = [_mm(pp, vblk) for pp in p]
                for i, h in enumerate(hs):
                    acc_ref[h] = jnp.concatenate([alpha[i], alpha[i]], axis=1) * acc_ref[h] + pv[i]
                    m_ref[h] = m_new[i]
        return carry

    lax.fori_loop(0, nkb, att_body, 0)
    for h in range(8):
        a = acc_ref[h]
        o_ref[0, h] = (a[:, :LANES] / a[:, LANES:]).astype(BF16)


def _attention(iq, iw, q, ikb, kb, vb1, *, R, KB, SB, HG, ltot, q_off):
    B, _, Sq, _ = q.shape
    Lp = kb.shape[1]
    topk = min(TOPK_MAX, ltot // 4)
    assert Lp % KB == 0 and KB % SB == 0 and KB >= topk and Sq % R == 0
    hspec = pl.BlockSpec((1, 8, R, LANES), lambda b, i: (b, 0, i, 0))
    kspec = pl.BlockSpec((1, Lp, LANES), lambda b, i: (b, 0, 0))
    vspec = pl.BlockSpec((1, Lp, 2 * LANES), lambda b, i: (b, 0, 0))
    kern = functools.partial(_attn_kernel, R=R, KB=KB, SB=SB, HG=HG, ltot=ltot, q_off=q_off, topk=topk)
    return pl.pallas_call(
        kern,
        grid=(B, Sq // R),
        in_specs=[hspec, pl.BlockSpec((1, R, 8), lambda b, i: (b, i, 0)), hspec, kspec, kspec, vspec],
        out_specs=hspec,
        out_shape=jax.ShapeDtypeStruct((B, 8, Sq, LANES), BF16),
        scratch_shapes=[pltpu.VMEM((Lp // KB, R, KB), F32), pltpu.VMEM((Lp // KB, R, KB), BF16),
                        pltpu.VMEM((8, R, LANES), F32),
                        pltpu.VMEM((8, R, 2 * LANES), F32)],
        compiler_params=_params(("arbitrary", "arbitrary")),
        name="dsa_attention",
    )(iq, iw, q, ikb, kb, vb1)


def _head_sum(x):
    parts = []
    for h in range(8):
        s = jnp.sum(x[:, h * LANES:(h + 1) * LANES], axis=1, keepdims=True)
        parts.append(jnp.broadcast_to(s, (x.shape[0], LANES)))
    return jnp.concatenate(parts, axis=1)


def _rwkv_prep_kernel(z_ref, sh0_ref, mu_ref, w0_ref, wup_ref, a0_ref, aup_ref, gup_ref, kk_ref, ka_ref, rk_ref,
                      r_o, lw_o, k_o, v_o, kkn_o, b_o, g_o, bg_o, lwt_o, kt_o, bt_o, carry_ref,
                      *, tm, t_valid):
    i = pl.program_id(1)

    @pl.when(i == 0)
    def _():
        carry_ref[...] = sh0_ref[0]

    z = z_ref[0]
    row = lax.broadcasted_iota(jnp.int32, (tm, 1), 0)
    prev = jnp.where(row == 0, carry_ref[...], pltpu.roll(z, 1, axis=0))
    carry_ref[...] = z[tm - 1:tm, :]
    zm = z + (prev - z) * mu_ref[...]
    live = i * tm + row < t_valid
    zm = jnp.where(live, zm, 0.0)
    r = zm[:, 0:HP]
    k = zm[:, HP:2 * HP]
    v = zm[:, 2 * HP:3 * HP]
    wa = zm[:, 3 * HP:3 * HP + LANES]
    gd = zm[:, 3 * HP + LANES:3 * HP + 2 * LANES]
    w_raw = w0_ref[...] + _mm(jnp.tanh(wa), wup_ref[...], HI)
    lw = (-float(np.exp(-0.5))) / (1.0 + jnp.exp(-w_raw))
    lw = jnp.where(live, lw, 0.0)
    a = 1.0 / (1.0 + jnp.exp(-(a0_ref[...] + _mm(wa, aup_ref[...], HI))))
    g = _mm(1.0 / (1.0 + jnp.exp(-gd)), gup_ref[...], HI)
    kk = k * kk_ref[...]
    kk = kk * lax.rsqrt(jnp.maximum(_head_sum(kk * kk), L2_EPS))
    k_mod = k * (1.0 + (a - 1.0) * ka_ref[...])
    b = kk * a
    bonus = _head_sum(r * k_mod * rk_ref[...]) * v
    r_o[0] = r
    lw_o[0] = lw
    k_o[0] = k_mod
    v_o[0] = v
    kkn_o[0] = kk
    b_o[0] = b
    g_o[0] = g
    bg_o[0] = bonus * g
    lwt_o[0] = lw.T
    kt_o[0] = k_mod.T
    bt_o[0] = b.T


def _rwkv_prep(zr, shift0, prm, tm, t_valid):
    B, T, _ = zr.shape
    tspec = pl.BlockSpec((1, tm, HP), lambda b, i: (b, i, 0))
    fspec = pl.BlockSpec((1, HP, tm), lambda b, i: (b, 0, i))
    full = lambda a: pl.BlockSpec(a.shape, lambda b, i: (0,) * a.ndim)
    tok = jax.ShapeDtypeStruct((B, T, HP), F32)
    feat = jax.ShapeDtypeStruct((B, HP, T), F32)
    names = ("mu", "w0", "w_up", "a0", "a_up", "g_up", "k_k", "k_a", "r_k")
    ws = [prm[n] for n in names]
    return pl.pallas_call(
        functools.partial(_rwkv_prep_kernel, tm=tm, t_valid=t_valid),
        grid=(B, T // tm),
        in_specs=[pl.BlockSpec((1, tm, ZRW), lambda b, i: (b, i, 0)),
                  pl.BlockSpec((1, 1, ZRW), lambda b, i: (b, 0, 0))] + [full(w) for w in ws],
        out_specs=[tspec] * 8 + [fspec] * 3,
        out_shape=[tok] * 8 + [feat] * 3,
        scratch_shapes=[pltpu.VMEM((1, ZRW), F32)],
        compiler_params=_params(("arbitrary", "arbitrary")),
        name="rwkv_prep",
    )(zr, shift0, *ws)


def _rwkv_chunk_kernel(r_ref, lw_ref, k_ref, v_ref, kk_ref, b_ref, lwt_ref, kt_ref, bt_ref, a1_ref, a2_ref,
                       *, C, hb, pg, pc, po):
    ri = lax.broadcasted_iota(jnp.int32, (C, C), 0)
    ci = lax.broadcasted_iota(jnp.int32, (C, C), 1)
    low_incl = jnp.where(ri >= ci, 1.0, 0.0).astype(BF16)
    up_incl = jnp.where(ri <= ci, 1.0, 0.0).astype(BF16)
    strict = ri > ci
    incl = ri >= ci
    eye64 = jnp.where(lax.broadcasted_iota(jnp.int32, (64, 64), 0) == lax.broadcasted_iota(jnp.int32, (64, 64), 1),
                      1.0, 0.0)
    sls = [slice(u * LANES, (u + 1) * LANES) for u in range(hb)]
    each = lambda f, *ls: [f(*a) for a in zip(*ls)]
    lw = [lw_ref[0, :, sl] for sl in sls]
    r = [r_ref[0, :, sl] for sl in sls]
    k = [k_ref[0, :, sl] for sl in sls]
    v = [v_ref[0, :, sl] for sl in sls]
    kk = [kk_ref[0, :, sl] for sl in sls]
    b = [b_ref[0, :, sl] for sl in sls]
    lwt = [lwt_ref[0, sl, :][:64] for sl in sls]
    kt = [kt_ref[0, sl, :][:64] for sl in sls]
    bt = [bt_ref[0, sl, :][:64] for sl in sls]
    cum = each(lambda x: sum(_mm(low_incl, p) for p in _split_bf16(x, 3)), lw)
    cumt = each(lambda x: sum(_mm(p, up_incl) for p in _split_bf16(x, 3)), lwt)
    e_neg = each(lambda c: jnp.exp(-c), cum)
    at = each(lambda kk_, c, l: -kk_ * jnp.exp(c - l), kk, cum, lw)
    rt = each(lambda r_, c: r_ * jnp.exp(c), r, cum)
    g = each(lambda a, r_, b_, k_, e: _mmp(jnp.concatenate([a, r_], axis=0),
                                           jnp.concatenate([b_ * e, k_ * e], axis=0), pg, nt=True),
             at, rt, b, k, e_neg)
    n = each(lambda g_: jnp.where(strict, g_[:C, :C], 0.0), g)
    aak = each(lambda g_: jnp.where(strict, g_[:C, C:], 0.0), g)
    lrb = each(lambda g_: jnp.where(incl, g_[C:, :C], 0.0), g)
    lrk = each(lambda g_: jnp.where(incl, g_[C:, C:], 0.0), g)
    blk8 = lax.shift_right_logical(ri, 3) == lax.shift_right_logical(ci, 3)
    eye = jnp.where(ri == ci, 1.0, 0.0)
    n0 = each(lambda n_: jnp.where(blk8, n_, 0.0), n)
    x = each(lambda n_: eye + n_, n0)
    p = each(lambda n_: _mmp(n_, n_, pc), n0)
    x = each(lambda x_, p_: x_ + _mmp(x_, p_, pc), x, p)
    p = each(lambda p_: _mmp(p_, p_, pc), p)
    x = each(lambda x_, p_: x_ + _mmp(x_, p_, pc), x, p)
    for lv in range(3, int(C).bit_length() - 1):
        off = (lax.shift_right_logical(ri, lv + 1) == lax.shift_right_logical(ci, lv + 1)) & (
            lax.shift_right_logical(ri, lv) != lax.shift_right_logical(ci, lv))
        t = each(lambda n_, x_: _mmp(jnp.where(off, n_, 0.0), x_, pc), n, x)
        x = each(lambda x_, t_: x_ + _mmp(x_, t_, pc), x, t)
    z = each(lambda x_, a, b_: _mmp(x_, jnp.concatenate([a, b_], axis=1), pc), x, aak, at)
    w2v = each(lambda z_, v_: _mmp(z_[:, :C], v_, po), z, v)
    cl = each(lambda c: c[:, C - 1:C], cumt)
    eb = each(lambda c_, ct: jnp.exp(c_ - ct), cl, cumt)
    lhs = each(lambda bt_, kt_, e, lb, lk: jnp.concatenate(
        [jnp.concatenate([bt_ * e, kt_ * e], axis=1), jnp.concatenate([lb, lk], axis=1)], axis=0),
        bt, kt, eb, lrb, lrk)
    left = each(lambda l_, z_: _mmp(l_[:, :C], z_[:, C:], po), lhs, z)
    right = each(lambda l_, w, v_: _mmp(l_, jnp.concatenate([w, v_], axis=0), po), lhs, w2v, v)
    for u in range(hb):
        diag = eye64 * jnp.exp(cl[u])
        a1_ref[0, u, 0] = jnp.concatenate([left[u][:64, :64] + diag, (left[u][64:] + rt[u])[:, :64]], axis=0)
        a2_ref[0, u, 0] = right[u]


def _rwkv_chunks(r, lw, k, v, kk, b, lwt, kt, bt, C, hb, passes=RW_PASSES):
    B, T, _ = r.shape
    nc = T // C
    tspec = pl.BlockSpec((1, C, hb * LANES), lambda bb, h, c: (bb, c, h))
    fspec = pl.BlockSpec((1, hb * LANES, C), lambda bb, h, c: (bb, h, c))
    pg, pc, po = passes
    return pl.pallas_call(
        functools.partial(_rwkv_chunk_kernel, C=C, hb=hb, pg=pg, pc=pc, po=po),
        grid=(B, 8 // hb, nc),
        in_specs=[tspec] * 6 + [fspec] * 3,
        out_specs=[pl.BlockSpec((1, hb, 1, 64 + C, 64), lambda bb, h, c: (bb, h, c, 0, 0)),
                   pl.BlockSpec((1, hb, 1, 64 + C, LANES), lambda bb, h, c: (bb, h, c, 0, 0))],
        out_shape=[jax.ShapeDtypeStruct((B, 8, nc, 64 + C, 64), F32),
                   jax.ShapeDtypeStruct((B, 8, nc, 64 + C, LANES), F32)],
        compiler_params=_params(("arbitrary", "arbitrary", "arbitrary")),
        name="rwkv_chunks",
    )(r, lw, k, v, kk, b, lwt, kt, bt)


def _rwkv_scan_kernel(a1_ref, a2_ref, s0_ref, y_ref, sT_ref, h_ref, *, C, cb):
    c = pl.program_id(1)

    @pl.when(c == 0)
    def _():
        h_ref[...] = s0_ref[0]

    for cc in range(cb):
        for h in range(8):
            res = _mm(a1_ref[0, h, cc], h_ref[h], HI) + a2_ref[0, h, cc]
            h_ref[h] = res[:64]
            y_ref[0, cc * C:(cc + 1) * C, h * LANES:(h + 1) * LANES] = res[64:]

    @pl.when(c == pl.num_programs(1) - 1)
    def _():
        sT_ref[0] = h_ref[...]


def _rwkv_scan(a1, a2, s0t, C, cb):
    B, _, nc, _, _ = a1.shape
    return pl.pallas_call(
        functools.partial(_rwkv_scan_kernel, C=C, cb=cb),
        grid=(B, nc // cb),
        in_specs=[pl.BlockSpec((1, 8, cb, 64 + C, 64), lambda b, c: (b, 0, c, 0, 0)),
                  pl.BlockSpec((1, 8, cb, 64 + C, LANES), lambda b, c: (b, 0, c, 0, 0)),
                  pl.BlockSpec((1, 8, 64, LANES), lambda b, c: (b, 0, 0, 0))],
        out_specs=[pl.BlockSpec((1, cb * C, HP), lambda b, c: (b, c, 0)),
                   pl.BlockSpec((1, 8, 64, LANES), lambda b, c: (b, 0, 0, 0))],
        out_shape=[jax.ShapeDtypeStruct((B, nc * C, HP), F32),
                   jax.ShapeDtypeStruct((B, 8, 64, LANES), F32)],
        scratch_shapes=[pltpu.VMEM((8, 64, LANES), F32)],
        compiler_params=_params(("arbitrary", "arbitrary")),
        name="rwkv_scan",
    )(a1, a2, s0t)


def _mix_kernel(att_ref, y_ref, g_ref, bg_ref, sga_ref, sgb_ref, x_ref, gt1_ref, sc2_ref, sh2_ref,
                woa_ref, wor_ref, wout_ref, lnw_ref, lnb_ref, gpost_ref, gpre_ref, wr_ref, br_ref,
                x1_ref, h2_ref, gate_ref, eidx_ref, wgt_ref, rel_ref, cnt_out_ref, cnt_ref, *, tm):
    att = jnp.concatenate([att_ref[0, h] for h in range(8)], axis=1)
    lane = lax.broadcasted_iota(jnp.int32, (tm, LANES), 1)
    valid = lane < RWKV_HEAD
    parts = []
    for h in range(8):
        y = y_ref[0, :, h * LANES:(h + 1) * LANES]
        mu = jnp.sum(y, axis=1, keepdims=True) * (1.0 / RWKV_HEAD)
        d = jnp.where(valid, y - mu, 0.0)
        var = jnp.sum(d * d, axis=1, keepdims=True) * (1.0 / RWKV_HEAD)
        parts.append(d * lax.rsqrt(var + GN_EPS))
    yn = jnp.concatenate(parts, axis=1)
    rw = (yn * lnw_ref[...] + lnb_ref[...]) * g_ref[0] + bg_ref[0]
    mix = sga_ref[0] * _mm(att, woa_ref[...]) + sgb_ref[0] * _mm(rw.astype(BF16), wor_ref[...])
    o = _mm(mix.astype(BF16), wout_ref[...])
    x1 = x_ref[0] + gt1_ref[0] * _rms(o, gpost_ref[...])
    x1_ref[0] = x1
    h2 = _rms(x1, gpre_ref[...]) * (1.0 + sc2_ref[0]) + sh2_ref[0]
    h2_ref[0] = h2
    logits = _mm(h2, wr_ref[...], HI) + br_ref[...]
    el = lax.broadcasted_iota(jnp.int32, logits.shape, 1).astype(F32)
    work = logits
    sel = jnp.zeros(logits.shape, jnp.bool_)
    vmax = None
    firsts = []
    for kk in range(TOP_K):
        mx = jnp.max(work, axis=1, keepdims=True)
        if kk == 0:
            vmax = mx
        first = jnp.min(jnp.where(work == mx, el, float(N_EXPERTS)), axis=1, keepdims=True)
        hit = el == first
        firsts.append(first)
        sel = sel | hit
        work = jnp.where(hit, -jnp.inf, work)
    e = jnp.where(sel, jnp.exp(logits - vmax), 0.0)
    gate = e / jnp.sum(e, axis=1, keepdims=True)
    gate_ref[0] = gate
    @pl.when((pl.program_id(0) == 0) & (pl.program_id(1) == 0))
    def _():
        cnt_ref[...] = jnp.zeros(cnt_ref.shape, F32)

    self = jnp.where(sel, 1.0, 0.0)
    ti = lax.broadcasted_iota(jnp.int32, (tm, tm), 0)
    tj = lax.broadcasted_iota(jnp.int32, (tm, tm), 1)
    before = _mm(jnp.where(tj < ti, 1.0, 0.0).astype(BF16), self.astype(BF16)) + cnt_ref[...]
    cnt_ref[...] = cnt_ref[...] + jnp.sum(self, axis=0, keepdims=True)
    cnt_out_ref[...] = cnt_ref[...]
    ln = lax.broadcasted_iota(jnp.int32, (tm, LANES), 1)
    eo = jnp.zeros((tm, LANES), F32)
    go = jnp.zeros((tm, LANES), F32)
    ro = jnp.zeros((tm, LANES), F32)
    for kk in range(TOP_K):
        hit = el == firsts[kk]
        gk = jnp.sum(jnp.where(hit, gate, 0.0), axis=1, keepdims=True)
        rk = jnp.sum(jnp.where(hit, before, 0.0), axis=1, keepdims=True)
        eo = jnp.where(ln == kk, firsts[kk], eo)
        go = jnp.where(ln == kk, gk, go)
        ro = jnp.where(ln == kk, rk, ro)
    eidx_ref[0] = eo.astype(jnp.int32)
    wgt_ref[0] = go
    rel_ref[0] = ro.astype(jnp.int32)


def _mix(att, y, g, bg, sga, sgb, x, gt1, sc2, sh2, wts, tm):
    B, T, D = x.shape
    tspec = lambda n: pl.BlockSpec((1, tm, n), lambda b, i: (b, i, 0))
    full = lambda a: pl.BlockSpec(a.shape, lambda b, i: (0,) * a.ndim)
    tok = lambda n, dt: jax.ShapeDtypeStruct((B, T, n), dt)
    return pl.pallas_call(
        functools.partial(_mix_kernel, tm=tm),
        grid=(B, T // tm),
        in_specs=[pl.BlockSpec((1, 8, tm, LANES), lambda b, i: (b, 0, i, 0)), tspec(HP), tspec(HP), tspec(HP),
                  tspec(D), tspec(D), tspec(D), _mod_spec(gt1, tm), _mod_spec(sc2, tm), _mod_spec(sh2, tm)]
                 + [full(w) for w in wts],
        out_specs=[tspec(D), tspec(D), tspec(N_EXPERTS), tspec(LANES), tspec(LANES), tspec(LANES),
                   pl.BlockSpec((1, N_EXPERTS), lambda b, i: (0, 0))],
        out_shape=[tok(D, F32), tok(D, F32), tok(N_EXPERTS, F32), tok(LANES, jnp.int32), tok(LANES, F32),
                   tok(LANES, jnp.int32), jax.ShapeDtypeStruct((1, N_EXPERTS), F32)],
        scratch_shapes=[pltpu.VMEM((1, N_EXPERTS), F32)],
        compiler_params=_params(("arbitrary", "arbitrary")),
        name="mix_router",
    )(att, y, g, bg, sga, sgb, x, gt1, sc2, sh2, *wts)


def _expert(xb, wgu, bgu, wd, bd, d_ff):
    wgu = wgu.astype(BF16)
    wd = wd.astype(BF16)
    hgu = _mm(xb, wgu) + bgu
    hg = jnp.minimum(hgu[:, :d_ff], SWIGLU_LIMIT)
    hl = jnp.clip(hgu[:, d_ff:], -SWIGLU_LIMIT, SWIGLU_LIMIT)
    act = hg * (1.0 / (1.0 + jnp.exp(-SWIGLU_ALPHA * hg))) * (hl + 1.0)
    return _mm(act.astype(BF16), wd) + bd


def _sc_mesh():
    return plsc.VectorSubcoreMesh(core_axis_name="core", subcore_axis_name="subcore")


def _sc_dispatch(h, pos_slots, n_rows, q):
    N, D = h.shape
    mesh = _sc_mesh()
    steps = N // (mesh.num_cores * SC_WINDOW)
    assert steps * mesh.num_cores * SC_WINDOW == N

    @pl.kernel(out_type=jax.ShapeDtypeStruct((n_rows, SC_PIECE), h.dtype), mesh=mesh, scratch_types=[])
    def scatter(x_hbm, *refs):
        idx_hbm, o_hbm = refs[:-1], refs[-1]
        base = lax.axis_index("core") * steps

        def body(x_vmem, *i_vmem):
            for iv in i_vmem:
                pltpu.sync_copy(x_vmem, o_hbm.at[iv.at[0]])

        pltpu.emit_pipeline(
            body,
            grid=(steps,),
            in_specs=[pl.BlockSpec((SC_WINDOW, SC_PIECE), index_map=lambda i: (base + i, q))]
                     + [pl.BlockSpec((1, SC_WINDOW), index_map=lambda i: (0, base + i))] * len(idx_hbm),
            out_specs=[],
            core_axis_name="subcore",
            dimension_semantics=(pltpu.PARALLEL,),
        )(x_hbm, *idx_hbm)

    return scatter(h, *pos_slots)


def _sc_gather(table, idx):
    M = idx.shape[1]
    mesh = _sc_mesh()
    steps = M // (mesh.num_cores * SC_WINDOW)
    assert steps * mesh.num_cores * SC_WINDOW == M

    @pl.kernel(out_type=jax.ShapeDtypeStruct((M, SC_PIECE), table.dtype), mesh=mesh)
    def gather(x_hbm, i_hbm, o_hbm):
        base = lax.axis_index("core") * steps

        def body(i_vmem, o_vmem):
            pltpu.sync_copy(x_hbm.at[i_vmem.at[0]], o_vmem)

        pltpu.emit_pipeline(
            body,
            grid=(steps,),
            in_specs=[pl.BlockSpec((1, SC_WINDOW), index_map=lambda i: (0, base + i))],
            out_specs=[pl.BlockSpec((SC_WINDOW, SC_PIECE), index_map=lambda i: (base + i, 0))],
            core_axis_name="subcore",
            dimension_semantics=(pltpu.PARALLEL,),
        )(i_hbm, o_hbm)

    return gather(table, idx)


def _moe_grouped_kernel(te_ref, nt_ref, *refs, d_ff, npiece):
    x_refs, (wgu_ref, bgu_ref, wd_ref, bd_ref) = refs[:npiece], refs[npiece:npiece + 4]
    o_refs, (wgu_b, wd_b) = refs[npiece + 4:2 * npiece + 4], refs[2 * npiece + 4:]
    g = pl.program_id(0)

    @pl.when((g == 0) | (te_ref[g] != te_ref[jnp.maximum(g - 1, 0)]))
    def _():
        wgu_b[...] = wgu_ref[0].astype(BF16)
        wd_b[...] = wd_ref[0].astype(BF16)

    @pl.when(g < nt_ref[0])
    def _():
        xb = jnp.concatenate([r[...] for r in x_refs], axis=1).astype(BF16)
        y = _expert(xb, wgu_b[...], bgu_ref[0], wd_b[...], bd_ref[0], d_ff)
        for q, o_ref in enumerate(o_refs):
            o_ref[...] = y[:, q * SC_PIECE:(q + 1) * SC_PIECE]


def _moe_grouped(xs, te, nt, wgu, bgu, wd, bd, TG):
    NP = xs[0].shape[0]
    E, D, F2 = wgu.shape
    npiece = len(xs)
    wmap = lambda g, te_, nt_: (te_[g], 0, 0)
    xmap = lambda g, te_, nt_: (jnp.minimum(g, nt_[0] - 1), 0)
    pspec = pl.BlockSpec((TG, SC_PIECE), xmap)
    return pl.pallas_call(
        functools.partial(_moe_grouped_kernel, d_ff=F2 // 2, npiece=npiece),
        grid_spec=pltpu.PrefetchScalarGridSpec(
            num_scalar_prefetch=2,
            grid=(NP // TG,),
            in_specs=[pspec] * npiece + [pl.BlockSpec((1, D, F2), wmap), pl.BlockSpec((1, 1, F2), wmap),
                                         pl.BlockSpec((1, F2 // 2, D), wmap), pl.BlockSpec((1, 1, D), wmap)],
            out_specs=[pspec] * npiece,
            scratch_shapes=[pltpu.VMEM((D, F2), BF16), pltpu.VMEM((F2 // 2, D), BF16)]),
        out_shape=[jax.ShapeDtypeStruct((NP, SC_PIECE), F32)] * npiece,
        compiler_params=_params(("arbitrary",)),
        name="moe_grouped",
    )(te, nt, *xs, wgu, bgu, wd, bd)


def _moe_combine_kernel(*refs, npiece):
    y_refs = refs[:TOP_K * npiece]
    w_ref, x1_ref, gt2_ref, gpost_ref, o_ref = refs[TOP_K * npiece:]
    w = w_ref[...]
    acc = None
    for k in range(TOP_K):
        yk = jnp.concatenate([y_refs[k * npiece + q][...] for q in range(npiece)], axis=1)
        term = w[:, k:k + 1] * yk
        acc = term if acc is None else acc + term
    o_ref[...] = x1_ref[...] + gt2_ref[0] * _rms(acc, gpost_ref[...])


def _moe_combine(yslot, wgt, x1, gt2, gpost, tm):
    N, D = x1.shape
    nb = N // tm
    tpb = N // gt2.shape[0]
    npiece = len(yslot)
    yspec = lambda k: pl.BlockSpec((tm, SC_PIECE), lambda i, k=k: (i + k * nb, 0))
    return pl.pallas_call(
        functools.partial(_moe_combine_kernel, npiece=npiece),
        grid=(nb,),
        in_specs=[yspec(k) for k in range(TOP_K) for _ in range(npiece)]
                 + [pl.BlockSpec((tm, LANES), lambda i: (i, 0)), pl.BlockSpec((tm, D), lambda i: (i, 0)),
                    pl.BlockSpec((1, 1, D), lambda i: ((i * tm) // tpb, 0, 0)), pl.BlockSpec((1, D), lambda i: (0, 0))],
        out_specs=pl.BlockSpec((tm, D), lambda i: (i, 0)),
        out_shape=jax.ShapeDtypeStruct((N, D), F32),
        compiler_params=_params(("arbitrary",)),
        name="moe_combine",
    )(*[yslot[q] for _ in range(TOP_K) for q in range(npiece)], wgt, x1, gt2, gpost)


def _moe_kernel(h_ref, gate_ref, x1_ref, gt2_ref, gpost_ref, wgu_ref, bgu_ref, wd_ref, bd_ref, o_ref, acc_ref,
                *, d_ff):
    e = pl.program_id(1)

    @pl.when(e == 0)
    def _():
        acc_ref[...] = jnp.zeros(acc_ref.shape, F32)

    contrib = _expert(h_ref[...].astype(BF16), wgu_ref[0], bgu_ref[0], wd_ref[0], bd_ref[0], d_ff)
    gate = gate_ref[...]
    el = lax.broadcasted_iota(jnp.int32, gate.shape, 1)
    ge = jnp.sum(jnp.where(el == e, gate, 0.0), axis=1, keepdims=True)
    acc_ref[...] += ge * contrib

    @pl.when(e == pl.num_programs(1) - 1)
    def _():
        o_ref[...] = x1_ref[...] + gt2_ref[0] * _rms(acc_ref[...], gpost_ref[...])


def _moe(h2, gate, x1, gt2, gpost, wgu, bgu, wd, bd, tm):
    N, D = h2.shape
    E, _, F2 = wgu.shape
    tspec = lambda n: pl.BlockSpec((tm, n), lambda i, e: (i, 0))
    if gt2.shape[1] == 1:
        tpb = N // gt2.shape[0]
        gspec = pl.BlockSpec((1, 1, D), lambda i, e: ((i * tm) // tpb, 0, 0))
    else:
        gspec = pl.BlockSpec((1, tm, D), lambda i, e: (i, 0, 0))
    return pl.pallas_call(
        functools.partial(_moe_kernel, d_ff=F2 // 2),
        grid=(N // tm, E),
        in_specs=[tspec(D), tspec(E), tspec(D), gspec, pl.BlockSpec((1, D), lambda i, e: (0, 0)),
                  pl.BlockSpec((1, D, F2), lambda i, e: (e, 0, 0)), pl.BlockSpec((1, 1, F2), lambda i, e: (e, 0, 0)),
                  pl.BlockSpec((1, F2 // 2, D), lambda i, e: (e, 0, 0)), pl.BlockSpec((1, 1, D), lambda i, e: (e, 0, 0))],
        out_specs=tspec(D),
        out_shape=jax.ShapeDtypeStruct((N, D), F32),
        scratch_shapes=[pltpu.VMEM((tm, D), F32)],
        compiler_params=_params(("arbitrary", "arbitrary")),
        name="moe",
    )(h2, gate, x1, gt2, gpost, wgu, bgu, wd, bd)


def _per_token(m, T):
    B, _, D = m.shape
    return jnp.broadcast_to(m, (B, T, D)).reshape(1, B * T, D)


def _layer(x, mods, past, s0, shift0, P, cfg):
    B, T, D = x.shape
    sh1, sc1, gt1, sh2, sc2, gt2 = mods
    tm = cfg["tm"]

    q, k, v, kb, vb, iq, ik, ikb, iw = _proj_a(x, sc1, sh1, P["g_pre_mix"], P["w_a"], P["ik_ln_w"], P["ik_ln_b"], tm)
    zr, sga, sgb = _proj_b(x, sc1, sh1, P["g_pre_mix"], P["w_b"], tm)

    if past is not None:
        pk, pv, pik = past
        plen = pk.shape[1]
        kb_all = jnp.concatenate([pk.reshape(B, plen, 128).astype(BF16), kb], axis=1)
        vb_all = jnp.concatenate([pv.reshape(B, plen, 128).astype(BF16), vb], axis=1)
        ik_all = jnp.concatenate([jnp.pad(pik, ((0, 0), (0, 0), (0, 64))).astype(BF16), ikb], axis=1)
    else:
        plen = 0
        kb_all, vb_all, ik_all = kb, vb, ikb
    ltot = plen + T
    KB = cfg["KB"]
    lp = -(-ltot // KB) * KB
    if lp != ltot:
        padk = ((0, 0), (0, lp - ltot), (0, 0))
        kb_all, vb_all, ik_all = jnp.pad(kb_all, padk), jnp.pad(vb_all, padk), jnp.pad(ik_all, padk)
    vb1 = jnp.concatenate([vb_all, jnp.ones_like(vb_all)], axis=-1)
    att = _attention(iq, iw, q, ik_all, kb_all, vb1, R=cfg["R"], KB=KB, SB=cfg["SB"], HG=cfg["HG"], ltot=ltot,
                     q_off=plen)

    C = RW_CHUNK
    tp = -(-T // C) * C
    zr_p = zr if tp == T else jnp.pad(zr, ((0, 0), (0, tp - T), (0, 0)))
    shift_pad = _take_cols(shift0.reshape(B, RWKV_COLS), _COLS_ZR).reshape(B, 1, ZRW)
    r, lw, km, vv, kkn, bb, g, bg, lwt, kt, bt = _rwkv_prep(zr_p, shift_pad, P, min(cfg["tm_rw"], tp), T)
    a1, a2 = _rwkv_chunks(r, lw, km, vv, kkn, bb, lwt, kt, bt, C, cfg["hb"])
    s0t = jnp.pad(jnp.swapaxes(s0, 2, 3), ((0, 0), (0, 0), (0, 0), (0, 64)))
    nc = tp // C
    cb = min(cfg["cb"], nc)
    y, sT = _rwkv_scan(a1, a2, s0t, C, cb)
    s_new = jnp.swapaxes(sT[..., :64], 2, 3)
    zlast = zr[:, T - 1]
    inv = np.zeros((RWKV_COLS,), np.int32)
    inv[_COLS_ZR[0][_COLS_ZR[1]]] = np.nonzero(_COLS_ZR[1])[0]
    shift_new = jnp.take(zlast, jnp.asarray(inv), axis=1).reshape(B, 1, RWKV_COLS)
    if tp != T:
        y, g, bg = y[:, :T], g[:, :T], bg[:, :T]

    wts = [P[n] for n in ("w_o_att", "w_o_rwkv", "w_out", "ln_x_w", "ln_x_b", "g_post_mix", "g_pre_ffn",
                          "w_router", "b_router")]
    x1, h2, gate, eidx, wgt, rel, cnt = _mix(att, y, g, bg, sga, sgb, x, gt1, sc2, sh2, wts, tm)

    N = B * T
    tmm = min(cfg["tm_moe"], N)
    if cfg["routed"]:
        TG = cfg["TG"]
        ntm = N * TOP_K // TG + N_EXPERTS
        cnt_i = cnt[0].astype(jnp.int32)
        padded = (cnt_i + TG - 1) // TG * TG
        ends = jnp.cumsum(padded)
        off = ends - padded
        nt = ends[-1] // TG
        gi = jnp.minimum(jnp.arange(ntm, dtype=jnp.int32), nt - 1)
        te = jnp.sum((ends // TG)[None, :] <= gi[:, None], axis=1).astype(jnp.int32)
        e4 = eidx.reshape(N, LANES)[:, :TOP_K]
        pos = jnp.take(off, e4) + rel.reshape(N, LANES)[:, :TOP_K]
        pos_t = pos.T
        npiece = D // SC_PIECE
        h2f = h2.reshape(N, D)
        xs = [_sc_dispatch(h2f, [pos_t[k:k + 1] for k in range(TOP_K)], ntm * TG, q) for q in range(npiece)]
        ys = _moe_grouped(xs, te, nt.reshape(1), P["w_gu"], P["b_gu"], P["w_down"], P["b_down"], TG)
        idx_c = pos_t.reshape(1, TOP_K * N)
        yslot = [_sc_gather(ys[q], idx_c) for q in range(npiece)]
        out = _moe_combine(yslot, wgt.reshape(N, LANES), x1.reshape(N, D), gt2, P["g_post_ffn"], tm)
    else:
        if T % tmm == 0:
            gt2m = gt2
        else:
            gt2m = _per_token(gt2, T).reshape(N // tmm, tmm, D)
        out = _moe(h2.reshape(N, D), gate.reshape(N, N_EXPERTS), x1.reshape(N, D), gt2m, P["g_post_ffn"],
                   P["w_gu"], P["b_gu"], P["w_down"], P["b_down"], tmm)
    return (out.reshape(B, T, D), k.reshape(B, T, ATT_KV_HEADS, HEAD_DIM), v.reshape(B, T, ATT_KV_HEADS, HEAD_DIM),
            ik, s_new, shift_new)


def _prep_weights(l, w_in, ik_ln_w, ik_ln_b, mu_rwkv, w0, w_up, a0, a_up, g_up, k_k, k_a, r_k, ln_x_w, ln_x_b,
                  w_o_att, w_o_rwkv, w_out, w_router, b_router, w_gu, b_gu, w_down, b_down,
                  g_pre_mix, g_post_mix, g_pre_ffn, g_post_ffn):
    row = lambda a: a.reshape(1, -1)
    hp = lambda a: _take_cols(row(a), _HEAD_IDX)
    P = {}
    P["w_a"] = _take_cols(w_in[l], _COLS_A).astype(BF16)
    P["w_b"] = _take_cols(w_in[l], _COLS_B).astype(BF16)
    P["ik_ln_w"] = jnp.pad(row(ik_ln_w[l]), ((0, 0), (0, 64)))
    P["ik_ln_b"] = jnp.pad(row(ik_ln_b[l]), ((0, 0), (0, 64)))
    P["mu"] = _take_cols(row(mu_rwkv[l]), _COLS_ZR)
    P["w0"], P["a0"], P["k_k"], P["k_a"] = hp(w0[l]), hp(a0[l]), hp(k_k[l]), hp(k_a[l])
    P["r_k"] = hp(r_k[l].reshape(-1))
    P["ln_x_w"], P["ln_x_b"] = hp(ln_x_w[l]), hp(ln_x_b[l])
    z64 = jnp.zeros((64, HP), F32)
    P["w_up"] = jnp.concatenate([_take_cols(w_up[l], _HEAD_IDX), z64], axis=0)
    P["a_up"] = jnp.concatenate([z64, _take_cols(a_up[l], _HEAD_IDX)], axis=0)
    P["g_up"] = _take_cols(g_up[l], _HEAD_IDX)
    P["w_o_att"] = _take_rows(w_o_att[l], _ATT_ROW_IDX).astype(BF16)
    P["w_o_rwkv"] = _take_rows(w_o_rwkv[l], _HEAD_IDX).astype(BF16)
    P["w_out"] = w_out[l].astype(BF16)
    P["w_router"] = w_router[l]
    P["b_router"] = row(b_router[l])
    P["w_gu"] = w_gu[l]
    P["b_gu"] = b_gu[l].reshape(N_EXPERTS, 1, -1)
    P["w_down"] = w_down[l]
    P["b_down"] = b_down[l].reshape(N_EXPERTS, 1, -1)
    P["g_pre_mix"], P["g_post_mix"] = row(g_pre_mix[l]), row(g_post_mix[l])
    P["g_pre_ffn"], P["g_post_ffn"] = row(g_pre_ffn[l]), row(g_post_ffn[l])
    return P


CFG_PROMPT = dict(tm=256, R=256,KB=512, SB=256, HG=2,tm_rw=256, hb=8, cb=2, tm_moe=1024, routed=True, TG=512)
CFG_SAMPLE = dict(tm=32, R=32, KB=512, SB=256, HG=2,tm_rw=128, hb=8, cb=1, tm_moe=512, routed=False)


def kernel(x_prompt, x_sample, c_prompt, c_sample, cache_k, cache_v, cache_idx_k, state_rwkv, state_shift, w_ada, b_ada, g_pre_mix, g_post_mix, g_pre_ffn, g_post_ffn, w_in, ik_ln_w, ik_ln_b, mu_rwkv, w0, w_up, a0, a_up, g_up, k_k, k_a, r_k, ln_x_w, ln_x_b, w_o_att, w_o_rwkv, w_out, w_router, b_router, w_gu, b_gu, w_down, b_down):
    depth = w_in.shape[0]
    bp, tp_, D = x_prompt.shape
    bs, ts, _ = x_sample.shape
    y_p, y_s = x_prompt, x_sample
    st_p = [[] for _ in range(5)]
    st_s = [[] for _ in range(5)]
    nc_all = bp + bs
    npad = -(-nc_all // 8) * 8
    c_all = jnp.pad(jnp.concatenate([c_prompt, c_sample], axis=0), ((0, npad - nc_all), (0, 0)))
    for l in range(depth):
        P = _prep_weights(l, w_in, ik_ln_w, ik_ln_b, mu_rwkv, w0, w_up, a0, a_up, g_up, k_k, k_a, r_k, ln_x_w,
                          ln_x_b, w_o_att, w_o_rwkv, w_out, w_router, b_router, w_gu, b_gu, w_down, b_down,
                          g_pre_mix, g_post_mix, g_pre_ffn, g_post_ffn)
        ada = _ada(c_all, w_ada[l], b_ada[l])
        mods_p = [m[:bp, None, :] for m in jnp.split(ada, 6, axis=-1)]
        mods_s = [m[bp:nc_all, None, :] for m in jnp.split(ada, 6, axis=-1)]
        zero_state = jnp.zeros((bp, RWKV_HEADS, RWKV_HEAD, RWKV_HEAD), F32)
        zero_shift = jnp.zeros((bp, 1, RWKV_COLS), F32)
        outs_p = _layer(y_p, mods_p, None, zero_state, zero_shift, P, CFG_PROMPT)
        outs_s = _layer(y_s, mods_s, (cache_k[l], cache_v[l], cache_idx_k[l]), state_rwkv[l], state_shift[l], P,
                        CFG_SAMPLE)
        y_p, y_s = outs_p[0], outs_s[0]
        for lst, val in zip(st_p, outs_p[1:]):
            lst.append(val)
        for lst, val in zip(st_s, outs_s[1:]):
            lst.append(val)
    sp = [jnp.stack(v, axis=0) for v in st_p]
    ss = [jnp.stack(v, axis=0) for v in st_s]
    return (y_p, y_s, sp[0], sp[1], sp[2], sp[3], sp[4], ss[0], ss[1], ss[2], ss[3], ss[4])
```

```python
import functools

import numpy as np
import jax
import jax.numpy as jnp
from jax import lax
from jax.experimental import pallas as pl
from jax.experimental.pallas import tpu as pltpu
from jax.experimental.pallas import tpu_sc as plsc

F32 = jnp.float32
BF16 = jnp.bfloat16
HI = lax.Precision.HIGHEST

CHUNK = 64
ATT_HEADS = 8
ATT_KV_HEADS = 2
HEAD_DIM = 64
IDX_HEADS = 8
IDX_DIM = 64
IDX_SCALE = (IDX_DIM ** -0.5) * (IDX_HEADS ** -0.5)
TOPK_MAX = 256
RWKV_HEADS = 8
RWKV_HEAD = 64
RWKV_WIDTH = RWKV_HEADS * RWKV_HEAD
DECAY_LORA = 64
AAA_LORA = 64
GATE_LORA = 128
RWKV_COLS = 3 * RWKV_WIDTH + DECAY_LORA + AAA_LORA + GATE_LORA
GN_EPS = 64e-5
L2_EPS = 1e-24
N_EXPERTS = 32
TOP_K = 4
SWIGLU_LIMIT = 7.0
SWIGLU_ALPHA = 1.702
NORM_EPS = 1e-6
LN_EPS = 1e-6

LANES = 128
VMEM_LIMIT = 56 * 1024 * 1024

HP = 8 * LANES
RW_CHUNK = 128
NEG_BIG = -1e30
LOG2E = 1.4426950408889634
KEY_NEG_INF = -2139095041
INT_MIN = -2147483648
SC_WINDOW = 128
SC_PIECE = 256
LO_GROUP = 4
RW_PASSES = (1, 1, 1)


def _params(sem):
    return pltpu.CompilerParams(dimension_semantics=sem, vmem_limit_bytes=VMEM_LIMIT)


def _nt(a, b, precision=None):
    return lax.dot_general(a, b, (((1,), (1,)), ((), ())), precision=precision,
                           preferred_element_type=F32)


def _mm(a, b, precision=None):
    return jnp.dot(a, b, precision=precision, preferred_element_type=F32)


def _split_bf16(x, terms):
    out = []
    for _ in range(terms):
        p = x.astype(BF16)
        out.append(p)
        x = x - p.astype(F32)
    return out


def _mmp(a, b, passes, nt=False):
    dot = _nt if nt else _mm
    if passes == 1:
        return dot(a.astype(BF16), b.astype(BF16))
    ah, al = _split_bf16(a, 2)
    bh, bl = _split_bf16(b, 2)
    return dot(ah, bh) + (dot(ah, bl) + dot(al, bh))


def _rms(x, g):
    return x * lax.rsqrt(jnp.mean(x * x, axis=-1, keepdims=True) + NORM_EPS) * g


def _headpad_idx(seg_off, lane_off_fn=lambda h: 0):
    idx = np.zeros((HP,), np.int32)
    ok = np.zeros((HP,), bool)
    for h in range(8):
        d0 = h * LANES + lane_off_fn(h)
        idx[d0:d0 + 64] = seg_off + h * 64 + np.arange(64)
        ok[d0:d0 + 64] = True
    return idx, ok


def _plain_idx(seg_off, n, width):
    idx = np.zeros((width,), np.int32)
    ok = np.zeros((width,), bool)
    idx[:n] = seg_off + np.arange(n)
    ok[:n] = True
    return idx, ok


def _cat(parts):
    return np.concatenate([p[0] for p in parts]), np.concatenate([p[1] for p in parts])


_O_Q, _O_K, _O_V, _O_IQ, _O_IK, _O_IW, _O_ZR, _O_GA, _O_GB = 0, 512, 640, 768, 1280, 1344, 1352, 3144, 4168
_COLS_A = _cat([_headpad_idx(_O_Q, lambda h: (h // 4) * 64), _plain_idx(_O_K, 128, 128), _plain_idx(_O_V, 128, 128),
                _headpad_idx(_O_IQ), _plain_idx(_O_IK, 64, 128), _plain_idx(_O_IW, 8, 128)])
_COLS_ZR = _cat([_headpad_idx(0), _headpad_idx(512), _headpad_idx(1024), _plain_idx(1536, 128, 128),
                 _plain_idx(1664, 128, 128)])
ZRW = _COLS_ZR[0].shape[0]
_COLS_B = _cat([(_COLS_ZR[0] + _O_ZR, _COLS_ZR[1]), _plain_idx(_O_GA, 1024, 1024), _plain_idx(_O_GB, 1024, 1024)])
NA = _COLS_A[0].shape[0]
NB = _COLS_B[0].shape[0]
_HEAD_IDX = _headpad_idx(0)
_ATT_ROW_IDX = _headpad_idx(0, lambda h: (h // 4) * 64)


def _take_cols(w, cols):
    idx, ok = cols
    return jnp.where(jnp.asarray(ok)[None, :], jnp.take(w, jnp.asarray(idx), axis=1), 0.0)


def _take_rows(w, cols):
    idx, ok = cols
    return jnp.where(jnp.asarray(ok)[:, None], jnp.take(w, jnp.asarray(idx), axis=0), 0.0)


def _mod_spec(m, tm):
    assert m.shape[1] == 1
    return pl.BlockSpec((1, 1, m.shape[2]), lambda b, i: (b, 0, 0))


def _ada_kernel(c_ref, w_ref, b_ref, o_ref):
    c = c_ref[...]
    s = c * (1.0 / (1.0 + jnp.exp(-c)))
    o_ref[...] = _mm(s, w_ref[...], HI) + b_ref[...]


def _ada(c, w, b):
    n, d = c.shape
    nout = w.shape[1]
    bn = 768
    return pl.pallas_call(
        _ada_kernel,
        grid=(nout // bn,),
        in_specs=[pl.BlockSpec((n, d), lambda j: (0, 0)),
                  pl.BlockSpec((d, bn), lambda j: (0, j)),
                  pl.BlockSpec((1, bn), lambda j: (0, j))],
        out_specs=pl.BlockSpec((n, bn), lambda j: (0, j)),
        out_shape=jax.ShapeDtypeStruct((n, nout), F32),
        compiler_params=_params(("arbitrary",)),
        name="ada",
    )(c, w, b.reshape(1, nout))


def _modulated(x_ref, sc_ref, sh_ref, g_ref):
    h = _rms(x_ref[0], g_ref[...])
    return (h * (1.0 + sc_ref[0]) + sh_ref[0]).astype(BF16)


def _proj_a_kernel(x_ref, sc_ref, sh_ref, g_ref, w_ref, lnw_ref, lnb_ref,
                   q_ref, k_ref, v_ref, kb_ref, vb_ref, iq_ref, ik_ref, ikb_ref, iw_ref):
    hb = _modulated(x_ref, sc_ref, sh_ref, g_ref)
    for h in range(8):
        zq = _mm(hb, w_ref[:, h * LANES:(h + 1) * LANES])
        q_ref[0, h] = (zq * (HEAD_DIM ** -0.5 * LOG2E)).astype(BF16)
    kv = _mm(hb, w_ref[:, 1024:1280])
    k = kv[:, :128]
    v = kv[:, 128:]
    k_ref[0] = k
    v_ref[0] = v
    kb_ref[0] = k.astype(BF16)
    vb_ref[0] = v.astype(BF16)
    for h in range(8):
        zi = _mm(hb, w_ref[:, 1280 + h * LANES:1280 + (h + 1) * LANES])
        iq_ref[0, h] = zi.astype(BF16)
    t = _mm(hb, w_ref[:, 2304:2560])
    ik = t[:, :128]
    lane = lax.broadcasted_iota(jnp.int32, ik.shape, 1)
    valid = lane < IDX_DIM
    mu = jnp.sum(ik, axis=-1, keepdims=True) * (1.0 / IDX_DIM)
    d = jnp.where(valid, ik - mu, 0.0)
    var = jnp.sum(d * d, axis=-1, keepdims=True) * (1.0 / IDX_DIM)
    ikn = d * lax.rsqrt(var + LN_EPS) * lnw_ref[...] + lnb_ref[...]
    ik_ref[0] = ikn[:, :IDX_DIM]
    ikb_ref[0] = ikn.astype(BF16)
    iw_ref[0] = t[:, 128:136] * IDX_SCALE


def _proj_a(x, sc, sh, g, w, lnw, lnb, tm):
    B, T, D = x.shape
    nt = T // tm
    tok = lambda n, dt: jax.ShapeDtypeStruct((B, T, n), dt)
    hm = lambda dt: jax.ShapeDtypeStruct((B, 8, T, LANES), dt)
    tspec = lambda n: pl.BlockSpec((1, tm, n), lambda b, i: (b, i, 0))
    hspec = pl.BlockSpec((1, 8, tm, LANES), lambda b, i: (b, 0, i, 0))
    full = lambda a: pl.BlockSpec(a.shape, lambda b, i: (0,) * a.ndim)
    return pl.pallas_call(
        _proj_a_kernel,
        grid=(B, nt),
        in_specs=[tspec(D), _mod_spec(sc, tm), _mod_spec(sh, tm), full(g), full(w), full(lnw), full(lnb)],
        out_specs=[hspec, tspec(128), tspec(128), tspec(128), tspec(128), hspec, tspec(IDX_DIM), tspec(128),
                   tspec(8)],
        out_shape=[hm(BF16), tok(128, F32), tok(128, F32), tok(128, BF16), tok(128, BF16), hm(BF16),
                   tok(IDX_DIM, F32), tok(128, BF16), tok(8, F32)],
        compiler_params=_params(("arbitrary", "arbitrary")),
        name="proj_att",
    )(x, sc, sh, g, w, lnw, lnb)


def _proj_b_kernel(x_ref, sc_ref, sh_ref, g_ref, w_ref, zr_ref, ga_ref, gb_ref):
    hb = _modulated(x_ref, sc_ref, sh_ref, g_ref)
    for j in range(ZRW // 256):
        zr_ref[0, :, j * 256:(j + 1) * 256] = _mm(hb, w_ref[:, j * 256:(j + 1) * 256])
    for j in range(4):
        za = _mm(hb, w_ref[:, ZRW + j * 256:ZRW + (j + 1) * 256])
        ga_ref[0, :, j * 256:(j + 1) * 256] = 1.0 / (1.0 + jnp.exp(-za))
        zb = _mm(hb, w_ref[:, ZRW + 1024 + j * 256:ZRW + 1024 + (j + 1) * 256])
        gb_ref[0, :, j * 256:(j + 1) * 256] = 1.0 / (1.0 + jnp.exp(-zb))


def _proj_b(x, sc, sh, g, w, tm):
    B, T, D = x.shape
    nt = T // tm
    tok = lambda n: jax.ShapeDtypeStruct((B, T, n), F32)
    tspec = lambda n: pl.BlockSpec((1, tm, n), lambda b, i: (b, i, 0))
    full = lambda a: pl.BlockSpec(a.shape, lambda b, i: (0,) * a.ndim)
    return pl.pallas_call(
        _proj_b_kernel,
        grid=(B, nt),
        in_specs=[tspec(D), _mod_spec(sc, tm), _mod_spec(sh, tm), full(g), full(w)],
        out_specs=[tspec(ZRW), tspec(1024), tspec(1024)],
        out_shape=[tok(ZRW), tok(1024), tok(1024)],
        compiler_params=_params(("arbitrary", "arbitrary")),
        name="proj_rwkv",
    )(x, sc, sh, g, w)


def _sortable_to_float(t):
    bits = t ^ (lax.shift_right_arithmetic(t, 31) & 0x7FFFFFFF)
    return lax.bitcast_convert_type(bits, F32)


def _top16(x):
    bits = lax.bitcast_convert_type(x, jnp.int32) & jnp.int32(-65536)
    return lax.bitcast_convert_type(bits, F32).astype(BF16)


def _attn_kernel(iq_ref, iw_ref, q_ref, ik_ref, k_ref, v_ref, o_ref, s_ref, sb_ref, m_ref, acc_ref,
                 *, R, KB, SB, HG, ltot, q_off, topk):
    j = pl.program_id(1)
    q0 = q_off + j * R
    pos = q0 + lax.broadcasted_iota(jnp.int32, (R, 1), 0)
    qchunk = lax.shift_right_logical(pos, 6)
    kend = jnp.minimum(ltot, (lax.shift_right_logical(q0 + R - 1, 6) + 1) * CHUNK)
    nkb = lax.shift_right_logical(kend + KB - 1, KB.bit_length() - 1)
    lane_k = lax.broadcasted_iota(jnp.int32, (R, KB), 1)
    nch = KB // LANES

    iq_all = iq_ref[0].reshape(8 * R, LANES)
    iw = iw_ref[0]
    iwb = [jnp.broadcast_to(iw[:, h:h + 1], (R, SB)) for h in range(8)]
    lane_s = lax.broadcasted_iota(jnp.int32, (R, SB), 1)
    nsb = KB // SB

    def score_body(kb, carry):
        for u in range(nsb):
            off = pl.multiple_of(kb * KB + u * SB, SB)
            ikb = ik_ref[0, pl.ds(off, SB), :]
            s_all = _nt(iq_all, ikb)
            acc = jnp.zeros((R, SB), F32)
            for h in range(8):
                acc = acc + jnp.maximum(s_all[h * R:(h + 1) * R], 0.0) * iwb[h]
            kidx = off + lane_s
            adm = (lax.shift_right_logical(kidx, 6) <= qchunk) & (kidx < ltot)
            sc = jnp.where(adm, acc, -jnp.inf)
            s_ref[kb, :, u * SB:(u + 1) * SB] = sc
            sb_ref[kb, :, u * SB:(u + 1) * SB] = _top16(sc)
        return carry

    lax.fori_loop(0, nkb, score_body, 0)

    def count_ge(ref, cand, dt):
        one, zero = jnp.ones((), dt), jnp.zeros((), dt)

        def body(kb, part):
            blk = ref[kb]
            m = jnp.where(blk >= cand, one, zero)
            for c in range(nch):
                part = part + m[:, c * LANES:(c + 1) * LANES]
            return part
        part = lax.fori_loop(0, nkb, body, jnp.zeros((R, LANES), dt))
        return jnp.sum(part.astype(F32), axis=1, keepdims=True)

    kf = float(topk)
    c0 = count_ge(sb_ref, jnp.zeros((R, 1), BF16), BF16)
    t0 = jnp.where(c0 >= kf, 0, INT_MIN).astype(jnp.int32)
    ex0 = jnp.where(c0 == kf, 1.0, 0.0)

    def accept(c, cand, cnt):
        t, ex = c
        ok = cnt >= kf
        return jnp.where(ok, cand, t), jnp.where(ok & (cnt == kf), 1.0, ex)

    def hi_body(i, c):
        cand = c[0] + lax.shift_left(jnp.int32(1), 30 - i)
        return accept(c, cand, count_ge(sb_ref, _top16(_sortable_to_float(cand)), BF16))

    def lo_pass(i, c):
        cand = c[0] + lax.shift_left(jnp.int32(1), 15 - i)
        return accept(c, cand, count_ge(s_ref, _sortable_to_float(cand), F32))

    def lo_group(c):
        gi, t, ex, _ = c
        tc = (t, ex)
        for jj in range(LO_GROUP):
            tc = lo_pass(gi * LO_GROUP + jj, tc)
        return gi + 1, tc[0], tc[1], (jnp.min(tc[1]) > 0.5).astype(jnp.int32)

    t, ex = lax.fori_loop(0, 15, hi_body, (t0, ex0))
    _, t, _, _ = lax.while_loop(lambda c: (c[0] < 16 // LO_GROUP) & (c[3] == 0), lo_group,
                                (jnp.int32(0), t, ex, (jnp.min(ex) > 0.5).astype(jnp.int32)))
    all_finite = t <= KEY_NEG_INF
    tau = jnp.where(all_finite, -jnp.inf, _sortable_to_float(jnp.maximum(t, KEY_NEG_INF)))

    def count_gt_eq(_):
        def body(kb, carry):
            pg, pe = carry
            blk = s_ref[kb]
            mg = jnp.where(blk > tau, 1.0, 0.0)
            me = jnp.where(blk == tau, 1.0, 0.0)
            for c in range(nch):
                pg = pg + mg[:, c * LANES:(c + 1) * LANES]
                pe = pe + me[:, c * LANES:(c + 1) * LANES]
            return pg, pe
        z = jnp.zeros((R, LANES), F32)
        pg, pe = lax.fori_loop(0, nkb, body, (z, z))
        return jnp.sum(pg, axis=1, keepdims=True), jnp.sum(pe, axis=1, keepdims=True)

    cnt_gt, cnt_eq = count_gt_eq(0)
    need = kf - cnt_gt
    tie = (cnt_eq > need) & jnp.logical_not(all_finite)
    any_tie = jnp.max(jnp.where(tie, 1.0, 0.0)) > 0.0

    def tie_bound():
        def count_eq_below(x):
            def body(kb, part):
                blk = s_ref[kb]
                kidx = kb * KB + lane_k
                m = jnp.where((blk == tau) & (kidx < x), 1.0, 0.0)
                for c in range(nch):
                    part = part + m[:, c * LANES:(c + 1) * LANES]
                return part
            part = lax.fori_loop(0, nkb, body, jnp.zeros((R, LANES), F32))
            return jnp.sum(part, axis=1, keepdims=True)

        nbits = int(ltot).bit_length()

        def body(i, x):
            cand = x + lax.shift_left(jnp.int32(1), nbits - 1 - i)
            ok = count_eq_below(cand) < need
            return jnp.where(ok, cand, x)
        x = lax.fori_loop(0, nbits, body, jnp.zeros((R, 1), jnp.int32))
        return x + 1

    big = jnp.full((R, 1), 1 << 30, jnp.int32)
    bound = lax.cond(any_tie, lambda: jnp.where(tie, tie_bound(), big), lambda: big)

    def bias_body(kb, carry):
        blk = s_ref[kb]
        kidx = kb * KB + lane_k
        sel = (blk > tau) | ((blk == tau) & (kidx < bound))
        sel = sel & (blk > -jnp.inf)
        s_ref[kb] = jnp.where(sel, 0.0, NEG_BIG)
        return carry

    lax.fori_loop(0, nkb, bias_body, 0)

    q_all = q_ref[0].reshape(8 * R, LANES)
    m_ref[...] = jnp.full(m_ref.shape, NEG_BIG, F32)
    acc_ref[...] = jnp.zeros(acc_ref.shape, F32)

    def att_body(kb, carry):
        for u in range(nsb):
            off = pl.multiple_of(kb * KB + u * SB, SB)
            kblk = k_ref[0, pl.ds(off, SB), :]
            vblk = v_ref[0, pl.ds(off, SB), :]
            bias = s_ref[kb, :, u * SB:(u + 1) * SB]
            logits = _nt(q_all, kblk)
            nch = SB // LANES
            for h0 in range(0, 8, HG):
                hs = list(range(h0, h0 + HG))
                lg = [logits[h * R:(h + 1) * R] + bias for h in hs]
                cm = [functools.reduce(jnp.maximum, [l[:, c * LANES:(c + 1) * LANES] for c in range(nch)])
                      for l in lg]
                m_old = [m_ref[h] for h in hs]
                m_new = [jnp.maximum(mo, jnp.max(c, axis=1, keepdims=True)) for mo, c in zip(m_old, cm)]
                alpha = [jnp.exp2(mo - mn) for mo, mn in zip(m_old, m_new)]
                p = [jnp.exp2(l - jnp.concatenate([mn] * nch, axis=1)).astype(BF16) for l, mn in zip(lg, m_new)]
                pv = [_mm(pp, vblk) for pp in p]
                for i, h in enumerate(hs):
                    acc_ref[h] = jnp.concatenate([alpha[i], alpha[i]], axis=1) * acc_ref[h] + pv[i]
                    m_ref[h] = m_new[i]
        return carry

    lax.fori_loop(0, nkb, att_body, 0)
    for h in range(8):
        a = acc_ref[h]
        o_ref[0, h] = (a[:, :LANES] / a[:, LANES:]).astype(BF16)


def _attention(iq, iw, q, ikb, kb, vb1, *, R, KB, SB, HG, ltot, q_off):
    B, _, Sq, _ = q.shape
    Lp = kb.shape[1]
    topk = min(TOPK_MAX, ltot // 4)
    assert Lp % KB == 0 and KB % SB == 0 and KB >= topk and Sq % R == 0
    hspec = pl.BlockSpec((1, 8, R, LANES), lambda b, i: (b, 0, i, 0))
    kspec = pl.BlockSpec((1, Lp, LANES), lambda b, i: (b, 0, 0))
    vspec = pl.BlockSpec((1, Lp, 2 * LANES), lambda b, i: (b, 0, 0))
    kern = functools.partial(_attn_kernel, R=R, KB=KB, SB=SB, HG=HG, ltot=ltot, q_off=q_off, topk=topk)
    return pl.pallas_call(
        kern,
        grid=(B, Sq // R),
        in_specs=[hspec, pl.BlockSpec((1, R, 8), lambda b, i: (b, i, 0)), hspec, kspec, kspec, vspec],
        out_specs=hspec,
        out_shape=jax.ShapeDtypeStruct((B, 8, Sq, LANES), BF16),
        scratch_shapes=[pltpu.VMEM((Lp // KB, R, KB), F32), pltpu.VMEM((Lp // KB, R, KB), BF16),
                        pltpu.VMEM((8, R, LANES), F32),
                        pltpu.VMEM((8, R, 2 * LANES), F32)],
        compiler_params=_params(("arbitrary", "arbitrary")),
        name="dsa_attention",
    )(iq, iw, q, ikb, kb, vb1)


def _head_sum(x):
    parts = []
    for h in range(8):
        s = jnp.sum(x[:, h * LANES:(h + 1) * LANES], axis=1, keepdims=True)
        parts.append(jnp.broadcast_to(s, (x.shape[0], LANES)))
    return jnp.concatenate(parts, axis=1)


def _rwkv_prep_kernel(z_ref, sh0_ref, mu_ref, w0_ref, wup_ref, a0_ref, aup_ref, gup_ref, kk_ref, ka_ref, rk_ref,
                      r_o, lw_o, k_o, v_o, kkn_o, b_o, g_o, bg_o, lwt_o, kt_o, bt_o, carry_ref,
                      *, tm, t_valid):
    i = pl.program_id(1)

    @pl.when(i == 0)
    def _():
        carry_ref[...] = sh0_ref[0]

    z = z_ref[0]
    row = lax.broadcasted_iota(jnp.int32, (tm, 1), 0)
    prev = jnp.where(row == 0, carry_ref[...], pltpu.roll(z, 1, axis=0))
    carry_ref[...] = z[tm - 1:tm, :]
    zm = z + (prev - z) * mu_ref[...]
    live = i * tm + row < t_valid
    zm = jnp.where(live, zm, 0.0)
    r = zm[:, 0:HP]
    k = zm[:, HP:2 * HP]
    v = zm[:, 2 * HP:3 * HP]
    wa = zm[:, 3 * HP:3 * HP + LANES]
    gd = zm[:, 3 * HP + LANES:3 * HP + 2 * LANES]
    w_raw = w0_ref[...] + _mm(jnp.tanh(wa), wup_ref[...], HI)
    lw = (-float(np.exp(-0.5))) / (1.0 + jnp.exp(-w_raw))
    lw = jnp.where(live, lw, 0.0)
    a = 1.0 / (1.0 + jnp.exp(-(a0_ref[...] + _mm(wa, aup_ref[...], HI))))
    g = _mm(1.0 / (1.0 + jnp.exp(-gd)), gup_ref[...], HI)
    kk = k * kk_ref[...]
    kk = kk * lax.rsqrt(jnp.maximum(_head_sum(kk * kk), L2_EPS))
    k_mod = k * (1.0 + (a - 1.0) * ka_ref[...])
    b = kk * a
    bonus = _head_sum(r * k_mod * rk_ref[...]) * v
    r_o[0] = r.astype(BF16)
    lw_o[0] = lw
    k_o[0] = k_mod.astype(BF16)
    v_o[0] = v.astype(BF16)
    kkn_o[0] = kk.astype(BF16)
    b_o[0] = b.astype(BF16)
    g_o[0] = g
    bg_o[0] = bonus * g
    lwt_o[0] = lw.T
    kt_o[0] = k_mod.T.astype(BF16)
    bt_o[0] = b.T.astype(BF16)


def _rwkv_prep(zr, shift0, prm, tm, t_valid):
    B, T, _ = zr.shape
    tspec = pl.BlockSpec((1, tm, HP), lambda b, i: (b, i, 0))
    fspec = pl.BlockSpec((1, HP, tm), lambda b, i: (b, 0, i))
    full = lambda a: pl.BlockSpec(a.shape, lambda b, i: (0,) * a.ndim)
    tok = jax.ShapeDtypeStruct((B, T, HP), F32)
    feat = jax.ShapeDtypeStruct((B, HP, T), F32)
    tokb = jax.ShapeDtypeStruct((B, T, HP), BF16)
    featb = jax.ShapeDtypeStruct((B, HP, T), BF16)
    names =("mu", "w0", "w_up", "a0", "a_up", "g_up", "k_k", "k_a", "r_k")
    ws = [prm[n] for n in names]
    return pl.pallas_call(
        functools.partial(_rwkv_prep_kernel, tm=tm, t_valid=t_valid),
        grid=(B, T // tm),
        in_specs=[pl.BlockSpec((1, tm, ZRW), lambda b, i: (b, i, 0)),
                  pl.BlockSpec((1, 1, ZRW), lambda b, i: (b, 0, 0))] + [full(w) for w in ws],
        out_specs=[tspec] * 8 + [fspec] * 3,
        out_shape=[tokb, tok, tokb, tokb, tokb, tokb, tok, tok, feat, featb, featb],
        scratch_shapes=[pltpu.VMEM((1, ZRW), F32)],
        compiler_params=_params(("arbitrary", "arbitrary")),
        name="rwkv_prep",
    )(zr, shift0, *ws)


def _rwkv_chunk_kernel(r_ref, lw_ref, k_ref, v_ref, kk_ref, b_ref, lwt_ref, kt_ref, bt_ref, a1_ref, a2_ref,
                       *, C, hb, pg, pc, po):
    ri = lax.broadcasted_iota(jnp.int32, (C, C), 0)
    ci = lax.broadcasted_iota(jnp.int32, (C, C), 1)
    low_incl = jnp.where(ri >= ci, 1.0, 0.0).astype(BF16)
    up_incl = jnp.where(ri <= ci, 1.0, 0.0).astype(BF16)
    strict = ri > ci
    incl = ri >= ci
    eye64 = jnp.where(lax.broadcasted_iota(jnp.int32, (64, 64), 0) == lax.broadcasted_iota(jnp.int32, (64, 64), 1),
                      1.0, 0.0)
    sls = [slice(u * LANES, (u + 1) * LANES) for u in range(hb)]
    each = lambda f, *ls: [f(*a) for a in zip(*ls)]
    lw = [lw_ref[0, :, sl] for sl in sls]
    r = [r_ref[0, :, sl].astype(F32) for sl in sls]
    k = [k_ref[0, :, sl].astype(F32) for sl in sls]
    v = [v_ref[0, :, sl].astype(F32) for sl in sls]
    kk = [kk_ref[0, :, sl].astype(F32) for sl in sls]
    b = [b_ref[0, :, sl].astype(F32) for sl in sls]
    lwt = [lwt_ref[0, sl, :][:64] for sl in sls]
    kt = [kt_ref[0, sl, :][:64].astype(F32) for sl in sls]
    bt = [bt_ref[0, sl, :][:64].astype(F32) for sl in sls]
    cum = each(lambda x: sum(_mm(low_incl, p) for p in _split_bf16(x, 3)), lw)
    cumt = each(lambda x: sum(_mm(p, up_incl) for p in _split_bf16(x, 3)), lwt)
    e_neg = each(lambda c: jnp.exp(-c), cum)
    at = each(lambda kk_, c, l: -kk_ * jnp.exp(c - l), kk, cum, lw)
    rt = each(lambda r_, c: r_ * jnp.exp(c), r, cum)
    g = each(lambda a, r_, b_, k_, e: _mmp(jnp.concatenate([a, r_], axis=0),
                                           jnp.concatenate([b_ * e, k_ * e], axis=0), pg, nt=True),
             at, rt, b, k, e_neg)
    n = each(lambda g_: jnp.where(strict, g_[:C, :C], 0.0), g)
    aak = each(lambda g_: jnp.where(strict, g_[:C, C:], 0.0), g)
    lrb = each(lambda g_: jnp.where(incl, g_[C:, :C], 0.0), g)
    lrk = each(lambda g_: jnp.where(incl, g_[C:, C:], 0.0), g)
    blk8 = lax.shift_right_logical(ri, 3) == lax.shift_right_logical(ci, 3)
    eye = jnp.where(ri == ci, 1.0, 0.0)
    n0 = each(lambda n_: jnp.where(blk8, n_, 0.0), n)
    x = each(lambda n_: eye + n_, n0)
    p = each(lambda n_: _mmp(n_, n_, pc), n0)
    x = each(lambda x_, p_: x_ + _mmp(x_, p_, pc), x, p)
    p = each(lambda p_: _mmp(p_, p_, pc), p)
    x = each(lambda x_, p_: x_ + _mmp(x_, p_, pc), x, p)
    for lv in range(3, int(C).bit_length() - 1):
        off = (lax.shift_right_logical(ri, lv + 1) == lax.shift_right_logical(ci, lv + 1)) & (
            lax.shift_right_logical(ri, lv) != lax.shift_right_logical(ci, lv))
        t = each(lambda n_, x_: _mmp(jnp.where(off, n_, 0.0), x_, pc), n, x)
        x = each(lambda x_, t_: x_ + _mmp(x_, t_, pc), x, t)
    z = each(lambda x_, a, b_: _mmp(x_, jnp.concatenate([a, b_], axis=1), pc), x, aak, at)
    w2v = each(lambda z_, v_: _mmp(z_[:, :C], v_, po), z, v)
    cl = each(lambda c: c[:, C - 1:C], cumt)
    eb = each(lambda c_, ct: jnp.exp(c_ - ct), cl, cumt)
    lhs = each(lambda bt_, kt_, e, lb, lk: jnp.concatenate(
        [jnp.concatenate([bt_ * e, kt_ * e], axis=1), jnp.concatenate([lb, lk], axis=1)], axis=0),
        bt, kt, eb, lrb, lrk)
    left = each(lambda l_, z_: _mmp(l_[:, :C], z_[:, C:], po), lhs, z)
    right = each(lambda l_, w, v_: _mmp(l_, jnp.concatenate([w, v_], axis=0), po), lhs, w2v, v)
    for u in range(hb):
        diag = eye64 * jnp.exp(cl[u])
        a1_ref[0, u, 0] = jnp.concatenate([left[u][:64, :64] + diag, (left[u][64:] + rt[u])[:, :64]], axis=0)
        a2_ref[0, u, 0] = right[u]


def _rwkv_chunks(r, lw, k, v, kk, b, lwt, kt, bt, C, hb, passes=RW_PASSES):
    B, T, _ = r.shape
    nc = T // C
    tspec = pl.BlockSpec((1, C, hb * LANES), lambda bb, h, c: (bb, c, h))
    fspec = pl.BlockSpec((1, hb * LANES, C), lambda bb, h, c: (bb, h, c))
    pg, pc, po = passes
    return pl.pallas_call(
        functools.partial(_rwkv_chunk_kernel, C=C, hb=hb, pg=pg, pc=pc, po=po),
        grid=(B, 8 // hb, nc),
        in_specs=[tspec] * 6 + [fspec] * 3,
        out_specs=[pl.BlockSpec((1, hb, 1, 64 + C, 64), lambda bb, h, c: (bb, h, c, 0, 0)),
                   pl.BlockSpec((1, hb, 1, 64 + C, LANES), lambda bb, h, c: (bb, h, c, 0, 0))],
        out_shape=[jax.ShapeDtypeStruct((B, 8, nc, 64 + C, 64), F32),
                   jax.ShapeDtypeStruct((B, 8, nc, 64 + C, LANES), F32)],
        compiler_params=_params(("arbitrary", "arbitrary", "arbitrary")),
        name="rwkv_chunks",
    )(r, lw, k, v, kk, b, lwt, kt, bt)


def _rwkv_scan_kernel(a1_ref, a2_ref, s0_ref, y_ref, sT_ref, h_ref, *, C, cb):
    c = pl.program_id(1)

    @pl.when(c == 0)
    def _():
        h_ref[...] = s0_ref[0]

    for cc in range(cb):
        for h in range(8):
            res = _mm(a1_ref[0, h, cc], h_ref[h], HI) + a2_ref[0, h, cc]
            h_ref[h] = res[:64]
            y_ref[0, cc * C:(cc + 1) * C, h * LANES:(h + 1) * LANES] = res[64:]

    @pl.when(c == pl.num_programs(1) - 1)
    def _():
        sT_ref[0] = h_ref[...]


def _rwkv_scan(a1, a2, s0t, C, cb):
    B, _, nc, _, _ = a1.shape
    return pl.pallas_call(
        functools.partial(_rwkv_scan_kernel, C=C, cb=cb),
        grid=(B, nc // cb),
        in_specs=[pl.BlockSpec((1, 8, cb, 64 + C, 64), lambda b, c: (b, 0, c, 0, 0)),
                  pl.BlockSpec((1, 8, cb, 64 + C, LANES), lambda b, c: (b, 0, c, 0, 0)),
                  pl.BlockSpec((1, 8, 64, LANES), lambda b, c: (b, 0, 0, 0))],
        out_specs=[pl.BlockSpec((1, cb * C, HP), lambda b, c: (b, c, 0)),
                   pl.BlockSpec((1, 8, 64, LANES), lambda b, c: (b, 0, 0, 0))],
        out_shape=[jax.ShapeDtypeStruct((B, nc * C, HP), F32),
                   jax.ShapeDtypeStruct((B, 8, 64, LANES), F32)],
        scratch_shapes=[pltpu.VMEM((8, 64, LANES), F32)],
        compiler_params=_params(("arbitrary", "arbitrary")),
        name="rwkv_scan",
    )(a1, a2, s0t)


def _mix_kernel(att_ref, y_ref, g_ref, bg_ref, sga_ref, sgb_ref, x_ref, gt1_ref, sc2_ref, sh2_ref,
                woa_ref, wor_ref, wout_ref, lnw_ref, lnb_ref, gpost_ref, gpre_ref, wr_ref, br_ref,
                x1_ref, h2_ref, gate_ref, eidx_ref, wgt_ref, rel_ref, cnt_out_ref, cnt_ref, *, tm):
    att = jnp.concatenate([att_ref[0, h] for h in range(8)], axis=1)
    lane = lax.broadcasted_iota(jnp.int32, (tm, LANES), 1)
    valid = lane < RWKV_HEAD
    parts = []
    for h in range(8):
        y = y_ref[0, :, h * LANES:(h + 1) * LANES]
        mu = jnp.sum(y, axis=1, keepdims=True) * (1.0 / RWKV_HEAD)
        d = jnp.where(valid, y - mu, 0.0)
        var = jnp.sum(d * d, axis=1, keepdims=True) * (1.0 / RWKV_HEAD)
        parts.append(d * lax.rsqrt(var + GN_EPS))
    yn = jnp.concatenate(parts, axis=1)
    rw = (yn * lnw_ref[...] + lnb_ref[...]) * g_ref[0] + bg_ref[0]
    mix = sga_ref[0] * _mm(att, woa_ref[...]) + sgb_ref[0] * _mm(rw.astype(BF16), wor_ref[...])
    o = _mm(mix.astype(BF16), wout_ref[...])
    x1 = x_ref[0] + gt1_ref[0] * _rms(o, gpost_ref[...])
    x1_ref[0] = x1
    h2 = _rms(x1, gpre_ref[...]) * (1.0 + sc2_ref[0]) + sh2_ref[0]
    h2_ref[0] = h2
    logits = _mm(h2, wr_ref[...], HI) + br_ref[...]
    el = lax.broadcasted_iota(jnp.int32, logits.shape, 1).astype(F32)
    work = logits
    sel = jnp.zeros(logits.shape, jnp.bool_)
    vmax = None
    firsts = []
    for kk in range(TOP_K):
        mx = jnp.max(work, axis=1, keepdims=True)
        if kk == 0:
            vmax = mx
        first = jnp.min(jnp.where(work == mx, el, float(N_EXPERTS)), axis=1, keepdims=True)
        hit = el == first
        firsts.append(first)
        sel = sel | hit
        work = jnp.where(hit, -jnp.inf, work)
    e = jnp.where(sel, jnp.exp(logits - vmax), 0.0)
    gate = e / jnp.sum(e, axis=1, keepdims=True)
    gate_ref[0] = gate
    @pl.when((pl.program_id(0) == 0) & (pl.program_id(1) == 0))
    def _():
        cnt_ref[...] = jnp.zeros(cnt_ref.shape, F32)

    self = jnp.where(sel, 1.0, 0.0)
    ti = lax.broadcasted_iota(jnp.int32, (tm, tm), 0)
    tj = lax.broadcasted_iota(jnp.int32, (tm, tm), 1)
    before = _mm(jnp.where(tj < ti, 1.0, 0.0).astype(BF16), self.astype(BF16)) + cnt_ref[...]
    cnt_ref[...] = cnt_ref[...] + jnp.sum(self, axis=0, keepdims=True)
    cnt_out_ref[...] = cnt_ref[...]
    ln = lax.broadcasted_iota(jnp.int32, (tm, LANES), 1)
    eo = jnp.zeros((tm, LANES), F32)
    go = jnp.zeros((tm, LANES), F32)
    ro = jnp.zeros((tm, LANES), F32)
    for kk in range(TOP_K):
        hit = el == firsts[kk]
        gk = jnp.sum(jnp.where(hit, gate, 0.0), axis=1, keepdims=True)
        rk = jnp.sum(jnp.where(hit, before, 0.0), axis=1, keepdims=True)
        eo = jnp.where(ln == kk, firsts[kk], eo)
        go = jnp.where(ln == kk, gk, go)
        ro = jnp.where(ln == kk, rk, ro)
    eidx_ref[0] = eo.astype(jnp.int32)
    wgt_ref[0] = go
    rel_ref[0] = ro.astype(jnp.int32)


def _mix(att, y, g, bg, sga, sgb, x, gt1, sc2, sh2, wts, tm):
    B, T, D = x.shape
    tspec = lambda n: pl.BlockSpec((1, tm, n), lambda b, i: (b, i, 0))
    full = lambda a: pl.BlockSpec(a.shape, lambda b, i: (0,) * a.ndim)
    tok = lambda n, dt: jax.ShapeDtypeStruct((B, T, n), dt)
    return pl.pallas_call(
        functools.partial(_mix_kernel, tm=tm),
        grid=(B, T // tm),
        in_specs=[pl.BlockSpec((1, 8, tm, LANES), lambda b, i: (b, 0, i, 0)), tspec(HP), tspec(HP), tspec(HP),
                  tspec(D), tspec(D), tspec(D), _mod_spec(gt1, tm), _mod_spec(sc2, tm), _mod_spec(sh2, tm)]
                 + [full(w) for w in wts],
        out_specs=[tspec(D), tspec(D), tspec(N_EXPERTS), tspec(LANES), tspec(LANES), tspec(LANES),
                   pl.BlockSpec((1, N_EXPERTS), lambda b, i: (0, 0))],
        out_shape=[tok(D, F32), tok(D, F32), tok(N_EXPERTS, F32), tok(LANES, jnp.int32), tok(LANES, F32),
                   tok(LANES, jnp.int32), jax.ShapeDtypeStruct((1, N_EXPERTS), F32)],
        scratch_shapes=[pltpu.VMEM((1, N_EXPERTS), F32)],
        compiler_params=_params(("arbitrary", "arbitrary")),
        name="mix_router",
    )(att, y, g, bg, sga, sgb, x, gt1, sc2, sh2, *wts)


def _expert(xb, wgu, bgu, wd, bd, d_ff):
    wgu = wgu.astype(BF16)
    wd = wd.astype(BF16)
    hgu = _mm(xb, wgu) + bgu
    hg = jnp.minimum(hgu[:, :d_ff], SWIGLU_LIMIT)
    hl = jnp.clip(hgu[:, d_ff:], -SWIGLU_LIMIT, SWIGLU_LIMIT)
    act = hg * (1.0 / (1.0 + jnp.exp(-SWIGLU_ALPHA * hg))) * (hl + 1.0)
    return _mm(act.astype(BF16), wd) + bd


def _sc_mesh():
    return plsc.VectorSubcoreMesh(core_axis_name="core", subcore_axis_name="subcore")


def _sc_dispatch(h, pos_slots, n_rows, q):
    N, D = h.shape
    mesh = _sc_mesh()
    steps = N // (mesh.num_cores * SC_WINDOW)
    assert steps * mesh.num_cores * SC_WINDOW == N

    @pl.kernel(out_type=jax.ShapeDtypeStruct((n_rows, SC_PIECE), h.dtype), mesh=mesh, scratch_types=[])
    def scatter(x_hbm, *refs):
        idx_hbm, o_hbm = refs[:-1], refs[-1]
        base = lax.axis_index("core") * steps

        def body(x_vmem, *i_vmem):
            for iv in i_vmem:
                pltpu.sync_copy(x_vmem, o_hbm.at[iv.at[0]])

        pltpu.emit_pipeline(
            body,
            grid=(steps,),
            in_specs=[pl.BlockSpec((SC_WINDOW, SC_PIECE), index_map=lambda i: (base + i, q))]
                     + [pl.BlockSpec((1, SC_WINDOW), index_map=lambda i: (0, base + i))] * len(idx_hbm),
            out_specs=[],
            core_axis_name="subcore",
            dimension_semantics=(pltpu.PARALLEL,),
        )(x_hbm, *idx_hbm)

    return scatter(h, *pos_slots)


def _sc_gather(table, idx):
    M = idx.shape[1]
    mesh = _sc_mesh()
    steps = M // (mesh.num_cores * SC_WINDOW)
    assert steps * mesh.num_cores * SC_WINDOW == M

    @pl.kernel(out_type=jax.ShapeDtypeStruct((M, SC_PIECE), table.dtype), mesh=mesh)
    def gather(x_hbm, i_hbm, o_hbm):
        base = lax.axis_index("core") * steps

        def body(i_vmem, o_vmem):
            pltpu.sync_copy(x_hbm.at[i_vmem.at[0]], o_vmem)

        pltpu.emit_pipeline(
            body,
            grid=(steps,),
            in_specs=[pl.BlockSpec((1, SC_WINDOW), index_map=lambda i: (0, base + i))],
            out_specs=[pl.BlockSpec((SC_WINDOW, SC_PIECE), index_map=lambda i: (base + i, 0))],
            core_axis_name="subcore",
            dimension_semantics=(pltpu.PARALLEL,),
        )(i_hbm, o_hbm)

    return gather(table, idx)


def _moe_grouped_kernel(te_ref, nt_ref, *refs, d_ff, npiece):
    x_refs, (wgu_ref, bgu_ref, wd_ref, bd_ref) = refs[:npiece], refs[npiece:npiece + 4]
    o_refs, (wgu_b, wd_b) = refs[npiece + 4:2 * npiece + 4], refs[2 * npiece + 4:]
    g = pl.program_id(0)

    @pl.when((g == 0) | (te_ref[g] != te_ref[jnp.maximum(g - 1, 0)]))
    def _():
        wgu_b[...] = wgu_ref[0].astype(BF16)
        wd_b[...] = wd_ref[0].astype(BF16)

    @pl.when(g < nt_ref[0])
    def _():
        xb = jnp.concatenate([r[...] for r in x_refs], axis=1).astype(BF16)
        y = _expert(xb, wgu_b[...], bgu_ref[0], wd_b[...], bd_ref[0], d_ff)
        for q, o_ref in enumerate(o_refs):
            o_ref[...] = y[:, q * SC_PIECE:(q + 1) * SC_PIECE]


def _moe_grouped(xs, te, nt, wgu, bgu, wd, bd, TG):
    NP = xs[0].shape[0]
    E, D, F2 = wgu.shape
    npiece = len(xs)
    wmap = lambda g, te_, nt_: (te_[g], 0, 0)
    xmap = lambda g, te_, nt_: (jnp.minimum(g, nt_[0] - 1), 0)
    pspec = pl.BlockSpec((TG, SC_PIECE), xmap)
    return pl.pallas_call(
        functools.partial(_moe_grouped_kernel, d_ff=F2 // 2, npiece=npiece),
        grid_spec=pltpu.PrefetchScalarGridSpec(
            num_scalar_prefetch=2,
            grid=(NP // TG,),
            in_specs=[pspec] * npiece + [pl.BlockSpec((1, D, F2), wmap), pl.BlockSpec((1, 1, F2), wmap),
                                         pl.BlockSpec((1, F2 // 2, D), wmap), pl.BlockSpec((1, 1, D), wmap)],
            out_specs=[pspec] * npiece,
            scratch_shapes=[pltpu.VMEM((D, F2), BF16), pltpu.VMEM((F2 // 2, D), BF16)]),
        out_shape=[jax.ShapeDtypeStruct((NP, SC_PIECE), F32)] * npiece,
        compiler_params=_params(("arbitrary",)),
        name="moe_grouped",
    )(te, nt, *xs, wgu, bgu, wd, bd)


def _moe_combine_kernel(*refs, npiece):
    y_refs = refs[:TOP_K * npiece]
    w_ref, x1_ref, gt2_ref, gpost_ref, o_ref = refs[TOP_K * npiece:]
    w = w_ref[...]
    acc = None
    for k in range(TOP_K):
        yk = jnp.concatenate([y_refs[k * npiece + q][...] for q in range(npiece)], axis=1)
        term = w[:, k:k + 1] * yk
        acc = term if acc is None else acc + term
    o_ref[...] = x1_ref[...] + gt2_ref[0] * _rms(acc, gpost_ref[...])


def _moe_combine(yslot, wgt, x1, gt2, gpost, tm):
    N, D = x1.shape
    nb = N // tm
    tpb = N // gt2.shape[0]
    npiece = len(yslot)
    yspec = lambda k: pl.BlockSpec((tm, SC_PIECE), lambda i, k=k: (i + k * nb, 0))
    return pl.pallas_call(
        functools.partial(_moe_combine_kernel, npiece=npiece),
        grid=(nb,),
        in_specs=[yspec(k) for k in range(TOP_K) for _ in range(npiece)]
                 + [pl.BlockSpec((tm, LANES), lambda i: (i, 0)), pl.BlockSpec((tm, D), lambda i: (i, 0)),
                    pl.BlockSpec((1, 1, D), lambda i: ((i * tm) // tpb, 0, 0)), pl.BlockSpec((1, D), lambda i: (0, 0))],
        out_specs=pl.BlockSpec((tm, D), lambda i: (i, 0)),
        out_shape=jax.ShapeDtypeStruct((N, D), F32),
        compiler_params=_params(("arbitrary",)),
        name="moe_combine",
    )(*[yslot[q] for _ in range(TOP_K) for q in range(npiece)], wgt, x1, gt2, gpost)


def _moe_kernel(h_ref, gate_ref, x1_ref, gt2_ref, gpost_ref, wgu_ref, bgu_ref, wd_ref, bd_ref, o_ref, acc_ref,
                *, d_ff):
    e = pl.program_id(1)

    @pl.when(e == 0)
    def _():
        acc_ref[...] = jnp.zeros(acc_ref.shape, F32)

    contrib = _expert(h_ref[...].astype(BF16), wgu_ref[0], bgu_ref[0], wd_ref[0], bd_ref[0], d_ff)
    gate = gate_ref[...]
    el = lax.broadcasted_iota(jnp.int32, gate.shape, 1)
    ge = jnp.sum(jnp.where(el == e, gate, 0.0), axis=1, keepdims=True)
    acc_ref[...] += ge * contrib

    @pl.when(e == pl.num_programs(1) - 1)
    def _():
        o_ref[...] = x1_ref[...] + gt2_ref[0] * _rms(acc_ref[...], gpost_ref[...])


def _moe(h2, gate, x1, gt2, gpost, wgu, bgu, wd, bd, tm):
    N, D = h2.shape
    E, _, F2 = wgu.shape
    tspec = lambda n: pl.BlockSpec((tm, n), lambda i, e: (i, 0))
    if gt2.shape[1] == 1:
        tpb = N // gt2.shape[0]
        gspec = pl.BlockSpec((1, 1, D), lambda i, e: ((i * tm) // tpb, 0, 0))
    else:
        gspec = pl.BlockSpec((1, tm, D), lambda i, e: (i, 0, 0))
    return pl.pallas_call(
        functools.partial(_moe_kernel, d_ff=F2 // 2),
        grid=(N // tm, E),
        in_specs=[tspec(D), tspec(E), tspec(D), gspec, pl.BlockSpec((1, D), lambda i, e: (0, 0)),
                  pl.BlockSpec((1, D, F2), lambda i, e: (e, 0, 0)), pl.BlockSpec((1, 1, F2), lambda i, e: (e, 0, 0)),
                  pl.BlockSpec((1, F2 // 2, D), lambda i, e: (e, 0, 0)), pl.BlockSpec((1, 1, D), lambda i, e: (e, 0, 0))],
        out_specs=tspec(D),
        out_shape=jax.ShapeDtypeStruct((N, D), F32),
        scratch_shapes=[pltpu.VMEM((tm, D), F32)],
        compiler_params=_params(("arbitrary", "arbitrary")),
        name="moe",
    )(h2, gate, x1, gt2, gpost, wgu, bgu, wd, bd)


def _per_token(m, T):
    B, _, D = m.shape
    return jnp.broadcast_to(m, (B, T, D)).reshape(1, B * T, D)


def _layer(x, mods, past, s0, shift0, P, cfg):
    B, T, D = x.shape
    sh1, sc1, gt1, sh2, sc2, gt2 = mods
    tm = cfg["tm"]

    q, k, v, kb, vb, iq, ik, ikb, iw = _proj_a(x, sc1, sh1, P["g_pre_mix"], P["w_a"], P["ik_ln_w"], P["ik_ln_b"],
                                              cfg["tm_a"])
    zr, sga, sgb = _proj_b(x, sc1, sh1, P["g_pre_mix"], P["w_b"], tm)

    if past is not None:
        pk, pv, pik = past
        plen = pk.shape[1]
        kb_all = jnp.concatenate([pk.reshape(B, plen, 128).astype(BF16), kb], axis=1)
        vb_all = jnp.concatenate([pv.reshape(B, plen, 128).astype(BF16), vb], axis=1)
        ik_all = jnp.concatenate([jnp.pad(pik, ((0, 0), (0, 0), (0, 64))).astype(BF16), ikb], axis=1)
    else:
        plen = 0
        kb_all, vb_all, ik_all = kb, vb, ikb
    ltot = plen + T
    KB = cfg["KB"]
    lp = -(-ltot // KB) * KB
    if lp != ltot:
        padk = ((0, 0), (0, lp - ltot), (0, 0))
        kb_all, vb_all, ik_all = jnp.pad(kb_all, padk), jnp.pad(vb_all, padk), jnp.pad(ik_all, padk)
    vb1 = jnp.concatenate([vb_all, jnp.ones_like(vb_all)], axis=-1)
    att = _attention(iq, iw, q, ik_all, kb_all, vb1, R=cfg["R"], KB=KB, SB=cfg["SB"], HG=cfg["HG"], ltot=ltot,
                     q_off=plen)

    C = RW_CHUNK
    tp = -(-T // C) * C
    zr_p = zr if tp == T else jnp.pad(zr, ((0, 0), (0, tp - T), (0, 0)))
    shift_pad = _take_cols(shift0.reshape(B, RWKV_COLS), _COLS_ZR).reshape(B, 1, ZRW)
    r, lw, km, vv, kkn, bb, g, bg, lwt, kt, bt = _rwkv_prep(zr_p, shift_pad, P, min(cfg["tm_rw"], tp), T)
    a1, a2 = _rwkv_chunks(r, lw, km, vv, kkn, bb, lwt, kt, bt, C, cfg["hb"])
    s0t = jnp.pad(jnp.swapaxes(s0, 2, 3), ((0, 0), (0, 0), (0, 0), (0, 64)))
    nc = tp // C
    cb = min(cfg["cb"], nc)
    y, sT = _rwkv_scan(a1, a2, s0t, C, cb)
    s_new = jnp.swapaxes(sT[..., :64], 2, 3)
    zlast = zr[:, T - 1]
    inv = np.zeros((RWKV_COLS,), np.int32)
    inv[_COLS_ZR[0][_COLS_ZR[1]]] = np.nonzero(_COLS_ZR[1])[0]
    shift_new = jnp.take(zlast, jnp.asarray(inv), axis=1).reshape(B, 1, RWKV_COLS)
    if tp != T:
        y, g, bg = y[:, :T], g[:, :T], bg[:, :T]

    wts = [P[n] for n in ("w_o_att", "w_o_rwkv", "w_out", "ln_x_w", "ln_x_b", "g_post_mix", "g_pre_ffn",
                          "w_router", "b_router")]
    x1, h2, gate, eidx, wgt, rel, cnt = _mix(att, y, g, bg, sga, sgb, x, gt1, sc2, sh2, wts, tm)

    N = B * T
    tmm = min(cfg["tm_moe"], N)
    if cfg["routed"]:
        TG = cfg["TG"]
        ntm = N * TOP_K // TG + N_EXPERTS
        cnt_i = cnt[0].astype(jnp.int32)
        padded = (cnt_i + TG - 1) // TG * TG
        ends = jnp.cumsum(padded)
        off = ends - padded
        nt = ends[-1] // TG
        gi = jnp.minimum(jnp.arange(ntm, dtype=jnp.int32), nt - 1)
        te = jnp.sum((ends // TG)[None, :] <= gi[:, None], axis=1).astype(jnp.int32)
        e4 = eidx.reshape(N, LANES)[:, :TOP_K]
        pos = jnp.take(off, e4) + rel.reshape(N, LANES)[:, :TOP_K]
        pos_t = pos.T
        npiece = D // SC_PIECE
        h2f = h2.reshape(N, D)
        xs = [_sc_dispatch(h2f, [pos_t[k:k + 1] for k in range(TOP_K)], ntm * TG, q) for q in range(npiece)]
        ys = _moe_grouped(xs, te, nt.reshape(1), P["w_gu"], P["b_gu"], P["w_down"], P["b_down"], TG)
        idx_c = pos_t.reshape(1, TOP_K * N)
        yslot = [_sc_gather(ys[q], idx_c) for q in range(npiece)]
        out = _moe_combine(yslot, wgt.reshape(N, LANES), x1.reshape(N, D), gt2, P["g_post_ffn"], tm)
    else:
        if T % tmm == 0:
            gt2m = gt2
        else:
            gt2m = _per_token(gt2, T).reshape(N // tmm, tmm, D)
        out = _moe(h2.reshape(N, D), gate.reshape(N, N_EXPERTS), x1.reshape(N, D), gt2m, P["g_post_ffn"],
                   P["w_gu"], P["b_gu"], P["w_down"], P["b_down"], tmm)
    return (out.reshape(B, T, D), k.reshape(B, T, ATT_KV_HEADS, HEAD_DIM), v.reshape(B, T, ATT_KV_HEADS, HEAD_DIM),
            ik, s_new, shift_new)


def _prep_weights(l, w_in, ik_ln_w, ik_ln_b, mu_rwkv, w0, w_up, a0, a_up, g_up, k_k, k_a, r_k, ln_x_w, ln_x_b,
                  w_o_att, w_o_rwkv, w_out, w_router, b_router, w_gu, b_gu, w_down, b_down,
                  g_pre_mix, g_post_mix, g_pre_ffn, g_post_ffn):
    row = lambda a: a.reshape(1, -1)
    hp = lambda a: _take_cols(row(a), _HEAD_IDX)
    P = {}
    P["w_a"] = _take_cols(w_in[l], _COLS_A).astype(BF16)
    P["w_b"] = _take_cols(w_in[l], _COLS_B).astype(BF16)
    P["ik_ln_w"] = jnp.pad(row(ik_ln_w[l]), ((0, 0), (0, 64)))
    P["ik_ln_b"] = jnp.pad(row(ik_ln_b[l]), ((0, 0), (0, 64)))
    P["mu"] = _take_cols(row(mu_rwkv[l]), _COLS_ZR)
    P["w0"], P["a0"], P["k_k"], P["k_a"] = hp(w0[l]), hp(a0[l]), hp(k_k[l]), hp(k_a[l])
    P["r_k"] = hp(r_k[l].reshape(-1))
    P["ln_x_w"], P["ln_x_b"] = hp(ln_x_w[l]), hp(ln_x_b[l])
    z64 = jnp.zeros((64, HP), F32)
    P["w_up"] = jnp.concatenate([_take_cols(w_up[l], _HEAD_IDX), z64], axis=0)
    P["a_up"] = jnp.concatenate([z64, _take_cols(a_up[l], _HEAD_IDX)], axis=0)
    P["g_up"] = _take_cols(g_up[l], _HEAD_IDX)
    P["w_o_att"] = _take_rows(w_o_att[l], _ATT_ROW_IDX).astype(BF16)
    P["w_o_rwkv"] = _take_rows(w_o_rwkv[l], _HEAD_IDX).astype(BF16)
    P["w_out"] = w_out[l].astype(BF16)
    P["w_router"] = w_router[l]
    P["b_router"] = row(b_router[l])
    P["w_gu"] = w_gu[l]
    P["b_gu"] = b_gu[l].reshape(N_EXPERTS, 1, -1)
    P["w_down"] = w_down[l]
    P["b_down"] = b_down[l].reshape(N_EXPERTS, 1, -1)
    P["g_pre_mix"], P["g_post_mix"] = row(g_pre_mix[l]), row(g_post_mix[l])
    P["g_pre_ffn"], P["g_post_ffn"] = row(g_pre_ffn[l]), row(g_post_ffn[l])
    return P


CFG_PROMPT = dict(tm=256, tm_a=1024, R=256,KB=512, SB=256, HG=2,tm_rw=256, hb=8, cb=2, tm_moe=1024, routed=True, TG=512)
CFG_SAMPLE = dict(tm=32, tm_a=32, R=32, KB=512, SB=256, HG=2,tm_rw=128, hb=8, cb=1, tm_moe=512, routed=False)


def kernel(x_prompt, x_sample, c_prompt, c_sample, cache_k, cache_v, cache_idx_k, state_rwkv, state_shift, w_ada, b_ada, g_pre_mix, g_post_mix, g_pre_ffn, g_post_ffn, w_in, ik_ln_w, ik_ln_b, mu_rwkv, w0, w_up, a0, a_up, g_up, k_k, k_a, r_k, ln_x_w, ln_x_b, w_o_att, w_o_rwkv, w_out, w_router, b_router, w_gu, b_gu, w_down, b_down):
    depth = w_in.shape[0]
    bp, tp_, D = x_prompt.shape
    bs, ts, _ = x_sample.shape
    y_p, y_s = x_prompt, x_sample
    st_p = [[] for _ in range(5)]
    st_s = [[] for _ in range(5)]
    nc_all = bp + bs
    npad = -(-nc_all // 8) * 8
    c_all = jnp.pad(jnp.concatenate([c_prompt, c_sample], axis=0), ((0, npad - nc_all), (0, 0)))
    for l in range(depth):
        P = _prep_weights(l, w_in, ik_ln_w, ik_ln_b, mu_rwkv, w0, w_up, a0, a_up, g_up, k_k, k_a, r_k, ln_x_w,
                          ln_x_b, w_o_att, w_o_rwkv, w_out, w_router, b_router, w_gu, b_gu, w_down, b_down,
                          g_pre_mix, g_post_mix, g_pre_ffn, g_post_ffn)
        ada = _ada(c_all, w_ada[l], b_ada[l])
        mods_p = [m[:bp, None, :] for m in jnp.split(ada, 6, axis=-1)]
        mods_s = [m[bp:nc_all, None, :] for m in jnp.split(ada, 6, axis=-1)]
        zero_state = jnp.zeros((bp, RWKV_HEADS, RWKV_HEAD, RWKV_HEAD), F32)
        zero_shift = jnp.zeros((bp, 1, RWKV_COLS), F32)
        outs_p = _layer(y_p, mods_p, None, zero_state, zero_shift, P, CFG_PROMPT)
        outs_s = _layer(y_s, mods_s, (cache_k[l], cache_v[l], cache_idx_k[l]), state_rwkv[l], state_shift[l], P,
                        CFG_SAMPLE)
        y_p, y_s = outs_p[0], outs_s[0]
        for lst, val in zip(st_p, outs_p[1:]):
            lst.append(val)
        for lst, val in zip(st_s, outs_s[1:]):
            lst.append(val)
    sp = [jnp.stack(v, axis=0) for v in st_p]
    ss = [jnp.stack(v, axis=0) for v in st_s]
    return (y_p, y_s, sp[0], sp[1], sp[2], sp[3], sp[4], ss[0], ss[1], ss[2], ss[3], ss[4])
```

```python
import functools

import numpy as np
import jax
import jax.numpy as jnp
from jax import lax
from jax.experimental import pallas as pl
from jax.experimental.pallas import tpu as pltpu
from jax.experimental.pallas import tpu_sc as plsc

F32 = jnp.float32
BF16 = jnp.bfloat16
HI = lax.Precision.HIGHEST

CHUNK = 64
ATT_HEADS = 8
ATT_KV_HEADS = 2
HEAD_DIM = 64
IDX_HEADS = 8
IDX_DIM = 64
IDX_SCALE = (IDX_DIM ** -0.5) * (IDX_HEADS ** -0.5)
TOPK_MAX = 256
RWKV_HEADS = 8
RWKV_HEAD = 64
RWKV_WIDTH = RWKV_HEADS * RWKV_HEAD
DECAY_LORA = 64
AAA_LORA = 64
GATE_LORA = 128
RWKV_COLS = 3 * RWKV_WIDTH + DECAY_LORA + AAA_LORA + GATE_LORA
GN_EPS = 64e-5
L2_EPS = 1e-24
N_EXPERTS = 32
TOP_K = 4
SWIGLU_LIMIT = 7.0
SWIGLU_ALPHA = 1.702
NORM_EPS = 1e-6
LN_EPS = 1e-6

LANES = 128
VMEM_LIMIT = 56 * 1024 * 1024

HP = 8 * LANES
RW_CHUNK = 128
NEG_BIG = -1e30
LOG2E = 1.4426950408889634
KEY_NEG_INF = -2139095041
INT_MIN = -2147483648
SC_WINDOW = 128
SC_PIECE = 256
LO_GROUP = 4
RW_PASSES = (1, 1, 1)


def _params(sem):
    return pltpu.CompilerParams(dimension_semantics=sem, vmem_limit_bytes=VMEM_LIMIT)


def _nt(a, b, precision=None):
    return lax.dot_general(a, b, (((1,), (1,)), ((), ())), precision=precision,
                           preferred_element_type=F32)


def _mm(a, b, precision=None):
    return jnp.dot(a, b, precision=precision, preferred_element_type=F32)


def _split_bf16(x, terms):
    out = []
    for _ in range(terms):
        p = x.astype(BF16)
        out.append(p)
        x = x - p.astype(F32)
    return out


def _mmp(a, b, passes, nt=False):
    dot = _nt if nt else _mm
    if passes == 1:
        return dot(a.astype(BF16), b.astype(BF16))
    ah, al = _split_bf16(a, 2)
    bh, bl = _split_bf16(b, 2)
    return dot(ah, bh) + (dot(ah, bl) + dot(al, bh))


def _rms(x, g):
    return x * lax.rsqrt(jnp.mean(x * x, axis=-1, keepdims=True) + NORM_EPS) * g


def _headpad_idx(seg_off, lane_off_fn=lambda h: 0):
    idx = np.zeros((HP,), np.int32)
    ok = np.zeros((HP,), bool)
    for h in range(8):
        d0 = h * LANES + lane_off_fn(h)
        idx[d0:d0 + 64] = seg_off + h * 64 + np.arange(64)
        ok[d0:d0 + 64] = True
    return idx, ok


def _plain_idx(seg_off, n, width):
    idx = np.zeros((width,), np.int32)
    ok = np.zeros((width,), bool)
    idx[:n] = seg_off + np.arange(n)
    ok[:n] = True
    return idx, ok


def _cat(parts):
    return np.concatenate([p[0] for p in parts]), np.concatenate([p[1] for p in parts])


_O_Q, _O_K, _O_V, _O_IQ, _O_IK, _O_IW, _O_ZR, _O_GA, _O_GB = 0, 512, 640, 768, 1280, 1344, 1352, 3144, 4168
_COLS_A = _cat([_headpad_idx(_O_Q, lambda h: (h // 4) * 64), _plain_idx(_O_K, 128, 128), _plain_idx(_O_V, 128, 128),
                _headpad_idx(_O_IQ), _plain_idx(_O_IK, 64, 128), _plain_idx(_O_IW, 8, 128)])
_COLS_ZR = _cat([_headpad_idx(0), _headpad_idx(512), _headpad_idx(1024), _plain_idx(1536, 128, 128),
                 _plain_idx(1664, 128, 128)])
ZRW = _COLS_ZR[0].shape[0]
_COLS_B = _cat([(_COLS_ZR[0] + _O_ZR, _COLS_ZR[1]), _plain_idx(_O_GA, 1024, 1024), _plain_idx(_O_GB, 1024, 1024)])
NA = _COLS_A[0].shape[0]
NB = _COLS_B[0].shape[0]
_HEAD_IDX = _headpad_idx(0)
_ATT_ROW_IDX = _headpad_idx(0, lambda h: (h // 4) * 64)


def _take_cols(w, cols):
    idx, ok = cols
    return jnp.where(jnp.asarray(ok)[None, :], jnp.take(w, jnp.asarray(idx), axis=1), 0.0)


def _take_rows(w, cols):
    idx, ok = cols
    return jnp.where(jnp.asarray(ok)[:, None], jnp.take(w, jnp.asarray(idx), axis=0), 0.0)


def _mod_spec(m, tm):
    assert m.shape[1] == 1
    return pl.BlockSpec((1, 1, m.shape[2]), lambda b, i: (b, 0, 0))


def _ada_kernel(c_ref, w_ref, b_ref, o_ref):
    c = c_ref[...]
    s = c * (1.0 / (1.0 + jnp.exp(-c)))
    o_ref[...] = _mm(s, w_ref[...], HI) + b_ref[...]


def _ada(c, w, b):
    n, d = c.shape
    nout = w.shape[1]
    bn = 768
    return pl.pallas_call(
        _ada_kernel,
        grid=(nout // bn,),
        in_specs=[pl.BlockSpec((n, d), lambda j: (0, 0)),
                  pl.BlockSpec((d, bn), lambda j: (0, j)),
                  pl.BlockSpec((1, bn), lambda j: (0, j))],
        out_specs=pl.BlockSpec((n, bn), lambda j: (0, j)),
        out_shape=jax.ShapeDtypeStruct((n, nout), F32),
        compiler_params=_params(("arbitrary",)),
        name="ada",
    )(c, w, b.reshape(1, nout))


def _modulated(x_ref, sc_ref, sh_ref, g_ref):
    h = _rms(x_ref[0], g_ref[...])
    return (h * (1.0 + sc_ref[0]) + sh_ref[0]).astype(BF16)


def _proj_a_kernel(x_ref, sc_ref, sh_ref, g_ref, w_ref, lnw_ref, lnb_ref,
                   q_ref, k_ref, v_ref, kb_ref, vb_ref, iq_ref, ik_ref, ikb_ref, iw_ref):
    hb = _modulated(x_ref, sc_ref, sh_ref, g_ref)
    for h in range(8):
        zq = _mm(hb, w_ref[:, h * LANES:(h + 1) * LANES])
        q_ref[0, h] = (zq * (HEAD_DIM ** -0.5 * LOG2E)).astype(BF16)
    kv = _mm(hb, w_ref[:, 1024:1280])
    k = kv[:, :128]
    v = kv[:, 128:]
    k_ref[0] = k
    v_ref[0] = v
    kb_ref[0] = k.astype(BF16)
    vb_ref[0] = v.astype(BF16)
    for h in range(8):
        zi = _mm(hb, w_ref[:, 1280 + h * LANES:1280 + (h + 1) * LANES])
        iq_ref[0, h] = zi.astype(BF16)
    t = _mm(hb, w_ref[:, 2304:2560])
    ik = t[:, :128]
    lane = lax.broadcasted_iota(jnp.int32, ik.shape, 1)
    valid = lane < IDX_DIM
    mu = jnp.sum(ik, axis=-1, keepdims=True) * (1.0 / IDX_DIM)
    d = jnp.where(valid, ik - mu, 0.0)
    var = jnp.sum(d * d, axis=-1, keepdims=True) * (1.0 / IDX_DIM)
    ikn = d * lax.rsqrt(var + LN_EPS) * lnw_ref[...] + lnb_ref[...]
    ik_ref[0] = ikn[:, :IDX_DIM]
    ikb_ref[0] = ikn.astype(BF16)
    iw_ref[0] = t[:, 128:136] * IDX_SCALE


def _proj_a(x, sc, sh, g, w, lnw, lnb, tm):
    B, T, D = x.shape
    nt = T // tm
    tok = lambda n, dt: jax.ShapeDtypeStruct((B, T, n), dt)
    hm = lambda dt: jax.ShapeDtypeStruct((B, 8, T, LANES), dt)
    tspec = lambda n: pl.BlockSpec((1, tm, n), lambda b, i: (b, i, 0))
    hspec = pl.BlockSpec((1, 8, tm, LANES), lambda b, i: (b, 0, i, 0))
    full = lambda a: pl.BlockSpec(a.shape, lambda b, i: (0,) * a.ndim)
    return pl.pallas_call(
        _proj_a_kernel,
        grid=(B, nt),
        in_specs=[tspec(D), _mod_spec(sc, tm), _mod_spec(sh, tm), full(g), full(w), full(lnw), full(lnb)],
        out_specs=[hspec, tspec(128), tspec(128), tspec(128), tspec(128), hspec, tspec(IDX_DIM), tspec(128),
                   tspec(8)],
        out_shape=[hm(BF16), tok(128, F32), tok(128, F32), tok(128, BF16), tok(128, BF16), hm(BF16),
                   tok(IDX_DIM, F32), tok(128, BF16), tok(8, F32)],
        compiler_params=_params(("arbitrary", "arbitrary")),
        name="proj_att",
    )(x, sc, sh, g, w, lnw, lnb)


def _proj_b_kernel(x_ref, sc_ref, sh_ref, g_ref, w_ref, zr_ref, ga_ref, gb_ref):
    hb = _modulated(x_ref, sc_ref, sh_ref, g_ref)
    for j in range(ZRW // 256):
        zr_ref[0, :, j * 256:(j + 1) * 256] = _mm(hb, w_ref[:, j * 256:(j + 1) * 256])
    for j in range(4):
        za = _mm(hb, w_ref[:, ZRW + j * 256:ZRW + (j + 1) * 256])
        ga_ref[0, :, j * 256:(j + 1) * 256] = 1.0 / (1.0 + jnp.exp(-za))
        zb = _mm(hb, w_ref[:, ZRW + 1024 + j * 256:ZRW + 1024 + (j + 1) * 256])
        gb_ref[0, :, j * 256:(j + 1) * 256] = 1.0 / (1.0 + jnp.exp(-zb))


def _proj_b(x, sc, sh, g, w, tm):
    B, T, D = x.shape
    nt = T // tm
    tok = lambda n: jax.ShapeDtypeStruct((B, T, n), F32)
    tspec = lambda n: pl.BlockSpec((1, tm, n), lambda b, i: (b, i, 0))
    full = lambda a: pl.BlockSpec(a.shape, lambda b, i: (0,) * a.ndim)
    return pl.pallas_call(
        _proj_b_kernel,
        grid=(B, nt),
        in_specs=[tspec(D), _mod_spec(sc, tm), _mod_spec(sh, tm), full(g), full(w)],
        out_specs=[tspec(ZRW), tspec(1024), tspec(1024)],
        out_shape=[tok(ZRW), tok(1024), tok(1024)],
        compiler_params=_params(("arbitrary", "arbitrary")),
        name="proj_rwkv",
    )(x, sc, sh, g, w)


def _sortable_to_float(t):
    bits = t ^ (lax.shift_right_arithmetic(t, 31) & 0x7FFFFFFF)
    return lax.bitcast_convert_type(bits, F32)


def _float_to_sortable(x):
    b = lax.bitcast_convert_type(x, jnp.int32)
    return b ^ (lax.shift_right_arithmetic(b, 31) & 0x7FFFFFFF)


def _top16(x):
    bits = lax.bitcast_convert_type(x, jnp.int32) & jnp.int32(-65536)
    return lax.bitcast_convert_type(bits, F32).astype(BF16)


def _attn_kernel(iq_ref, iw_ref, q_ref, ik_ref, k_ref, v_ref, o_ref, s_ref, sb_ref, m_ref, acc_ref,
                 *, R, KB, SB, HG, ltot, q_off, topk):
    j = pl.program_id(1)
    q0 = q_off + j * R
    pos = q0 + lax.broadcasted_iota(jnp.int32, (R, 1), 0)
    qchunk = lax.shift_right_logical(pos, 6)
    kend = jnp.minimum(ltot, (lax.shift_right_logical(q0 + R - 1, 6) + 1) * CHUNK)
    nkb = lax.shift_right_logical(kend + KB - 1, KB.bit_length() - 1)
    lane_k = lax.broadcasted_iota(jnp.int32, (R, KB), 1)
    nch = KB // LANES

    iq_all = iq_ref[0].reshape(8 * R, LANES)
    iw = iw_ref[0]
    iwb = [jnp.broadcast_to(iw[:, h:h + 1], (R, SB)) for h in range(8)]
    lane_s = lax.broadcasted_iota(jnp.int32, (R, SB), 1)
    nsb = KB // SB

    def score_body(kb, carry):
        for u in range(nsb):
            off = pl.multiple_of(kb * KB + u * SB, SB)
            ikb = ik_ref[0, pl.ds(off, SB), :]
            s_all = _nt(iq_all, ikb)
            acc = jnp.zeros((R, SB), F32)
            for h in range(8):
                acc = acc + jnp.maximum(s_all[h * R:(h + 1) * R], 0.0) * iwb[h]
            kidx = off + lane_s
            adm = (lax.shift_right_logical(kidx, 6) <= qchunk) & (kidx < ltot)
            sc = jnp.where(adm, acc, -jnp.inf)
            s_ref[kb, :, u * SB:(u + 1) * SB] = sc
            sb_ref[kb, :, u * SB:(u + 1) * SB] = _top16(sc)
        return carry

    lax.fori_loop(0, nkb, score_body, 0)

    def count_ge(ref, cand, dt):
        one, zero = jnp.ones((), dt), jnp.zeros((), dt)

        def body(kb, part):
            blk = ref[kb]
            m = jnp.where(blk >= cand, one, zero)
            for c in range(nch):
                part = part + m[:, c * LANES:(c + 1) * LANES]
            return part
        part = lax.fori_loop(0, nkb, body, jnp.zeros((R, LANES), dt))
        return jnp.sum(part.astype(F32), axis=1, keepdims=True)

    kf = float(topk)
    c0 = count_ge(sb_ref, jnp.zeros((R, 1), BF16), BF16)
    t0 = jnp.where(c0 >= kf, 0, INT_MIN).astype(jnp.int32)
    ex0 = jnp.where(c0 == kf, 1.0, 0.0)

    def accept(c, cand, cnt):
        t, ex = c
        ok = cnt >= kf
        return jnp.where(ok, cand, t), jnp.where(ok & (cnt == kf), 1.0, ex)

    def hi_body(i, c):
        cand = c[0] + lax.shift_left(jnp.int32(1), 30 - i)
        return accept(c, cand, count_ge(sb_ref, _top16(_sortable_to_float(cand)), BF16))

    def lo_pass(i, c):
        cand = c[0] + lax.shift_left(jnp.int32(1), 15 - i)
        return accept(c, cand, count_ge(s_ref, _sortable_to_float(cand), F32))

    def lo_group(c):
        gi, t, ex, _ = c
        tc = (t, ex)
        for jj in range(LO_GROUP):
            tc = lo_pass(gi * LO_GROUP + jj, tc)
        return gi + 1, tc[0], tc[1], (jnp.min(tc[1]) > 0.5).astype(jnp.int32)

    def cls_body(kb, gm):
        blk = sb_ref[kb]
        for c in range(nsb):
            gm = jnp.maximum(gm, blk[:, c * SB:(c + 1) * SB])
        return gm

    gm = lax.fori_loop(0, nkb, cls_body, jnp.full((R, SB), -jnp.inf, BF16)).astype(F32)
    k_lo = _float_to_sortable(jnp.min(gm, axis=1, keepdims=True)) & jnp.int32(-65536)
    k_hi = _float_to_sortable(jnp.max(gm, axis=1, keepdims=True)) | jnp.int32(0xFFFF)
    top = jnp.max((31 - lax.clz(k_lo ^ k_hi)).astype(F32)).astype(jnp.int32)
    bracketed = top < 31
    keep = jnp.bitwise_not(lax.shift_left(jnp.int32(1), jnp.minimum(top, 30) + 1) - 1)
    t1 = jnp.where(bracketed, k_lo & keep, t0)
    ex1 = jnp.where(bracketed, 0.0, ex0)
    t, ex = lax.fori_loop(jnp.where(bracketed, 30 - jnp.minimum(top, 30), 0), 15, hi_body, (t1, ex1))
    _, t, _, _ = lax.while_loop(lambda c: (c[0] < 16 // LO_GROUP) & (c[3] == 0), lo_group,
                                (jnp.int32(0), t, ex, (jnp.min(ex) > 0.5).astype(jnp.int32)))
    all_finite = t <= KEY_NEG_INF
    tau = jnp.where(all_finite, -jnp.inf, _sortable_to_float(jnp.maximum(t, KEY_NEG_INF)))

    def count_gt_eq(_):
        def body(kb, carry):
            pg, pe = carry
            blk = s_ref[kb]
            mg = jnp.where(blk > tau, 1.0, 0.0)
            me = jnp.where(blk == tau, 1.0, 0.0)
            for c in range(nch):
                pg = pg + mg[:, c * LANES:(c + 1) * LANES]
                pe = pe + me[:, c * LANES:(c + 1) * LANES]
            return pg, pe
        z = jnp.zeros((R, LANES), F32)
        pg, pe = lax.fori_loop(0, nkb, body, (z, z))
        return jnp.sum(pg, axis=1, keepdims=True), jnp.sum(pe, axis=1, keepdims=True)

    cnt_gt, cnt_eq = count_gt_eq(0)
    need = kf - cnt_gt
    tie = (cnt_eq > need) & jnp.logical_not(all_finite)
    any_tie = jnp.max(jnp.where(tie, 1.0, 0.0)) > 0.0

    def tie_bound():
        def count_eq_below(x):
            def body(kb, part):
                blk = s_ref[kb]
                kidx = kb * KB + lane_k
                m = jnp.where((blk == tau) & (kidx < x), 1.0, 0.0)
                for c in range(nch):
                    part = part + m[:, c * LANES:(c + 1) * LANES]
                return part
            part = lax.fori_loop(0, nkb, body, jnp.zeros((R, LANES), F32))
            return jnp.sum(part, axis=1, keepdims=True)

        nbits = int(ltot).bit_length()

        def body(i, x):
            cand = x + lax.shift_left(jnp.int32(1), nbits - 1 - i)
            ok = count_eq_below(cand) < need
            return jnp.where(ok, cand, x)
        x = lax.fori_loop(0, nbits, body, jnp.zeros((R, 1), jnp.int32))
        return x + 1

    big = jnp.full((R, 1), 1 << 30, jnp.int32)
    bound = lax.cond(any_tie, lambda: jnp.where(tie, tie_bound(), big), lambda: big)

    def bias_body(kb, carry):
        blk = s_ref[kb]
        kidx = kb * KB + lane_k
        sel = (blk > tau) | ((blk == tau) & (kidx < bound))
        sel = sel & (blk > -jnp.inf)
        s_ref[kb] = jnp.where(sel, 0.0, NEG_BIG)
        return carry

    lax.fori_loop(0, nkb, bias_body, 0)

    q_all = q_ref[0].reshape(8 * R, LANES)
    m_ref[...] = jnp.full(m_ref.shape, NEG_BIG, F32)
    acc_ref[...] = jnp.zeros(acc_ref.shape, F32)

    def att_body(kb, carry):
        for u in range(nsb):
            off = pl.multiple_of(kb * KB + u * SB, SB)
            kblk = k_ref[0, pl.ds(off, SB), :]
            vblk = v_ref[0, pl.ds(off, SB), :]
            bias = s_ref[kb, :, u * SB:(u + 1) * SB]
            logits = _nt(q_all, kblk)
            nch = SB // LANES
            for h0 in range(0, 8, HG):
                hs = list(range(h0, h0 + HG))
                lg = [logits[h * R:(h + 1) * R] + bias for h in hs]
                cm = [functools.reduce(jnp.maximum, [l[:, c * LANES:(c + 1) * LANES] for c in range(nch)])
                      for l in lg]
                m_old = [m_ref[h] for h in hs]
                m_new = [jnp.maximum(mo, jnp.max(c, axis=1, keepdims=True)) for mo, c in zip(m_old, cm)]
                alpha = [jnp.exp2(mo - mn) for mo, mn in zip(m_old, m_new)]
                p = [jnp.exp2(l - jnp.concatenate([mn] * nch, axis=1)).astype(BF16) for l, mn in zip(lg, m_new)]
                pv = [_mm(pp, vblk) for pp in p]
                for i, h in enumerate(hs):
                    acc_ref[h] = jnp.concatenate([alpha[i], alpha[i]], axis=1) * acc_ref[h] + pv[i]
                    m_ref[h] = m_new[i]
        return carry

    lax.fori_loop(0, nkb, att_body, 0)
    for h in range(8):
        a = acc_ref[h]
        o_ref[0, h] = (a[:, :LANES] / a[:, LANES:]).astype(BF16)


def _attention(iq, iw, q, ikb, kb, vb1, *, R, KB, SB, HG, ltot, q_off):
    B, _, Sq, _ = q.shape
    Lp = kb.shape[1]
    topk = min(TOPK_MAX, ltot // 4)
    assert Lp % KB == 0 and KB % SB == 0 and SB >= topk and Sq % R == 0
    hspec = pl.BlockSpec((1, 8, R, LANES), lambda b, i: (b, 0, i, 0))
    kspec = pl.BlockSpec((1, Lp, LANES), lambda b, i: (b, 0, 0))
    vspec = pl.BlockSpec((1, Lp, 2 * LANES), lambda b, i: (b, 0, 0))
    kern = functools.partial(_attn_kernel, R=R, KB=KB, SB=SB, HG=HG, ltot=ltot, q_off=q_off, topk=topk)
    return pl.pallas_call(
        kern,
        grid=(B, Sq // R),
        in_specs=[hspec, pl.BlockSpec((1, R, 8), lambda b, i: (b, i, 0)), hspec, kspec, kspec, vspec],
        out_specs=hspec,
        out_shape=jax.ShapeDtypeStruct((B, 8, Sq, LANES), BF16),
        scratch_shapes=[pltpu.VMEM((Lp // KB, R, KB), F32), pltpu.VMEM((Lp // KB, R, KB), BF16),
                        pltpu.VMEM((8, R, LANES), F32),
                        pltpu.VMEM((8, R, 2 * LANES), F32)],
        compiler_params=_params(("arbitrary", "arbitrary")),
        name="dsa_attention",
    )(iq, iw, q, ikb, kb, vb1)


def _head_sum(x):
    parts = []
    for h in range(8):
        s = jnp.sum(x[:, h * LANES:(h + 1) * LANES], axis=1, keepdims=True)
        parts.append(jnp.broadcast_to(s, (x.shape[0], LANES)))
    return jnp.concatenate(parts, axis=1)


def _rwkv_prep_kernel(z_ref, sh0_ref, mu_ref, w0_ref, wup_ref, a0_ref, aup_ref, gup_ref, kk_ref, ka_ref, rk_ref,
                      r_o, lw_o, k_o, v_o, kkn_o, b_o, g_o, bg_o, lwt_o, kt_o, bt_o, carry_ref,
                      *, tm, t_valid):
    i = pl.program_id(1)

    @pl.when(i == 0)
    def _():
        carry_ref[...] = sh0_ref[0]

    z = z_ref[0]
    row = lax.broadcasted_iota(jnp.int32, (tm, 1), 0)
    prev = jnp.where(row == 0, carry_ref[...], pltpu.roll(z, 1, axis=0))
    carry_ref[...] = z[tm - 1:tm, :]
    zm = z + (prev - z) * mu_ref[...]
    live = i * tm + row < t_valid
    zm = jnp.where(live, zm, 0.0)
    r = zm[:, 0:HP]
    k = zm[:, HP:2 * HP]
    v = zm[:, 2 * HP:3 * HP]
    wa = zm[:, 3 * HP:3 * HP + LANES]
    gd = zm[:, 3 * HP + LANES:3 * HP + 2 * LANES]
    w_raw = w0_ref[...] + _mm(jnp.tanh(wa), wup_ref[...], HI)
    lw = (-float(np.exp(-0.5))) / (1.0 + jnp.exp(-w_raw))
    lw = jnp.where(live, lw, 0.0)
    a = 1.0 / (1.0 + jnp.exp(-(a0_ref[...] + _mm(wa, aup_ref[...], HI))))
    g = _mm(1.0 / (1.0 + jnp.exp(-gd)), gup_ref[...], HI)
    kk = k * kk_ref[...]
    kk = kk * lax.rsqrt(jnp.maximum(_head_sum(kk * kk), L2_EPS))
    k_mod = k * (1.0 + (a - 1.0) * ka_ref[...])
    b = kk * a
    bonus = _head_sum(r * k_mod * rk_ref[...]) * v
    r_o[0] = r.astype(BF16)
    lw_o[0] = lw
    k_o[0] = k_mod.astype(BF16)
    v_o[0] = v.astype(BF16)
    kkn_o[0] = kk.astype(BF16)
    b_o[0] = b.astype(BF16)
    g_o[0] = g
    bg_o[0] = bonus * g
    lwt_o[0] = lw.T
    kt_o[0] = k_mod.T.astype(BF16)
    bt_o[0] = b.T.astype(BF16)


def _rwkv_prep(zr, shift0, prm, tm, t_valid):
    B, T, _ = zr.shape
    tspec = pl.BlockSpec((1, tm, HP), lambda b, i: (b, i, 0))
    fspec = pl.BlockSpec((1, HP, tm), lambda b, i: (b, 0, i))
    full = lambda a: pl.BlockSpec(a.shape, lambda b, i: (0,) * a.ndim)
    tok = jax.ShapeDtypeStruct((B, T, HP), F32)
    feat = jax.ShapeDtypeStruct((B, HP, T), F32)
    tokb = jax.ShapeDtypeStruct((B, T, HP), BF16)
    featb = jax.ShapeDtypeStruct((B, HP, T), BF16)
    names =("mu", "w0", "w_up", "a0", "a_up", "g_up", "k_k", "k_a", "r_k")
    ws = [prm[n] for n in names]
    return pl.pallas_call(
        functools.partial(_rwkv_prep_kernel, tm=tm, t_valid=t_valid),
        grid=(B, T // tm),
        in_specs=[pl.BlockSpec((1, tm, ZRW), lambda b, i: (b, i, 0)),
                  pl.BlockSpec((1, 1, ZRW), lambda b, i: (b, 0, 0))] + [full(w) for w in ws],
        out_specs=[tspec] * 8 + [fspec] * 3,
        out_shape=[tokb, tok, tokb, tokb, tokb, tokb, tok, tok, feat, featb, featb],
        scratch_shapes=[pltpu.VMEM((1, ZRW), F32)],
        compiler_params=_params(("arbitrary", "arbitrary")),
        name="rwkv_prep",
    )(zr, shift0, *ws)


def _rwkv_chunk_kernel(r_ref, lw_ref, k_ref, v_ref, kk_ref, b_ref, lwt_ref, kt_ref, bt_ref, a1_ref, a2_ref,
                       *, C, hb, pg, pc, po):
    ri = lax.broadcasted_iota(jnp.int32, (C, C), 0)
    ci = lax.broadcasted_iota(jnp.int32, (C, C), 1)
    low_incl = jnp.where(ri >= ci, 1.0, 0.0).astype(BF16)
    up_incl = jnp.where(ri <= ci, 1.0, 0.0).astype(BF16)
    strict = ri > ci
    incl = ri >= ci
    eye64 = jnp.where(lax.broadcasted_iota(jnp.int32, (64, 64), 0) == lax.broadcasted_iota(jnp.int32, (64, 64), 1),
                      1.0, 0.0)
    sls = [slice(u * LANES, (u + 1) * LANES) for u in range(hb)]
    each = lambda f, *ls: [f(*a) for a in zip(*ls)]
    lw = [lw_ref[0, :, sl] for sl in sls]
    r = [r_ref[0, :, sl].astype(F32) for sl in sls]
    k = [k_ref[0, :, sl].astype(F32) for sl in sls]
    v = [v_ref[0, :, sl].astype(F32) for sl in sls]
    kk = [kk_ref[0, :, sl].astype(F32) for sl in sls]
    b = [b_ref[0, :, sl].astype(F32) for sl in sls]
    lwt = [lwt_ref[0, sl, :][:64] for sl in sls]
    kt = [kt_ref[0, sl, :][:64].astype(F32) for sl in sls]
    bt = [bt_ref[0, sl, :][:64].astype(F32) for sl in sls]
    cum = each(lambda x: sum(_mm(low_incl, p) for p in _split_bf16(x, 3)), lw)
    cumt = each(lambda x: sum(_mm(p, up_incl) for p in _split_bf16(x, 3)), lwt)
    e_neg = each(lambda c: jnp.exp(-c), cum)
    at = each(lambda kk_, c, l: -kk_ * jnp.exp(c - l), kk, cum, lw)
    rt = each(lambda r_, c: r_ * jnp.exp(c), r, cum)
    g = each(lambda a, r_, b_, k_, e: _mmp(jnp.concatenate([a, r_], axis=0),
                                           jnp.concatenate([b_ * e, k_ * e], axis=0), pg, nt=True),
             at, rt, b, k, e_neg)
    n = each(lambda g_: jnp.where(strict, g_[:C, :C], 0.0), g)
    aak = each(lambda g_: jnp.where(strict, g_[:C, C:], 0.0), g)
    lrb = each(lambda g_: jnp.where(incl, g_[C:, :C], 0.0), g)
    lrk = each(lambda g_: jnp.where(incl, g_[C:, C:], 0.0), g)
    blk8 = lax.shift_right_logical(ri, 3) == lax.shift_right_logical(ci, 3)
    eye = jnp.where(ri == ci, 1.0, 0.0)
    n0 = each(lambda n_: jnp.where(blk8, n_, 0.0), n)
    x = each(lambda n_: eye + n_, n0)
    p = each(lambda n_: _mmp(n_, n_, pc), n0)
    x = each(lambda x_, p_: x_ + _mmp(x_, p_, pc), x, p)
    p = each(lambda p_: _mmp(p_, p_, pc), p)
    x = each(lambda x_, p_: x_ + _mmp(x_, p_, pc), x, p)
    for lv in range(3, int(C).bit_length() - 1):
        off = (lax.shift_right_logical(ri, lv + 1) == lax.shift_right_logical(ci, lv + 1)) & (
            lax.shift_right_logical(ri, lv) != lax.shift_right_logical(ci, lv))
        t = each(lambda n_, x_: _mmp(jnp.where(off, n_, 0.0), x_, pc), n, x)
        x = each(lambda x_, t_: x_ + _mmp(x_, t_, pc), x, t)
    z = each(lambda x_, a, b_: _mmp(x_, jnp.concatenate([a, b_], axis=1), pc), x, aak, at)
    w2v = each(lambda z_, v_: _mmp(z_[:, :C], v_, po), z, v)
    cl = each(lambda c: c[:, C - 1:C], cumt)
    eb = each(lambda c_, ct: jnp.exp(c_ - ct), cl, cumt)
    lhs = each(lambda bt_, kt_, e, lb, lk: jnp.concatenate(
        [jnp.concatenate([bt_ * e, kt_ * e], axis=1), jnp.concatenate([lb, lk], axis=1)], axis=0),
        bt, kt, eb, lrb, lrk)
    left = each(lambda l_, z_: _mmp(l_[:, :C], z_[:, C:], po), lhs, z)
    right = each(lambda l_, w, v_: _mmp(l_, jnp.concatenate([w, v_], axis=0), po), lhs, w2v, v)
    for u in range(hb):
        diag = eye64 * jnp.exp(cl[u])
        a1_ref[0, u, 0] = jnp.concatenate([left[u][:64, :64] + diag, (left[u][64:] + rt[u])[:, :64]], axis=0)
        a2_ref[0, u, 0] = right[u]


def _rwkv_chunks(r, lw, k, v, kk, b, lwt, kt, bt, C, hb, passes=RW_PASSES):
    B, T, _ = r.shape
    nc = T // C
    tspec = pl.BlockSpec((1, C, hb * LANES), lambda bb, h, c: (bb, c, h))
    fspec = pl.BlockSpec((1, hb * LANES, C), lambda bb, h, c: (bb, h, c))
    pg, pc, po = passes
    return pl.pallas_call(
        functools.partial(_rwkv_chunk_kernel, C=C, hb=hb, pg=pg, pc=pc, po=po),
        grid=(B, 8 // hb, nc),
        in_specs=[tspec] * 6 + [fspec] * 3,
        out_specs=[pl.BlockSpec((1, hb, 1, 64 + C, 64), lambda bb, h, c: (bb, h, c, 0, 0)),
                   pl.BlockSpec((1, hb, 1, 64 + C, LANES), lambda bb, h, c: (bb, h, c, 0, 0))],
        out_shape=[jax.ShapeDtypeStruct((B, 8, nc, 64 + C, 64), F32),
                   jax.ShapeDtypeStruct((B, 8, nc, 64 + C, LANES), F32)],
        compiler_params=_params(("arbitrary", "arbitrary", "arbitrary")),
        name="rwkv_chunks",
    )(r, lw, k, v, kk, b, lwt, kt, bt)


def _rwkv_scan_kernel(a1_ref, a2_ref, s0_ref, y_ref, sT_ref, h_ref, *, C, cb):
    c = pl.program_id(1)

    @pl.when(c == 0)
    def _():
        h_ref[...] = s0_ref[0]

    for cc in range(cb):
        for h in range(8):
            res = _mm(a1_ref[0, h, cc], h_ref[h], HI) + a2_ref[0, h, cc]
            h_ref[h] = res[:64]
            y_ref[0, cc * C:(cc + 1) * C, h * LANES:(h + 1) * LANES] = res[64:]

    @pl.when(c == pl.num_programs(1) - 1)
    def _():
        sT_ref[0] = h_ref[...]


def _rwkv_scan(a1, a2, s0t, C, cb):
    B, _, nc, _, _ = a1.shape
    return pl.pallas_call(
        functools.partial(_rwkv_scan_kernel, C=C, cb=cb),
        grid=(B, nc // cb),
        in_specs=[pl.BlockSpec((1, 8, cb, 64 + C, 64), lambda b, c: (b, 0, c, 0, 0)),
                  pl.BlockSpec((1, 8, cb, 64 + C, LANES), lambda b, c: (b, 0, c, 0, 0)),
                  pl.BlockSpec((1, 8, 64, LANES), lambda b, c: (b, 0, 0, 0))],
        out_specs=[pl.BlockSpec((1, cb * C, HP), lambda b, c: (b, c, 0)),
                   pl.BlockSpec((1, 8, 64, LANES), lambda b, c: (b, 0, 0, 0))],
        out_shape=[jax.ShapeDtypeStruct((B, nc * C, HP), F32),
                   jax.ShapeDtypeStruct((B, 8, 64, LANES), F32)],
        scratch_shapes=[pltpu.VMEM((8, 64, LANES), F32)],
        compiler_params=_params(("arbitrary", "arbitrary")),
        name="rwkv_scan",
    )(a1, a2, s0t)


def _mix_kernel(att_ref, y_ref, g_ref, bg_ref, sga_ref, sgb_ref, x_ref, gt1_ref, sc2_ref, sh2_ref,
                woa_ref, wor_ref, wout_ref, lnw_ref, lnb_ref, gpost_ref, gpre_ref, wr_ref, br_ref,
                x1_ref, h2_ref, gate_ref, eidx_ref, wgt_ref, rel_ref, cnt_out_ref, cnt_ref, *, tm):
    att = jnp.concatenate([att_ref[0, h] for h in range(8)], axis=1)
    lane = lax.broadcasted_iota(jnp.int32, (tm, LANES), 1)
    valid = lane < RWKV_HEAD
    parts = []
    for h in range(8):
        y = y_ref[0, :, h * LANES:(h + 1) * LANES]
        mu = jnp.sum(y, axis=1, keepdims=True) * (1.0 / RWKV_HEAD)
        d = jnp.where(valid, y - mu, 0.0)
        var = jnp.sum(d * d, axis=1, keepdims=True) * (1.0 / RWKV_HEAD)
        parts.append(d * lax.rsqrt(var + GN_EPS))
    yn = jnp.concatenate(parts, axis=1)
    rw = (yn * lnw_ref[...] + lnb_ref[...]) * g_ref[0] + bg_ref[0]
    mix = sga_ref[0] * _mm(att, woa_ref[...]) + sgb_ref[0] * _mm(rw.astype(BF16), wor_ref[...])
    o = _mm(mix.astype(BF16), wout_ref[...])
    x1 = x_ref[0] + gt1_ref[0] * _rms(o, gpost_ref[...])
    x1_ref[0] = x1
    h2 = _rms(x1, gpre_ref[...]) * (1.0 + sc2_ref[0]) + sh2_ref[0]
    h2_ref[0] = h2
    logits = _mm(h2, wr_ref[...], HI) + br_ref[...]
    el = lax.broadcasted_iota(jnp.int32, logits.shape, 1).astype(F32)
    work = logits
    sel = jnp.zeros(logits.shape, jnp.bool_)
    vmax = None
    firsts = []
    for kk in range(TOP_K):
        mx = jnp.max(work, axis=1, keepdims=True)
        if kk == 0:
            vmax = mx
        first = jnp.min(jnp.where(work == mx, el, float(N_EXPERTS)), axis=1, keepdims=True)
        hit = el == first
        firsts.append(first)
        sel = sel | hit
        work = jnp.where(hit, -jnp.inf, work)
    e = jnp.where(sel, jnp.exp(logits - vmax), 0.0)
    gate = e / jnp.sum(e, axis=1, keepdims=True)
    gate_ref[0] = gate
    @pl.when((pl.program_id(0) == 0) & (pl.program_id(1) == 0))
    def _():
        cnt_ref[...] = jnp.zeros(cnt_ref.shape, F32)

    self = jnp.where(sel, 1.0, 0.0)
    ti = lax.broadcasted_iota(jnp.int32, (tm, tm), 0)
    tj = lax.broadcasted_iota(jnp.int32, (tm, tm), 1)
    before = _mm(jnp.where(tj < ti, 1.0, 0.0).astype(BF16), self.astype(BF16)) + cnt_ref[...]
    cnt_ref[...] = cnt_ref[...] + jnp.sum(self, axis=0, keepdims=True)
    cnt_out_ref[...] = cnt_ref[...]
    ln = lax.broadcasted_iota(jnp.int32, (tm, LANES), 1)
    eo = jnp.zeros((tm, LANES), F32)
    go = jnp.zeros((tm, LANES), F32)
    ro = jnp.zeros((tm, LANES), F32)
    for kk in range(TOP_K):
        hit = el == firsts[kk]
        gk = jnp.sum(jnp.where(hit, gate, 0.0), axis=1, keepdims=True)
        rk = jnp.sum(jnp.where(hit, before, 0.0), axis=1, keepdims=True)
        eo = jnp.where(ln == kk, firsts[kk], eo)
        go = jnp.where(ln == kk, gk, go)
        ro = jnp.where(ln == kk, rk, ro)
    eidx_ref[0] = eo.astype(jnp.int32)
    wgt_ref[0] = go
    rel_ref[0] = ro.astype(jnp.int32)


def _mix(att, y, g, bg, sga, sgb, x, gt1, sc2, sh2, wts, tm):
    B, T, D = x.shape
    tspec = lambda n: pl.BlockSpec((1, tm, n), lambda b, i: (b, i, 0))
    full = lambda a: pl.BlockSpec(a.shape, lambda b, i: (0,) * a.ndim)
    tok = lambda n, dt: jax.ShapeDtypeStruct((B, T, n), dt)
    return pl.pallas_call(
        functools.partial(_mix_kernel, tm=tm),
        grid=(B, T // tm),
        in_specs=[pl.BlockSpec((1, 8, tm, LANES), lambda b, i: (b, 0, i, 0)), tspec(HP), tspec(HP), tspec(HP),
                  tspec(D), tspec(D), tspec(D), _mod_spec(gt1, tm), _mod_spec(sc2, tm), _mod_spec(sh2, tm)]
                 + [full(w) for w in wts],
        out_specs=[tspec(D), tspec(D), tspec(N_EXPERTS), tspec(LANES), tspec(LANES), tspec(LANES),
                   pl.BlockSpec((1, N_EXPERTS), lambda b, i: (0, 0))],
        out_shape=[tok(D, F32), tok(D, F32), tok(N_EXPERTS, F32), tok(LANES, jnp.int32), tok(LANES, F32),
                   tok(LANES, jnp.int32), jax.ShapeDtypeStruct((1, N_EXPERTS), F32)],
        scratch_shapes=[pltpu.VMEM((1, N_EXPERTS), F32)],
        compiler_params=_params(("arbitrary", "arbitrary")),
        name="mix_router",
    )(att, y, g, bg, sga, sgb, x, gt1, sc2, sh2, *wts)


def _expert(xb, wgu, bgu, wd, bd, d_ff):
    wgu = wgu.astype(BF16)
    wd = wd.astype(BF16)
    hgu = _mm(xb, wgu) + bgu
    hg = jnp.minimum(hgu[:, :d_ff], SWIGLU_LIMIT)
    hl = jnp.clip(hgu[:, d_ff:], -SWIGLU_LIMIT, SWIGLU_LIMIT)
    act = hg * (1.0 / (1.0 + jnp.exp(-SWIGLU_ALPHA * hg))) * (hl + 1.0)
    return _mm(act.astype(BF16), wd) + bd


def _sc_mesh():
    return plsc.VectorSubcoreMesh(core_axis_name="core", subcore_axis_name="subcore")


def _sc_dispatch(h, pos_slots, n_rows, q):
    N, D = h.shape
    mesh = _sc_mesh()
    steps = N // (mesh.num_cores * SC_WINDOW)
    assert steps * mesh.num_cores * SC_WINDOW == N

    @pl.kernel(out_type=jax.ShapeDtypeStruct((n_rows, SC_PIECE), h.dtype), mesh=mesh, scratch_types=[])
    def scatter(x_hbm, *refs):
        idx_hbm, o_hbm = refs[:-1], refs[-1]
        base = lax.axis_index("core") * steps

        def body(x_vmem, *i_vmem):
            for iv in i_vmem:
                pltpu.sync_copy(x_vmem, o_hbm.at[iv.at[0]])

        pltpu.emit_pipeline(
            body,
            grid=(steps,),
            in_specs=[pl.BlockSpec((SC_WINDOW, SC_PIECE), index_map=lambda i: (base + i, q))]
                     + [pl.BlockSpec((1, SC_WINDOW), index_map=lambda i: (0, base + i))] * len(idx_hbm),
            out_specs=[],
            core_axis_name="subcore",
            dimension_semantics=(pltpu.PARALLEL,),
        )(x_hbm, *idx_hbm)

    return scatter(h, *pos_slots)


def _sc_gather(table, idx):
    M = idx.shape[1]
    mesh = _sc_mesh()
    steps = M // (mesh.num_cores * SC_WINDOW)
    assert steps * mesh.num_cores * SC_WINDOW == M

    @pl.kernel(out_type=jax.ShapeDtypeStruct((M, SC_PIECE), table.dtype), mesh=mesh)
    def gather(x_hbm, i_hbm, o_hbm):
        base = lax.axis_index("core") * steps

        def body(i_vmem, o_vmem):
            pltpu.sync_copy(x_hbm.at[i_vmem.at[0]], o_vmem)

        pltpu.emit_pipeline(
            body,
            grid=(steps,),
            in_specs=[pl.BlockSpec((1, SC_WINDOW), index_map=lambda i: (0, base + i))],
            out_specs=[pl.BlockSpec((SC_WINDOW, SC_PIECE), index_map=lambda i: (base + i, 0))],
            core_axis_name="subcore",
            dimension_semantics=(pltpu.PARALLEL,),
        )(i_hbm, o_hbm)

    return gather(table, idx)


def _moe_grouped_kernel(te_ref, nt_ref, *refs, d_ff, npiece):
    x_refs, (wgu_ref, bgu_ref, wd_ref, bd_ref) = refs[:npiece], refs[npiece:npiece + 4]
    o_refs, (wgu_b, wd_b) = refs[npiece + 4:2 * npiece + 4], refs[2 * npiece + 4:]
    g = pl.program_id(0)

    @pl.when((g == 0) | (te_ref[g] != te_ref[jnp.maximum(g - 1, 0)]))
    def _():
        wgu_b[...] = wgu_ref[0].astype(BF16)
        wd_b[...] = wd_ref[0].astype(BF16)

    @pl.when(g < nt_ref[0])
    def _():
        xb = jnp.concatenate([r[...] for r in x_refs], axis=1).astype(BF16)
        y = _expert(xb, wgu_b[...], bgu_ref[0], wd_b[...], bd_ref[0], d_ff)
        for q, o_ref in enumerate(o_refs):
            o_ref[...] = y[:, q * SC_PIECE:(q + 1) * SC_PIECE]


def _moe_grouped(xs, te, nt, wgu, bgu, wd, bd, TG):
    NP = xs[0].shape[0]
    E, D, F2 = wgu.shape
    npiece = len(xs)
    wmap = lambda g, te_, nt_: (te_[g], 0, 0)
    xmap = lambda g, te_, nt_: (jnp.minimum(g, nt_[0] - 1), 0)
    pspec = pl.BlockSpec((TG, SC_PIECE), xmap)
    return pl.pallas_call(
        functools.partial(_moe_grouped_kernel, d_ff=F2 // 2, npiece=npiece),
        grid_spec=pltpu.PrefetchScalarGridSpec(
            num_scalar_prefetch=2,
            grid=(NP // TG,),
            in_specs=[pspec] * npiece + [pl.BlockSpec((1, D, F2), wmap), pl.BlockSpec((1, 1, F2), wmap),
                                         pl.BlockSpec((1, F2 // 2, D), wmap), pl.BlockSpec((1, 1, D), wmap)],
            out_specs=[pspec] * npiece,
            scratch_shapes=[pltpu.VMEM((D, F2), BF16), pltpu.VMEM((F2 // 2, D), BF16)]),
        out_shape=[jax.ShapeDtypeStruct((NP, SC_PIECE), F32)] * npiece,
        compiler_params=_params(("arbitrary",)),
        name="moe_grouped",
    )(te, nt, *xs, wgu, bgu, wd, bd)


def _moe_combine_kernel(*refs, npiece):
    y_refs = refs[:TOP_K * npiece]
    w_ref, x1_ref, gt2_ref, gpost_ref, o_ref = refs[TOP_K * npiece:]
    w = w_ref[...]
    acc = None
    for k in range(TOP_K):
        yk = jnp.concatenate([y_refs[k * npiece + q][...] for q in range(npiece)], axis=1)
        term = w[:, k:k + 1] * yk
        acc = term if acc is None else acc + term
    o_ref[...] = x1_ref[...] + gt2_ref[0] * _rms(acc, gpost_ref[...])


def _moe_combine(yslot, wgt, x1, gt2, gpost, tm):
    N, D = x1.shape
    nb = N // tm
    tpb = N // gt2.shape[0]
    npiece = len(yslot)
    yspec = lambda k: pl.BlockSpec((tm, SC_PIECE), lambda i, k=k: (i + k * nb, 0))
    return pl.pallas_call(
        functools.partial(_moe_combine_kernel, npiece=npiece),
        grid=(nb,),
        in_specs=[yspec(k) for k in range(TOP_K) for _ in range(npiece)]
                 + [pl.BlockSpec((tm, LANES), lambda i: (i, 0)), pl.BlockSpec((tm, D), lambda i: (i, 0)),
                    pl.BlockSpec((1, 1, D), lambda i: ((i * tm) // tpb, 0, 0)), pl.BlockSpec((1, D), lambda i: (0, 0))],
        out_specs=pl.BlockSpec((tm, D), lambda i: (i, 0)),
        out_shape=jax.ShapeDtypeStruct((N, D), F32),
        compiler_params=_params(("arbitrary",)),
        name="moe_combine",
    )(*[yslot[q] for _ in range(TOP_K) for q in range(npiece)], wgt, x1, gt2, gpost)


def _moe_kernel(h_ref, gate_ref, x1_ref, gt2_ref, gpost_ref, wgu_ref, bgu_ref, wd_ref, bd_ref, o_ref, acc_ref,
                *, d_ff):
    e = pl.program_id(1)

    @pl.when(e == 0)
    def _():
        acc_ref[...] = jnp.zeros(acc_ref.shape, F32)

    contrib = _expert(h_ref[...].astype(BF16), wgu_ref[0], bgu_ref[0], wd_ref[0], bd_ref[0], d_ff)
    gate = gate_ref[...]
    el = lax.broadcasted_iota(jnp.int32, gate.shape, 1)
    ge = jnp.sum(jnp.where(el == e, gate, 0.0), axis=1, keepdims=True)
    acc_ref[...] += ge * contrib

    @pl.when(e == pl.num_programs(1) - 1)
    def _():
        o_ref[...] = x1_ref[...] + gt2_ref[0] * _rms(acc_ref[...], gpost_ref[...])


def _moe(h2, gate, x1, gt2, gpost, wgu, bgu, wd, bd, tm):
    N, D = h2.shape
    E, _, F2 = wgu.shape
    tspec = lambda n: pl.BlockSpec((tm, n), lambda i, e: (i, 0))
    if gt2.shape[1] == 1:
        tpb = N // gt2.shape[0]
        gspec = pl.BlockSpec((1, 1, D), lambda i, e: ((i * tm) // tpb, 0, 0))
    else:
        gspec = pl.BlockSpec((1, tm, D), lambda i, e: (i, 0, 0))
    return pl.pallas_call(
        functools.partial(_moe_kernel, d_ff=F2 // 2),
        grid=(N // tm, E),
        in_specs=[tspec(D), tspec(E), tspec(D), gspec, pl.BlockSpec((1, D), lambda i, e: (0, 0)),
                  pl.BlockSpec((1, D, F2), lambda i, e: (e, 0, 0)), pl.BlockSpec((1, 1, F2), lambda i, e: (e, 0, 0)),
                  pl.BlockSpec((1, F2 // 2, D), lambda i, e: (e, 0, 0)), pl.BlockSpec((1, 1, D), lambda i, e: (e, 0, 0))],
        out_specs=tspec(D),
        out_shape=jax.ShapeDtypeStruct((N, D), F32),
        scratch_shapes=[pltpu.VMEM((tm, D), F32)],
        compiler_params=_params(("arbitrary", "arbitrary")),
        name="moe",
    )(h2, gate, x1, gt2, gpost, wgu, bgu, wd, bd)


def _per_token(m, T):
    B, _, D = m.shape
    return jnp.broadcast_to(m, (B, T, D)).reshape(1, B * T, D)


def _layer(x, mods, past, s0, shift0, P, cfg):
    B, T, D = x.shape
    sh1, sc1, gt1, sh2, sc2, gt2 = mods
    tm = cfg["tm"]

    q, k, v, kb, vb, iq, ik, ikb, iw = _proj_a(x, sc1, sh1, P["g_pre_mix"], P["w_a"], P["ik_ln_w"], P["ik_ln_b"],
                                              cfg["tm_a"])
    zr, sga, sgb = _proj_b(x, sc1, sh1, P["g_pre_mix"], P["w_b"], tm)

    if past is not None:
        pk, pv, pik = past
        plen = pk.shape[1]
        kb_all = jnp.concatenate([pk.reshape(B, plen, 128).astype(BF16), kb], axis=1)
        vb_all = jnp.concatenate([pv.reshape(B, plen, 128).astype(BF16), vb], axis=1)
        ik_all = jnp.concatenate([jnp.pad(pik, ((0, 0), (0, 0), (0, 64))).astype(BF16), ikb], axis=1)
    else:
        plen = 0
        kb_all, vb_all, ik_all = kb, vb, ikb
    ltot = plen + T
    KB = cfg["KB"]
    lp = -(-ltot // KB) * KB
    if lp != ltot:
        padk = ((0, 0), (0, lp - ltot), (0, 0))
        kb_all, vb_all, ik_all = jnp.pad(kb_all, padk), jnp.pad(vb_all, padk), jnp.pad(ik_all, padk)
    vb1 = jnp.concatenate([vb_all, jnp.ones_like(vb_all)], axis=-1)
    att = _attention(iq, iw, q, ik_all, kb_all, vb1, R=cfg["R"], KB=KB, SB=cfg["SB"], HG=cfg["HG"], ltot=ltot,
                     q_off=plen)

    C = RW_CHUNK
    tp = -(-T // C) * C
    zr_p = zr if tp == T else jnp.pad(zr, ((0, 0), (0, tp - T), (0, 0)))
    shift_pad = _take_cols(shift0.reshape(B, RWKV_COLS), _COLS_ZR).reshape(B, 1, ZRW)
    r, lw, km, vv, kkn, bb, g, bg, lwt, kt, bt = _rwkv_prep(zr_p, shift_pad, P, min(cfg["tm_rw"], tp), T)
    a1, a2 = _rwkv_chunks(r, lw, km, vv, kkn, bb, lwt, kt, bt, C, cfg["hb"])
    s0t = jnp.pad(jnp.swapaxes(s0, 2, 3), ((0, 0), (0, 0), (0, 0), (0, 64)))
    nc = tp // C
    cb = min(cfg["cb"], nc)
    y, sT = _rwkv_scan(a1, a2, s0t, C, cb)
    s_new = jnp.swapaxes(sT[..., :64], 2, 3)
    zlast = zr[:, T - 1]
    inv = np.zeros((RWKV_COLS,), np.int32)
    inv[_COLS_ZR[0][_COLS_ZR[1]]] = np.nonzero(_COLS_ZR[1])[0]
    shift_new = jnp.take(zlast, jnp.asarray(inv), axis=1).reshape(B, 1, RWKV_COLS)
    if tp != T:
        y, g, bg = y[:, :T], g[:, :T], bg[:, :T]

    wts = [P[n] for n in ("w_o_att", "w_o_rwkv", "w_out", "ln_x_w", "ln_x_b", "g_post_mix", "g_pre_ffn",
                          "w_router", "b_router")]
    x1, h2, gate, eidx, wgt, rel, cnt = _mix(att, y, g, bg, sga, sgb, x, gt1, sc2, sh2, wts, tm)

    N = B * T
    tmm = min(cfg["tm_moe"], N)
    if cfg["routed"]:
        TG = cfg["TG"]
        ntm = N * TOP_K // TG + N_EXPERTS
        cnt_i = cnt[0].astype(jnp.int32)
        padded = (cnt_i + TG - 1) // TG * TG
        ends = jnp.cumsum(padded)
        off = ends - padded
        nt = ends[-1] // TG
        gi = jnp.minimum(jnp.arange(ntm, dtype=jnp.int32), nt - 1)
        te = jnp.sum((ends // TG)[None, :] <= gi[:, None], axis=1).astype(jnp.int32)
        e4 = eidx.reshape(N, LANES)[:, :TOP_K]
        pos = jnp.take(off, e4) + rel.reshape(N, LANES)[:, :TOP_K]
        pos_t = pos.T
        npiece = D // SC_PIECE
        h2f = h2.reshape(N, D)
        xs = [_sc_dispatch(h2f, [pos_t[k:k + 1] for k in range(TOP_K)], ntm * TG, q) for q in range(npiece)]
        ys = _moe_grouped(xs, te, nt.reshape(1), P["w_gu"], P["b_gu"], P["w_down"], P["b_down"], TG)
        idx_c = pos_t.reshape(1, TOP_K * N)
        yslot = [_sc_gather(ys[q], idx_c) for q in range(npiece)]
        out = _moe_combine(yslot, wgt.reshape(N, LANES), x1.reshape(N, D), gt2, P["g_post_ffn"], tm)
    else:
        if T % tmm == 0:
            gt2m = gt2
        else:
            gt2m = _per_token(gt2, T).reshape(N // tmm, tmm, D)
        out = _moe(h2.reshape(N, D), gate.reshape(N, N_EXPERTS), x1.reshape(N, D), gt2m, P["g_post_ffn"],
                   P["w_gu"], P["b_gu"], P["w_down"], P["b_down"], tmm)
    return (out.reshape(B, T, D), k.reshape(B, T, ATT_KV_HEADS, HEAD_DIM), v.reshape(B, T, ATT_KV_HEADS, HEAD_DIM),
            ik, s_new, shift_new)


def _prep_weights(l, w_in, ik_ln_w, ik_ln_b, mu_rwkv, w0, w_up, a0, a_up, g_up, k_k, k_a, r_k, ln_x_w, ln_x_b,
                  w_o_att, w_o_rwkv, w_out, w_router, b_router, w_gu, b_gu, w_down, b_down,
                  g_pre_mix, g_post_mix, g_pre_ffn, g_post_ffn):
    row = lambda a: a.reshape(1, -1)
    hp = lambda a: _take_cols(row(a), _HEAD_IDX)
    P = {}
    P["w_a"] = _take_cols(w_in[l], _COLS_A).astype(BF16)
    P["w_b"] = _take_cols(w_in[l], _COLS_B).astype(BF16)
    P["ik_ln_w"] = jnp.pad(row(ik_ln_w[l]), ((0, 0), (0, 64)))
    P["ik_ln_b"] = jnp.pad(row(ik_ln_b[l]), ((0, 0), (0, 64)))
    P["mu"] = _take_cols(row(mu_rwkv[l]), _COLS_ZR)
    P["w0"], P["a0"], P["k_k"], P["k_a"] = hp(w0[l]), hp(a0[l]), hp(k_k[l]), hp(k_a[l])
    P["r_k"] = hp(r_k[l].reshape(-1))
    P["ln_x_w"], P["ln_x_b"] = hp(ln_x_w[l]), hp(ln_x_b[l])
    z64 = jnp.zeros((64, HP), F32)
    P["w_up"] = jnp.concatenate([_take_cols(w_up[l], _HEAD_IDX), z64], axis=0)
    P["a_up"] = jnp.concatenate([z64, _take_cols(a_up[l], _HEAD_IDX)], axis=0)
    P["g_up"] = _take_cols(g_up[l], _HEAD_IDX)
    P["w_o_att"] = _take_rows(w_o_att[l], _ATT_ROW_IDX).astype(BF16)
    P["w_o_rwkv"] = _take_rows(w_o_rwkv[l], _HEAD_IDX).astype(BF16)
    P["w_out"] = w_out[l].astype(BF16)
    P["w_router"] = w_router[l]
    P["b_router"] = row(b_router[l])
    P["w_gu"] = w_gu[l]
    P["b_gu"] = b_gu[l].reshape(N_EXPERTS, 1, -1)
    P["w_down"] = w_down[l]
    P["b_down"] = b_down[l].reshape(N_EXPERTS, 1, -1)
    P["g_pre_mix"], P["g_post_mix"] = row(g_pre_mix[l]), row(g_post_mix[l])
    P["g_pre_ffn"], P["g_post_ffn"] = row(g_pre_ffn[l]), row(g_post_ffn[l])
    return P


CFG_PROMPT = dict(tm=256, tm_a=1024, R=256,KB=512, SB=256, HG=2,tm_rw=256, hb=8, cb=2, tm_moe=1024, routed=True, TG=512)
CFG_SAMPLE = dict(tm=32, tm_a=32, R=32, KB=512, SB=256, HG=2,tm_rw=128, hb=8, cb=1, tm_moe=512, routed=False)


def kernel(x_prompt, x_sample, c_prompt, c_sample, cache_k, cache_v, cache_idx_k, state_rwkv, state_shift, w_ada, b_ada, g_pre_mix, g_post_mix, g_pre_ffn, g_post_ffn, w_in, ik_ln_w, ik_ln_b, mu_rwkv, w0, w_up, a0, a_up, g_up, k_k, k_a, r_k, ln_x_w, ln_x_b, w_o_att, w_o_rwkv, w_out, w_router, b_router, w_gu, b_gu, w_down, b_down):
    depth = w_in.shape[0]
    bp, tp_, D = x_prompt.shape
    bs, ts, _ = x_sample.shape
    y_p, y_s = x_prompt, x_sample
    st_p = [[] for _ in range(5)]
    st_s = [[] for _ in range(5)]
    nc_all = bp + bs
    npad = -(-nc_all // 8) * 8
    c_all = jnp.pad(jnp.concatenate([c_prompt, c_sample], axis=0), ((0, npad - nc_all), (0, 0)))
    for l in range(depth):
        P = _prep_weights(l, w_in, ik_ln_w, ik_ln_b, mu_rwkv, w0, w_up, a0, a_up, g_up, k_k, k_a, r_k, ln_x_w,
                          ln_x_b, w_o_att, w_o_rwkv, w_out, w_router, b_router, w_gu, b_gu, w_down, b_down,
                          g_pre_mix, g_post_mix, g_pre_ffn, g_post_ffn)
        ada = _ada(c_all, w_ada[l], b_ada[l])
        mods_p = [m[:bp, None, :] for m in jnp.split(ada, 6, axis=-1)]
        mods_s = [m[bp:nc_all, None, :] for m in jnp.split(ada, 6, axis=-1)]
        zero_state = jnp.zeros((bp, RWKV_HEADS, RWKV_HEAD, RWKV_HEAD), F32)
        zero_shift = jnp.zeros((bp, 1, RWKV_COLS), F32)
        outs_p = _layer(y_p, mods_p, None, zero_state, zero_shift, P, CFG_PROMPT)
        outs_s = _layer(y_s, mods_s, (cache_k[l], cache_v[l], cache_idx_k[l]), state_rwkv[l], state_shift[l], P,
                        CFG_SAMPLE)
        y_p, y_s = outs_p[0], outs_s[0]
        for lst, val in zip(st_p, outs_p[1:]):
            lst.append(val)
        for lst, val in zip(st_s, outs_s[1:]):
            lst.append(val)
    sp = [jnp.stack(v, axis=0) for v in st_p]
    ss = [jnp.stack(v, axis=0) for v in st_s]
    return (y_p, y_s, sp[0], sp[1], sp[2], sp[3], sp[4], ss[0], ss[1], ss[2], ss[3], ss[4])
```

```python
import functools

import numpy as np
import jax
import jax.numpy as jnp
from jax import lax
from jax.experimental import pallas as pl
from jax.experimental.pallas import tpu as pltpu
from jax.experimental.pallas import tpu_sc as plsc

F32 = jnp.float32
BF16 = jnp.bfloat16
HI = lax.Precision.HIGHEST

CHUNK = 64
ATT_HEADS = 8
ATT_KV_HEADS = 2
HEAD_DIM = 64
IDX_HEADS = 8
IDX_DIM = 64
IDX_SCALE = (IDX_DIM ** -0.5) * (IDX_HEADS ** -0.5)
TOPK_MAX = 256
RWKV_HEADS = 8
RWKV_HEAD = 64
RWKV_WIDTH = RWKV_HEADS * RWKV_HEAD
DECAY_LORA = 64
AAA_LORA = 64
GATE_LORA = 128
RWKV_COLS = 3 * RWKV_WIDTH + DECAY_LORA + AAA_LORA + GATE_LORA
GN_EPS = 64e-5
L2_EPS = 1e-24
N_EXPERTS = 32
TOP_K = 4
SWIGLU_LIMIT = 7.0
SWIGLU_ALPHA = 1.702
NORM_EPS = 1e-6
LN_EPS = 1e-6

LANES = 128
VMEM_LIMIT = 56 * 1024 * 1024

HP = 8 * LANES
RW_CHUNK = 128
NEG_BIG = -1e30
LOG2E = 1.4426950408889634
KEY_NEG_INF = -2139095041
INT_MIN = -2147483648
SC_WINDOW = 128
SC_PIECE = 256
LO_GROUP = 4
RW_PASSES = (1, 1, 1)


def _params(sem):
    return pltpu.CompilerParams(dimension_semantics=sem, vmem_limit_bytes=VMEM_LIMIT)


def _nt(a, b, precision=None):
    return lax.dot_general(a, b, (((1,), (1,)), ((), ())), precision=precision,
                           preferred_element_type=F32)


def _mm(a, b, precision=None):
    return jnp.dot(a, b, precision=precision, preferred_element_type=F32)


def _split_bf16(x, terms):
    out = []
    for _ in range(terms):
        p = x.astype(BF16)
        out.append(p)
        x = x - p.astype(F32)
    return out


def _mmp(a, b, passes, nt=False):
    dot = _nt if nt else _mm
    if passes == 1:
        return dot(a.astype(BF16), b.astype(BF16))
    ah, al = _split_bf16(a, 2)
    bh, bl = _split_bf16(b, 2)
    return dot(ah, bh) + (dot(ah, bl) + dot(al, bh))


def _rms(x, g):
    return x * lax.rsqrt(jnp.mean(x * x, axis=-1, keepdims=True) + NORM_EPS) * g


def _headpad_idx(seg_off, lane_off_fn=lambda h: 0):
    idx = np.zeros((HP,), np.int32)
    ok = np.zeros((HP,), bool)
    for h in range(8):
        d0 = h * LANES + lane_off_fn(h)
        idx[d0:d0 + 64] = seg_off + h * 64 + np.arange(64)
        ok[d0:d0 + 64] = True
    return idx, ok


def _plain_idx(seg_off, n, width):
    idx = np.zeros((width,), np.int32)
    ok = np.zeros((width,), bool)
    idx[:n] = seg_off + np.arange(n)
    ok[:n] = True
    return idx, ok


def _cat(parts):
    return np.concatenate([p[0] for p in parts]), np.concatenate([p[1] for p in parts])


_O_Q, _O_K, _O_V, _O_IQ, _O_IK, _O_IW, _O_ZR, _O_GA, _O_GB = 0, 512, 640, 768, 1280, 1344, 1352, 3144, 4168
_COLS_A = _cat([_headpad_idx(_O_Q, lambda h: (h // 4) * 64), _plain_idx(_O_K, 128, 128), _plain_idx(_O_V, 128, 128),
                _headpad_idx(_O_IQ), _plain_idx(_O_IK, 64, 128), _plain_idx(_O_IW, 8, 128)])
_COLS_ZR = _cat([_headpad_idx(0), _headpad_idx(512), _headpad_idx(1024), _plain_idx(1536, 128, 128),
                 _plain_idx(1664, 128, 128)])
ZRW = _COLS_ZR[0].shape[0]
_COLS_B = _cat([(_COLS_ZR[0] + _O_ZR, _COLS_ZR[1]), _plain_idx(_O_GA, 1024, 1024), _plain_idx(_O_GB, 1024, 1024)])
NA = _COLS_A[0].shape[0]
NB = _COLS_B[0].shape[0]
_HEAD_IDX = _headpad_idx(0)
_ATT_ROW_IDX = _headpad_idx(0, lambda h: (h // 4) * 64)


def _take_cols(w, cols):
    idx, ok = cols
    return jnp.where(jnp.asarray(ok)[None, :], jnp.take(w, jnp.asarray(idx), axis=1), 0.0)


def _take_rows(w, cols):
    idx, ok = cols
    return jnp.where(jnp.asarray(ok)[:, None], jnp.take(w, jnp.asarray(idx), axis=0), 0.0)


def _mod_spec(m, tm):
    assert m.shape[1] == 1
    return pl.BlockSpec((1, 1, m.shape[2]), lambda b, i: (b, 0, 0))


def _ada_kernel(c_ref, w_ref, b_ref, o_ref):
    c = c_ref[...]
    s = c * (1.0 / (1.0 + jnp.exp(-c)))
    o_ref[...] = _mm(s, w_ref[...], HI) + b_ref[...]


def _ada(c, w, b):
    n, d = c.shape
    nout = w.shape[1]
    bn = 768
    return pl.pallas_call(
        _ada_kernel,
        grid=(nout // bn,),
        in_specs=[pl.BlockSpec((n, d), lambda j: (0, 0)),
                  pl.BlockSpec((d, bn), lambda j: (0, j)),
                  pl.BlockSpec((1, bn), lambda j: (0, j))],
        out_specs=pl.BlockSpec((n, bn), lambda j: (0, j)),
        out_shape=jax.ShapeDtypeStruct((n, nout), F32),
        compiler_params=_params(("arbitrary",)),
        name="ada",
    )(c, w, b.reshape(1, nout))


def _modulated(x_ref, sc_ref, sh_ref, g_ref):
    h = _rms(x_ref[0], g_ref[...])
    return (h * (1.0 + sc_ref[0]) + sh_ref[0]).astype(BF16)


def _proj_a_kernel(x_ref, sc_ref, sh_ref, g_ref, w_ref, lnw_ref, lnb_ref,
                   q_ref, k_ref, v_ref, kb_ref, vb_ref, iq_ref, ik_ref, ikb_ref, iw_ref):
    hb = _modulated(x_ref, sc_ref, sh_ref, g_ref)
    for h in range(8):
        zq = _mm(hb, w_ref[:, h * LANES:(h + 1) * LANES])
        q_ref[0, h] = (zq * (HEAD_DIM ** -0.5 * LOG2E)).astype(BF16)
    kv = _mm(hb, w_ref[:, 1024:1280])
    k = kv[:, :128]
    v = kv[:, 128:]
    k_ref[0] = k
    v_ref[0] = v
    kb_ref[0] = k.astype(BF16)
    vb_ref[0] = v.astype(BF16)
    for h in range(8):
        zi = _mm(hb, w_ref[:, 1280 + h * LANES:1280 + (h + 1) * LANES])
        iq_ref[0, h] = zi.astype(BF16)
    t = _mm(hb, w_ref[:, 2304:2560])
    ik = t[:, :128]
    lane = lax.broadcasted_iota(jnp.int32, ik.shape, 1)
    valid = lane < IDX_DIM
    mu = jnp.sum(ik, axis=-1, keepdims=True) * (1.0 / IDX_DIM)
    d = jnp.where(valid, ik - mu, 0.0)
    var = jnp.sum(d * d, axis=-1, keepdims=True) * (1.0 / IDX_DIM)
    ikn = d * lax.rsqrt(var + LN_EPS) * lnw_ref[...] + lnb_ref[...]
    ik_ref[0] = ikn[:, :IDX_DIM]
    ikb_ref[0] = ikn.astype(BF16)
    iw_ref[0] = t[:, 128:136] * IDX_SCALE


def _proj_a(x, sc, sh, g, w, lnw, lnb, tm):
    B, T, D = x.shape
    nt = T // tm
    tok = lambda n, dt: jax.ShapeDtypeStruct((B, T, n), dt)
    hm = lambda dt: jax.ShapeDtypeStruct((B, 8, T, LANES), dt)
    tspec = lambda n: pl.BlockSpec((1, tm, n), lambda b, i: (b, i, 0))
    hspec = pl.BlockSpec((1, 8, tm, LANES), lambda b, i: (b, 0, i, 0))
    full = lambda a: pl.BlockSpec(a.shape, lambda b, i: (0,) * a.ndim)
    return pl.pallas_call(
        _proj_a_kernel,
        grid=(B, nt),
        in_specs=[tspec(D), _mod_spec(sc, tm), _mod_spec(sh, tm), full(g), full(w), full(lnw), full(lnb)],
        out_specs=[hspec, tspec(128), tspec(128), tspec(128), tspec(128), hspec, tspec(IDX_DIM), tspec(128),
                   tspec(8)],
        out_shape=[hm(BF16), tok(128, F32), tok(128, F32), tok(128, BF16), tok(128, BF16), hm(BF16),
                   tok(IDX_DIM, F32), tok(128, BF16), tok(8, F32)],
        compiler_params=_params(("arbitrary", "arbitrary")),
        name="proj_att",
    )(x, sc, sh, g, w, lnw, lnb)


def _proj_b_kernel(x_ref, sc_ref, sh_ref, g_ref, w_ref, zr_ref, ga_ref, gb_ref):
    hb = _modulated(x_ref, sc_ref, sh_ref, g_ref)
    for j in range(ZRW // 256):
        zr_ref[0, :, j * 256:(j + 1) * 256] = _mm(hb, w_ref[:, j * 256:(j + 1) * 256])
    for j in range(4):
        za = _mm(hb, w_ref[:, ZRW + j * 256:ZRW + (j + 1) * 256])
        ga_ref[0, :, j * 256:(j + 1) * 256] = 1.0 / (1.0 + jnp.exp(-za))
        zb = _mm(hb, w_ref[:, ZRW + 1024 + j * 256:ZRW + 1024 + (j + 1) * 256])
        gb_ref[0, :, j * 256:(j + 1) * 256] = 1.0 / (1.0 + jnp.exp(-zb))


def _proj_b(x, sc, sh, g, w, tm):
    B, T, D = x.shape
    nt = T // tm
    tok = lambda n: jax.ShapeDtypeStruct((B, T, n), F32)
    tspec = lambda n: pl.BlockSpec((1, tm, n), lambda b, i: (b, i, 0))
    full = lambda a: pl.BlockSpec(a.shape, lambda b, i: (0,) * a.ndim)
    return pl.pallas_call(
        _proj_b_kernel,
        grid=(B, nt),
        in_specs=[tspec(D), _mod_spec(sc, tm), _mod_spec(sh, tm), full(g), full(w)],
        out_specs=[tspec(ZRW), tspec(1024), tspec(1024)],
        out_shape=[tok(ZRW), tok(1024), tok(1024)],
        compiler_params=_params(("arbitrary", "arbitrary")),
        name="proj_rwkv",
    )(x, sc, sh, g, w)


def _sortable_to_float(t):
    bits = t ^ (lax.shift_right_arithmetic(t, 31) & 0x7FFFFFFF)
    return lax.bitcast_convert_type(bits, F32)


def _top16(x):
    bits = lax.bitcast_convert_type(x, jnp.int32) & jnp.int32(-65536)
    return lax.bitcast_convert_type(bits, F32).astype(BF16)


def _attn_kernel(iq_ref, iw_ref, q_ref, ik_ref, k_ref, v_ref, o_ref, s_ref, sb_ref, m_ref, acc_ref,
                 *, R, KB, SB, HG, ltot, q_off, topk):
    j = pl.program_id(1)
    q0 = q_off + j * R
    pos = q0 + lax.broadcasted_iota(jnp.int32, (R, 1), 0)
    qchunk = lax.shift_right_logical(pos, 6)
    kend = jnp.minimum(ltot, (lax.shift_right_logical(q0 + R - 1, 6) + 1) * CHUNK)
    nkb = lax.shift_right_logical(kend + KB - 1, KB.bit_length() - 1)
    lane_k = lax.broadcasted_iota(jnp.int32, (R, KB), 1)
    nch = KB // LANES

    iq_all = iq_ref[0].reshape(8 * R, LANES)
    iw = iw_ref[0]
    iwb = [jnp.broadcast_to(iw[:, h:h + 1], (R, SB)) for h in range(8)]
    lane_s = lax.broadcasted_iota(jnp.int32, (R, SB), 1)
    nsb = KB // SB

    def score_body(kb, carry):
        for u in range(nsb):
            off = pl.multiple_of(kb * KB + u * SB, SB)
            s_all = _mm(iq_all, ik_ref[0, kb * nsb + u])
            acc = jnp.zeros((R, SB), F32)
            for h in range(8):
                acc = acc + jnp.maximum(s_all[h * R:(h + 1) * R], 0.0) * iwb[h]
            kidx = off + lane_s
            adm = (lax.shift_right_logical(kidx, 6) <= qchunk) & (kidx < ltot)
            sc = jnp.where(adm, acc, -jnp.inf)
            s_ref[kb, :, u * SB:(u + 1) * SB] = sc
            sb_ref[kb, :, u * SB:(u + 1) * SB] = _top16(sc)
        return carry

    lax.fori_loop(0, nkb, score_body, 0)

    def count_ge(ref, cand, dt):
        one, zero = jnp.ones((), dt), jnp.zeros((), dt)

        def body(kb, part):
            blk = ref[kb]
            m = jnp.where(blk >= cand, one, zero)
            for c in range(nch):
                part = part + m[:, c * LANES:(c + 1) * LANES]
            return part
        part = lax.fori_loop(0, nkb, body, jnp.zeros((R, LANES), dt))
        return jnp.sum(part.astype(F32), axis=1, keepdims=True)

    kf = float(topk)
    c0 = count_ge(sb_ref, jnp.zeros((R, 1), BF16), BF16)
    t0 = jnp.where(c0 >= kf, 0, INT_MIN).astype(jnp.int32)
    ex0 = jnp.where(c0 == kf, 1.0, 0.0)

    def accept(c, cand, cnt):
        t, ex = c
        ok = cnt >= kf
        return jnp.where(ok, cand, t), jnp.where(ok & (cnt == kf), 1.0, ex)

    def hi_body(i, c):
        cand = c[0] + lax.shift_left(jnp.int32(1), 30 - i)
        return accept(c, cand, count_ge(sb_ref, _top16(_sortable_to_float(cand)), BF16))

    def lo_pass(i, c):
        cand = c[0] + lax.shift_left(jnp.int32(1), 15 - i)
        return accept(c, cand, count_ge(s_ref, _sortable_to_float(cand), F32))

    def lo_group(c):
        gi, t, ex, _ = c
        tc = (t, ex)
        for jj in range(LO_GROUP):
            tc = lo_pass(gi * LO_GROUP + jj, tc)
        return gi + 1, tc[0], tc[1], (jnp.min(tc[1]) > 0.5).astype(jnp.int32)

    t, ex = lax.fori_loop(0, 15, hi_body, (t0, ex0))
    _, t, _, _ = lax.while_loop(lambda c: (c[0] < 16 // LO_GROUP) & (c[3] == 0), lo_group,
                                (jnp.int32(0), t, ex, (jnp.min(ex) > 0.5).astype(jnp.int32)))
    all_finite = t <= KEY_NEG_INF
    tau = jnp.where(all_finite, -jnp.inf, _sortable_to_float(jnp.maximum(t, KEY_NEG_INF)))

    def count_gt_eq(_):
        def body(kb, carry):
            pg, pe = carry
            blk = s_ref[kb]
            mg = jnp.where(blk > tau, 1.0, 0.0)
            me = jnp.where(blk == tau, 1.0, 0.0)
            for c in range(nch):
                pg = pg + mg[:, c * LANES:(c + 1) * LANES]
                pe = pe + me[:, c * LANES:(c + 1) * LANES]
            return pg, pe
        z = jnp.zeros((R, LANES), F32)
        pg, pe = lax.fori_loop(0, nkb, body, (z, z))
        return jnp.sum(pg, axis=1, keepdims=True), jnp.sum(pe, axis=1, keepdims=True)

    cnt_gt, cnt_eq = count_gt_eq(0)
    need = kf - cnt_gt
    tie = (cnt_eq > need) & jnp.logical_not(all_finite)
    any_tie = jnp.max(jnp.where(tie, 1.0, 0.0)) > 0.0

    def tie_bound():
        def count_eq_below(x):
            def body(kb, part):
                blk = s_ref[kb]
                kidx = kb * KB + lane_k
                m = jnp.where((blk == tau) & (kidx < x), 1.0, 0.0)
                for c in range(nch):
                    part = part + m[:, c * LANES:(c + 1) * LANES]
                return part
            part = lax.fori_loop(0, nkb, body, jnp.zeros((R, LANES), F32))
            return jnp.sum(part, axis=1, keepdims=True)

        nbits = int(ltot).bit_length()

        def body(i, x):
            cand = x + lax.shift_left(jnp.int32(1), nbits - 1 - i)
            ok = count_eq_below(cand) < need
            return jnp.where(ok, cand, x)
        x = lax.fori_loop(0, nbits, body, jnp.zeros((R, 1), jnp.int32))
        return x + 1

    big = jnp.full((R, 1), 1 << 30, jnp.int32)
    bound = lax.cond(any_tie, lambda: jnp.where(tie, tie_bound(), big), lambda: big)

    def bias_body(kb, carry):
        blk = s_ref[kb]
        kidx = kb * KB + lane_k
        sel = (blk > tau) | ((blk == tau) & (kidx < bound))
        sel = sel & (blk > -jnp.inf)
        s_ref[kb] = jnp.where(sel, 0.0, NEG_BIG)
        return carry

    lax.fori_loop(0, nkb, bias_body, 0)

    q_all = q_ref[0].reshape(8 * R, LANES)
    m_ref[...] = jnp.full(m_ref.shape, NEG_BIG, F32)
    acc_ref[...] = jnp.zeros(acc_ref.shape, F32)

    def att_body(kb, carry):
        for u in range(nsb):
            off = pl.multiple_of(kb * KB + u * SB, SB)
            kblk = k_ref[0, kb * nsb + u]
            vblk = v_ref[0, pl.ds(off, SB), :]
            bias = s_ref[kb, :, u * SB:(u + 1) * SB]
            logits = _mm(q_all, kblk)
            nch = SB // LANES
            for h0 in range(0, 8, HG):
                hs = list(range(h0, h0 + HG))
                lg = [logits[h * R:(h + 1) * R] + bias for h in hs]
                cm = [functools.reduce(jnp.maximum, [l[:, c * LANES:(c + 1) * LANES] for c in range(nch)])
                      for l in lg]
                m_old = [m_ref[h] for h in hs]
                m_new = [jnp.maximum(mo, jnp.max(c, axis=1, keepdims=True)) for mo, c in zip(m_old, cm)]
                alpha = [jnp.exp2(mo - mn) for mo, mn in zip(m_old, m_new)]
                p = [jnp.exp2(l - jnp.concatenate([mn] * nch, axis=1)).astype(BF16) for l, mn in zip(lg, m_new)]
                pv = [_mm(pp, vblk) for pp in p]
                for i, h in enumerate(hs):
                    acc_ref[h] = jnp.concatenate([alpha[i], alpha[i]], axis=1) * acc_ref[h] + pv[i]
                    m_ref[h] = m_new[i]
        return carry

    lax.fori_loop(0, nkb, att_body, 0)
    for h in range(8):
        a = acc_ref[h]
        o_ref[0, h] = (a[:, :LANES] / a[:, LANES:]).astype(BF16)


def _attention(iq, iw, q, ikb, kb, vb1, *, R, KB, SB, HG, ltot, q_off):
    B, _, Sq, _ = q.shape
    Lp = kb.shape[1]
    topk = min(TOPK_MAX, ltot // 4)
    assert Lp % KB == 0 and KB % SB == 0 and KB >= topk and Sq % R == 0
    to_blocks = lambda a: jnp.swapaxes(a.reshape(B, Lp // SB, SB, LANES), 2, 3)
    ikb, kb = to_blocks(ikb), to_blocks(kb)
    hspec = pl.BlockSpec((1, 8, R, LANES), lambda b, i: (b, 0, i, 0))
    kspec = pl.BlockSpec((1, Lp // SB, LANES, SB), lambda b, i: (b, 0, 0, 0))
    vspec = pl.BlockSpec((1, Lp, 2 * LANES), lambda b, i: (b, 0, 0))
    kern = functools.partial(_attn_kernel, R=R, KB=KB, SB=SB, HG=HG, ltot=ltot, q_off=q_off, topk=topk)
    return pl.pallas_call(
        kern,
        grid=(B, Sq // R),
        in_specs=[hspec, pl.BlockSpec((1, R, 8), lambda b, i: (b, i, 0)), hspec, kspec, kspec, vspec],
        out_specs=hspec,
        out_shape=jax.ShapeDtypeStruct((B, 8, Sq, LANES), BF16),
        scratch_shapes=[pltpu.VMEM((Lp // KB, R, KB), F32), pltpu.VMEM((Lp // KB, R, KB), BF16),
                        pltpu.VMEM((8, R, LANES), F32),
                        pltpu.VMEM((8, R, 2 * LANES), F32)],
        compiler_params=_params(("arbitrary", "arbitrary")),
        name="dsa_attention",
    )(iq, iw, q, ikb, kb, vb1)


def _head_sum(x):
    parts = []
    for h in range(8):
        s = jnp.sum(x[:, h * LANES:(h + 1) * LANES], axis=1, keepdims=True)
        parts.append(jnp.broadcast_to(s, (x.shape[0], LANES)))
    return jnp.concatenate(parts, axis=1)


def _rwkv_prep_kernel(z_ref, sh0_ref, mu_ref, w0_ref, wup_ref, a0_ref, aup_ref, gup_ref, kk_ref, ka_ref, rk_ref,
                      r_o, lw_o, k_o, v_o, kkn_o, b_o, g_o, bg_o, lwt_o, kt_o, bt_o, carry_ref,
                      *, tm, t_valid):
    i = pl.program_id(1)

    @pl.when(i == 0)
    def _():
        carry_ref[...] = sh0_ref[0]

    z = z_ref[0]
    row = lax.broadcasted_iota(jnp.int32, (tm, 1), 0)
    prev = jnp.where(row == 0, carry_ref[...], pltpu.roll(z, 1, axis=0))
    carry_ref[...] = z[tm - 1:tm, :]
    zm = z + (prev - z) * mu_ref[...]
    live = i * tm + row < t_valid
    zm = jnp.where(live, zm, 0.0)
    r = zm[:, 0:HP]
    k = zm[:, HP:2 * HP]
    v = zm[:, 2 * HP:3 * HP]
    wa = zm[:, 3 * HP:3 * HP + LANES]
    gd = zm[:, 3 * HP + LANES:3 * HP + 2 * LANES]
    w_raw = w0_ref[...] + _mm(jnp.tanh(wa), wup_ref[...], HI)
    lw = (-float(np.exp(-0.5))) / (1.0 + jnp.exp(-w_raw))
    lw = jnp.where(live, lw, 0.0)
    a = 1.0 / (1.0 + jnp.exp(-(a0_ref[...] + _mm(wa, aup_ref[...], HI))))
    g = _mm(1.0 / (1.0 + jnp.exp(-gd)), gup_ref[...], HI)
    kk = k * kk_ref[...]
    kk = kk * lax.rsqrt(jnp.maximum(_head_sum(kk * kk), L2_EPS))
    k_mod = k * (1.0 + (a - 1.0) * ka_ref[...])
    b = kk * a
    bonus = _head_sum(r * k_mod * rk_ref[...]) * v
    r_o[0] = r.astype(BF16)
    lw_o[0] = lw
    k_o[0] = k_mod.astype(BF16)
    v_o[0] = v.astype(BF16)
    kkn_o[0] = kk.astype(BF16)
    b_o[0] = b.astype(BF16)
    g_o[0] = g
    bg_o[0] = bonus * g
    lwt_o[0] = lw.T
    kt_o[0] = k_mod.T.astype(BF16)
    bt_o[0] = b.T.astype(BF16)


def _rwkv_prep(zr, shift0, prm, tm, t_valid):
    B, T, _ = zr.shape
    tspec = pl.BlockSpec((1, tm, HP), lambda b, i: (b, i, 0))
    fspec = pl.BlockSpec((1, HP, tm), lambda b, i: (b, 0, i))
    full = lambda a: pl.BlockSpec(a.shape, lambda b, i: (0,) * a.ndim)
    tok = jax.ShapeDtypeStruct((B, T, HP), F32)
    feat = jax.ShapeDtypeStruct((B, HP, T), F32)
    tokb = jax.ShapeDtypeStruct((B, T, HP), BF16)
    featb = jax.ShapeDtypeStruct((B, HP, T), BF16)
    names =("mu", "w0", "w_up", "a0", "a_up", "g_up", "k_k", "k_a", "r_k")
    ws = [prm[n] for n in names]
    return pl.pallas_call(
        functools.partial(_rwkv_prep_kernel, tm=tm, t_valid=t_valid),
        grid=(B, T // tm),
        in_specs=[pl.BlockSpec((1, tm, ZRW), lambda b, i: (b, i, 0)),
                  pl.BlockSpec((1, 1, ZRW), lambda b, i: (b, 0, 0))] + [full(w) for w in ws],
        out_specs=[tspec] * 8 + [fspec] * 3,
        out_shape=[tokb, tok, tokb, tokb, tokb, tokb, tok, tok, feat, featb, featb],
        scratch_shapes=[pltpu.VMEM((1, ZRW), F32)],
        compiler_params=_params(("arbitrary", "arbitrary")),
        name="rwkv_prep",
    )(zr, shift0, *ws)


def _rwkv_chunk_kernel(r_ref, lw_ref, k_ref, v_ref, kk_ref, b_ref, lwt_ref, kt_ref, bt_ref, a1_ref, a2_ref,
                       *, C, hb, pg, pc, po):
    ri = lax.broadcasted_iota(jnp.int32, (C, C), 0)
    ci = lax.broadcasted_iota(jnp.int32, (C, C), 1)
    low_incl = jnp.where(ri >= ci, 1.0, 0.0).astype(BF16)
    up_incl = jnp.where(ri <= ci, 1.0, 0.0).astype(BF16)
    strict = ri > ci
    incl = ri >= ci
    eye64 = jnp.where(lax.broadcasted_iota(jnp.int32, (64, 64), 0) == lax.broadcasted_iota(jnp.int32, (64, 64), 1),
                      1.0, 0.0)
    sls = [slice(u * LANES, (u + 1) * LANES) for u in range(hb)]
    each = lambda f, *ls: [f(*a) for a in zip(*ls)]
    lw = [lw_ref[0, :, sl] for sl in sls]
    r = [r_ref[0, :, sl].astype(F32) for sl in sls]
    k = [k_ref[0, :, sl].astype(F32) for sl in sls]
    v = [v_ref[0, :, sl].astype(F32) for sl in sls]
    kk = [kk_ref[0, :, sl].astype(F32) for sl in sls]
    b = [b_ref[0, :, sl].astype(F32) for sl in sls]
    lwt = [lwt_ref[0, sl, :][:64] for sl in sls]
    kt = [kt_ref[0, sl, :][:64].astype(F32) for sl in sls]
    bt = [bt_ref[0, sl, :][:64].astype(F32) for sl in sls]
    cum = each(lambda x: sum(_mm(low_incl, p) for p in _split_bf16(x, 3)), lw)
    cumt = each(lambda x: sum(_mm(p, up_incl) for p in _split_bf16(x, 3)), lwt)
    e_neg = each(lambda c: jnp.exp(-c), cum)
    at = each(lambda kk_, c, l: -kk_ * jnp.exp(c - l), kk, cum, lw)
    rt = each(lambda r_, c: r_ * jnp.exp(c), r, cum)
    g = each(lambda a, r_, b_, k_, e: _mmp(jnp.concatenate([a, r_], axis=0),
                                           jnp.concatenate([b_ * e, k_ * e], axis=0), pg, nt=True),
             at, rt, b, k, e_neg)
    n = each(lambda g_: jnp.where(strict, g_[:C, :C], 0.0), g)
    aak = each(lambda g_: jnp.where(strict, g_[:C, C:], 0.0), g)
    lrb = each(lambda g_: jnp.where(incl, g_[C:, :C], 0.0), g)
    lrk = each(lambda g_: jnp.where(incl, g_[C:, C:], 0.0), g)
    blk8 = lax.shift_right_logical(ri, 3) == lax.shift_right_logical(ci, 3)
    eye = jnp.where(ri == ci, 1.0, 0.0)
    n0 = each(lambda n_: jnp.where(blk8, n_, 0.0), n)
    x = each(lambda n_: eye + n_, n0)
    p = each(lambda n_: _mmp(n_, n_, pc), n0)
    x = each(lambda x_, p_: x_ + _mmp(x_, p_, pc), x, p)
    p = each(lambda p_: _mmp(p_, p_, pc), p)
    x = each(lambda x_, p_: x_ + _mmp(x_, p_, pc), x, p)
    for lv in range(3, int(C).bit_length() - 1):
        off = (lax.shift_right_logical(ri, lv + 1) == lax.shift_right_logical(ci, lv + 1)) & (
            lax.shift_right_logical(ri, lv) != lax.shift_right_logical(ci, lv))
        t = each(lambda n_, x_: _mmp(jnp.where(off, n_, 0.0), x_, pc), n, x)
        x = each(lambda x_, t_: x_ + _mmp(x_, t_, pc), x, t)
    z = each(lambda x_, a, b_: _mmp(x_, jnp.concatenate([a, b_], axis=1), pc), x, aak, at)
    w2v = each(lambda z_, v_: _mmp(z_[:, :C], v_, po), z, v)
    cl = each(lambda c: c[:, C - 1:C], cumt)
    eb = each(lambda c_, ct: jnp.exp(c_ - ct), cl, cumt)
    lhs = each(lambda bt_, kt_, e, lb, lk: jnp.concatenate(
        [jnp.concatenate([bt_ * e, kt_ * e], axis=1), jnp.concatenate([lb, lk], axis=1)], axis=0),
        bt, kt, eb, lrb, lrk)
    left = each(lambda l_, z_: _mmp(l_[:, :C], z_[:, C:], po), lhs, z)
    right = each(lambda l_, w, v_: _mmp(l_, jnp.concatenate([w, v_], axis=0), po), lhs, w2v, v)
    for u in range(hb):
        diag = eye64 * jnp.exp(cl[u])
        a1_ref[0, u, 0] = jnp.concatenate([left[u][:64, :64] + diag, (left[u][64:] + rt[u])[:, :64]], axis=0)
        a2_ref[0, u, 0] = right[u]


def _rwkv_chunks(r, lw, k, v, kk, b, lwt, kt, bt, C, hb, passes=RW_PASSES):
    B, T, _ = r.shape
    nc = T // C
    tspec = pl.BlockSpec((1, C, hb * LANES), lambda bb, h, c: (bb, c, h))
    fspec = pl.BlockSpec((1, hb * LANES, C), lambda bb, h, c: (bb, h, c))
    pg, pc, po = passes
    return pl.pallas_call(
        functools.partial(_rwkv_chunk_kernel, C=C, hb=hb, pg=pg, pc=pc, po=po),
        grid=(B, 8 // hb, nc),
        in_specs=[tspec] * 6 + [fspec] * 3,
        out_specs=[pl.BlockSpec((1, hb, 1, 64 + C, 64), lambda bb, h, c: (bb, h, c, 0, 0)),
                   pl.BlockSpec((1, hb, 1, 64 + C, LANES), lambda bb, h, c: (bb, h, c, 0, 0))],
        out_shape=[jax.ShapeDtypeStruct((B, 8, nc, 64 + C, 64), F32),
                   jax.ShapeDtypeStruct((B, 8, nc, 64 + C, LANES), F32)],
        compiler_params=_params(("arbitrary", "arbitrary", "arbitrary")),
        name="rwkv_chunks",
    )(r, lw, k, v, kk, b, lwt, kt, bt)


def _rwkv_scan_kernel(a1_ref, a2_ref, s0_ref, y_ref, sT_ref, h_ref, *, C, cb):
    c = pl.program_id(1)

    @pl.when(c == 0)
    def _():
        h_ref[...] = s0_ref[0]

    for cc in range(cb):
        for h in range(8):
            res = _mm(a1_ref[0, h, cc], h_ref[h], HI) + a2_ref[0, h, cc]
            h_ref[h] = res[:64]
            y_ref[0, cc * C:(cc + 1) * C, h * LANES:(h + 1) * LANES] = res[64:]

    @pl.when(c == pl.num_programs(1) - 1)
    def _():
        sT_ref[0] = h_ref[...]


def _rwkv_scan(a1, a2, s0t, C, cb):
    B, _, nc, _, _ = a1.shape
    return pl.pallas_call(
        functools.partial(_rwkv_scan_kernel, C=C, cb=cb),
        grid=(B, nc // cb),
        in_specs=[pl.BlockSpec((1, 8, cb, 64 + C, 64), lambda b, c: (b, 0, c, 0, 0)),
                  pl.BlockSpec((1, 8, cb, 64 + C, LANES), lambda b, c: (b, 0, c, 0, 0)),
                  pl.BlockSpec((1, 8, 64, LANES), lambda b, c: (b, 0, 0, 0))],
        out_specs=[pl.BlockSpec((1, cb * C, HP), lambda b, c: (b, c, 0)),
                   pl.BlockSpec((1, 8, 64, LANES), lambda b, c: (b, 0, 0, 0))],
        out_shape=[jax.ShapeDtypeStruct((B, nc * C, HP), F32),
                   jax.ShapeDtypeStruct((B, 8, 64, LANES), F32)],
        scratch_shapes=[pltpu.VMEM((8, 64, LANES), F32)],
        compiler_params=_params(("arbitrary", "arbitrary")),
        name="rwkv_scan",
    )(a1, a2, s0t)


def _mix_kernel(att_ref, y_ref, g_ref, bg_ref, sga_ref, sgb_ref, x_ref, gt1_ref, sc2_ref, sh2_ref,
                woa_ref, wor_ref, wout_ref, lnw_ref, lnb_ref, gpost_ref, gpre_ref, wr_ref, br_ref,
                x1_ref, h2_ref, gate_ref, eidx_ref, wgt_ref, rel_ref, cnt_out_ref, cnt_ref, *, tm):
    att = jnp.concatenate([att_ref[0, h] for h in range(8)], axis=1)
    lane = lax.broadcasted_iota(jnp.int32, (tm, LANES), 1)
    valid = lane < RWKV_HEAD
    parts = []
    for h in range(8):
        y = y_ref[0, :, h * LANES:(h + 1) * LANES]
        mu = jnp.sum(y, axis=1, keepdims=True) * (1.0 / RWKV_HEAD)
        d = jnp.where(valid, y - mu, 0.0)
        var = jnp.sum(d * d, axis=1, keepdims=True) * (1.0 / RWKV_HEAD)
        parts.append(d * lax.rsqrt(var + GN_EPS))
    yn = jnp.concatenate(parts, axis=1)
    rw = (yn * lnw_ref[...] + lnb_ref[...]) * g_ref[0] + bg_ref[0]
    mix = sga_ref[0] * _mm(att, woa_ref[...]) + sgb_ref[0] * _mm(rw.astype(BF16), wor_ref[...])
    o = _mm(mix.astype(BF16), wout_ref[...])
    x1 = x_ref[0] + gt1_ref[0] * _rms(o, gpost_ref[...])
    x1_ref[0] = x1
    h2 = _rms(x1, gpre_ref[...]) * (1.0 + sc2_ref[0]) + sh2_ref[0]
    h2_ref[0] = h2
    logits = _mm(h2, wr_ref[...], HI) + br_ref[...]
    el = lax.broadcasted_iota(jnp.int32, logits.shape, 1).astype(F32)
    work = logits
    sel = jnp.zeros(logits.shape, jnp.bool_)
    vmax = None
    firsts = []
    for kk in range(TOP_K):
        mx = jnp.max(work, axis=1, keepdims=True)
        if kk == 0:
            vmax = mx
        first = jnp.min(jnp.where(work == mx, el, float(N_EXPERTS)), axis=1, keepdims=True)
        hit = el == first
        firsts.append(first)
        sel = sel | hit
        work = jnp.where(hit, -jnp.inf, work)
    e = jnp.where(sel, jnp.exp(logits - vmax), 0.0)
    gate = e / jnp.sum(e, axis=1, keepdims=True)
    gate_ref[0] = gate
    @pl.when((pl.program_id(0) == 0) & (pl.program_id(1) == 0))
    def _():
        cnt_ref[...] = jnp.zeros(cnt_ref.shape, F32)

    self = jnp.where(sel, 1.0, 0.0)
    ti = lax.broadcasted_iota(jnp.int32, (tm, tm), 0)
    tj = lax.broadcasted_iota(jnp.int32, (tm, tm), 1)
    before = _mm(jnp.where(tj < ti, 1.0, 0.0).astype(BF16), self.astype(BF16)) + cnt_ref[...]
    cnt_ref[...] = cnt_ref[...] + jnp.sum(self, axis=0, keepdims=True)
    cnt_out_ref[...] = cnt_ref[...]
    ln = lax.broadcasted_iota(jnp.int32, (tm, LANES), 1)
    eo = jnp.zeros((tm, LANES), F32)
    go = jnp.zeros((tm, LANES), F32)
    ro = jnp.zeros((tm, LANES), F32)
    for kk in range(TOP_K):
        hit = el == firsts[kk]
        gk = jnp.sum(jnp.where(hit, gate, 0.0), axis=1, keepdims=True)
        rk = jnp.sum(jnp.where(hit, before, 0.0), axis=1, keepdims=True)
        eo = jnp.where(ln == kk, firsts[kk], eo)
        go = jnp.where(ln == kk, gk, go)
        ro = jnp.where(ln == kk, rk, ro)
    eidx_ref[0] = eo.astype(jnp.int32)
    wgt_ref[0] = go
    rel_ref[0] = ro.astype(jnp.int32)


def _mix(att, y, g, bg, sga, sgb, x, gt1, sc2, sh2, wts, tm):
    B, T, D = x.shape
    tspec = lambda n: pl.BlockSpec((1, tm, n), lambda b, i: (b, i, 0))
    full = lambda a: pl.BlockSpec(a.shape, lambda b, i: (0,) * a.ndim)
    tok = lambda n, dt: jax.ShapeDtypeStruct((B, T, n), dt)
    return pl.pallas_call(
        functools.partial(_mix_kernel, tm=tm),
        grid=(B, T // tm),
        in_specs=[pl.BlockSpec((1, 8, tm, LANES), lambda b, i: (b, 0, i, 0)), tspec(HP), tspec(HP), tspec(HP),
                  tspec(D), tspec(D), tspec(D), _mod_spec(gt1, tm), _mod_spec(sc2, tm), _mod_spec(sh2, tm)]
                 + [full(w) for w in wts],
        out_specs=[tspec(D), tspec(D), tspec(N_EXPERTS), tspec(LANES), tspec(LANES), tspec(LANES),
                   pl.BlockSpec((1, N_EXPERTS), lambda b, i: (0, 0))],
        out_shape=[tok(D, F32), tok(D, F32), tok(N_EXPERTS, F32), tok(LANES, jnp.int32), tok(LANES, F32),
                   tok(LANES, jnp.int32), jax.ShapeDtypeStruct((1, N_EXPERTS), F32)],
        scratch_shapes=[pltpu.VMEM((1, N_EXPERTS), F32)],
        compiler_params=_params(("arbitrary", "arbitrary")),
        name="mix_router",
    )(att, y, g, bg, sga, sgb, x, gt1, sc2, sh2, *wts)


def _expert(xb, wgu, bgu, wd, bd, d_ff):
    wgu = wgu.astype(BF16)
    wd = wd.astype(BF16)
    hgu = _mm(xb, wgu) + bgu
    hg = jnp.minimum(hgu[:, :d_ff], SWIGLU_LIMIT)
    hl = jnp.clip(hgu[:, d_ff:], -SWIGLU_LIMIT, SWIGLU_LIMIT)
    act = hg * (1.0 / (1.0 + jnp.exp(-SWIGLU_ALPHA * hg))) * (hl + 1.0)
    return _mm(act.astype(BF16), wd) + bd


def _sc_mesh():
    return plsc.VectorSubcoreMesh(core_axis_name="core", subcore_axis_name="subcore")


def _sc_dispatch(h, pos_slots, n_rows, q):
    N, D = h.shape
    mesh = _sc_mesh()
    steps = N // (mesh.num_cores * SC_WINDOW)
    assert steps * mesh.num_cores * SC_WINDOW == N

    @pl.kernel(out_type=jax.ShapeDtypeStruct((n_rows, SC_PIECE), h.dtype), mesh=mesh, scratch_types=[])
    def scatter(x_hbm, *refs):
        idx_hbm, o_hbm = refs[:-1], refs[-1]
        base = lax.axis_index("core") * steps

        def body(x_vmem, *i_vmem):
            for iv in i_vmem:
                pltpu.sync_copy(x_vmem, o_hbm.at[iv.at[0]])

        pltpu.emit_pipeline(
            body,
            grid=(steps,),
            in_specs=[pl.BlockSpec((SC_WINDOW, SC_PIECE), index_map=lambda i: (base + i, q))]
                     + [pl.BlockSpec((1, SC_WINDOW), index_map=lambda i: (0, base + i))] * len(idx_hbm),
            out_specs=[],
            core_axis_name="subcore",
            dimension_semantics=(pltpu.PARALLEL,),
        )(x_hbm, *idx_hbm)

    return scatter(h, *pos_slots)


def _sc_gather(table, idx):
    M = idx.shape[1]
    mesh = _sc_mesh()
    steps = M // (mesh.num_cores * SC_WINDOW)
    assert steps * mesh.num_cores * SC_WINDOW == M

    @pl.kernel(out_type=jax.ShapeDtypeStruct((M, SC_PIECE), table.dtype), mesh=mesh)
    def gather(x_hbm, i_hbm, o_hbm):
        base = lax.axis_index("core") * steps

        def body(i_vmem, o_vmem):
            pltpu.sync_copy(x_hbm.at[i_vmem.at[0]], o_vmem)

        pltpu.emit_pipeline(
            body,
            grid=(steps,),
            in_specs=[pl.BlockSpec((1, SC_WINDOW), index_map=lambda i: (0, base + i))],
            out_specs=[pl.BlockSpec((SC_WINDOW, SC_PIECE), index_map=lambda i: (base + i, 0))],
            core_axis_name="subcore",
            dimension_semantics=(pltpu.PARALLEL,),
        )(i_hbm, o_hbm)

    return gather(table, idx)


def _moe_grouped_kernel(te_ref, nt_ref, *refs, d_ff, npiece):
    x_refs, (wgu_ref, bgu_ref, wd_ref, bd_ref) = refs[:npiece], refs[npiece:npiece + 4]
    o_refs, (wgu_b, wd_b) = refs[npiece + 4:2 * npiece + 4], refs[2 * npiece + 4:]
    g = pl.program_id(0)

    @pl.when((g == 0) | (te_ref[g] != te_ref[jnp.maximum(g - 1, 0)]))
    def _():
        wgu_b[...] = wgu_ref[0].astype(BF16)
        wd_b[...] = wd_ref[0].astype(BF16)

    @pl.when(g < nt_ref[0])
    def _():
        xb = jnp.concatenate([r[...] for r in x_refs], axis=1).astype(BF16)
        y = _expert(xb, wgu_b[...], bgu_ref[0], wd_b[...], bd_ref[0], d_ff)
        for q, o_ref in enumerate(o_refs):
            o_ref[...] = y[:, q * SC_PIECE:(q + 1) * SC_PIECE]


def _moe_grouped(xs, te, nt, wgu, bgu, wd, bd, TG):
    NP = xs[0].shape[0]
    E, D, F2 = wgu.shape
    npiece = len(xs)
    wmap = lambda g, te_, nt_: (te_[g], 0, 0)
    xmap = lambda g, te_, nt_: (jnp.minimum(g, nt_[0] - 1), 0)
    pspec = pl.BlockSpec((TG, SC_PIECE), xmap)
    return pl.pallas_call(
        functools.partial(_moe_grouped_kernel, d_ff=F2 // 2, npiece=npiece),
        grid_spec=pltpu.PrefetchScalarGridSpec(
            num_scalar_prefetch=2,
            grid=(NP // TG,),
            in_specs=[pspec] * npiece + [pl.BlockSpec((1, D, F2), wmap), pl.BlockSpec((1, 1, F2), wmap),
                                         pl.BlockSpec((1, F2 // 2, D), wmap), pl.BlockSpec((1, 1, D), wmap)],
            out_specs=[pspec] * npiece,
            scratch_shapes=[pltpu.VMEM((D, F2), BF16), pltpu.VMEM((F2 // 2, D), BF16)]),
        out_shape=[jax.ShapeDtypeStruct((NP, SC_PIECE), F32)] * npiece,
        compiler_params=_params(("arbitrary",)),
        name="moe_grouped",
    )(te, nt, *xs, wgu, bgu, wd, bd)


def _moe_combine_kernel(*refs, npiece):
    y_refs = refs[:TOP_K * npiece]
    w_ref, x1_ref, gt2_ref, gpost_ref, o_ref = refs[TOP_K * npiece:]
    w = w_ref[...]
    acc = None
    for k in range(TOP_K):
        yk = jnp.concatenate([y_refs[k * npiece + q][...] for q in range(npiece)], axis=1)
        term = w[:, k:k + 1] * yk
        acc = term if acc is None else acc + term
    o_ref[...] = x1_ref[...] + gt2_ref[0] * _rms(acc, gpost_ref[...])


def _moe_combine(yslot, wgt, x1, gt2, gpost, tm):
    N, D = x1.shape
    nb = N // tm
    tpb = N // gt2.shape[0]
    npiece = len(yslot)
    yspec = lambda k: pl.BlockSpec((tm, SC_PIECE), lambda i, k=k: (i + k * nb, 0))
    return pl.pallas_call(
        functools.partial(_moe_combine_kernel, npiece=npiece),
        grid=(nb,),
        in_specs=[yspec(k) for k in range(TOP_K) for _ in range(npiece)]
                 + [pl.BlockSpec((tm, LANES), lambda i: (i, 0)), pl.BlockSpec((tm, D), lambda i: (i, 0)),
                    pl.BlockSpec((1, 1, D), lambda i: ((i * tm) // tpb, 0, 0)), pl.BlockSpec((1, D), lambda i: (0, 0))],
        out_specs=pl.BlockSpec((tm, D), lambda i: (i, 0)),
        out_shape=jax.ShapeDtypeStruct((N, D), F32),
        compiler_params=_params(("arbitrary",)),
        name="moe_combine",
    )(*[yslot[q] for _ in range(TOP_K) for q in range(npiece)], wgt, x1, gt2, gpost)


def _moe_kernel(h_ref, gate_ref, x1_ref, gt2_ref, gpost_ref, wgu_ref, bgu_ref, wd_ref, bd_ref, o_ref, acc_ref,
                *, d_ff):
    e = pl.program_id(1)

    @pl.when(e == 0)
    def _():
        acc_ref[...] = jnp.zeros(acc_ref.shape, F32)

    contrib = _expert(h_ref[...].astype(BF16), wgu_ref[0], bgu_ref[0], wd_ref[0], bd_ref[0], d_ff)
    gate = gate_ref[...]
    el = lax.broadcasted_iota(jnp.int32, gate.shape, 1)
    ge = jnp.sum(jnp.where(el == e, gate, 0.0), axis=1, keepdims=True)
    acc_ref[...] += ge * contrib

    @pl.when(e == pl.num_programs(1) - 1)
    def _():
        o_ref[...] = x1_ref[...] + gt2_ref[0] * _rms(acc_ref[...], gpost_ref[...])


def _moe(h2, gate, x1, gt2, gpost, wgu, bgu, wd, bd, tm):
    N, D = h2.shape
    E, _, F2 = wgu.shape
    tspec = lambda n: pl.BlockSpec((tm, n), lambda i, e: (i, 0))
    if gt2.shape[1] == 1:
        tpb = N // gt2.shape[0]
        gspec = pl.BlockSpec((1, 1, D), lambda i, e: ((i * tm) // tpb, 0, 0))
    else:
        gspec = pl.BlockSpec((1, tm, D), lambda i, e: (i, 0, 0))
    return pl.pallas_call(
        functools.partial(_moe_kernel, d_ff=F2 // 2),
        grid=(N // tm, E),
        in_specs=[tspec(D), tspec(E), tspec(D), gspec, pl.BlockSpec((1, D), lambda i, e: (0, 0)),
                  pl.BlockSpec((1, D, F2), lambda i, e: (e, 0, 0)), pl.BlockSpec((1, 1, F2), lambda i, e: (e, 0, 0)),
                  pl.BlockSpec((1, F2 // 2, D), lambda i, e: (e, 0, 0)), pl.BlockSpec((1, 1, D), lambda i, e: (e, 0, 0))],
        out_specs=tspec(D),
        out_shape=jax.ShapeDtypeStruct((N, D), F32),
        scratch_shapes=[pltpu.VMEM((tm, D), F32)],
        compiler_params=_params(("arbitrary", "arbitrary")),
        name="moe",
    )(h2, gate, x1, gt2, gpost, wgu, bgu, wd, bd)


def _per_token(m, T):
    B, _, D = m.shape
    return jnp.broadcast_to(m, (B, T, D)).reshape(1, B * T, D)


def _layer(x, mods, past, s0, shift0, P, cfg):
    B, T, D = x.shape
    sh1, sc1, gt1, sh2, sc2, gt2 = mods
    tm = cfg["tm"]

    q, k, v, kb, vb, iq, ik, ikb, iw = _proj_a(x, sc1, sh1, P["g_pre_mix"], P["w_a"], P["ik_ln_w"], P["ik_ln_b"],
                                              cfg["tm_a"])
    zr, sga, sgb = _proj_b(x, sc1, sh1, P["g_pre_mix"], P["w_b"], tm)

    if past is not None:
        pk, pv, pik = past
        plen = pk.shape[1]
        kb_all = jnp.concatenate([pk.reshape(B, plen, 128).astype(BF16), kb], axis=1)
        vb_all = jnp.concatenate([pv.reshape(B, plen, 128).astype(BF16), vb], axis=1)
        ik_all = jnp.concatenate([jnp.pad(pik, ((0, 0), (0, 0), (0, 64))).astype(BF16), ikb], axis=1)
    else:
        plen = 0
        kb_all, vb_all, ik_all = kb, vb, ikb
    ltot = plen + T
    KB = cfg["KB"]
    lp = -(-ltot // KB) * KB
    if lp != ltot:
        padk = ((0, 0), (0, lp - ltot), (0, 0))
        kb_all, vb_all, ik_all = jnp.pad(kb_all, padk), jnp.pad(vb_all, padk), jnp.pad(ik_all, padk)
    vb1 = jnp.concatenate([vb_all, jnp.ones_like(vb_all)], axis=-1)
    att = _attention(iq, iw, q, ik_all, kb_all, vb1, R=cfg["R"], KB=KB, SB=cfg["SB"], HG=cfg["HG"], ltot=ltot,
                     q_off=plen)

    C = RW_CHUNK
    tp = -(-T // C) * C
    zr_p = zr if tp == T else jnp.pad(zr, ((0, 0), (0, tp - T), (0, 0)))
    shift_pad = _take_cols(shift0.reshape(B, RWKV_COLS), _COLS_ZR).reshape(B, 1, ZRW)
    r, lw, km, vv, kkn, bb, g, bg, lwt, kt, bt = _rwkv_prep(zr_p, shift_pad, P, min(cfg["tm_rw"], tp), T)
    a1, a2 = _rwkv_chunks(r, lw, km, vv, kkn, bb, lwt, kt, bt, C, cfg["hb"])
    s0t = jnp.pad(jnp.swapaxes(s0, 2, 3), ((0, 0), (0, 0), (0, 0), (0, 64)))
    nc = tp // C
    cb = min(cfg["cb"], nc)
    y, sT = _rwkv_scan(a1, a2, s0t, C, cb)
    s_new = jnp.swapaxes(sT[..., :64], 2, 3)
    zlast = zr[:, T - 1]
    inv = np.zeros((RWKV_COLS,), np.int32)
    inv[_COLS_ZR[0][_COLS_ZR[1]]] = np.nonzero(_COLS_ZR[1])[0]
    shift_new = jnp.take(zlast, jnp.asarray(inv), axis=1).reshape(B, 1, RWKV_COLS)
    if tp != T:
        y, g, bg = y[:, :T], g[:, :T], bg[:, :T]

    wts = [P[n] for n in ("w_o_att", "w_o_rwkv", "w_out", "ln_x_w", "ln_x_b", "g_post_mix", "g_pre_ffn",
                          "w_router", "b_router")]
    x1, h2, gate, eidx, wgt, rel, cnt = _mix(att, y, g, bg, sga, sgb, x, gt1, sc2, sh2, wts, tm)

    N = B * T
    tmm = min(cfg["tm_moe"], N)
    if cfg["routed"]:
        TG = cfg["TG"]
        ntm = N * TOP_K // TG + N_EXPERTS
        cnt_i = cnt[0].astype(jnp.int32)
        padded = (cnt_i + TG - 1) // TG * TG
        ends = jnp.cumsum(padded)
        off = ends - padded
        nt = ends[-1] // TG
        gi = jnp.minimum(jnp.arange(ntm, dtype=jnp.int32), nt - 1)
        te = jnp.sum((ends // TG)[None, :] <= gi[:, None], axis=1).astype(jnp.int32)
        e4 = eidx.reshape(N, LANES)[:, :TOP_K]
        pos = jnp.take(off, e4) + rel.reshape(N, LANES)[:, :TOP_K]
        pos_t = pos.T
        npiece = D // SC_PIECE
        h2f = h2.reshape(N, D)
        xs = [_sc_dispatch(h2f, [pos_t[k:k + 1] for k in range(TOP_K)], ntm * TG, q) for q in range(npiece)]
        ys = _moe_grouped(xs, te, nt.reshape(1), P["w_gu"], P["b_gu"], P["w_down"], P["b_down"], TG)
        idx_c = pos_t.reshape(1, TOP_K * N)
        yslot = [_sc_gather(ys[q], idx_c) for q in range(npiece)]
        out = _moe_combine(yslot, wgt.reshape(N, LANES), x1.reshape(N, D), gt2, P["g_post_ffn"], tm)
    else:
        if T % tmm == 0:
            gt2m = gt2
        else:
            gt2m = _per_token(gt2, T).reshape(N // tmm, tmm, D)
        out = _moe(h2.reshape(N, D), gate.reshape(N, N_EXPERTS), x1.reshape(N, D), gt2m, P["g_post_ffn"],
                   P["w_gu"], P["b_gu"], P["w_down"], P["b_down"], tmm)
    return (out.reshape(B, T, D), k.reshape(B, T, ATT_KV_HEADS, HEAD_DIM), v.reshape(B, T, ATT_KV_HEADS, HEAD_DIM),
            ik, s_new, shift_new)


def _prep_weights(l, w_in, ik_ln_w, ik_ln_b, mu_rwkv, w0, w_up, a0, a_up, g_up, k_k, k_a, r_k, ln_x_w, ln_x_b,
                  w_o_att, w_o_rwkv, w_out, w_router, b_router, w_gu, b_gu, w_down, b_down,
                  g_pre_mix, g_post_mix, g_pre_ffn, g_post_ffn):
    row = lambda a: a.reshape(1, -1)
    hp = lambda a: _take_cols(row(a), _HEAD_IDX)
    P = {}
    P["w_a"] = _take_cols(w_in[l], _COLS_A).astype(BF16)
    P["w_b"] = _take_cols(w_in[l], _COLS_B).astype(BF16)
    P["ik_ln_w"] = jnp.pad(row(ik_ln_w[l]), ((0, 0), (0, 64)))
    P["ik_ln_b"] = jnp.pad(row(ik_ln_b[l]), ((0, 0), (0, 64)))
    P["mu"] = _take_cols(row(mu_rwkv[l]), _COLS_ZR)
    P["w0"], P["a0"], P["k_k"], P["k_a"] = hp(w0[l]), hp(a0[l]), hp(k_k[l]), hp(k_a[l])
    P["r_k"] = hp(r_k[l].reshape(-1))
    P["ln_x_w"], P["ln_x_b"] = hp(ln_x_w[l]), hp(ln_x_b[l])
    z64 = jnp.zeros((64, HP), F32)
    P["w_up"] = jnp.concatenate([_take_cols(w_up[l], _HEAD_IDX), z64], axis=0)
    P["a_up"] = jnp.concatenate([z64, _take_cols(a_up[l], _HEAD_IDX)], axis=0)
    P["g_up"] = _take_cols(g_up[l], _HEAD_IDX)
    P["w_o_att"] = _take_rows(w_o_att[l], _ATT_ROW_IDX).astype(BF16)
    P["w_o_rwkv"] = _take_rows(w_o_rwkv[l], _HEAD_IDX).astype(BF16)
    P["w_out"] = w_out[l].astype(BF16)
    P["w_router"] = w_router[l]
    P["b_router"] = row(b_router[l])
    P["w_gu"] = w_gu[l]
    P["b_gu"] = b_gu[l].reshape(N_EXPERTS, 1, -1)
    P["w_down"] = w_down[l]
    P["b_down"] = b_down[l].reshape(N_EXPERTS, 1, -1)
    P["g_pre_mix"], P["g_post_mix"] = row(g_pre_mix[l]), row(g_post_mix[l])
    P["g_pre_ffn"], P["g_post_ffn"] = row(g_pre_ffn[l]), row(g_post_ffn[l])
    return P


CFG_PROMPT = dict(tm=256, tm_a=1024, R=256,KB=512, SB=256, HG=2,tm_rw=256, hb=8, cb=2, tm_moe=1024, routed=True, TG=512)
CFG_SAMPLE = dict(tm=32, tm_a=32, R=32, KB=512, SB=256, HG=2,tm_rw=128, hb=8, cb=1, tm_moe=512, routed=False)


def kernel(x_prompt, x_sample, c_prompt, c_sample, cache_k, cache_v, cache_idx_k, state_rwkv, state_shift, w_ada, b_ada, g_pre_mix, g_post_mix, g_pre_ffn, g_post_ffn, w_in, ik_ln_w, ik_ln_b, mu_rwkv, w0, w_up, a0, a_up, g_up, k_k, k_a, r_k, ln_x_w, ln_x_b, w_o_att, w_o_rwkv, w_out, w_router, b_router, w_gu, b_gu, w_down, b_down):
    depth = w_in.shape[0]
    bp, tp_, D = x_prompt.shape
    bs, ts, _ = x_sample.shape
    y_p, y_s = x_prompt, x_sample
    st_p = [[] for _ in range(5)]
    st_s = [[] for _ in range(5)]
    nc_all = bp + bs
    npad = -(-nc_all // 8) * 8
    c_all = jnp.pad(jnp.concatenate([c_prompt, c_sample], axis=0), ((0, npad - nc_all), (0, 0)))
    for l in range(depth):
        P = _prep_weights(l, w_in, ik_ln_w, ik_ln_b, mu_rwkv, w0, w_up, a0, a_up, g_up, k_k, k_a, r_k, ln_x_w,
                          ln_x_b, w_o_att, w_o_rwkv, w_out, w_router, b_router, w_gu, b_gu, w_down, b_down,
                          g_pre_mix, g_post_mix, g_pre_ffn, g_post_ffn)
        ada = _ada(c_all, w_ada[l], b_ada[l])
        mods_p = [m[:bp, None, :] for m in jnp.split(ada, 6, axis=-1)]
        mods_s = [m[bp:nc_all, None, :] for m in jnp.split(ada, 6, axis=-1)]
        zero_state = jnp.zeros((bp, RWKV_HEADS, RWKV_HEAD, RWKV_HEAD), F32)
        zero_shift = jnp.zeros((bp, 1, RWKV_COLS), F32)
        outs_p = _layer(y_p, mods_p, None, zero_state, zero_shift, P, CFG_PROMPT)
        outs_s = _layer(y_s, mods_s, (cache_k[l], cache_v[l], cache_idx_k[l]), state_rwkv[l], state_shift[l], P,
                        CFG_SAMPLE)
        y_p, y_s = outs_p[0], outs_s[0]
        for lst, val in zip(st_p, outs_p[1:]):
            lst.append(val)
        for lst, val in zip(st_s, outs_s[1:]):
            lst.append(val)
    sp = [jnp.stack(v, axis=0) for v in st_p]
    ss = [jnp.stack(v, axis=0) for v in st_s]
    return (y_p, y_s, sp[0], sp[1], sp[2], sp[3], sp[4], ss[0], ss[1], ss[2], ss[3], ss[4])
```

```python
import functools

import numpy as np
import jax
import jax.numpy as jnp
from jax import lax
from jax.experimental import pallas as pl
from jax.experimental.pallas import tpu as pltpu
from jax.experimental.pallas import tpu_sc as plsc

F32 = jnp.float32
BF16 = jnp.bfloat16
HI = lax.Precision.HIGHEST

CHUNK = 64
ATT_HEADS = 8
ATT_KV_HEADS = 2
HEAD_DIM = 64
IDX_HEADS = 8
IDX_DIM = 64
IDX_SCALE = (IDX_DIM ** -0.5) * (IDX_HEADS ** -0.5)
TOPK_MAX = 256
RWKV_HEADS = 8
RWKV_HEAD = 64
RWKV_WIDTH = RWKV_HEADS * RWKV_HEAD
DECAY_LORA = 64
AAA_LORA = 64
GATE_LORA = 128
RWKV_COLS = 3 * RWKV_WIDTH + DECAY_LORA + AAA_LORA + GATE_LORA
GN_EPS = 64e-5
L2_EPS = 1e-24
N_EXPERTS = 32
TOP_K = 4
SWIGLU_LIMIT = 7.0
SWIGLU_ALPHA = 1.702
NORM_EPS = 1e-6
LN_EPS = 1e-6

LANES = 128
VMEM_LIMIT = 56 * 1024 * 1024

HP = 8 * LANES
RW_CHUNK = 128
NEG_BIG = -1e30
LOG2E = 1.4426950408889634
KEY_NEG_INF = -2139095041
INT_MIN = -2147483648
SC_WINDOW = 128
SC_PIECE = 256
LO_GROUP = 4
RW_PASSES = (1, 1, 1)


def _params(sem):
    return pltpu.CompilerParams(dimension_semantics=sem, vmem_limit_bytes=VMEM_LIMIT)


def _nt(a, b, precision=None):
    return lax.dot_general(a, b, (((1,), (1,)), ((), ())), precision=precision,
                           preferred_element_type=F32)


def _mm(a, b, precision=None):
    return jnp.dot(a, b, precision=precision, preferred_element_type=F32)


def _split_bf16(x, terms):
    out = []
    for _ in range(terms):
        p = x.astype(BF16)
        out.append(p)
        x = x - p.astype(F32)
    return out


def _mmp(a, b, passes, nt=False):
    dot = _nt if nt else _mm
    if passes == 1:
        return dot(a.astype(BF16), b.astype(BF16))
    ah, al = _split_bf16(a, 2)
    bh, bl = _split_bf16(b, 2)
    return dot(ah, bh) + (dot(ah, bl) + dot(al, bh))


def _rms(x, g):
    return x * lax.rsqrt(jnp.mean(x * x, axis=-1, keepdims=True) + NORM_EPS) * g


def _headpad_idx(seg_off, lane_off_fn=lambda h: 0):
    idx = np.zeros((HP,), np.int32)
    ok = np.zeros((HP,), bool)
    for h in range(8):
        d0 = h * LANES + lane_off_fn(h)
        idx[d0:d0 + 64] = seg_off + h * 64 + np.arange(64)
        ok[d0:d0 + 64] = True
    return idx, ok


def _plain_idx(seg_off, n, width):
    idx = np.zeros((width,), np.int32)
    ok = np.zeros((width,), bool)
    idx[:n] = seg_off + np.arange(n)
    ok[:n] = True
    return idx, ok


def _cat(parts):
    return np.concatenate([p[0] for p in parts]), np.concatenate([p[1] for p in parts])


_O_Q, _O_K, _O_V, _O_IQ, _O_IK, _O_IW, _O_ZR, _O_GA, _O_GB = 0, 512, 640, 768, 1280, 1344, 1352, 3144, 4168
_COLS_A = _cat([_headpad_idx(_O_Q, lambda h: (h // 4) * 64), _plain_idx(_O_K, 128, 128), _plain_idx(_O_V, 128, 128),
                _headpad_idx(_O_IQ), _plain_idx(_O_IK, 64, 128), _plain_idx(_O_IW, 8, 128)])
_COLS_ZR = _cat([_headpad_idx(0), _headpad_idx(512), _headpad_idx(1024), _plain_idx(1536, 128, 128),
                 _plain_idx(1664, 128, 128)])
ZRW = _COLS_ZR[0].shape[0]
_COLS_B = _cat([(_COLS_ZR[0] + _O_ZR, _COLS_ZR[1]), _plain_idx(_O_GA, 1024, 1024), _plain_idx(_O_GB, 1024, 1024)])
NA = _COLS_A[0].shape[0]
NB = _COLS_B[0].shape[0]
_HEAD_IDX = _headpad_idx(0)
_ATT_ROW_IDX = _headpad_idx(0, lambda h: (h // 4) * 64)


def _take_cols(w, cols):
    idx, ok = cols
    return jnp.where(jnp.asarray(ok)[None, :], jnp.take(w, jnp.asarray(idx), axis=1), 0.0)


def _take_rows(w, cols):
    idx, ok = cols
    return jnp.where(jnp.asarray(ok)[:, None], jnp.take(w, jnp.asarray(idx), axis=0), 0.0)


def _mod_spec(m, tm):
    assert m.shape[1] == 1
    return pl.BlockSpec((1, 1, m.shape[2]), lambda b, i: (b, 0, 0))


def _ada_kernel(c_ref, w_ref, b_ref, o_ref):
    c = c_ref[...]
    s = c * (1.0 / (1.0 + jnp.exp(-c)))
    o_ref[...] = _mm(s, w_ref[...], HI) + b_ref[...]


def _ada(c, w, b):
    n, d = c.shape
    nout = w.shape[1]
    bn = 768
    return pl.pallas_call(
        _ada_kernel,
        grid=(nout // bn,),
        in_specs=[pl.BlockSpec((n, d), lambda j: (0, 0)),
                  pl.BlockSpec((d, bn), lambda j: (0, j)),
                  pl.BlockSpec((1, bn), lambda j: (0, j))],
        out_specs=pl.BlockSpec((n, bn), lambda j: (0, j)),
        out_shape=jax.ShapeDtypeStruct((n, nout), F32),
        compiler_params=_params(("arbitrary",)),
        name="ada",
    )(c, w, b.reshape(1, nout))


def _modulated(x_ref, sc_ref, sh_ref, g_ref):
    h = _rms(x_ref[0], g_ref[...])
    return (h * (1.0 + sc_ref[0]) + sh_ref[0]).astype(BF16)


def _proj_a_kernel(x_ref, sc_ref, sh_ref, g_ref, w_ref, lnw_ref, lnb_ref,
                   q_ref, k_ref, v_ref, kb_ref, vb_ref, iq_ref, ik_ref, ikb_ref, iw_ref):
    hb = _modulated(x_ref, sc_ref, sh_ref, g_ref)
    for h in range(8):
        zq = _mm(hb, w_ref[:, h * LANES:(h + 1) * LANES])
        q_ref[0, h] = (zq * (HEAD_DIM ** -0.5 * LOG2E)).astype(BF16)
    kv = _mm(hb, w_ref[:, 1024:1280])
    k = kv[:, :128]
    v = kv[:, 128:]
    k_ref[0] = k
    v_ref[0] = v
    kb_ref[0] = k.astype(BF16)
    vb_ref[0] = v.astype(BF16)
    for h in range(8):
        zi = _mm(hb, w_ref[:, 1280 + h * LANES:1280 + (h + 1) * LANES])
        iq_ref[0, h] = zi.astype(BF16)
    t = _mm(hb, w_ref[:, 2304:2560])
    ik = t[:, :128]
    lane = lax.broadcasted_iota(jnp.int32, ik.shape, 1)
    valid = lane < IDX_DIM
    mu = jnp.sum(ik, axis=-1, keepdims=True) * (1.0 / IDX_DIM)
    d = jnp.where(valid, ik - mu, 0.0)
    var = jnp.sum(d * d, axis=-1, keepdims=True) * (1.0 / IDX_DIM)
    ikn = d * lax.rsqrt(var + LN_EPS) * lnw_ref[...] + lnb_ref[...]
    ik_ref[0] = ikn[:, :IDX_DIM]
    ikb_ref[0] = ikn.astype(BF16)
    iw_ref[0] = t[:, 128:136] * IDX_SCALE


def _proj_a(x, sc, sh, g, w, lnw, lnb, tm):
    B, T, D = x.shape
    nt = T // tm
    tok = lambda n, dt: jax.ShapeDtypeStruct((B, T, n), dt)
    hm = lambda dt: jax.ShapeDtypeStruct((B, 8, T, LANES), dt)
    tspec = lambda n: pl.BlockSpec((1, tm, n), lambda b, i: (b, i, 0))
    hspec = pl.BlockSpec((1, 8, tm, LANES), lambda b, i: (b, 0, i, 0))
    full = lambda a: pl.BlockSpec(a.shape, lambda b, i: (0,) * a.ndim)
    return pl.pallas_call(
        _proj_a_kernel,
        grid=(B, nt),
        in_specs=[tspec(D), _mod_spec(sc, tm), _mod_spec(sh, tm), full(g), full(w), full(lnw), full(lnb)],
        out_specs=[hspec, tspec(128), tspec(128), tspec(128), tspec(128), hspec, tspec(IDX_DIM), tspec(128),
                   tspec(8)],
        out_shape=[hm(BF16), tok(128, F32), tok(128, F32), tok(128, BF16), tok(128, BF16), hm(BF16),
                   tok(IDX_DIM, F32), tok(128, BF16), tok(8, F32)],
        compiler_params=_params(("arbitrary", "arbitrary")),
        name="proj_att",
    )(x, sc, sh, g, w, lnw, lnb)


def _proj_b_kernel(x_ref, sc_ref, sh_ref, g_ref, w_ref, zr_ref, ga_ref, gb_ref):
    hb = _modulated(x_ref, sc_ref, sh_ref, g_ref)
    for j in range(ZRW // 256):
        zr_ref[0, :, j * 256:(j + 1) * 256] = _mm(hb, w_ref[:, j * 256:(j + 1) * 256])
    for j in range(4):
        za = _mm(hb, w_ref[:, ZRW + j * 256:ZRW + (j + 1) * 256])
        ga_ref[0, :, j * 256:(j + 1) * 256] = 1.0 / (1.0 + jnp.exp(-za))
        zb = _mm(hb, w_ref[:, ZRW + 1024 + j * 256:ZRW + 1024 + (j + 1) * 256])
        gb_ref[0, :, j * 256:(j + 1) * 256] = 1.0 / (1.0 + jnp.exp(-zb))


def _proj_b(x, sc, sh, g, w, tm):
    B, T, D = x.shape
    nt = T // tm
    tok = lambda n: jax.ShapeDtypeStruct((B, T, n), F32)
    tspec = lambda n: pl.BlockSpec((1, tm, n), lambda b, i: (b, i, 0))
    full = lambda a: pl.BlockSpec(a.shape, lambda b, i: (0,) * a.ndim)
    return pl.pallas_call(
        _proj_b_kernel,
        grid=(B, nt),
        in_specs=[tspec(D), _mod_spec(sc, tm), _mod_spec(sh, tm), full(g), full(w)],
        out_specs=[tspec(ZRW), tspec(1024), tspec(1024)],
        out_shape=[tok(ZRW), tok(1024), tok(1024)],
        compiler_params=_params(("arbitrary", "arbitrary")),
        name="proj_rwkv",
    )(x, sc, sh, g, w)


def _sortable_to_float(t):
    bits = t ^ (lax.shift_right_arithmetic(t, 31) & 0x7FFFFFFF)
    return lax.bitcast_convert_type(bits, F32)


def _top16(x):
    bits = lax.bitcast_convert_type(x, jnp.int32) & jnp.int32(-65536)
    return lax.bitcast_convert_type(bits, F32).astype(BF16)


def _attn_kernel(iq_ref, iw_ref, q_ref, ik_ref, k_ref, v_ref, o_ref, s_ref, sb_ref, m_ref, acc_ref,
                 *, R, KB, SB, HG, ltot, q_off, topk):
    j = pl.program_id(1)
    q0 = q_off + j * R
    pos = q0 + lax.broadcasted_iota(jnp.int32, (R, 1), 0)
    qchunk = lax.shift_right_logical(pos, 6)
    kend = jnp.minimum(ltot, (lax.shift_right_logical(q0 + R - 1, 6) + 1) * CHUNK)
    nkb = lax.shift_right_logical(kend + KB - 1, KB.bit_length() - 1)
    lane_k = lax.broadcasted_iota(jnp.int32, (R, KB), 1)
    nch = KB // LANES

    iq_all = iq_ref[0].reshape(8 * R, LANES)
    iw = iw_ref[0]
    iwb = [jnp.broadcast_to(iw[:, h:h + 1], (R, SB)) for h in range(8)]
    lane_s = lax.broadcasted_iota(jnp.int32, (R, SB), 1)
    nsb = KB // SB

    def score_body(kb, carry):
        for u in range(nsb):
            off = pl.multiple_of(kb * KB + u * SB, SB)
            s_all = _mm(iq_all, ik_ref[0, kb * nsb + u])
            acc = jnp.zeros((R, SB), F32)
            for h in range(8):
                acc = acc + jnp.maximum(s_all[h * R:(h + 1) * R], 0.0) * iwb[h]
            kidx = off + lane_s
            adm = (lax.shift_right_logical(kidx, 6) <= qchunk) & (kidx < ltot)
            sc = jnp.where(adm, acc, -jnp.inf)
            s_ref[kb, :, u * SB:(u + 1) * SB] = sc
            sb_ref[kb, :, u * SB:(u + 1) * SB] = _top16(sc)
        return carry

    lax.fori_loop(0, nkb, score_body, 0)

    def count_ge(ref, cand, dt):
        one, zero = jnp.ones((), dt), jnp.zeros((), dt)

        def body(kb, part):
            blk = ref[kb]
            m = jnp.where(blk >= cand, one, zero)
            for c in range(nch):
                part = part + m[:, c * LANES:(c + 1) * LANES]
            return part
        part = lax.fori_loop(0, nkb, body, jnp.zeros((R, LANES), dt))
        return jnp.sum(part.astype(F32), axis=1, keepdims=True)

    kf = float(topk)
    c0 = count_ge(sb_ref, jnp.zeros((R, 1), BF16), BF16)
    t0 = jnp.where(c0 >= kf, 0, INT_MIN).astype(jnp.int32)
    ex0 = jnp.where(c0 == kf, 1.0, 0.0)

    def accept(c, cand, cnt):
        t, ex = c
        ok = cnt >= kf
        return jnp.where(ok, cand, t), jnp.where(ok & (cnt == kf), 1.0, ex)

    def hi_body(i, c):
        cand = c[0] + lax.shift_left(jnp.int32(1), 30 - i)
        return accept(c, cand, count_ge(sb_ref, _top16(_sortable_to_float(cand)), BF16))

    def lo_pass(i, c):
        cand = c[0] + lax.shift_left(jnp.int32(1), 15 - i)
        return accept(c, cand, count_ge(s_ref, _sortable_to_float(cand), F32))

    def lo_group(c):
        gi, t, ex, _ = c
        tc = (t, ex)
        for jj in range(LO_GROUP):
            tc = lo_pass(gi * LO_GROUP + jj, tc)
        return gi + 1, tc[0], tc[1], (jnp.min(tc[1]) > 0.5).astype(jnp.int32)

    t, ex = lax.fori_loop(0, 15, hi_body, (t0, ex0))
    _, t, _, _ = lax.while_loop(lambda c: (c[0] < 16 // LO_GROUP) & (c[3] == 0), lo_group,
                                (jnp.int32(0), t, ex, (jnp.min(ex) > 0.5).astype(jnp.int32)))
    all_finite = t <= KEY_NEG_INF
    tau = jnp.where(all_finite, -jnp.inf, _sortable_to_float(jnp.maximum(t, KEY_NEG_INF)))

    def count_gt_eq(_):
        def body(kb, carry):
            pg, pe = carry
            blk = s_ref[kb]
            mg = jnp.where(blk > tau, 1.0, 0.0)
            me = jnp.where(blk == tau, 1.0, 0.0)
            for c in range(nch):
                pg = pg + mg[:, c * LANES:(c + 1) * LANES]
                pe = pe + me[:, c * LANES:(c + 1) * LANES]
            return pg, pe
        z = jnp.zeros((R, LANES), F32)
        pg, pe = lax.fori_loop(0, nkb, body, (z, z))
        return jnp.sum(pg, axis=1, keepdims=True), jnp.sum(pe, axis=1, keepdims=True)

    cnt_gt, cnt_eq = count_gt_eq(0)
    need = kf - cnt_gt
    tie = (cnt_eq > need) & jnp.logical_not(all_finite)
    any_tie = jnp.max(jnp.where(tie, 1.0, 0.0)) > 0.0

    def tie_bound():
        def count_eq_below(x):
            def body(kb, part):
                blk = s_ref[kb]
                kidx = kb * KB + lane_k
                m = jnp.where((blk == tau) & (kidx < x), 1.0, 0.0)
                for c in range(nch):
                    part = part + m[:, c * LANES:(c + 1) * LANES]
                return part
            part = lax.fori_loop(0, nkb, body, jnp.zeros((R, LANES), F32))
            return jnp.sum(part, axis=1, keepdims=True)

        nbits = int(ltot).bit_length()

        def body(i, x):
            cand = x + lax.shift_left(jnp.int32(1), nbits - 1 - i)
            ok = count_eq_below(cand) < need
            return jnp.where(ok, cand, x)
        x = lax.fori_loop(0, nbits, body, jnp.zeros((R, 1), jnp.int32))
        return x + 1

    big = jnp.full((R, 1), 1 << 30, jnp.int32)
    bound = lax.cond(any_tie, lambda: jnp.where(tie, tie_bound(), big), lambda: big)

    def bias_body(kb, carry):
        blk = s_ref[kb]
        kidx = kb * KB + lane_k
        sel = (blk > tau) | ((blk == tau) & (kidx < bound))
        sel = sel & (blk > -jnp.inf)
        s_ref[kb] = jnp.where(sel, 0.0, NEG_BIG)
        return carry

    lax.fori_loop(0, nkb, bias_body, 0)

    q_all = q_ref[0].reshape(8 * R, LANES)
    m_ref[...] = jnp.full(m_ref.shape, NEG_BIG, F32)
    acc_ref[...] = jnp.zeros(acc_ref.shape, F32)

    def att_body(kb, carry):
        for u in range(nsb):
            off = pl.multiple_of(kb * KB + u * SB, SB)
            kblk = k_ref[0, kb * nsb + u]
            vblk = v_ref[0, pl.ds(off, SB), :]
            bias = s_ref[kb, :, u * SB:(u + 1) * SB]
            logits = _mm(q_all, kblk)
            nch = SB // LANES
            for h0 in range(0, 8, HG):
                hs = list(range(h0, h0 + HG))
                lg = [logits[h * R:(h + 1) * R] + bias for h in hs]
                cm = [functools.reduce(jnp.maximum, [l[:, c * LANES:(c + 1) * LANES] for c in range(nch)])
                      for l in lg]
                m_old = [m_ref[h] for h in hs]
                m_new = [jnp.maximum(mo, jnp.max(c, axis=1, keepdims=True)) for mo, c in zip(m_old, cm)]
                alpha = [jnp.exp2(mo - mn) for mo, mn in zip(m_old, m_new)]
                p = [jnp.exp2(l - jnp.concatenate([mn] * nch, axis=1)).astype(BF16) for l, mn in zip(lg, m_new)]
                pv = [_mm(pp, vblk) for pp in p]
                for i, h in enumerate(hs):
                    acc_ref[h] = jnp.concatenate([alpha[i], alpha[i]], axis=1) * acc_ref[h] + pv[i]
                    m_ref[h] = m_new[i]
        return carry

    lax.fori_loop(0, nkb, att_body, 0)
    for h in range(8):
        a = acc_ref[h]
        o_ref[0, h] = (a[:, :LANES] / a[:, LANES:]).astype(BF16)


def _attention(iq, iw, q, ikb, kb, vb1, *, R, KB, SB, HG, ltot, q_off):
    B, _, Sq, _ = q.shape
    Lp = kb.shape[1]
    topk = min(TOPK_MAX, ltot // 4)
    assert Lp % KB == 0 and KB % SB == 0 and KB >= topk and Sq % R == 0
    to_blocks = lambda a: jnp.swapaxes(a.reshape(B, Lp // SB, SB, LANES), 2, 3)
    ikb, kb = to_blocks(ikb), to_blocks(kb)
    hspec = pl.BlockSpec((1, 8, R, LANES), lambda b, i: (b, 0, i, 0))
    kspec = pl.BlockSpec((1, Lp // SB, LANES, SB), lambda b, i: (b, 0, 0, 0))
    vspec = pl.BlockSpec((1, Lp, 2 * LANES), lambda b, i: (b, 0, 0))
    kern = functools.partial(_attn_kernel, R=R, KB=KB, SB=SB, HG=HG, ltot=ltot, q_off=q_off, topk=topk)
    return pl.pallas_call(
        kern,
        grid=(B, Sq // R),
        in_specs=[hspec, pl.BlockSpec((1, R, 8), lambda b, i: (b, i, 0)), hspec, kspec, kspec, vspec],
        out_specs=hspec,
        out_shape=jax.ShapeDtypeStruct((B, 8, Sq, LANES), BF16),
        scratch_shapes=[pltpu.VMEM((Lp // KB, R, KB), F32), pltpu.VMEM((Lp // KB, R, KB), BF16),
                        pltpu.VMEM((8, R, LANES), F32),
                        pltpu.VMEM((8, R, 2 * LANES), F32)],
        compiler_params=_params(("arbitrary", "arbitrary")),
        name="dsa_attention",
    )(iq, iw, q, ikb, kb, vb1)


def _head_sum(x):
    parts = []
    for h in range(8):
        s = jnp.sum(x[:, h * LANES:(h + 1) * LANES], axis=1, keepdims=True)
        parts.append(jnp.broadcast_to(s, (x.shape[0], LANES)))
    return jnp.concatenate(parts, axis=1)


def _rwkv_prep_kernel(z_ref, sh0_ref, mu_ref, w0_ref, wup_ref, a0_ref, aup_ref, gup_ref, kk_ref, ka_ref, rk_ref,
                      r_o, lw_o, k_o, v_o, kkn_o, b_o, g_o, bg_o, lwt_o, kt_o, bt_o, carry_ref,
                      *, tm, t_valid):
    i = pl.program_id(1)

    @pl.when(i == 0)
    def _():
        carry_ref[...] = sh0_ref[0]

    z = z_ref[0]
    row = lax.broadcasted_iota(jnp.int32, (tm, 1), 0)
    prev = jnp.where(row == 0, carry_ref[...], pltpu.roll(z, 1, axis=0))
    carry_ref[...] = z[tm - 1:tm, :]
    zm = z + (prev - z) * mu_ref[...]
    live = i * tm + row < t_valid
    zm = jnp.where(live, zm, 0.0)
    r = zm[:, 0:HP]
    k = zm[:, HP:2 * HP]
    v = zm[:, 2 * HP:3 * HP]
    wa = zm[:, 3 * HP:3 * HP + LANES]
    gd = zm[:, 3 * HP + LANES:3 * HP + 2 * LANES]
    w_raw = w0_ref[...] + _mm(jnp.tanh(wa), wup_ref[...], HI)
    lw = (-float(np.exp(-0.5))) / (1.0 + jnp.exp(-w_raw))
    lw = jnp.where(live, lw, 0.0)
    a = 1.0 / (1.0 + jnp.exp(-(a0_ref[...] + _mm(wa, aup_ref[...], HI))))
    g = _mm(1.0 / (1.0 + jnp.exp(-gd)), gup_ref[...], HI)
    kk = k * kk_ref[...]
    kk = kk * lax.rsqrt(jnp.maximum(_head_sum(kk * kk), L2_EPS))
    k_mod = k * (1.0 + (a - 1.0) * ka_ref[...])
    b = kk * a
    bonus = _head_sum(r * k_mod * rk_ref[...]) * v
    r_o[0] = r.astype(BF16)
    lw_o[0] = lw
    k_o[0] = k_mod.astype(BF16)
    v_o[0] = v.astype(BF16)
    kkn_o[0] = kk.astype(BF16)
    b_o[0] = b.astype(BF16)
    g_o[0] = g
    bg_o[0] = bonus * g
    lwt_o[0] = lw.T
    kt_o[0] = k_mod.T.astype(BF16)
    bt_o[0] = b.T.astype(BF16)


def _rwkv_prep(zr, shift0, prm, tm, t_valid):
    B, T, _ = zr.shape
    tspec = pl.BlockSpec((1, tm, HP), lambda b, i: (b, i, 0))
    fspec = pl.BlockSpec((1, HP, tm), lambda b, i: (b, 0, i))
    full = lambda a: pl.BlockSpec(a.shape, lambda b, i: (0,) * a.ndim)
    tok = jax.ShapeDtypeStruct((B, T, HP), F32)
    feat = jax.ShapeDtypeStruct((B, HP, T), F32)
    tokb = jax.ShapeDtypeStruct((B, T, HP), BF16)
    featb = jax.ShapeDtypeStruct((B, HP, T), BF16)
    names =("mu", "w0", "w_up", "a0", "a_up", "g_up", "k_k", "k_a", "r_k")
    ws = [prm[n] for n in names]
    return pl.pallas_call(
        functools.partial(_rwkv_prep_kernel, tm=tm, t_valid=t_valid),
        grid=(B, T // tm),
        in_specs=[pl.BlockSpec((1, tm, ZRW), lambda b, i: (b, i, 0)),
                  pl.BlockSpec((1, 1, ZRW), lambda b, i: (b, 0, 0))] + [full(w) for w in ws],
        out_specs=[tspec] * 8 + [fspec] * 3,
        out_shape=[tokb, tok, tokb, tokb, tokb, tokb, tok, tok, feat, featb, featb],
        scratch_shapes=[pltpu.VMEM((1, ZRW), F32)],
        compiler_params=_params(("arbitrary", "arbitrary")),
        name="rwkv_prep",
    )(zr, shift0, *ws)


def _rwkv_chunk_kernel(r_ref, lw_ref, k_ref, v_ref, kk_ref, b_ref, lwt_ref, kt_ref, bt_ref, a1_ref, a2_ref,
                       *, C, hb, pg, pc, po):
    ri = lax.broadcasted_iota(jnp.int32, (C, C), 0)
    ci = lax.broadcasted_iota(jnp.int32, (C, C), 1)
    low_incl = jnp.where(ri >= ci, 1.0, 0.0).astype(BF16)
    up_incl = jnp.where(ri <= ci, 1.0, 0.0).astype(BF16)
    strict = ri > ci
    incl = ri >= ci
    eye64 = jnp.where(lax.broadcasted_iota(jnp.int32, (64, 64), 0) == lax.broadcasted_iota(jnp.int32, (64, 64), 1),
                      1.0, 0.0)
    sls = [slice(u * LANES, (u + 1) * LANES) for u in range(hb)]
    each = lambda f, *ls: [f(*a) for a in zip(*ls)]
    lw = [lw_ref[0, :, sl] for sl in sls]
    r = [r_ref[0, :, sl].astype(F32) for sl in sls]
    k = [k_ref[0, :, sl].astype(F32) for sl in sls]
    v = [v_ref[0, :, sl].astype(F32) for sl in sls]
    kk = [kk_ref[0, :, sl].astype(F32) for sl in sls]
    b = [b_ref[0, :, sl].astype(F32) for sl in sls]
    lwt = [lwt_ref[0, sl, :][:64] for sl in sls]
    kt = [kt_ref[0, sl, :][:64].astype(F32) for sl in sls]
    bt = [bt_ref[0, sl, :][:64].astype(F32) for sl in sls]
    cum = each(lambda x: sum(_mm(low_incl, p) for p in _split_bf16(x, 3)), lw)
    cumt = each(lambda x: sum(_mm(p, up_incl) for p in _split_bf16(x, 3)), lwt)
    e_neg = each(lambda c: jnp.exp(-c), cum)
    at = each(lambda kk_, c, l: -kk_ * jnp.exp(c - l), kk, cum, lw)
    rt = each(lambda r_, c: r_ * jnp.exp(c), r, cum)
    g = each(lambda a, r_, b_, k_, e: _mmp(jnp.concatenate([a, r_], axis=0),
                                           jnp.concatenate([b_ * e, k_ * e], axis=0), pg, nt=True),
             at, rt, b, k, e_neg)
    n = each(lambda g_: jnp.where(strict, g_[:C, :C], 0.0), g)
    aak = each(lambda g_: jnp.where(strict, g_[:C, C:], 0.0), g)
    lrb = each(lambda g_: jnp.where(incl, g_[C:, :C], 0.0), g)
    lrk = each(lambda g_: jnp.where(incl, g_[C:, C:], 0.0), g)
    blk8 = lax.shift_right_logical(ri, 3) == lax.shift_right_logical(ci, 3)
    eye = jnp.where(ri == ci, 1.0, 0.0)
    n0 = each(lambda n_: jnp.where(blk8, n_, 0.0), n)
    x = each(lambda n_: eye + n_, n0)
    p = each(lambda n_: _mmp(n_, n_, pc), n0)
    x = each(lambda x_, p_: x_ + _mmp(x_, p_, pc), x, p)
    p = each(lambda p_: _mmp(p_, p_, pc), p)
    x = each(lambda x_, p_: x_ + _mmp(x_, p_, pc), x, p)
    for lv in range(3, int(C).bit_length() - 1):
        off = (lax.shift_right_logical(ri, lv + 1) == lax.shift_right_logical(ci, lv + 1)) & (
            lax.shift_right_logical(ri, lv) != lax.shift_right_logical(ci, lv))
        t = each(lambda n_, x_: _mmp(jnp.where(off, n_, 0.0), x_, pc), n, x)
        x = each(lambda x_, t_: x_ + _mmp(x_, t_, pc), x, t)
    z = each(lambda x_, a, b_: _mmp(x_, jnp.concatenate([a, b_], axis=1), pc), x, aak, at)
    w2v = each(lambda z_, v_: _mmp(z_[:, :C], v_, po), z, v)
    cl = each(lambda c: c[:, C - 1:C], cumt)
    eb = each(lambda c_, ct: jnp.exp(c_ - ct), cl, cumt)
    lhs = each(lambda bt_, kt_, e, lb, lk: jnp.concatenate(
        [jnp.concatenate([bt_ * e, kt_ * e], axis=1), jnp.concatenate([lb, lk], axis=1)], axis=0),
        bt, kt, eb, lrb, lrk)
    left = each(lambda l_, z_: _mmp(l_[:, :C], z_[:, C:], po), lhs, z)
    right = each(lambda l_, w, v_: _mmp(l_, jnp.concatenate([w, v_], axis=0), po), lhs, w2v, v)
    for u in range(hb):
        diag = eye64 * jnp.exp(cl[u])
        a1_ref[0, u, 0] = jnp.concatenate([left[u][:64, :64] + diag, (left[u][64:] + rt[u])[:, :64]], axis=0)
        a2_ref[0, u, 0] = right[u]


def _rwkv_chunks(r, lw, k, v, kk, b, lwt, kt, bt, C, hb, passes=RW_PASSES):
    B, T, _ = r.shape
    nc = T // C
    tspec = pl.BlockSpec((1, C, hb * LANES), lambda bb, h, c: (bb, c, h))
    fspec = pl.BlockSpec((1, hb * LANES, C), lambda bb, h, c: (bb, h, c))
    pg, pc, po = passes
    return pl.pallas_call(
        functools.partial(_rwkv_chunk_kernel, C=C, hb=hb, pg=pg, pc=pc, po=po),
        grid=(B, 8 // hb, nc),
        in_specs=[tspec] * 6 + [fspec] * 3,
        out_specs=[pl.BlockSpec((1, hb, 1, 64 + C, 64), lambda bb, h, c: (bb, h, c, 0, 0)),
                   pl.BlockSpec((1, hb, 1, 64 + C, LANES), lambda bb, h, c: (bb, h, c, 0, 0))],
        out_shape=[jax.ShapeDtypeStruct((B, 8, nc, 64 + C, 64), F32),
                   jax.ShapeDtypeStruct((B, 8, nc, 64 + C, LANES), F32)],
        compiler_params=_params(("arbitrary", "arbitrary", "arbitrary")),
        name="rwkv_chunks",
    )(r, lw, k, v, kk, b, lwt, kt, bt)


def _rwkv_scan_kernel(a1_ref, a2_ref, s0_ref, y_ref, sT_ref, h_ref, *, C, cb):
    c = pl.program_id(1)

    @pl.when(c == 0)
    def _():
        h_ref[...] = s0_ref[0]

    for cc in range(cb):
        for h in range(8):
            res = _mmp(a1_ref[0, h, cc], h_ref[h], 3) + a2_ref[0, h, cc]
            h_ref[h] = res[:64]
            y_ref[0, cc * C:(cc + 1) * C, h * LANES:(h + 1) * LANES] = res[64:]

    @pl.when(c == pl.num_programs(1) - 1)
    def _():
        sT_ref[0] = h_ref[...]


def _rwkv_scan(a1, a2, s0t, C, cb):
    B, _, nc, _, _ = a1.shape
    return pl.pallas_call(
        functools.partial(_rwkv_scan_kernel, C=C, cb=cb),
        grid=(B, nc // cb),
        in_specs=[pl.BlockSpec((1, 8, cb, 64 + C, 64), lambda b, c: (b, 0, c, 0, 0)),
                  pl.BlockSpec((1, 8, cb, 64 + C, LANES), lambda b, c: (b, 0, c, 0, 0)),
                  pl.BlockSpec((1, 8, 64, LANES), lambda b, c: (b, 0, 0, 0))],
        out_specs=[pl.BlockSpec((1, cb * C, HP), lambda b, c: (b, c, 0)),
                   pl.BlockSpec((1, 8, 64, LANES), lambda b, c: (b, 0, 0, 0))],
        out_shape=[jax.ShapeDtypeStruct((B, nc * C, HP), F32),
                   jax.ShapeDtypeStruct((B, 8, 64, LANES), F32)],
        scratch_shapes=[pltpu.VMEM((8, 64, LANES), F32)],
        compiler_params=_params(("arbitrary", "arbitrary")),
        name="rwkv_scan",
    )(a1, a2, s0t)


def _mix_kernel(att_ref, y_ref, g_ref, bg_ref, sga_ref, sgb_ref, x_ref, gt1_ref, sc2_ref, sh2_ref,
                woa_ref, wor_ref, wout_ref, lnw_ref, lnb_ref, gpost_ref, gpre_ref, wr_ref, br_ref,
                x1_ref, h2_ref, gate_ref, eidx_ref, wgt_ref, rel_ref, cnt_out_ref, cnt_ref, *, tm):
    att = jnp.concatenate([att_ref[0, h] for h in range(8)], axis=1)
    lane = lax.broadcasted_iota(jnp.int32, (tm, LANES), 1)
    valid = lane < RWKV_HEAD
    parts = []
    for h in range(8):
        y = y_ref[0, :, h * LANES:(h + 1) * LANES]
        mu = jnp.sum(y, axis=1, keepdims=True) * (1.0 / RWKV_HEAD)
        d = jnp.where(valid, y - mu, 0.0)
        var = jnp.sum(d * d, axis=1, keepdims=True) * (1.0 / RWKV_HEAD)
        parts.append(d * lax.rsqrt(var + GN_EPS))
    yn = jnp.concatenate(parts, axis=1)
    rw = (yn * lnw_ref[...] + lnb_ref[...]) * g_ref[0] + bg_ref[0]
    mix = sga_ref[0] * _mm(att, woa_ref[...]) + sgb_ref[0] * _mm(rw.astype(BF16), wor_ref[...])
    o = _mm(mix.astype(BF16), wout_ref[...])
    x1 = x_ref[0] + gt1_ref[0] * _rms(o, gpost_ref[...])
    x1_ref[0] = x1
    h2 = _rms(x1, gpre_ref[...]) * (1.0 + sc2_ref[0]) + sh2_ref[0]
    h2_ref[0] = h2
    logits = _mmp(h2, wr_ref[...], 3) + br_ref[...]
    el = lax.broadcasted_iota(jnp.int32, logits.shape, 1).astype(F32)
    work = logits
    sel = jnp.zeros(logits.shape, jnp.bool_)
    vmax = None
    firsts = []
    for kk in range(TOP_K):
        mx = jnp.max(work, axis=1, keepdims=True)
        if kk == 0:
            vmax = mx
        first = jnp.min(jnp.where(work == mx, el, float(N_EXPERTS)), axis=1, keepdims=True)
        hit = el == first
        firsts.append(first)
        sel = sel | hit
        work = jnp.where(hit, -jnp.inf, work)
    e = jnp.where(sel, jnp.exp(logits - vmax), 0.0)
    gate = e / jnp.sum(e, axis=1, keepdims=True)
    gate_ref[0] = gate
    @pl.when((pl.program_id(0) == 0) & (pl.program_id(1) == 0))
    def _():
        cnt_ref[...] = jnp.zeros(cnt_ref.shape, F32)

    self = jnp.where(sel, 1.0, 0.0)
    ti = lax.broadcasted_iota(jnp.int32, (tm, tm), 0)
    tj = lax.broadcasted_iota(jnp.int32, (tm, tm), 1)
    before = _mm(jnp.where(tj < ti, 1.0, 0.0).astype(BF16), self.astype(BF16)) + cnt_ref[...]
    cnt_ref[...] = cnt_ref[...] + jnp.sum(self, axis=0, keepdims=True)
    cnt_out_ref[...] = cnt_ref[...]
    ln = lax.broadcasted_iota(jnp.int32, (tm, LANES), 1)
    eo = jnp.zeros((tm, LANES), F32)
    go = jnp.zeros((tm, LANES), F32)
    ro = jnp.zeros((tm, LANES), F32)
    for kk in range(TOP_K):
        hit = el == firsts[kk]
        gk = jnp.sum(jnp.where(hit, gate, 0.0), axis=1, keepdims=True)
        rk = jnp.sum(jnp.where(hit, before, 0.0), axis=1, keepdims=True)
        eo = jnp.where(ln == kk, firsts[kk], eo)
        go = jnp.where(ln == kk, gk, go)
        ro = jnp.where(ln == kk, rk, ro)
    eidx_ref[0] = eo.astype(jnp.int32)
    wgt_ref[0] = go
    rel_ref[0] = ro.astype(jnp.int32)


def _mix(att, y, g, bg, sga, sgb, x, gt1, sc2, sh2, wts, tm):
    B, T, D = x.shape
    tspec = lambda n: pl.BlockSpec((1, tm, n), lambda b, i: (b, i, 0))
    full = lambda a: pl.BlockSpec(a.shape, lambda b, i: (0,) * a.ndim)
    tok = lambda n, dt: jax.ShapeDtypeStruct((B, T, n), dt)
    return pl.pallas_call(
        functools.partial(_mix_kernel, tm=tm),
        grid=(B, T // tm),
        in_specs=[pl.BlockSpec((1, 8, tm, LANES), lambda b, i: (b, 0, i, 0)), tspec(HP), tspec(HP), tspec(HP),
                  tspec(D), tspec(D), tspec(D), _mod_spec(gt1, tm), _mod_spec(sc2, tm), _mod_spec(sh2, tm)]
                 + [full(w) for w in wts],
        out_specs=[tspec(D), tspec(D), tspec(N_EXPERTS), tspec(LANES), tspec(LANES), tspec(LANES),
                   pl.BlockSpec((1, N_EXPERTS), lambda b, i: (0, 0))],
        out_shape=[tok(D, F32), tok(D, F32), tok(N_EXPERTS, F32), tok(LANES, jnp.int32), tok(LANES, F32),
                   tok(LANES, jnp.int32), jax.ShapeDtypeStruct((1, N_EXPERTS), F32)],
        scratch_shapes=[pltpu.VMEM((1, N_EXPERTS), F32)],
        compiler_params=_params(("arbitrary", "arbitrary")),
        name="mix_router",
    )(att, y, g, bg, sga, sgb, x, gt1, sc2, sh2, *wts)


def _expert(xb, wgu, bgu, wd, bd, d_ff):
    wgu = wgu.astype(BF16)
    wd = wd.astype(BF16)
    hgu = _mm(xb, wgu) + bgu
    hg = jnp.minimum(hgu[:, :d_ff], SWIGLU_LIMIT)
    hl = jnp.clip(hgu[:, d_ff:], -SWIGLU_LIMIT, SWIGLU_LIMIT)
    act = hg * (1.0 / (1.0 + jnp.exp(-SWIGLU_ALPHA * hg))) * (hl + 1.0)
    return _mm(act.astype(BF16), wd) + bd


def _sc_mesh():
    return plsc.VectorSubcoreMesh(core_axis_name="core", subcore_axis_name="subcore")


def _sc_dispatch(h, pos_slots, n_rows, q):
    N, D = h.shape
    mesh = _sc_mesh()
    steps = N // (mesh.num_cores * SC_WINDOW)
    assert steps * mesh.num_cores * SC_WINDOW == N

    @pl.kernel(out_type=jax.ShapeDtypeStruct((n_rows, SC_PIECE), h.dtype), mesh=mesh, scratch_types=[])
    def scatter(x_hbm, *refs):
        idx_hbm, o_hbm = refs[:-1], refs[-1]
        base = lax.axis_index("core") * steps

        def body(x_vmem, *i_vmem):
            for iv in i_vmem:
                pltpu.sync_copy(x_vmem, o_hbm.at[iv.at[0]])

        pltpu.emit_pipeline(
            body,
            grid=(steps,),
            in_specs=[pl.BlockSpec((SC_WINDOW, SC_PIECE), index_map=lambda i: (base + i, q))]
                     + [pl.BlockSpec((1, SC_WINDOW), index_map=lambda i: (0, base + i))] * len(idx_hbm),
            out_specs=[],
            core_axis_name="subcore",
            dimension_semantics=(pltpu.PARALLEL,),
        )(x_hbm, *idx_hbm)

    return scatter(h, *pos_slots)


def _sc_gather(table, idx):
    M = idx.shape[1]
    mesh = _sc_mesh()
    steps = M // (mesh.num_cores * SC_WINDOW)
    assert steps * mesh.num_cores * SC_WINDOW == M

    @pl.kernel(out_type=jax.ShapeDtypeStruct((M, SC_PIECE), table.dtype), mesh=mesh)
    def gather(x_hbm, i_hbm, o_hbm):
        base = lax.axis_index("core") * steps

        def body(i_vmem, o_vmem):
            pltpu.sync_copy(x_hbm.at[i_vmem.at[0]], o_vmem)

        pltpu.emit_pipeline(
            body,
            grid=(steps,),
            in_specs=[pl.BlockSpec((1, SC_WINDOW), index_map=lambda i: (0, base + i))],
            out_specs=[pl.BlockSpec((SC_WINDOW, SC_PIECE), index_map=lambda i: (base + i, 0))],
            core_axis_name="subcore",
            dimension_semantics=(pltpu.PARALLEL,),
        )(i_hbm, o_hbm)

    return gather(table, idx)


def _moe_grouped_kernel(te_ref, nt_ref, *refs, d_ff, npiece):
    x_refs, (wgu_ref, bgu_ref, wd_ref, bd_ref) = refs[:npiece], refs[npiece:npiece + 4]
    o_refs, (wgu_b, wd_b) = refs[npiece + 4:2 * npiece + 4], refs[2 * npiece + 4:]
    g = pl.program_id(0)

    @pl.when((g == 0) | (te_ref[g] != te_ref[jnp.maximum(g - 1, 0)]))
    def _():
        wgu_b[...] = wgu_ref[0].astype(BF16)
        wd_b[...] = wd_ref[0].astype(BF16)

    @pl.when(g < nt_ref[0])
    def _():
        xb = jnp.concatenate([r[...] for r in x_refs], axis=1).astype(BF16)
        y = _expert(xb, wgu_b[...], bgu_ref[0], wd_b[...], bd_ref[0], d_ff)
        for q, o_ref in enumerate(o_refs):
            o_ref[...] = y[:, q * SC_PIECE:(q + 1) * SC_PIECE]


def _moe_grouped(xs, te, nt, wgu, bgu, wd, bd, TG):
    NP = xs[0].shape[0]
    E, D, F2 = wgu.shape
    npiece = len(xs)
    wmap = lambda g, te_, nt_: (te_[g], 0, 0)
    xmap = lambda g, te_, nt_: (jnp.minimum(g, nt_[0] - 1), 0)
    pspec = pl.BlockSpec((TG, SC_PIECE), xmap)
    return pl.pallas_call(
        functools.partial(_moe_grouped_kernel, d_ff=F2 // 2, npiece=npiece),
        grid_spec=pltpu.PrefetchScalarGridSpec(
            num_scalar_prefetch=2,
            grid=(NP // TG,),
            in_specs=[pspec] * npiece + [pl.BlockSpec((1, D, F2), wmap), pl.BlockSpec((1, 1, F2), wmap),
                                         pl.BlockSpec((1, F2 // 2, D), wmap), pl.BlockSpec((1, 1, D), wmap)],
            out_specs=[pspec] * npiece,
            scratch_shapes=[pltpu.VMEM((D, F2), BF16), pltpu.VMEM((F2 // 2, D), BF16)]),
        out_shape=[jax.ShapeDtypeStruct((NP, SC_PIECE), F32)] * npiece,
        compiler_params=_params(("arbitrary",)),
        name="moe_grouped",
    )(te, nt, *xs, wgu, bgu, wd, bd)


def _moe_combine_kernel(*refs, npiece):
    y_refs = refs[:TOP_K * npiece]
    w_ref, x1_ref, gt2_ref, gpost_ref, o_ref = refs[TOP_K * npiece:]
    w = w_ref[...]
    acc = None
    for k in range(TOP_K):
        yk = jnp.concatenate([y_refs[k * npiece + q][...] for q in range(npiece)], axis=1)
        term = w[:, k:k + 1] * yk
        acc = term if acc is None else acc + term
    o_ref[...] = x1_ref[...] + gt2_ref[0] * _rms(acc, gpost_ref[...])


def _moe_combine(yslot, wgt, x1, gt2, gpost, tm):
    N, D = x1.shape
    nb = N // tm
    tpb = N // gt2.shape[0]
    npiece = len(yslot)
    yspec = lambda k: pl.BlockSpec((tm, SC_PIECE), lambda i, k=k: (i + k * nb, 0))
    return pl.pallas_call(
        functools.partial(_moe_combine_kernel, npiece=npiece),
        grid=(nb,),
        in_specs=[yspec(k) for k in range(TOP_K) for _ in range(npiece)]
                 + [pl.BlockSpec((tm, LANES), lambda i: (i, 0)), pl.BlockSpec((tm, D), lambda i: (i, 0)),
                    pl.BlockSpec((1, 1, D), lambda i: ((i * tm) // tpb, 0, 0)), pl.BlockSpec((1, D), lambda i: (0, 0))],
        out_specs=pl.BlockSpec((tm, D), lambda i: (i, 0)),
        out_shape=jax.ShapeDtypeStruct((N, D), F32),
        compiler_params=_params(("arbitrary",)),
        name="moe_combine",
    )(*[yslot[q] for _ in range(TOP_K) for q in range(npiece)], wgt, x1, gt2, gpost)


def _moe_kernel(h_ref, gate_ref, x1_ref, gt2_ref, gpost_ref, wgu_ref, bgu_ref, wd_ref, bd_ref, o_ref, acc_ref,
                *, d_ff):
    e = pl.program_id(1)

    @pl.when(e == 0)
    def _():
        acc_ref[...] = jnp.zeros(acc_ref.shape, F32)

    contrib = _expert(h_ref[...].astype(BF16), wgu_ref[0], bgu_ref[0], wd_ref[0], bd_ref[0], d_ff)
    gate = gate_ref[...]
    el = lax.broadcasted_iota(jnp.int32, gate.shape, 1)
    ge = jnp.sum(jnp.where(el == e, gate, 0.0), axis=1, keepdims=True)
    acc_ref[...] += ge * contrib

    @pl.when(e == pl.num_programs(1) - 1)
    def _():
        o_ref[...] = x1_ref[...] + gt2_ref[0] * _rms(acc_ref[...], gpost_ref[...])


def _moe(h2, gate, x1, gt2, gpost, wgu, bgu, wd, bd, tm):
    N, D = h2.shape
    E, _, F2 = wgu.shape
    tspec = lambda n: pl.BlockSpec((tm, n), lambda i, e: (i, 0))
    if gt2.shape[1] == 1:
        tpb = N // gt2.shape[0]
        gspec = pl.BlockSpec((1, 1, D), lambda i, e: ((i * tm) // tpb, 0, 0))
    else:
        gspec = pl.BlockSpec((1, tm, D), lambda i, e: (i, 0, 0))
    return pl.pallas_call(
        functools.partial(_moe_kernel, d_ff=F2 // 2),
        grid=(N // tm, E),
        in_specs=[tspec(D), tspec(E), tspec(D), gspec, pl.BlockSpec((1, D), lambda i, e: (0, 0)),
                  pl.BlockSpec((1, D, F2), lambda i, e: (e, 0, 0)), pl.BlockSpec((1, 1, F2), lambda i, e: (e, 0, 0)),
                  pl.BlockSpec((1, F2 // 2, D), lambda i, e: (e, 0, 0)), pl.BlockSpec((1, 1, D), lambda i, e: (e, 0, 0))],
        out_specs=tspec(D),
        out_shape=jax.ShapeDtypeStruct((N, D), F32),
        scratch_shapes=[pltpu.VMEM((tm, D), F32)],
        compiler_params=_params(("arbitrary", "arbitrary")),
        name="moe",
    )(h2, gate, x1, gt2, gpost, wgu, bgu, wd, bd)


def _per_token(m, T):
    B, _, D = m.shape
    return jnp.broadcast_to(m, (B, T, D)).reshape(1, B * T, D)


def _layer(x, mods, past, s0, shift0, P, cfg):
    B, T, D = x.shape
    sh1, sc1, gt1, sh2, sc2, gt2 = mods
    tm = cfg["tm"]

    q, k, v, kb, vb, iq, ik, ikb, iw = _proj_a(x, sc1, sh1, P["g_pre_mix"], P["w_a"], P["ik_ln_w"], P["ik_ln_b"],
                                              cfg["tm_a"])
    zr, sga, sgb = _proj_b(x, sc1, sh1, P["g_pre_mix"], P["w_b"], tm)

    if past is not None:
        pk, pv, pik = past
        plen = pk.shape[1]
        kb_all = jnp.concatenate([pk.reshape(B, plen, 128).astype(BF16), kb], axis=1)
        vb_all = jnp.concatenate([pv.reshape(B, plen, 128).astype(BF16), vb], axis=1)
        ik_all = jnp.concatenate([jnp.pad(pik, ((0, 0), (0, 0), (0, 64))).astype(BF16), ikb], axis=1)
    else:
        plen = 0
        kb_all, vb_all, ik_all = kb, vb, ikb
    ltot = plen + T
    KB = cfg["KB"]
    lp = -(-ltot // KB) * KB
    if lp != ltot:
        padk = ((0, 0), (0, lp - ltot), (0, 0))
        kb_all, vb_all, ik_all = jnp.pad(kb_all, padk), jnp.pad(vb_all, padk), jnp.pad(ik_all, padk)
    vb1 = jnp.concatenate([vb_all, jnp.ones_like(vb_all)], axis=-1)
    att = _attention(iq, iw, q, ik_all, kb_all, vb1, R=cfg["R"], KB=KB, SB=cfg["SB"], HG=cfg["HG"], ltot=ltot,
                     q_off=plen)

    C = RW_CHUNK
    tp = -(-T // C) * C
    zr_p = zr if tp == T else jnp.pad(zr, ((0, 0), (0, tp - T), (0, 0)))
    shift_pad = _take_cols(shift0.reshape(B, RWKV_COLS), _COLS_ZR).reshape(B, 1, ZRW)
    r, lw, km, vv, kkn, bb, g, bg, lwt, kt, bt = _rwkv_prep(zr_p, shift_pad, P, min(cfg["tm_rw"], tp), T)
    a1, a2 = _rwkv_chunks(r, lw, km, vv, kkn, bb, lwt, kt, bt, C, cfg["hb"])
    s0t = jnp.pad(jnp.swapaxes(s0, 2, 3), ((0, 0), (0, 0), (0, 0), (0, 64)))
    nc = tp // C
    cb = min(cfg["cb"], nc)
    y, sT = _rwkv_scan(a1, a2, s0t, C, cb)
    s_new = jnp.swapaxes(sT[..., :64], 2, 3)
    zlast = zr[:, T - 1]
    inv = np.zeros((RWKV_COLS,), np.int32)
    inv[_COLS_ZR[0][_COLS_ZR[1]]] = np.nonzero(_COLS_ZR[1])[0]
    shift_new = jnp.take(zlast, jnp.asarray(inv), axis=1).reshape(B, 1, RWKV_COLS)
    if tp != T:
        y, g, bg = y[:, :T], g[:, :T], bg[:, :T]

    wts = [P[n] for n in ("w_o_att", "w_o_rwkv", "w_out", "ln_x_w", "ln_x_b", "g_post_mix", "g_pre_ffn",
                          "w_router", "b_router")]
    x1, h2, gate, eidx, wgt, rel, cnt = _mix(att, y, g, bg, sga, sgb, x, gt1, sc2, sh2, wts, cfg["tm_mix"])

    N = B * T
    tmm = min(cfg["tm_moe"], N)
    if cfg["routed"]:
        TG = cfg["TG"]
        ntm = N * TOP_K // TG + N_EXPERTS
        cnt_i = cnt[0].astype(jnp.int32)
        padded = (cnt_i + TG - 1) // TG * TG
        ends = jnp.cumsum(padded)
        off = ends - padded
        nt = ends[-1] // TG
        gi = jnp.minimum(jnp.arange(ntm, dtype=jnp.int32), nt - 1)
        te = jnp.sum((ends // TG)[None, :] <= gi[:, None], axis=1).astype(jnp.int32)
        e4 = eidx.reshape(N, LANES)[:, :TOP_K]
        pos = jnp.take(off, e4) + rel.reshape(N, LANES)[:, :TOP_K]
        pos = jnp.clip(pos, 0, ntm * TG - 1)
        pos_t = pos.T
        npiece = D // SC_PIECE
        h2f = h2.reshape(N, D)
        xs = [_sc_dispatch(h2f, [pos_t[k:k + 1] for k in range(TOP_K)], ntm * TG, q) for q in range(npiece)]
        ys = _moe_grouped(xs, te, nt.reshape(1), P["w_gu"], P["b_gu"], P["w_down"], P["b_down"], TG)
        idx_c = pos_t.reshape(1, TOP_K * N)
        yslot = [_sc_gather(ys[q], idx_c) for q in range(npiece)]
        out = _moe_combine(yslot, wgt.reshape(N, LANES), x1.reshape(N, D), gt2, P["g_post_ffn"], tm)
    else:
        if T % tmm == 0:
            gt2m = gt2
        else:
            gt2m = _per_token(gt2, T).reshape(N // tmm, tmm, D)
        out = _moe(h2.reshape(N, D), gate.reshape(N, N_EXPERTS), x1.reshape(N, D), gt2m, P["g_post_ffn"],
                   P["w_gu"], P["b_gu"], P["w_down"], P["b_down"], tmm)
    return (out.reshape(B, T, D), k.reshape(B, T, ATT_KV_HEADS, HEAD_DIM), v.reshape(B, T, ATT_KV_HEADS, HEAD_DIM),
            ik, s_new, shift_new)


def _prep_weights(l, w_in, ik_ln_w, ik_ln_b, mu_rwkv, w0, w_up, a0, a_up, g_up, k_k, k_a, r_k, ln_x_w, ln_x_b,
                  w_o_att, w_o_rwkv, w_out, w_router, b_router, w_gu, b_gu, w_down, b_down,
                  g_pre_mix, g_post_mix, g_pre_ffn, g_post_ffn):
    row = lambda a: a.reshape(1, -1)
    hp = lambda a: _take_cols(row(a), _HEAD_IDX)
    P = {}
    P["w_a"] = _take_cols(w_in[l], _COLS_A).astype(BF16)
    P["w_b"] = _take_cols(w_in[l], _COLS_B).astype(BF16)
    P["ik_ln_w"] = jnp.pad(row(ik_ln_w[l]), ((0, 0), (0, 64)))
    P["ik_ln_b"] = jnp.pad(row(ik_ln_b[l]), ((0, 0), (0, 64)))
    P["mu"] = _take_cols(row(mu_rwkv[l]), _COLS_ZR)
    P["w0"], P["a0"], P["k_k"], P["k_a"] = hp(w0[l]), hp(a0[l]), hp(k_k[l]), hp(k_a[l])
    P["r_k"] = hp(r_k[l].reshape(-1))
    P["ln_x_w"], P["ln_x_b"] = hp(ln_x_w[l]), hp(ln_x_b[l])
    z64 = jnp.zeros((64, HP), F32)
    P["w_up"] = jnp.concatenate([_take_cols(w_up[l], _HEAD_IDX), z64], axis=0)
    P["a_up"] = jnp.concatenate([z64, _take_cols(a_up[l], _HEAD_IDX)], axis=0)
    P["g_up"] = _take_cols(g_up[l], _HEAD_IDX)
    P["w_o_att"] = _take_rows(w_o_att[l], _ATT_ROW_IDX).astype(BF16)
    P["w_o_rwkv"] = _take_rows(w_o_rwkv[l], _HEAD_IDX).astype(BF16)
    P["w_out"] = w_out[l].astype(BF16)
    P["w_router"] = w_router[l]
    P["b_router"] = row(b_router[l])
    P["w_gu"] = w_gu[l]
    P["b_gu"] = b_gu[l].reshape(N_EXPERTS, 1, -1)
    P["w_down"] = w_down[l]
    P["b_down"] = b_down[l].reshape(N_EXPERTS, 1, -1)
    P["g_pre_mix"], P["g_post_mix"] = row(g_pre_mix[l]), row(g_post_mix[l])
    P["g_pre_ffn"], P["g_post_ffn"] = row(g_pre_ffn[l]), row(g_post_ffn[l])
    return P


CFG_PROMPT = dict(tm=256, tm_a=1024, tm_mix=512, R=256,KB=512, SB=256, HG=2,tm_rw=256, hb=8, cb=2, tm_moe=1024, routed=True, TG=512)
CFG_SAMPLE = dict(tm=32, tm_a=32, tm_mix=32, R=32, KB=512, SB=256, HG=2,tm_rw=128, hb=8, cb=1, tm_moe=512, routed=False)


def kernel(x_prompt, x_sample, c_prompt, c_sample, cache_k, cache_v, cache_idx_k, state_rwkv, state_shift, w_ada, b_ada, g_pre_mix, g_post_mix, g_pre_ffn, g_post_ffn, w_in, ik_ln_w, ik_ln_b, mu_rwkv, w0, w_up, a0, a_up, g_up, k_k, k_a, r_k, ln_x_w, ln_x_b, w_o_att, w_o_rwkv, w_out, w_router, b_router, w_gu, b_gu, w_down, b_down):
    depth = w_in.shape[0]
    bp, tp_, D = x_prompt.shape
    bs, ts, _ = x_sample.shape
    y_p, y_s = x_prompt, x_sample
    st_p = [[] for _ in range(5)]
    st_s = [[] for _ in range(5)]
    nc_all = bp + bs
    npad = -(-nc_all // 8) * 8
    c_all = jnp.pad(jnp.concatenate([c_prompt, c_sample], axis=0), ((0, npad - nc_all), (0, 0)))
    for l in range(depth):
        P = _prep_weights(l, w_in, ik_ln_w, ik_ln_b, mu_rwkv, w0, w_up, a0, a_up, g_up, k_k, k_a, r_k, ln_x_w,
                          ln_x_b, w_o_att, w_o_rwkv, w_out, w_router, b_router, w_gu, b_gu, w_down, b_down,
                          g_pre_mix, g_post_mix, g_pre_ffn, g_post_ffn)
        ada = _ada(c_all, w_ada[l], b_ada[l])
        mods_p = [m[:bp, None, :] for m in jnp.split(ada, 6, axis=-1)]
        mods_s = [m[bp:nc_all, None, :] for m in jnp.split(ada, 6, axis=-1)]
        zero_state = jnp.zeros((bp, RWKV_HEADS, RWKV_HEAD, RWKV_HEAD), F32)
        zero_shift = jnp.zeros((bp, 1, RWKV_COLS), F32)
        outs_p = _layer(y_p, mods_p, None, zero_state, zero_shift, P, CFG_PROMPT)
        outs_s = _layer(y_s, mods_s, (cache_k[l], cache_v[l], cache_idx_k[l]), state_rwkv[l], state_shift[l], P,
                        CFG_SAMPLE)
        y_p, y_s = outs_p[0], outs_s[0]
        for lst, val in zip(st_p, outs_p[1:]):
            lst.append(val)
        for lst, val in zip(st_s, outs_s[1:]):
            lst.append(val)
    sp = [jnp.stack(v, axis=0) for v in st_p]
    ss = [jnp.stack(v, axis=0) for v in st_s]
    return (y_p, y_s, sp[0], sp[1], sp[2], sp[3], sp[4], ss[0], ss[1], ss[2], ss[3], ss[4])
```

```python
import functools

import numpy as np
import jax
import jax.numpy as jnp
from jax import lax
from jax.experimental import pallas as pl
from jax.experimental.pallas import tpu as pltpu
from jax.experimental.pallas import tpu_sc as plsc

F32 = jnp.float32
BF16 = jnp.bfloat16
HI = lax.Precision.HIGHEST

CHUNK = 64
ATT_HEADS = 8
ATT_KV_HEADS = 2
HEAD_DIM = 64
IDX_HEADS = 8
IDX_DIM = 64
IDX_SCALE = (IDX_DIM ** -0.5) * (IDX_HEADS ** -0.5)
TOPK_MAX = 256
RWKV_HEADS = 8
RWKV_HEAD = 64
RWKV_WIDTH = RWKV_HEADS * RWKV_HEAD
DECAY_LORA = 64
AAA_LORA = 64
GATE_LORA = 128
RWKV_COLS = 3 * RWKV_WIDTH + DECAY_LORA + AAA_LORA + GATE_LORA
GN_EPS = 64e-5
L2_EPS = 1e-24
N_EXPERTS = 32
TOP_K = 4
SWIGLU_LIMIT = 7.0
SWIGLU_ALPHA = 1.702
NORM_EPS = 1e-6
LN_EPS = 1e-6

LANES = 128
VMEM_LIMIT = 56 * 1024 * 1024

HP = 8 * LANES
RW_CHUNK = 128
NEG_BIG = -1e30
LOG2E = 1.4426950408889634
KEY_NEG_INF = -2139095041
INT_MIN = -2147483648
SC_WINDOW = 128
SC_PIECE = 256
LO_GROUP = 4
RW_PASSES = (1, 1, 1)


def _params(sem):
    return pltpu.CompilerParams(dimension_semantics=sem, vmem_limit_bytes=VMEM_LIMIT)


def _nt(a, b, precision=None):
    return lax.dot_general(a, b, (((1,), (1,)), ((), ())), precision=precision,
                           preferred_element_type=F32)


def _mm(a, b, precision=None):
    return jnp.dot(a, b, precision=precision, preferred_element_type=F32)


def _split_bf16(x, terms):
    out = []
    for _ in range(terms):
        p = x.astype(BF16)
        out.append(p)
        x = x - p.astype(F32)
    return out


def _mmp(a, b, passes, nt=False):
    dot = _nt if nt else _mm
    if passes == 1:
        return dot(a.astype(BF16), b.astype(BF16))
    ah, al = _split_bf16(a, 2)
    bh, bl = _split_bf16(b, 2)
    return dot(ah, bh) + (dot(ah, bl) + dot(al, bh))


def _rms(x, g):
    return x * lax.rsqrt(jnp.mean(x * x, axis=-1, keepdims=True) + NORM_EPS) * g


def _headpad_idx(seg_off, lane_off_fn=lambda h: 0):
    idx = np.zeros((HP,), np.int32)
    ok = np.zeros((HP,), bool)
    for h in range(8):
        d0 = h * LANES + lane_off_fn(h)
        idx[d0:d0 + 64] = seg_off + h * 64 + np.arange(64)
        ok[d0:d0 + 64] = True
    return idx, ok


def _plain_idx(seg_off, n, width):
    idx = np.zeros((width,), np.int32)
    ok = np.zeros((width,), bool)
    idx[:n] = seg_off + np.arange(n)
    ok[:n] = True
    return idx, ok


def _cat(parts):
    return np.concatenate([p[0] for p in parts]), np.concatenate([p[1] for p in parts])


_O_Q, _O_K, _O_V, _O_IQ, _O_IK, _O_IW, _O_ZR, _O_GA, _O_GB = 0, 512, 640, 768, 1280, 1344, 1352, 3144, 4168
_COLS_A = _cat([_headpad_idx(_O_Q, lambda h: (h // 4) * 64), _plain_idx(_O_K, 128, 128), _plain_idx(_O_V, 128, 128),
                _headpad_idx(_O_IQ), _plain_idx(_O_IK, 64, 128), _plain_idx(_O_IW, 8, 128)])
_COLS_ZR = _cat([_headpad_idx(0), _headpad_idx(512), _headpad_idx(1024), _plain_idx(1536, 128, 128),
                 _plain_idx(1664, 128, 128)])
ZRW = _COLS_ZR[0].shape[0]
_COLS_B = _cat([(_COLS_ZR[0] + _O_ZR, _COLS_ZR[1]), _plain_idx(_O_GA, 1024, 1024), _plain_idx(_O_GB, 1024, 1024)])
NA = _COLS_A[0].shape[0]
NB = _COLS_B[0].shape[0]
_HEAD_IDX = _headpad_idx(0)
_ATT_ROW_IDX = _headpad_idx(0, lambda h: (h // 4) * 64)


def _take_cols(w, cols):
    idx, ok = cols
    return jnp.where(jnp.asarray(ok)[None, :], jnp.take(w, jnp.asarray(idx), axis=1), 0.0)


def _take_rows(w, cols):
    idx, ok = cols
    return jnp.where(jnp.asarray(ok)[:, None], jnp.take(w, jnp.asarray(idx), axis=0), 0.0)


def _mod_spec(m, tm):
    assert m.shape[1] == 1
    return pl.BlockSpec((1, 1, m.shape[2]), lambda b, i: (b, 0, 0))


def _ada_kernel(c_ref, w_ref, b_ref, o_ref):
    c = c_ref[...]
    s = c * (1.0 / (1.0 + jnp.exp(-c)))
    o_ref[...] = _mm(s, w_ref[...], HI) + b_ref[...]


def _ada(c, w, b):
    n, d = c.shape
    nout = w.shape[1]
    bn = 768
    return pl.pallas_call(
        _ada_kernel,
        grid=(nout // bn,),
        in_specs=[pl.BlockSpec((n, d), lambda j: (0, 0)),
                  pl.BlockSpec((d, bn), lambda j: (0, j)),
                  pl.BlockSpec((1, bn), lambda j: (0, j))],
        out_specs=pl.BlockSpec((n, bn), lambda j: (0, j)),
        out_shape=jax.ShapeDtypeStruct((n, nout), F32),
        compiler_params=_params(("arbitrary",)),
        name="ada",
    )(c, w, b.reshape(1, nout))


def _modulated(x_ref, sc_ref, sh_ref, g_ref):
    h = _rms(x_ref[0], g_ref[...])
    return (h * (1.0 + sc_ref[0]) + sh_ref[0]).astype(BF16)


def _proj_a_kernel(x_ref, sc_ref, sh_ref, g_ref, w_ref, lnw_ref, lnb_ref,
                   q_ref, k_ref, v_ref, kb_ref, vb_ref, iq_ref, ik_ref, ikb_ref, iw_ref):
    hb = _modulated(x_ref, sc_ref, sh_ref, g_ref)
    for h in range(8):
        zq = _mm(hb, w_ref[:, h * LANES:(h + 1) * LANES])
        q_ref[0, h] = (zq * (HEAD_DIM ** -0.5 * LOG2E)).astype(BF16)
    kv = _mm(hb, w_ref[:, 1024:1280])
    k = kv[:, :128]
    v = kv[:, 128:]
    k_ref[0] = k
    v_ref[0] = v
    kb_ref[0] = k.astype(BF16)
    vb_ref[0] = v.astype(BF16)
    for h in range(8):
        zi = _mm(hb, w_ref[:, 1280 + h * LANES:1280 + (h + 1) * LANES])
        iq_ref[0, h] = zi.astype(BF16)
    t = _mm(hb, w_ref[:, 2304:2560])
    ik = t[:, :128]
    lane = lax.broadcasted_iota(jnp.int32, ik.shape, 1)
    valid = lane < IDX_DIM
    mu = jnp.sum(ik, axis=-1, keepdims=True) * (1.0 / IDX_DIM)
    d = jnp.where(valid, ik - mu, 0.0)
    var = jnp.sum(d * d, axis=-1, keepdims=True) * (1.0 / IDX_DIM)
    ikn = d * lax.rsqrt(var + LN_EPS) * lnw_ref[...] + lnb_ref[...]
    ik_ref[0] = ikn[:, :IDX_DIM]
    ikb_ref[0] = ikn.astype(BF16)
    iw_ref[0] = t[:, 128:136] * IDX_SCALE


def _proj_a(x, sc, sh, g, w, lnw, lnb, tm):
    B, T, D = x.shape
    nt = T // tm
    tok = lambda n, dt: jax.ShapeDtypeStruct((B, T, n), dt)
    hm = lambda dt: jax.ShapeDtypeStruct((B, 8, T, LANES), dt)
    tspec = lambda n: pl.BlockSpec((1, tm, n), lambda b, i: (b, i, 0))
    hspec = pl.BlockSpec((1, 8, tm, LANES), lambda b, i: (b, 0, i, 0))
    full = lambda a: pl.BlockSpec(a.shape, lambda b, i: (0,) * a.ndim)
    return pl.pallas_call(
        _proj_a_kernel,
        grid=(B, nt),
        in_specs=[tspec(D), _mod_spec(sc, tm), _mod_spec(sh, tm), full(g), full(w), full(lnw), full(lnb)],
        out_specs=[hspec, tspec(128), tspec(128), tspec(128), tspec(128), hspec, tspec(IDX_DIM), tspec(128),
                   tspec(8)],
        out_shape=[hm(BF16), tok(128, F32), tok(128, F32), tok(128, BF16), tok(128, BF16), hm(BF16),
                   tok(IDX_DIM, F32), tok(128, BF16), tok(8, F32)],
        compiler_params=_params(("arbitrary", "arbitrary")),
        name="proj_att",
    )(x, sc, sh, g, w, lnw, lnb)


def _proj_b_kernel(x_ref, sc_ref, sh_ref, g_ref, w_ref, zr_ref, ga_ref, gb_ref):
    hb = _modulated(x_ref, sc_ref, sh_ref, g_ref)
    for j in range(ZRW // 256):
        zr_ref[0, :, j * 256:(j + 1) * 256] = _mm(hb, w_ref[:, j * 256:(j + 1) * 256])
    for j in range(4):
        za = _mm(hb, w_ref[:, ZRW + j * 256:ZRW + (j + 1) * 256])
        ga_ref[0, :, j * 256:(j + 1) * 256] = 1.0 / (1.0 + jnp.exp(-za))
        zb = _mm(hb, w_ref[:, ZRW + 1024 + j * 256:ZRW + 1024 + (j + 1) * 256])
        gb_ref[0, :, j * 256:(j + 1) * 256] = 1.0 / (1.0 + jnp.exp(-zb))


def _proj_b(x, sc, sh, g, w, tm):
    B, T, D = x.shape
    nt = T // tm
    tok = lambda n: jax.ShapeDtypeStruct((B, T, n), F32)
    tspec = lambda n: pl.BlockSpec((1, tm, n), lambda b, i: (b, i, 0))
    full = lambda a: pl.BlockSpec(a.shape, lambda b, i: (0,) * a.ndim)
    return pl.pallas_call(
        _proj_b_kernel,
        grid=(B, nt),
        in_specs=[tspec(D), _mod_spec(sc, tm), _mod_spec(sh, tm), full(g), full(w)],
        out_specs=[tspec(ZRW), tspec(1024), tspec(1024)],
        out_shape=[tok(ZRW), tok(1024), tok(1024)],
        compiler_params=_params(("arbitrary", "arbitrary")),
        name="proj_rwkv",
    )(x, sc, sh, g, w)


def _sortable_to_float(t):
    bits = t ^ (lax.shift_right_arithmetic(t, 31) & 0x7FFFFFFF)
    return lax.bitcast_convert_type(bits, F32)


def _top16(x):
    bits = lax.bitcast_convert_type(x, jnp.int32) & jnp.int32(-65536)
    return lax.bitcast_convert_type(bits, F32).astype(BF16)


def _attn_kernel(iq_ref, iw_ref, q_ref, ik_ref, k_ref, v_ref, o_ref, s_ref, sb_ref, m_ref, acc_ref,
                 *, R, KB, SB, HG, ltot, q_off, topk):
    j = pl.program_id(1)
    q0 = q_off + j * R
    pos = q0 + lax.broadcasted_iota(jnp.int32, (R, 1), 0)
    qchunk = lax.shift_right_logical(pos, 6)
    kend = jnp.minimum(ltot, (lax.shift_right_logical(q0 + R - 1, 6) + 1) * CHUNK)
    nkb = lax.shift_right_logical(kend + KB - 1, KB.bit_length() - 1)
    lane_k = lax.broadcasted_iota(jnp.int32, (R, KB), 1)
    nch = KB // LANES

    iq_all = iq_ref[0].reshape(8 * R, LANES)
    iw = iw_ref[0]
    iwb = [jnp.broadcast_to(iw[:, h:h + 1], (R, SB)) for h in range(8)]
    lane_s = lax.broadcasted_iota(jnp.int32, (R, SB), 1)
    nsb = KB // SB

    def score_body(kb, carry):
        for u in range(nsb):
            off = pl.multiple_of(kb * KB + u * SB, SB)
            s_all = _mm(iq_all, ik_ref[0, kb * nsb + u])
            acc = jnp.zeros((R, SB), F32)
            for h in range(8):
                acc = acc + jnp.maximum(s_all[h * R:(h + 1) * R], 0.0) * iwb[h]
            kidx = off + lane_s
            adm = (lax.shift_right_logical(kidx, 6) <= qchunk) & (kidx < ltot)
            sc = jnp.where(adm, acc, -jnp.inf)
            s_ref[kb, :, u * SB:(u + 1) * SB] = sc
            sb_ref[kb, :, u * SB:(u + 1) * SB] = _top16(sc)
        return carry

    lax.fori_loop(0, nkb, score_body, 0)

    def count_ge(ref, cand, dt):
        one, zero = jnp.ones((), dt), jnp.zeros((), dt)

        def body(kb, part):
            blk = ref[kb]
            m = jnp.where(blk >= cand, one, zero)
            for c in range(nch):
                part = part + m[:, c * LANES:(c + 1) * LANES]
            return part
        part = lax.fori_loop(0, nkb, body, jnp.zeros((R, LANES), dt))
        return jnp.sum(part.astype(F32), axis=1, keepdims=True)

    kf = float(topk)
    c0 = count_ge(sb_ref, jnp.zeros((R, 1), BF16), BF16)
    t0 = jnp.where(c0 >= kf, 0, INT_MIN).astype(jnp.int32)
    ex0 = jnp.where(c0 == kf, 1.0, 0.0)

    def accept(c, cand, cnt):
        t, ex = c
        ok = cnt >= kf
        return jnp.where(ok, cand, t), jnp.where(ok & (cnt == kf), 1.0, ex)

    def hi_body(i, c):
        cand = c[0] + lax.shift_left(jnp.int32(1), 30 - i)
        return accept(c, cand, count_ge(sb_ref, _top16(_sortable_to_float(cand)), BF16))

    def lo_pass(i, c):
        cand = c[0] + lax.shift_left(jnp.int32(1), 15 - i)
        return accept(c, cand, count_ge(s_ref, _sortable_to_float(cand), F32))

    def lo_group(c):
        gi, t, ex, _ = c
        tc = (t, ex)
        for jj in range(LO_GROUP):
            tc = lo_pass(gi * LO_GROUP + jj, tc)
        return gi + 1, tc[0], tc[1], (jnp.min(tc[1]) > 0.5).astype(jnp.int32)

    t, ex = lax.fori_loop(0, 15, hi_body, (t0, ex0))
    _, t, _, _ = lax.while_loop(lambda c: (c[0] < 16 // LO_GROUP) & (c[3] == 0), lo_group,
                                (jnp.int32(0), t, ex, (jnp.min(ex) > 0.5).astype(jnp.int32)))
    all_finite = t <= KEY_NEG_INF
    tau = jnp.where(all_finite, -jnp.inf, _sortable_to_float(jnp.maximum(t, KEY_NEG_INF)))

    def count_gt_eq(_):
        def body(kb, carry):
            pg, pe = carry
            blk = s_ref[kb]
            mg = jnp.where(blk > tau, 1.0, 0.0)
            me = jnp.where(blk == tau, 1.0, 0.0)
            for c in range(nch):
                pg = pg + mg[:, c * LANES:(c + 1) * LANES]
                pe = pe + me[:, c * LANES:(c + 1) * LANES]
            return pg, pe
        z = jnp.zeros((R, LANES), F32)
        pg, pe = lax.fori_loop(0, nkb, body, (z, z))
        return jnp.sum(pg, axis=1, keepdims=True), jnp.sum(pe, axis=1, keepdims=True)

    cnt_gt, cnt_eq = count_gt_eq(0)
    need = kf - cnt_gt
    tie = (cnt_eq > need) & jnp.logical_not(all_finite)
    any_tie = jnp.max(jnp.where(tie, 1.0, 0.0)) > 0.0

    def tie_bound():
        def count_eq_below(x):
            def body(kb, part):
                blk = s_ref[kb]
                kidx = kb * KB + lane_k
                m = jnp.where((blk == tau) & (kidx < x), 1.0, 0.0)
                for c in range(nch):
                    part = part + m[:, c * LANES:(c + 1) * LANES]
                return part
            part = lax.fori_loop(0, nkb, body, jnp.zeros((R, LANES), F32))
            return jnp.sum(part, axis=1, keepdims=True)

        nbits = int(ltot).bit_length()

        def body(i, x):
            cand = x + lax.shift_left(jnp.int32(1), nbits - 1 - i)
            ok = count_eq_below(cand) < need
            return jnp.where(ok, cand, x)
        x = lax.fori_loop(0, nbits, body, jnp.zeros((R, 1), jnp.int32))
        return x + 1

    big = jnp.full((R, 1), 1 << 30, jnp.int32)
    bound = lax.cond(any_tie, lambda: jnp.where(tie, tie_bound(), big), lambda: big)

    def bias_body(kb, carry):
        blk = s_ref[kb]
        kidx = kb * KB + lane_k
        sel = (blk > tau) | ((blk == tau) & (kidx < bound))
        sel = sel & (blk > -jnp.inf)
        s_ref[kb] = jnp.where(sel, 0.0, NEG_BIG)
        return carry

    lax.fori_loop(0, nkb, bias_body, 0)

    q_all = q_ref[0].reshape(8 * R, LANES)
    m_ref[...] = jnp.full(m_ref.shape, NEG_BIG, F32)
    acc_ref[...] = jnp.zeros(acc_ref.shape, F32)

    def att_body(kb, carry):
        for u in range(nsb):
            off = pl.multiple_of(kb * KB + u * SB, SB)
            kblk = k_ref[0, kb * nsb + u]
            vblk = v_ref[0, pl.ds(off, SB), :]
            bias = s_ref[kb, :, u * SB:(u + 1) * SB]
            logits = _mm(q_all, kblk)
            nch = SB // LANES
            for h0 in range(0, 8, HG):
                hs = list(range(h0, h0 + HG))
                lg = [logits[h * R:(h + 1) * R] + bias for h in hs]
                cm = [functools.reduce(jnp.maximum, [l[:, c * LANES:(c + 1) * LANES] for c in range(nch)])
                      for l in lg]
                m_old = [m_ref[h] for h in hs]
                m_new = [jnp.maximum(mo, jnp.max(c, axis=1, keepdims=True)) for mo, c in zip(m_old, cm)]
                alpha = [jnp.exp2(mo - mn) for mo, mn in zip(m_old, m_new)]
                p = [jnp.exp2(l - jnp.concatenate([mn] * nch, axis=1)).astype(BF16) for l, mn in zip(lg, m_new)]
                pv = [_mm(pp, vblk) for pp in p]
                for i, h in enumerate(hs):
                    acc_ref[h] = jnp.concatenate([alpha[i], alpha[i]], axis=1) * acc_ref[h] + pv[i]
                    m_ref[h] = m_new[i]
        return carry

    lax.fori_loop(0, nkb, att_body, 0)
    for h in range(8):
        a = acc_ref[h]
        o_ref[0, h] = (a[:, :LANES] / a[:, LANES:]).astype(BF16)


def _attention(iq, iw, q, ikb, kb, vb1, *, R, KB, SB, HG, ltot, q_off):
    B, _, Sq, _ = q.shape
    Lp = kb.shape[1]
    topk = min(TOPK_MAX, ltot // 4)
    assert Lp % KB == 0 and KB % SB == 0 and KB >= topk and Sq % R == 0
    to_blocks = lambda a: jnp.swapaxes(a.reshape(B, Lp // SB, SB, LANES), 2, 3)
    ikb, kb = to_blocks(ikb), to_blocks(kb)
    hspec = pl.BlockSpec((1, 8, R, LANES), lambda b, i: (b, 0, i, 0))
    kspec = pl.BlockSpec((1, Lp // SB, LANES, SB), lambda b, i: (b, 0, 0, 0))
    vspec = pl.BlockSpec((1, Lp, 2 * LANES), lambda b, i: (b, 0, 0))
    kern = functools.partial(_attn_kernel, R=R, KB=KB, SB=SB, HG=HG, ltot=ltot, q_off=q_off, topk=topk)
    return pl.pallas_call(
        kern,
        grid=(B, Sq // R),
        in_specs=[hspec, pl.BlockSpec((1, R, 8), lambda b, i: (b, i, 0)), hspec, kspec, kspec, vspec],
        out_specs=hspec,
        out_shape=jax.ShapeDtypeStruct((B, 8, Sq, LANES), BF16),
        scratch_shapes=[pltpu.VMEM((Lp // KB, R, KB), F32), pltpu.VMEM((Lp // KB, R, KB), BF16),
                        pltpu.VMEM((8, R, LANES), F32),
                        pltpu.VMEM((8, R, 2 * LANES), F32)],
        compiler_params=_params(("arbitrary", "arbitrary")),
        name="dsa_attention",
    )(iq, iw, q, ikb, kb, vb1)


def _head_sum(x):
    parts = []
    for h in range(8):
        s = jnp.sum(x[:, h * LANES:(h + 1) * LANES], axis=1, keepdims=True)
        parts.append(jnp.broadcast_to(s, (x.shape[0], LANES)))
    return jnp.concatenate(parts, axis=1)


def _rwkv_prep_kernel(z_ref, sh0_ref, mu_ref, w0_ref, wup_ref, a0_ref, aup_ref, gup_ref, kk_ref, ka_ref, rk_ref,
                      r_o, lw_o, k_o, v_o, kkn_o, b_o, g_o, bg_o, lwt_o, kt_o, bt_o, carry_ref,
                      *, tm, t_valid):
    i = pl.program_id(1)

    @pl.when(i == 0)
    def _():
        carry_ref[...] = sh0_ref[0]

    z = z_ref[0]
    row = lax.broadcasted_iota(jnp.int32, (tm, 1), 0)
    prev = jnp.where(row == 0, carry_ref[...], pltpu.roll(z, 1, axis=0))
    carry_ref[...] = z[tm - 1:tm, :]
    zm = z + (prev - z) * mu_ref[...]
    live = i * tm + row < t_valid
    zm = jnp.where(live, zm, 0.0)
    r = zm[:, 0:HP]
    k = zm[:, HP:2 * HP]
    v = zm[:, 2 * HP:3 * HP]
    wa = zm[:, 3 * HP:3 * HP + LANES]
    gd = zm[:, 3 * HP + LANES:3 * HP + 2 * LANES]
    w_raw = w0_ref[...] + _mmp(jnp.tanh(wa), wup_ref[...], 3)
    lw = (-float(np.exp(-0.5))) / (1.0 + jnp.exp(-w_raw))
    lw = jnp.where(live, lw, 0.0)
    a = 1.0 / (1.0 + jnp.exp(-(a0_ref[...] + _mmp(wa, aup_ref[...], 3))))
    g = _mmp(1.0 / (1.0 + jnp.exp(-gd)), gup_ref[...], 3)
    kk = k * kk_ref[...]
    kk = kk * lax.rsqrt(jnp.maximum(_head_sum(kk * kk), L2_EPS))
    k_mod = k * (1.0 + (a - 1.0) * ka_ref[...])
    b = kk * a
    bonus = _head_sum(r * k_mod * rk_ref[...]) * v
    r_o[0] = r.astype(BF16)
    lw_o[0] = lw
    k_o[0] = k_mod.astype(BF16)
    v_o[0] = v.astype(BF16)
    kkn_o[0] = kk.astype(BF16)
    b_o[0] = b.astype(BF16)
    g_o[0] = g
    bg_o[0] = bonus * g
    lwt_o[0] = lw.T
    kt_o[0] = k_mod.T.astype(BF16)
    bt_o[0] = b.T.astype(BF16)


def _rwkv_prep(zr, shift0, prm, tm, t_valid):
    B, T, _ = zr.shape
    tspec = pl.BlockSpec((1, tm, HP), lambda b, i: (b, i, 0))
    fspec = pl.BlockSpec((1, HP, tm), lambda b, i: (b, 0, i))
    full = lambda a: pl.BlockSpec(a.shape, lambda b, i: (0,) * a.ndim)
    tok = jax.ShapeDtypeStruct((B, T, HP), F32)
    feat = jax.ShapeDtypeStruct((B, HP, T), F32)
    tokb = jax.ShapeDtypeStruct((B, T, HP), BF16)
    featb = jax.ShapeDtypeStruct((B, HP, T), BF16)
    names =("mu", "w0", "w_up", "a0", "a_up", "g_up", "k_k", "k_a", "r_k")
    ws = [prm[n] for n in names]
    return pl.pallas_call(
        functools.partial(_rwkv_prep_kernel, tm=tm, t_valid=t_valid),
        grid=(B, T // tm),
        in_specs=[pl.BlockSpec((1, tm, ZRW), lambda b, i: (b, i, 0)),
                  pl.BlockSpec((1, 1, ZRW), lambda b, i: (b, 0, 0))] + [full(w) for w in ws],
        out_specs=[tspec] * 8 + [fspec] * 3,
        out_shape=[tokb, tok, tokb, tokb, tokb, tokb, tok, tok, feat, featb, featb],
        scratch_shapes=[pltpu.VMEM((1, ZRW), F32)],
        compiler_params=_params(("arbitrary", "arbitrary")),
        name="rwkv_prep",
    )(zr, shift0, *ws)


def _rwkv_chunk_kernel(r_ref, lw_ref, k_ref, v_ref, kk_ref, b_ref, lwt_ref, kt_ref, bt_ref, a1_ref, a2_ref,
                       *, C, hb, pg, pc, po):
    ri = lax.broadcasted_iota(jnp.int32, (C, C), 0)
    ci = lax.broadcasted_iota(jnp.int32, (C, C), 1)
    low_incl = jnp.where(ri >= ci, 1.0, 0.0).astype(BF16)
    up_incl = jnp.where(ri <= ci, 1.0, 0.0).astype(BF16)
    strict = ri > ci
    incl = ri >= ci
    eye64 = jnp.where(lax.broadcasted_iota(jnp.int32, (64, 64), 0) == lax.broadcasted_iota(jnp.int32, (64, 64), 1),
                      1.0, 0.0)
    sls = [slice(u * LANES, (u + 1) * LANES) for u in range(hb)]
    each = lambda f, *ls: [f(*a) for a in zip(*ls)]
    lw = [lw_ref[0, :, sl] for sl in sls]
    r = [r_ref[0, :, sl].astype(F32) for sl in sls]
    k = [k_ref[0, :, sl].astype(F32) for sl in sls]
    v = [v_ref[0, :, sl].astype(F32) for sl in sls]
    kk = [kk_ref[0, :, sl].astype(F32) for sl in sls]
    b = [b_ref[0, :, sl].astype(F32) for sl in sls]
    lwt = [lwt_ref[0, sl, :][:64] for sl in sls]
    kt = [kt_ref[0, sl, :][:64].astype(F32) for sl in sls]
    bt = [bt_ref[0, sl, :][:64].astype(F32) for sl in sls]
    cum = each(lambda x: sum(_mm(low_incl, p) for p in _split_bf16(x, 3)), lw)
    cumt = each(lambda x: sum(_mm(p, up_incl) for p in _split_bf16(x, 3)), lwt)
    e_neg = each(lambda c: jnp.exp(-c), cum)
    at = each(lambda kk_, c, l: -kk_ * jnp.exp(c - l), kk, cum, lw)
    rt = each(lambda r_, c: r_ * jnp.exp(c), r, cum)
    g = each(lambda a, r_, b_, k_, e: _mmp(jnp.concatenate([a, r_], axis=0),
                                           jnp.concatenate([b_ * e, k_ * e], axis=0), pg, nt=True),
             at, rt, b, k, e_neg)
    n = each(lambda g_: jnp.where(strict, g_[:C, :C], 0.0), g)
    aak = each(lambda g_: jnp.where(strict, g_[:C, C:], 0.0), g)
    lrb = each(lambda g_: jnp.where(incl, g_[C:, :C], 0.0), g)
    lrk = each(lambda g_: jnp.where(incl, g_[C:, C:], 0.0), g)
    blk8 = lax.shift_right_logical(ri, 3) == lax.shift_right_logical(ci, 3)
    eye = jnp.where(ri == ci, 1.0, 0.0)
    n0 = each(lambda n_: jnp.where(blk8, n_, 0.0), n)
    x = each(lambda n_: eye + n_, n0)
    p = each(lambda n_: _mmp(n_, n_, pc), n0)
    x = each(lambda x_, p_: x_ + _mmp(x_, p_, pc), x, p)
    p = each(lambda p_: _mmp(p_, p_, pc), p)
    x = each(lambda x_, p_: x_ + _mmp(x_, p_, pc), x, p)
    for lv in range(3, int(C).bit_length() - 1):
        off = (lax.shift_right_logical(ri, lv + 1) == lax.shift_right_logical(ci, lv + 1)) & (
            lax.shift_right_logical(ri, lv) != lax.shift_right_logical(ci, lv))
        t = each(lambda n_, x_: _mmp(jnp.where(off, n_, 0.0), x_, pc), n, x)
        x = each(lambda x_, t_: x_ + _mmp(x_, t_, pc), x, t)
    z = each(lambda x_, a, b_: _mmp(x_, jnp.concatenate([a, b_], axis=1), pc), x, aak, at)
    w2v = each(lambda z_, v_: _mmp(z_[:, :C], v_, po), z, v)
    cl = each(lambda c: c[:, C - 1:C], cumt)
    eb = each(lambda c_, ct: jnp.exp(c_ - ct), cl, cumt)
    lhs = each(lambda bt_, kt_, e, lb, lk: jnp.concatenate(
        [jnp.concatenate([bt_ * e, kt_ * e], axis=1), jnp.concatenate([lb, lk], axis=1)], axis=0),
        bt, kt, eb, lrb, lrk)
    left = each(lambda l_, z_: _mmp(l_[:, :C], z_[:, C:], po), lhs, z)
    right = each(lambda l_, w, v_: _mmp(l_, jnp.concatenate([w, v_], axis=0), po), lhs, w2v, v)
    for u in range(hb):
        diag = eye64 * jnp.exp(cl[u])
        a1_ref[0, u, 0] = jnp.concatenate([left[u][:64, :64] + diag, (left[u][64:] + rt[u])[:, :64]], axis=0)
        a2_ref[0, u, 0] = right[u]


def _rwkv_chunks(r, lw, k, v, kk, b, lwt, kt, bt, C, hb, passes=RW_PASSES):
    B, T, _ = r.shape
    nc = T // C
    tspec = pl.BlockSpec((1, C, hb * LANES), lambda bb, h, c: (bb, c, h))
    fspec = pl.BlockSpec((1, hb * LANES, C), lambda bb, h, c: (bb, h, c))
    pg, pc, po = passes
    return pl.pallas_call(
        functools.partial(_rwkv_chunk_kernel, C=C, hb=hb, pg=pg, pc=pc, po=po),
        grid=(B, 8 // hb, nc),
        in_specs=[tspec] * 6 + [fspec] * 3,
        out_specs=[pl.BlockSpec((1, hb, 1, 64 + C, 64), lambda bb, h, c: (bb, h, c, 0, 0)),
                   pl.BlockSpec((1, hb, 1, 64 + C, LANES), lambda bb, h, c: (bb, h, c, 0, 0))],
        out_shape=[jax.ShapeDtypeStruct((B, 8, nc, 64 + C, 64), F32),
                   jax.ShapeDtypeStruct((B, 8, nc, 64 + C, LANES), F32)],
        compiler_params=_params(("arbitrary", "arbitrary", "arbitrary")),
        name="rwkv_chunks",
    )(r, lw, k, v, kk, b, lwt, kt, bt)


def _rwkv_scan_kernel(a1_ref, a2_ref, s0_ref, y_ref, sT_ref, h_ref, *, C, cb):
    c = pl.program_id(1)

    @pl.when(c == 0)
    def _():
        h_ref[...] = s0_ref[0]

    for cc in range(cb):
        for h in range(8):
            res = _mmp(a1_ref[0, h, cc], h_ref[h], 3) + a2_ref[0, h, cc]
            h_ref[h] = res[:64]
            y_ref[0, cc * C:(cc + 1) * C, h * LANES:(h + 1) * LANES] = res[64:]

    @pl.when(c == pl.num_programs(1) - 1)
    def _():
        sT_ref[0] = h_ref[...]


def _rwkv_scan(a1, a2, s0t, C, cb):
    B, _, nc, _, _ = a1.shape
    return pl.pallas_call(
        functools.partial(_rwkv_scan_kernel, C=C, cb=cb),
        grid=(B, nc // cb),
        in_specs=[pl.BlockSpec((1, 8, cb, 64 + C, 64), lambda b, c: (b, 0, c, 0, 0)),
                  pl.BlockSpec((1, 8, cb, 64 + C, LANES), lambda b, c: (b, 0, c, 0, 0)),
                  pl.BlockSpec((1, 8, 64, LANES), lambda b, c: (b, 0, 0, 0))],
        out_specs=[pl.BlockSpec((1, cb * C, HP), lambda b, c: (b, c, 0)),
                   pl.BlockSpec((1, 8, 64, LANES), lambda b, c: (b, 0, 0, 0))],
        out_shape=[jax.ShapeDtypeStruct((B, nc * C, HP), F32),
                   jax.ShapeDtypeStruct((B, 8, 64, LANES), F32)],
        scratch_shapes=[pltpu.VMEM((8, 64, LANES), F32)],
        compiler_params=_params(("arbitrary", "arbitrary")),
        name="rwkv_scan",
    )(a1, a2, s0t)


def _mix_kernel(att_ref, y_ref, g_ref, bg_ref, sga_ref, sgb_ref, x_ref, gt1_ref, sc2_ref, sh2_ref,
                woa_ref, wor_ref, wout_ref, lnw_ref, lnb_ref, gpost_ref, gpre_ref, wr_ref, br_ref,
                x1_ref, h2_ref, gate_ref, eidx_ref, wgt_ref, rel_ref, cnt_out_ref, cnt_ref, *, tm):
    att = jnp.concatenate([att_ref[0, h] for h in range(8)], axis=1)
    lane = lax.broadcasted_iota(jnp.int32, (tm, LANES), 1)
    valid = lane < RWKV_HEAD
    parts = []
    for h in range(8):
        y = y_ref[0, :, h * LANES:(h + 1) * LANES]
        mu = jnp.sum(y, axis=1, keepdims=True) * (1.0 / RWKV_HEAD)
        d = jnp.where(valid, y - mu, 0.0)
        var = jnp.sum(d * d, axis=1, keepdims=True) * (1.0 / RWKV_HEAD)
        parts.append(d * lax.rsqrt(var + GN_EPS))
    yn = jnp.concatenate(parts, axis=1)
    rw = (yn * lnw_ref[...] + lnb_ref[...]) * g_ref[0] + bg_ref[0]
    mix = sga_ref[0] * _mm(att, woa_ref[...]) + sgb_ref[0] * _mm(rw.astype(BF16), wor_ref[...])
    o = _mm(mix.astype(BF16), wout_ref[...])
    x1 = x_ref[0] + gt1_ref[0] * _rms(o, gpost_ref[...])
    x1_ref[0] = x1
    h2 = _rms(x1, gpre_ref[...]) * (1.0 + sc2_ref[0]) + sh2_ref[0]
    h2_ref[0] = h2
    logits = _mmp(h2, wr_ref[...], 3) + br_ref[...]
    el = lax.broadcasted_iota(jnp.int32, logits.shape, 1).astype(F32)
    work = logits
    sel = jnp.zeros(logits.shape, jnp.bool_)
    vmax = None
    firsts = []
    for kk in range(TOP_K):
        mx = jnp.max(work, axis=1, keepdims=True)
        if kk == 0:
            vmax = mx
        first = jnp.min(jnp.where(work == mx, el, float(N_EXPERTS)), axis=1, keepdims=True)
        hit = el == first
        firsts.append(first)
        sel = sel | hit
        work = jnp.where(hit, -jnp.inf, work)
    e = jnp.where(sel, jnp.exp(logits - vmax), 0.0)
    gate = e / jnp.sum(e, axis=1, keepdims=True)
    gate_ref[0] = gate
    @pl.when((pl.program_id(0) == 0) & (pl.program_id(1) == 0))
    def _():
        cnt_ref[...] = jnp.zeros(cnt_ref.shape, F32)

    self = jnp.where(sel, 1.0, 0.0)
    ti = lax.broadcasted_iota(jnp.int32, (tm, tm), 0)
    tj = lax.broadcasted_iota(jnp.int32, (tm, tm), 1)
    before = _mm(jnp.where(tj < ti, 1.0, 0.0).astype(BF16), self.astype(BF16)) + cnt_ref[...]
    cnt_ref[...] = cnt_ref[...] + jnp.sum(self, axis=0, keepdims=True)
    cnt_out_ref[...] = cnt_ref[...]
    ln = lax.broadcasted_iota(jnp.int32, (tm, LANES), 1)
    eo = jnp.zeros((tm, LANES), F32)
    go = jnp.zeros((tm, LANES), F32)
    ro = jnp.zeros((tm, LANES), F32)
    for kk in range(TOP_K):
        hit = el == firsts[kk]
        gk = jnp.sum(jnp.where(hit, gate, 0.0), axis=1, keepdims=True)
        rk = jnp.sum(jnp.where(hit, before, 0.0), axis=1, keepdims=True)
        eo = jnp.where(ln == kk, firsts[kk], eo)
        go = jnp.where(ln == kk, gk, go)
        ro = jnp.where(ln == kk, rk, ro)
    eidx_ref[0] = eo.astype(jnp.int32)
    wgt_ref[0] = go
    rel_ref[0] = ro.astype(jnp.int32)


def _mix(att, y, g, bg, sga, sgb, x, gt1, sc2, sh2, wts, tm):
    B, T, D = x.shape
    tspec = lambda n: pl.BlockSpec((1, tm, n), lambda b, i: (b, i, 0))
    full = lambda a: pl.BlockSpec(a.shape, lambda b, i: (0,) * a.ndim)
    tok = lambda n, dt: jax.ShapeDtypeStruct((B, T, n), dt)
    return pl.pallas_call(
        functools.partial(_mix_kernel, tm=tm),
        grid=(B, T // tm),
        in_specs=[pl.BlockSpec((1, 8, tm, LANES), lambda b, i: (b, 0, i, 0)), tspec(HP), tspec(HP), tspec(HP),
                  tspec(D), tspec(D), tspec(D), _mod_spec(gt1, tm), _mod_spec(sc2, tm), _mod_spec(sh2, tm)]
                 + [full(w) for w in wts],
        out_specs=[tspec(D), tspec(D), tspec(N_EXPERTS), tspec(LANES), tspec(LANES), tspec(LANES),
                   pl.BlockSpec((1, N_EXPERTS), lambda b, i: (0, 0))],
        out_shape=[tok(D, F32), tok(D, F32), tok(N_EXPERTS, F32), tok(LANES, jnp.int32), tok(LANES, F32),
                   tok(LANES, jnp.int32), jax.ShapeDtypeStruct((1, N_EXPERTS), F32)],
        scratch_shapes=[pltpu.VMEM((1, N_EXPERTS), F32)],
        compiler_params=_params(("arbitrary", "arbitrary")),
        name="mix_router",
    )(att, y, g, bg, sga, sgb, x, gt1, sc2, sh2, *wts)


def _expert(xb, wgu, bgu, wd, bd, d_ff):
    wgu = wgu.astype(BF16)
    wd = wd.astype(BF16)
    hgu = _mm(xb, wgu) + bgu
    hg = jnp.minimum(hgu[:, :d_ff], SWIGLU_LIMIT)
    hl = jnp.clip(hgu[:, d_ff:], -SWIGLU_LIMIT, SWIGLU_LIMIT)
    act = hg * (1.0 / (1.0 + jnp.exp(-SWIGLU_ALPHA * hg))) * (hl + 1.0)
    return _mm(act.astype(BF16), wd) + bd


def _sc_mesh():
    return plsc.VectorSubcoreMesh(core_axis_name="core", subcore_axis_name="subcore")


def _sc_dispatch(h, pos_slots, n_rows, q):
    N, D = h.shape
    mesh = _sc_mesh()
    steps = N // (mesh.num_cores * SC_WINDOW)
    assert steps * mesh.num_cores * SC_WINDOW == N

    @pl.kernel(out_type=jax.ShapeDtypeStruct((n_rows, SC_PIECE), h.dtype), mesh=mesh, scratch_types=[])
    def scatter(x_hbm, *refs):
        idx_hbm, o_hbm = refs[:-1], refs[-1]
        base = lax.axis_index("core") * steps

        def body(x_vmem, *i_vmem):
            for iv in i_vmem:
                pltpu.sync_copy(x_vmem, o_hbm.at[iv.at[0]])

        pltpu.emit_pipeline(
            body,
            grid=(steps,),
            in_specs=[pl.BlockSpec((SC_WINDOW, SC_PIECE), index_map=lambda i: (base + i, q))]
                     + [pl.BlockSpec((1, SC_WINDOW), index_map=lambda i: (0, base + i))] * len(idx_hbm),
            out_specs=[],
            core_axis_name="subcore",
            dimension_semantics=(pltpu.PARALLEL,),
        )(x_hbm, *idx_hbm)

    return scatter(h, *pos_slots)


def _sc_gather(table, idx):
    M = idx.shape[1]
    mesh = _sc_mesh()
    steps = M // (mesh.num_cores * SC_WINDOW)
    assert steps * mesh.num_cores * SC_WINDOW == M

    @pl.kernel(out_type=jax.ShapeDtypeStruct((M, SC_PIECE), table.dtype), mesh=mesh)
    def gather(x_hbm, i_hbm, o_hbm):
        base = lax.axis_index("core") * steps

        def body(i_vmem, o_vmem):
            pltpu.sync_copy(x_hbm.at[i_vmem.at[0]], o_vmem)

        pltpu.emit_pipeline(
            body,
            grid=(steps,),
            in_specs=[pl.BlockSpec((1, SC_WINDOW), index_map=lambda i: (0, base + i))],
            out_specs=[pl.BlockSpec((SC_WINDOW, SC_PIECE), index_map=lambda i: (base + i, 0))],
            core_axis_name="subcore",
            dimension_semantics=(pltpu.PARALLEL,),
        )(i_hbm, o_hbm)

    return gather(table, idx)


def _moe_grouped_kernel(te_ref, nt_ref, *refs, d_ff, npiece):
    x_refs, (wgu_ref, bgu_ref, wd_ref, bd_ref) = refs[:npiece], refs[npiece:npiece + 4]
    o_refs, (wgu_b, wd_b) = refs[npiece + 4:2 * npiece + 4], refs[2 * npiece + 4:]
    g = pl.program_id(0)

    @pl.when((g == 0) | (te_ref[g] != te_ref[jnp.maximum(g - 1, 0)]))
    def _():
        wgu_b[...] = wgu_ref[0].astype(BF16)
        wd_b[...] = wd_ref[0].astype(BF16)

    @pl.when(g < nt_ref[0])
    def _():
        xb = jnp.concatenate([r[...] for r in x_refs], axis=1).astype(BF16)
        y = _expert(xb, wgu_b[...], bgu_ref[0], wd_b[...], bd_ref[0], d_ff)
        for q, o_ref in enumerate(o_refs):
            o_ref[...] = y[:, q * SC_PIECE:(q + 1) * SC_PIECE]


def _moe_grouped(xs, te, nt, wgu, bgu, wd, bd, TG):
    NP = xs[0].shape[0]
    E, D, F2 = wgu.shape
    npiece = len(xs)
    wmap = lambda g, te_, nt_: (te_[g], 0, 0)
    xmap = lambda g, te_, nt_: (jnp.minimum(g, nt_[0] - 1), 0)
    pspec = pl.BlockSpec((TG, SC_PIECE), xmap)
    return pl.pallas_call(
        functools.partial(_moe_grouped_kernel, d_ff=F2 // 2, npiece=npiece),
        grid_spec=pltpu.PrefetchScalarGridSpec(
            num_scalar_prefetch=2,
            grid=(NP // TG,),
            in_specs=[pspec] * npiece + [pl.BlockSpec((1, D, F2), wmap), pl.BlockSpec((1, 1, F2), wmap),
                                         pl.BlockSpec((1, F2 // 2, D), wmap), pl.BlockSpec((1, 1, D), wmap)],
            out_specs=[pspec] * npiece,
            scratch_shapes=[pltpu.VMEM((D, F2), BF16), pltpu.VMEM((F2 // 2, D), BF16)]),
        out_shape=[jax.ShapeDtypeStruct((NP, SC_PIECE), F32)] * npiece,
        compiler_params=_params(("arbitrary",)),
        name="moe_grouped",
    )(te, nt, *xs, wgu, bgu, wd, bd)


def _moe_combine_kernel(*refs, npiece):
    y_refs = refs[:TOP_K * npiece]
    w_ref, x1_ref, gt2_ref, gpost_ref, o_ref = refs[TOP_K * npiece:]
    w = w_ref[...]
    acc = None
    for k in range(TOP_K):
        yk = jnp.concatenate([y_refs[k * npiece + q][...] for q in range(npiece)], axis=1)
        term = w[:, k:k + 1] * yk
        acc = term if acc is None else acc + term
    o_ref[...] = x1_ref[...] + gt2_ref[0] * _rms(acc, gpost_ref[...])


def _moe_combine(yslot, wgt, x1, gt2, gpost, tm):
    N, D = x1.shape
    nb = N // tm
    tpb = N // gt2.shape[0]
    npiece = len(yslot)
    yspec = lambda k: pl.BlockSpec((tm, SC_PIECE), lambda i, k=k: (i + k * nb, 0))
    return pl.pallas_call(
        functools.partial(_moe_combine_kernel, npiece=npiece),
        grid=(nb,),
        in_specs=[yspec(k) for k in range(TOP_K) for _ in range(npiece)]
                 + [pl.BlockSpec((tm, LANES), lambda i: (i, 0)), pl.BlockSpec((tm, D), lambda i: (i, 0)),
                    pl.BlockSpec((1, 1, D), lambda i: ((i * tm) // tpb, 0, 0)), pl.BlockSpec((1, D), lambda i: (0, 0))],
        out_specs=pl.BlockSpec((tm, D), lambda i: (i, 0)),
        out_shape=jax.ShapeDtypeStruct((N, D), F32),
        compiler_params=_params(("arbitrary",)),
        name="moe_combine",
    )(*[yslot[q] for _ in range(TOP_K) for q in range(npiece)], wgt, x1, gt2, gpost)


def _moe_kernel(h_ref, gate_ref, x1_ref, gt2_ref, gpost_ref, wgu_ref, bgu_ref, wd_ref, bd_ref, o_ref, acc_ref,
                *, d_ff):
    e = pl.program_id(1)

    @pl.when(e == 0)
    def _():
        acc_ref[...] = jnp.zeros(acc_ref.shape, F32)

    contrib = _expert(h_ref[...].astype(BF16), wgu_ref[0], bgu_ref[0], wd_ref[0], bd_ref[0], d_ff)
    gate = gate_ref[...]
    el = lax.broadcasted_iota(jnp.int32, gate.shape, 1)
    ge = jnp.sum(jnp.where(el == e, gate, 0.0), axis=1, keepdims=True)
    acc_ref[...] += ge * contrib

    @pl.when(e == pl.num_programs(1) - 1)
    def _():
        o_ref[...] = x1_ref[...] + gt2_ref[0] * _rms(acc_ref[...], gpost_ref[...])


def _moe(h2, gate, x1, gt2, gpost, wgu, bgu, wd, bd, tm):
    N, D = h2.shape
    E, _, F2 = wgu.shape
    tspec = lambda n: pl.BlockSpec((tm, n), lambda i, e: (i, 0))
    if gt2.shape[1] == 1:
        tpb = N // gt2.shape[0]
        gspec = pl.BlockSpec((1, 1, D), lambda i, e: ((i * tm) // tpb, 0, 0))
    else:
        gspec = pl.BlockSpec((1, tm, D), lambda i, e: (i, 0, 0))
    return pl.pallas_call(
        functools.partial(_moe_kernel, d_ff=F2 // 2),
        grid=(N // tm, E),
        in_specs=[tspec(D), tspec(E), tspec(D), gspec, pl.BlockSpec((1, D), lambda i, e: (0, 0)),
                  pl.BlockSpec((1, D, F2), lambda i, e: (e, 0, 0)), pl.BlockSpec((1, 1, F2), lambda i, e: (e, 0, 0)),
                  pl.BlockSpec((1, F2 // 2, D), lambda i, e: (e, 0, 0)), pl.BlockSpec((1, 1, D), lambda i, e: (e, 0, 0))],
        out_specs=tspec(D),
        out_shape=jax.ShapeDtypeStruct((N, D), F32),
        scratch_shapes=[pltpu.VMEM((tm, D), F32)],
        compiler_params=_params(("arbitrary", "arbitrary")),
        name="moe",
    )(h2, gate, x1, gt2, gpost, wgu, bgu, wd, bd)


def _per_token(m, T):
    B, _, D = m.shape
    return jnp.broadcast_to(m, (B, T, D)).reshape(1, B * T, D)


def _layer(x, mods, past, s0, shift0, P, cfg):
    B, T, D = x.shape
    sh1, sc1, gt1, sh2, sc2, gt2 = mods
    tm = cfg["tm"]

    q, k, v, kb, vb, iq, ik, ikb, iw = _proj_a(x, sc1, sh1, P["g_pre_mix"], P["w_a"], P["ik_ln_w"], P["ik_ln_b"],
                                              cfg["tm_a"])
    zr, sga, sgb = _proj_b(x, sc1, sh1, P["g_pre_mix"], P["w_b"], tm)

    if past is not None:
        pk, pv, pik = past
        plen = pk.shape[1]
        kb_all = jnp.concatenate([pk.reshape(B, plen, 128).astype(BF16), kb], axis=1)
        vb_all = jnp.concatenate([pv.reshape(B, plen, 128).astype(BF16), vb], axis=1)
        ik_all = jnp.concatenate([jnp.pad(pik, ((0, 0), (0, 0), (0, 64))).astype(BF16), ikb], axis=1)
    else:
        plen = 0
        kb_all, vb_all, ik_all = kb, vb, ikb
    ltot = plen + T
    KB = cfg["KB"]
    lp = -(-ltot // KB) * KB
    if lp != ltot:
        padk = ((0, 0), (0, lp - ltot), (0, 0))
        kb_all, vb_all, ik_all = jnp.pad(kb_all, padk), jnp.pad(vb_all, padk), jnp.pad(ik_all, padk)
    vb1 = jnp.concatenate([vb_all, jnp.ones_like(vb_all)], axis=-1)
    att = _attention(iq, iw, q, ik_all, kb_all, vb1, R=cfg["R"], KB=KB, SB=cfg["SB"], HG=cfg["HG"], ltot=ltot,
                     q_off=plen)

    C = RW_CHUNK
    tp = -(-T // C) * C
    zr_p = zr if tp == T else jnp.pad(zr, ((0, 0), (0, tp - T), (0, 0)))
    shift_pad = _take_cols(shift0.reshape(B, RWKV_COLS), _COLS_ZR).reshape(B, 1, ZRW)
    r, lw, km, vv, kkn, bb, g, bg, lwt, kt, bt = _rwkv_prep(zr_p, shift_pad, P, min(cfg["tm_rw"], tp), T)
    a1, a2 = _rwkv_chunks(r, lw, km, vv, kkn, bb, lwt, kt, bt, C, cfg["hb"])
    s0t = jnp.pad(jnp.swapaxes(s0, 2, 3), ((0, 0), (0, 0), (0, 0), (0, 64)))
    nc = tp // C
    cb = min(cfg["cb"], nc)
    y, sT = _rwkv_scan(a1, a2, s0t, C, cb)
    s_new = jnp.swapaxes(sT[..., :64], 2, 3)
    zlast = zr[:, T - 1]
    inv = np.zeros((RWKV_COLS,), np.int32)
    inv[_COLS_ZR[0][_COLS_ZR[1]]] = np.nonzero(_COLS_ZR[1])[0]
    shift_new = jnp.take(zlast, jnp.asarray(inv), axis=1).reshape(B, 1, RWKV_COLS)
    if tp != T:
        y, g, bg = y[:, :T], g[:, :T], bg[:, :T]

    wts = [P[n] for n in ("w_o_att", "w_o_rwkv", "w_out", "ln_x_w", "ln_x_b", "g_post_mix", "g_pre_ffn",
                          "w_router", "b_router")]
    x1, h2, gate, eidx, wgt, rel, cnt = _mix(att, y, g, bg, sga, sgb, x, gt1, sc2, sh2, wts, cfg["tm_mix"])

    N = B * T
    tmm = min(cfg["tm_moe"], N)
    if cfg["routed"]:
        TG = cfg["TG"]
        ntm = N * TOP_K // TG + N_EXPERTS
        cnt_i = cnt[0].astype(jnp.int32)
        padded = (cnt_i + TG - 1) // TG * TG
        ends = jnp.cumsum(padded)
        off = ends - padded
        nt = ends[-1] // TG
        gi = jnp.minimum(jnp.arange(ntm, dtype=jnp.int32), nt - 1)
        te = jnp.sum((ends // TG)[None, :] <= gi[:, None], axis=1).astype(jnp.int32)
        e4 = eidx.reshape(N, LANES)[:, :TOP_K]
        pos = jnp.take(off, e4) + rel.reshape(N, LANES)[:, :TOP_K]
        pos = jnp.clip(pos, 0, ntm * TG - 1)
        pos_t = pos.T
        npiece = D // SC_PIECE
        h2f = h2.reshape(N, D)
        xs = [_sc_dispatch(h2f, [pos_t[k:k + 1] for k in range(TOP_K)], ntm * TG, q) for q in range(npiece)]
        ys = _moe_grouped(xs, te, nt.reshape(1), P["w_gu"], P["b_gu"], P["w_down"], P["b_down"], TG)
        idx_c = pos_t.reshape(1, TOP_K * N)
        yslot = [_sc_gather(ys[q], idx_c) for q in range(npiece)]
        out = _moe_combine(yslot, wgt.reshape(N, LANES), x1.reshape(N, D), gt2, P["g_post_ffn"], tm)
    else:
        if T % tmm == 0:
            gt2m = gt2
        else:
            gt2m = _per_token(gt2, T).reshape(N // tmm, tmm, D)
        out = _moe(h2.reshape(N, D), gate.reshape(N, N_EXPERTS), x1.reshape(N, D), gt2m, P["g_post_ffn"],
                   P["w_gu"], P["b_gu"], P["w_down"], P["b_down"], tmm)
    return (out.reshape(B, T, D), k.reshape(B, T, ATT_KV_HEADS, HEAD_DIM), v.reshape(B, T, ATT_KV_HEADS, HEAD_DIM),
            ik, s_new, shift_new)


def _prep_weights(l, w_in, ik_ln_w, ik_ln_b, mu_rwkv, w0, w_up, a0, a_up, g_up, k_k, k_a, r_k, ln_x_w, ln_x_b,
                  w_o_att, w_o_rwkv, w_out, w_router, b_router, w_gu, b_gu, w_down, b_down,
                  g_pre_mix, g_post_mix, g_pre_ffn, g_post_ffn):
    row = lambda a: a.reshape(1, -1)
    hp = lambda a: _take_cols(row(a), _HEAD_IDX)
    P = {}
    P["w_a"] = _take_cols(w_in[l], _COLS_A).astype(BF16)
    P["w_b"] = _take_cols(w_in[l], _COLS_B).astype(BF16)
    P["ik_ln_w"] = jnp.pad(row(ik_ln_w[l]), ((0, 0), (0, 64)))
    P["ik_ln_b"] = jnp.pad(row(ik_ln_b[l]), ((0, 0), (0, 64)))
    P["mu"] = _take_cols(row(mu_rwkv[l]), _COLS_ZR)
    P["w0"], P["a0"], P["k_k"], P["k_a"] = hp(w0[l]), hp(a0[l]), hp(k_k[l]), hp(k_a[l])
    P["r_k"] = hp(r_k[l].reshape(-1))
    P["ln_x_w"], P["ln_x_b"] = hp(ln_x_w[l]), hp(ln_x_b[l])
    z64 = jnp.zeros((64, HP), F32)
    P["w_up"] = jnp.concatenate([_take_cols(w_up[l], _HEAD_IDX), z64], axis=0)
    P["a_up"] = jnp.concatenate([z64, _take_cols(a_up[l], _HEAD_IDX)], axis=0)
    P["g_up"] = _take_cols(g_up[l], _HEAD_IDX)
    P["w_o_att"] = _take_rows(w_o_att[l], _ATT_ROW_IDX).astype(BF16)
    P["w_o_rwkv"] = _take_rows(w_o_rwkv[l], _HEAD_IDX).astype(BF16)
    P["w_out"] = w_out[l].astype(BF16)
    P["w_router"] = w_router[l]
    P["b_router"] = row(b_router[l])
    P["w_gu"] = w_gu[l]
    P["b_gu"] = b_gu[l].reshape(N_EXPERTS, 1, -1)
    P["w_down"] = w_down[l]
    P["b_down"] = b_down[l].reshape(N_EXPERTS, 1, -1)
    P["g_pre_mix"], P["g_post_mix"] = row(g_pre_mix[l]), row(g_post_mix[l])
    P["g_pre_ffn"], P["g_post_ffn"] = row(g_pre_ffn[l]), row(g_post_ffn[l])
    return P


CFG_PROMPT = dict(tm=256, tm_a=1024, tm_mix=512, R=256,KB=512, SB=256, HG=2,tm_rw=256, hb=8, cb=2, tm_moe=1024, routed=True, TG=512)
CFG_SAMPLE = dict(tm=32, tm_a=32, tm_mix=32, R=32, KB=512, SB=256, HG=2,tm_rw=128, hb=8, cb=1, tm_moe=512, routed=False)


def kernel(x_prompt, x_sample, c_prompt, c_sample, cache_k, cache_v, cache_idx_k, state_rwkv, state_shift, w_ada, b_ada, g_pre_mix, g_post_mix, g_pre_ffn, g_post_ffn, w_in, ik_ln_w, ik_ln_b, mu_rwkv, w0, w_up, a0, a_up, g_up, k_k, k_a, r_k, ln_x_w, ln_x_b, w_o_att, w_o_rwkv, w_out, w_router, b_router, w_gu, b_gu, w_down, b_down):
    depth = w_in.shape[0]
    bp, tp_, D = x_prompt.shape
    bs, ts, _ = x_sample.shape
    y_p, y_s = x_prompt, x_sample
    st_p = [[] for _ in range(5)]
    st_s = [[] for _ in range(5)]
    nc_all = bp + bs
    npad = -(-nc_all // 8) * 8
    c_all = jnp.pad(jnp.concatenate([c_prompt, c_sample], axis=0), ((0, npad - nc_all), (0, 0)))
    for l in range(depth):
        P = _prep_weights(l, w_in, ik_ln_w, ik_ln_b, mu_rwkv, w0, w_up, a0, a_up, g_up, k_k, k_a, r_k, ln_x_w,
                          ln_x_b, w_o_att, w_o_rwkv, w_out, w_router, b_router, w_gu, b_gu, w_down, b_down,
                          g_pre_mix, g_post_mix, g_pre_ffn, g_post_ffn)
        ada = _ada(c_all, w_ada[l], b_ada[l])
        mods_p = [m[:bp, None, :] for m in jnp.split(ada, 6, axis=-1)]
        mods_s = [m[bp:nc_all, None, :] for m in jnp.split(ada, 6, axis=-1)]
        zero_state = jnp.zeros((bp, RWKV_HEADS, RWKV_HEAD, RWKV_HEAD), F32)
        zero_shift = jnp.zeros((bp, 1, RWKV_COLS), F32)
        outs_p = _layer(y_p, mods_p, None, zero_state, zero_shift, P, CFG_PROMPT)
        outs_s = _layer(y_s, mods_s, (cache_k[l], cache_v[l], cache_idx_k[l]), state_rwkv[l], state_shift[l], P,
                        CFG_SAMPLE)
        y_p, y_s = outs_p[0], outs_s[0]
        for lst, val in zip(st_p, outs_p[1:]):
            lst.append(val)
        for lst, val in zip(st_s, outs_s[1:]):
            lst.append(val)
    sp = [jnp.stack(v, axis=0) for v in st_p]
    ss = [jnp.stack(v, axis=0) for v in st_s]
    return (y_p, y_s, sp[0], sp[1], sp[2], sp[3], sp[4], ss[0], ss[1], ss[2], ss[3], ss[4])
```

```python
import functools

import numpy as np
import jax
import jax.numpy as jnp
from jax import lax
from jax.experimental import pallas as pl
from jax.experimental.pallas import tpu as pltpu
from jax.experimental.pallas import tpu_sc as plsc

F32 = jnp.float32
BF16 = jnp.bfloat16
HI = lax.Precision.HIGHEST

CHUNK = 64
ATT_HEADS = 8
ATT_KV_HEADS = 2
HEAD_DIM = 64
IDX_HEADS = 8
IDX_DIM = 64
IDX_SCALE = (IDX_DIM ** -0.5) * (IDX_HEADS ** -0.5)
TOPK_MAX = 256
RWKV_HEADS = 8
RWKV_HEAD = 64
RWKV_WIDTH = RWKV_HEADS * RWKV_HEAD
DECAY_LORA = 64
AAA_LORA = 64
GATE_LORA = 128
RWKV_COLS = 3 * RWKV_WIDTH + DECAY_LORA + AAA_LORA + GATE_LORA
GN_EPS = 64e-5
L2_EPS = 1e-24
N_EXPERTS = 32
TOP_K = 4
SWIGLU_LIMIT = 7.0
SWIGLU_ALPHA = 1.702
NORM_EPS = 1e-6
LN_EPS = 1e-6

LANES = 128
VMEM_LIMIT = 56 * 1024 * 1024

HP = 8 * LANES
RW_CHUNK = 128
NEG_BIG = -1e30
LOG2E = 1.4426950408889634
KEY_NEG_INF = -2139095041
INT_MIN = -2147483648
SC_WINDOW = 128
SC_PIECE = 256
LO_GROUP = 4
RW_PASSES = (1, 1, 1)


def _params(sem):
    return pltpu.CompilerParams(dimension_semantics=sem, vmem_limit_bytes=VMEM_LIMIT)


def _nt(a, b, precision=None):
    return lax.dot_general(a, b, (((1,), (1,)), ((), ())), precision=precision,
                           preferred_element_type=F32)


def _mm(a, b, precision=None):
    return jnp.dot(a, b, precision=precision, preferred_element_type=F32)


def _split_bf16(x, terms):
    out = []
    for _ in range(terms):
        p = x.astype(BF16)
        out.append(p)
        x = x - p.astype(F32)
    return out


def _mmp(a, b, passes, nt=False):
    dot = _nt if nt else _mm
    if passes == 1:
        return dot(a.astype(BF16), b.astype(BF16))
    ah, al = _split_bf16(a, 2)
    bh, bl = _split_bf16(b, 2)
    return dot(ah, bh) + (dot(ah, bl) + dot(al, bh))


def _rms(x, g):
    return x * lax.rsqrt(jnp.mean(x * x, axis=-1, keepdims=True) + NORM_EPS) * g


def _headpad_idx(seg_off, lane_off_fn=lambda h: 0):
    idx = np.zeros((HP,), np.int32)
    ok = np.zeros((HP,), bool)
    for h in range(8):
        d0 = h * LANES + lane_off_fn(h)
        idx[d0:d0 + 64] = seg_off + h * 64 + np.arange(64)
        ok[d0:d0 + 64] = True
    return idx, ok


def _plain_idx(seg_off, n, width):
    idx = np.zeros((width,), np.int32)
    ok = np.zeros((width,), bool)
    idx[:n] = seg_off + np.arange(n)
    ok[:n] = True
    return idx, ok


def _cat(parts):
    return np.concatenate([p[0] for p in parts]), np.concatenate([p[1] for p in parts])


_O_Q, _O_K, _O_V, _O_IQ, _O_IK, _O_IW, _O_ZR, _O_GA, _O_GB = 0, 512, 640, 768, 1280, 1344, 1352, 3144, 4168
_COLS_A = _cat([_headpad_idx(_O_Q, lambda h: (h // 4) * 64), _plain_idx(_O_K, 128, 128), _plain_idx(_O_V, 128, 128),
                _headpad_idx(_O_IQ), _plain_idx(_O_IK, 64, 128), _plain_idx(_O_IW, 8, 128)])
_COLS_ZR = _cat([_headpad_idx(0), _headpad_idx(512), _headpad_idx(1024), _plain_idx(1536, 128, 128),
                 _plain_idx(1664, 128, 128)])
ZRW = _COLS_ZR[0].shape[0]
_COLS_B = _cat([(_COLS_ZR[0] + _O_ZR, _COLS_ZR[1]), _plain_idx(_O_GA, 1024, 1024), _plain_idx(_O_GB, 1024, 1024)])
NA = _COLS_A[0].shape[0]
NB = _COLS_B[0].shape[0]
_HEAD_IDX = _headpad_idx(0)
_ATT_ROW_IDX = _headpad_idx(0, lambda h: (h // 4) * 64)


def _take_cols(w, cols):
    idx, ok = cols
    return jnp.where(jnp.asarray(ok)[None, :], jnp.take(w, jnp.asarray(idx), axis=1), 0.0)


def _take_rows(w, cols):
    idx, ok = cols
    return jnp.where(jnp.asarray(ok)[:, None], jnp.take(w, jnp.asarray(idx), axis=0), 0.0)


def _mod_spec(m, tm):
    assert m.shape[1] == 1
    return pl.BlockSpec((1, 1, m.shape[2]), lambda b, i: (b, 0, 0))


def _ada_kernel(c_ref, w_ref, b_ref, o_ref):
    c = c_ref[...]
    s = c * (1.0 / (1.0 + jnp.exp(-c)))
    o_ref[...] = _mm(s, w_ref[...], HI) + b_ref[...]


def _ada(c, w, b):
    n, d = c.shape
    nout = w.shape[1]
    bn = 768
    return pl.pallas_call(
        _ada_kernel,
        grid=(nout // bn,),
        in_specs=[pl.BlockSpec((n, d), lambda j: (0, 0)),
                  pl.BlockSpec((d, bn), lambda j: (0, j)),
                  pl.BlockSpec((1, bn), lambda j: (0, j))],
        out_specs=pl.BlockSpec((n, bn), lambda j: (0, j)),
        out_shape=jax.ShapeDtypeStruct((n, nout), F32),
        compiler_params=_params(("arbitrary",)),
        name="ada",
    )(c, w, b.reshape(1, nout))


def _modulated(x_ref, sc_ref, sh_ref, g_ref):
    h = _rms(x_ref[0], g_ref[...])
    return (h * (1.0 + sc_ref[0]) + sh_ref[0]).astype(BF16)


def _proj_a_kernel(x_ref, sc_ref, sh_ref, g_ref, w_ref, lnw_ref, lnb_ref,
                   q_ref, k_ref, v_ref, kb_ref, vb_ref, iq_ref, ik_ref, ikb_ref, iw_ref):
    hb = _modulated(x_ref, sc_ref, sh_ref, g_ref)
    for h in range(8):
        zq = _mm(hb, w_ref[:, h * LANES:(h + 1) * LANES])
        q_ref[0, h] = (zq * (HEAD_DIM ** -0.5 * LOG2E)).astype(BF16)
    kv = _mm(hb, w_ref[:, 1024:1280])
    k = kv[:, :128]
    v = kv[:, 128:]
    k_ref[0] = k
    v_ref[0] = v
    kb_ref[0] = k.astype(BF16)
    vb_ref[0] = v.astype(BF16)
    for h in range(8):
        zi = _mm(hb, w_ref[:, 1280 + h * LANES:1280 + (h + 1) * LANES])
        iq_ref[0, h] = zi.astype(BF16)
    t = _mm(hb, w_ref[:, 2304:2560])
    ik = t[:, :128]
    lane = lax.broadcasted_iota(jnp.int32, ik.shape, 1)
    valid = lane < IDX_DIM
    mu = jnp.sum(ik, axis=-1, keepdims=True) * (1.0 / IDX_DIM)
    d = jnp.where(valid, ik - mu, 0.0)
    var = jnp.sum(d * d, axis=-1, keepdims=True) * (1.0 / IDX_DIM)
    ikn = d * lax.rsqrt(var + LN_EPS) * lnw_ref[...] + lnb_ref[...]
    ik_ref[0] = ikn[:, :IDX_DIM]
    ikb_ref[0] = ikn.astype(BF16)
    iw_ref[0] = t[:, 128:136] * IDX_SCALE


def _proj_a(x, sc, sh, g, w, lnw, lnb, tm):
    B, T, D = x.shape
    nt = T // tm
    tok = lambda n, dt: jax.ShapeDtypeStruct((B, T, n), dt)
    hm = lambda dt: jax.ShapeDtypeStruct((B, 8, T, LANES), dt)
    tspec = lambda n: pl.BlockSpec((1, tm, n), lambda b, i: (b, i, 0))
    hspec = pl.BlockSpec((1, 8, tm, LANES), lambda b, i: (b, 0, i, 0))
    full = lambda a: pl.BlockSpec(a.shape, lambda b, i: (0,) * a.ndim)
    return pl.pallas_call(
        _proj_a_kernel,
        grid=(B, nt),
        in_specs=[tspec(D), _mod_spec(sc, tm), _mod_spec(sh, tm), full(g), full(w), full(lnw), full(lnb)],
        out_specs=[hspec, tspec(128), tspec(128), tspec(128), tspec(128), hspec, tspec(IDX_DIM), tspec(128),
                   tspec(8)],
        out_shape=[hm(BF16), tok(128, F32), tok(128, F32), tok(128, BF16), tok(128, BF16), hm(BF16),
                   tok(IDX_DIM, F32), tok(128, BF16), tok(8, F32)],
        compiler_params=_params(("arbitrary", "arbitrary")),
        name="proj_att",
    )(x, sc, sh, g, w, lnw, lnb)


def _proj_b_kernel(x_ref, sc_ref, sh_ref, g_ref, w_ref, zr_ref, ga_ref, gb_ref):
    hb = _modulated(x_ref, sc_ref, sh_ref, g_ref)
    for j in range(ZRW // 256):
        zr_ref[0, :, j * 256:(j + 1) * 256] = _mm(hb, w_ref[:, j * 256:(j + 1) * 256])
    for j in range(4):
        za = _mm(hb, w_ref[:, ZRW + j * 256:ZRW + (j + 1) * 256])
        ga_ref[0, :, j * 256:(j + 1) * 256] = 1.0 / (1.0 + jnp.exp(-za))
        zb = _mm(hb, w_ref[:, ZRW + 1024 + j * 256:ZRW + 1024 + (j + 1) * 256])
        gb_ref[0, :, j * 256:(j + 1) * 256] = 1.0 / (1.0 + jnp.exp(-zb))


def _proj_b(x, sc, sh, g, w, tm):
    B, T, D = x.shape
    nt = T // tm
    tok = lambda n: jax.ShapeDtypeStruct((B, T, n), F32)
    tspec = lambda n: pl.BlockSpec((1, tm, n), lambda b, i: (b, i, 0))
    full = lambda a: pl.BlockSpec(a.shape, lambda b, i: (0,) * a.ndim)
    return pl.pallas_call(
        _proj_b_kernel,
        grid=(B, nt),
        in_specs=[tspec(D), _mod_spec(sc, tm), _mod_spec(sh, tm), full(g), full(w)],
        out_specs=[tspec(ZRW), tspec(1024), tspec(1024)],
        out_shape=[tok(ZRW), tok(1024), tok(1024)],
        compiler_params=_params(("arbitrary", "arbitrary")),
        name="proj_rwkv",
    )(x, sc, sh, g, w)


def _sortable_to_float(t):
    bits = t ^ (lax.shift_right_arithmetic(t, 31) & 0x7FFFFFFF)
    return lax.bitcast_convert_type(bits, F32)


def _top16(x):
    bits = lax.bitcast_convert_type(x, jnp.int32) & jnp.int32(-65536)
    return lax.bitcast_convert_type(bits, F32).astype(BF16)


def _attn_kernel(iq_ref, iw_ref, q_ref, ik_ref, k_ref, v_ref, o_ref, s_ref, sb_ref, m_ref, acc_ref,
                 *, R, KB, SB, HG, ltot, q_off, topk):
    j = pl.program_id(1)
    q0 = q_off + j * R
    pos = q0 + lax.broadcasted_iota(jnp.int32, (R, 1), 0)
    qchunk = lax.shift_right_logical(pos, 6)
    kend = jnp.minimum(ltot, (lax.shift_right_logical(q0 + R - 1, 6) + 1) * CHUNK)
    nkb = lax.shift_right_logical(kend + KB - 1, KB.bit_length() - 1)
    lane_k = lax.broadcasted_iota(jnp.int32, (R, KB), 1)
    nch = KB // LANES

    iq_all = iq_ref[0].reshape(8 * R, LANES)
    iw = iw_ref[0]
    iwb = [jnp.broadcast_to(iw[:, h:h + 1], (R, SB)) for h in range(8)]
    lane_s = lax.broadcasted_iota(jnp.int32, (R, SB), 1)
    nsb = KB // SB
    key_lim = jnp.minimum((qchunk + 1) * CHUNK, ltot)

    def score_body(kb, carry):
        for u in range(nsb):
            off = pl.multiple_of(kb * KB + u * SB, SB)
            s_all = _mm(iq_all, ik_ref[0, kb * nsb + u])
            acc = jnp.zeros((R, SB), F32)
            for h in range(8):
                acc = acc + jnp.maximum(s_all[h * R:(h + 1) * R], 0.0) * iwb[h]
            kidx = off + lane_s
            sc = jnp.where(kidx < key_lim, acc, -jnp.inf)
            s_ref[kb, :, u * SB:(u + 1) * SB] = sc
            sb_ref[kb, :, u * SB:(u + 1) * SB] = _top16(sc)
        return carry

    lax.fori_loop(0, nkb, score_body, 0)

    def count_ge(ref, cand, dt):
        one, zero = jnp.ones((), dt), jnp.zeros((), dt)

        def body(kb, part):
            blk = ref[kb]
            m = jnp.where(blk >= cand, one, zero)
            for c in range(nch):
                part = part + m[:, c * LANES:(c + 1) * LANES]
            return part
        part = lax.fori_loop(0, nkb, body, jnp.zeros((R, LANES), dt))
        return jnp.sum(part.astype(F32), axis=1, keepdims=True)

    kf = float(topk)
    c0 = count_ge(sb_ref, jnp.zeros((R, 1), BF16), BF16)
    t0 = jnp.where(c0 >= kf, 0, INT_MIN).astype(jnp.int32)
    ex0 = jnp.where(c0 == kf, 1.0, 0.0)

    def accept(c, cand, cnt):
        t, ex = c
        ok = cnt >= kf
        return jnp.where(ok, cand, t), jnp.where(ok & (cnt == kf), 1.0, ex)

    def hi_body(i, c):
        cand = c[0] + lax.shift_left(jnp.int32(1), 30 - i)
        return accept(c, cand, count_ge(sb_ref, _top16(_sortable_to_float(cand)), BF16))

    def lo_pass(i, c):
        cand = c[0] + lax.shift_left(jnp.int32(1), 15 - i)
        return accept(c, cand, count_ge(s_ref, _sortable_to_float(cand), F32))

    def lo_group(c):
        gi, t, ex, _ = c
        tc = (t, ex)
        for jj in range(LO_GROUP):
            tc = lo_pass(gi * LO_GROUP + jj, tc)
        return gi + 1, tc[0], tc[1], (jnp.min(tc[1]) > 0.5).astype(jnp.int32)

    t, ex = lax.fori_loop(0, 15, hi_body, (t0, ex0))
    _, t, _, _ = lax.while_loop(lambda c: (c[0] < 16 // LO_GROUP) & (c[3] == 0), lo_group,
                                (jnp.int32(0), t, ex, (jnp.min(ex) > 0.5).astype(jnp.int32)))
    all_finite = t <= KEY_NEG_INF
    tau = jnp.where(all_finite, -jnp.inf, _sortable_to_float(jnp.maximum(t, KEY_NEG_INF)))

    def count_gt_eq(_):
        def body(kb, carry):
            pg, pe = carry
            blk = s_ref[kb]
            mg = jnp.where(blk > tau, 1.0, 0.0)
            me = jnp.where(blk == tau, 1.0, 0.0)
            for c in range(nch):
                pg = pg + mg[:, c * LANES:(c + 1) * LANES]
                pe = pe + me[:, c * LANES:(c + 1) * LANES]
            return pg, pe
        z = jnp.zeros((R, LANES), F32)
        pg, pe = lax.fori_loop(0, nkb, body, (z, z))
        return jnp.sum(pg, axis=1, keepdims=True), jnp.sum(pe, axis=1, keepdims=True)

    cnt_gt, cnt_eq = count_gt_eq(0)
    need = kf - cnt_gt
    tie = (cnt_eq > need) & jnp.logical_not(all_finite)
    any_tie = jnp.max(jnp.where(tie, 1.0, 0.0)) > 0.0

    def tie_bound():
        def count_eq_below(x):
            def body(kb, part):
                blk = s_ref[kb]
                kidx = kb * KB + lane_k
                m = jnp.where((blk == tau) & (kidx < x), 1.0, 0.0)
                for c in range(nch):
                    part = part + m[:, c * LANES:(c + 1) * LANES]
                return part
            part = lax.fori_loop(0, nkb, body, jnp.zeros((R, LANES), F32))
            return jnp.sum(part, axis=1, keepdims=True)

        nbits = int(ltot).bit_length()

        def body(i, x):
            cand = x + lax.shift_left(jnp.int32(1), nbits - 1 - i)
            ok = count_eq_below(cand) < need
            return jnp.where(ok, cand, x)
        x = lax.fori_loop(0, nbits, body, jnp.zeros((R, 1), jnp.int32))
        return x + 1

    big = jnp.full((R, 1), 1 << 30, jnp.int32)
    bound = lax.cond(any_tie, lambda: jnp.where(tie, tie_bound(), big), lambda: big)

    def bias_body(kb, carry):
        blk = s_ref[kb]
        kidx = kb * KB + lane_k
        sel = (blk > tau) | ((blk == tau) & (kidx < bound))
        sel = sel & (blk > -jnp.inf)
        s_ref[kb] = jnp.where(sel, 0.0, NEG_BIG)
        return carry

    tau_eff = jnp.where(all_finite, -float(np.finfo(np.float32).max), tau)

    def bias_fast(kb, carry):
        s_ref[kb] = jnp.where(s_ref[kb] >= tau_eff, 0.0, NEG_BIG)
        return carry

    @pl.when(any_tie)
    def _():
        lax.fori_loop(0, nkb, bias_body, 0)

    @pl.when(jnp.logical_not(any_tie))
    def _():
        lax.fori_loop(0, nkb, bias_fast, 0)

    q_all = q_ref[0].reshape(8 * R, LANES)
    m_ref[...] = jnp.full(m_ref.shape, NEG_BIG, F32)
    acc_ref[...] = jnp.zeros(acc_ref.shape, F32)

    def att_body(kb, carry):
        for u in range(nsb):
            off = pl.multiple_of(kb * KB + u * SB, SB)
            kblk = k_ref[0, kb * nsb + u]
            vblk = v_ref[0, pl.ds(off, SB), :]
            bias = s_ref[kb, :, u * SB:(u + 1) * SB]
            logits = _mm(q_all, kblk)
            nch = SB // LANES
            for h0 in range(0, 8, HG):
                hs = list(range(h0, h0 + HG))
                lg = [logits[h * R:(h + 1) * R] + bias for h in hs]
                cm = [functools.reduce(jnp.maximum, [l[:, c * LANES:(c + 1) * LANES] for c in range(nch)])
                      for l in lg]
                m_old = [m_ref[h] for h in hs]
                m_new = [jnp.maximum(mo, jnp.max(c, axis=1, keepdims=True)) for mo, c in zip(m_old, cm)]
                alpha = [jnp.exp2(mo - mn) for mo, mn in zip(m_old, m_new)]
                p = [jnp.exp2(l - jnp.concatenate([mn] * nch, axis=1)).astype(BF16) for l, mn in zip(lg, m_new)]
                pv = [_mm(pp, vblk) for pp in p]
                for i, h in enumerate(hs):
                    acc_ref[h] = jnp.concatenate([alpha[i], alpha[i]], axis=1) * acc_ref[h] + pv[i]
                    m_ref[h] = m_new[i]
        return carry

    lax.fori_loop(0, nkb, att_body, 0)
    for h in range(8):
        a = acc_ref[h]
        o_ref[0, h] = (a[:, :LANES] / a[:, LANES:]).astype(BF16)


def _attention(iq, iw, q, ikb, kb, vb1, *, R, KB, SB, HG, ltot, q_off):
    B, _, Sq, _ = q.shape
    Lp = kb.shape[1]
    topk = min(TOPK_MAX, ltot // 4)
    assert Lp % KB == 0 and KB % SB == 0 and KB >= topk and Sq % R == 0
    to_blocks = lambda a: jnp.swapaxes(a.reshape(B, Lp // SB, SB, LANES), 2, 3)
    ikb, kb = to_blocks(ikb), to_blocks(kb)
    hspec = pl.BlockSpec((1, 8, R, LANES), lambda b, i: (b, 0, i, 0))
    kspec = pl.BlockSpec((1, Lp // SB, LANES, SB), lambda b, i: (b, 0, 0, 0))
    vspec = pl.BlockSpec((1, Lp, 2 * LANES), lambda b, i: (b, 0, 0))
    kern = functools.partial(_attn_kernel, R=R, KB=KB, SB=SB, HG=HG, ltot=ltot, q_off=q_off, topk=topk)
    return pl.pallas_call(
        kern,
        grid=(B, Sq // R),
        in_specs=[hspec, pl.BlockSpec((1, R, 8), lambda b, i: (b, i, 0)), hspec, kspec, kspec, vspec],
        out_specs=hspec,
        out_shape=jax.ShapeDtypeStruct((B, 8, Sq, LANES), BF16),
        scratch_shapes=[pltpu.VMEM((Lp // KB, R, KB), F32), pltpu.VMEM((Lp // KB, R, KB), BF16),
                        pltpu.VMEM((8, R, LANES), F32),
                        pltpu.VMEM((8, R, 2 * LANES), F32)],
        compiler_params=_params(("arbitrary", "arbitrary")),
        name="dsa_attention",
    )(iq, iw, q, ikb, kb, vb1)


def _head_sum(x):
    parts = []
    for h in range(8):
        s = jnp.sum(x[:, h * LANES:(h + 1) * LANES], axis=1, keepdims=True)
        parts.append(jnp.broadcast_to(s, (x.shape[0], LANES)))
    return jnp.concatenate(parts, axis=1)


def _rwkv_prep_kernel(z_ref, sh0_ref, mu_ref, w0_ref, wup_ref, a0_ref, aup_ref, gup_ref, kk_ref, ka_ref, rk_ref,
                      r_o, lw_o, k_o, v_o, kkn_o, b_o, g_o, bg_o, lwt_o, kt_o, bt_o, carry_ref,
                      *, tm, t_valid):
    i = pl.program_id(1)

    @pl.when(i == 0)
    def _():
        carry_ref[...] = sh0_ref[0]

    z = z_ref[0]
    row = lax.broadcasted_iota(jnp.int32, (tm, 1), 0)
    prev = jnp.where(row == 0, carry_ref[...], pltpu.roll(z, 1, axis=0))
    carry_ref[...] = z[tm - 1:tm, :]
    zm = z + (prev - z) * mu_ref[...]
    live = i * tm + row < t_valid
    zm = jnp.where(live, zm, 0.0)
    r = zm[:, 0:HP]
    k = zm[:, HP:2 * HP]
    v = zm[:, 2 * HP:3 * HP]
    wa = zm[:, 3 * HP:3 * HP + LANES]
    gd = zm[:, 3 * HP + LANES:3 * HP + 2 * LANES]
    w_raw = w0_ref[...] + _mmp(jnp.tanh(wa), wup_ref[...], 3)
    lw = (-float(np.exp(-0.5))) / (1.0 + jnp.exp(-w_raw))
    lw = jnp.where(live, lw, 0.0)
    a = 1.0 / (1.0 + jnp.exp(-(a0_ref[...] + _mmp(wa, aup_ref[...], 3))))
    g = _mmp(1.0 / (1.0 + jnp.exp(-gd)), gup_ref[...], 3)
    kk = k * kk_ref[...]
    kk = kk * lax.rsqrt(jnp.maximum(_head_sum(kk * kk), L2_EPS))
    k_mod = k * (1.0 + (a - 1.0) * ka_ref[...])
    b = kk * a
    bonus = _head_sum(r * k_mod * rk_ref[...]) * v
    r_o[0] = r.astype(BF16)
    lw_o[0] = lw
    k_o[0] = k_mod.astype(BF16)
    v_o[0] = v.astype(BF16)
    kkn_o[0] = kk.astype(BF16)
    b_o[0] = b.astype(BF16)
    g_o[0] = g
    bg_o[0] = bonus * g
    lwt_o[0] = lw.T
    kt_o[0] = k_mod.T.astype(BF16)
    bt_o[0] = b.T.astype(BF16)


def _rwkv_prep(zr, shift0, prm, tm, t_valid):
    B, T, _ = zr.shape
    tspec = pl.BlockSpec((1, tm, HP), lambda b, i: (b, i, 0))
    fspec = pl.BlockSpec((1, HP, tm), lambda b, i: (b, 0, i))
    full = lambda a: pl.BlockSpec(a.shape, lambda b, i: (0,) * a.ndim)
    tok = jax.ShapeDtypeStruct((B, T, HP), F32)
    feat = jax.ShapeDtypeStruct((B, HP, T), F32)
    tokb = jax.ShapeDtypeStruct((B, T, HP), BF16)
    featb = jax.ShapeDtypeStruct((B, HP, T), BF16)
    names =("mu", "w0", "w_up", "a0", "a_up", "g_up", "k_k", "k_a", "r_k")
    ws = [prm[n] for n in names]
    return pl.pallas_call(
        functools.partial(_rwkv_prep_kernel, tm=tm, t_valid=t_valid),
        grid=(B, T // tm),
        in_specs=[pl.BlockSpec((1, tm, ZRW), lambda b, i: (b, i, 0)),
                  pl.BlockSpec((1, 1, ZRW), lambda b, i: (b, 0, 0))] + [full(w) for w in ws],
        out_specs=[tspec] * 8 + [fspec] * 3,
        out_shape=[tokb, tok, tokb, tokb, tokb, tokb, tok, tok, feat, featb, featb],
        scratch_shapes=[pltpu.VMEM((1, ZRW), F32)],
        compiler_params=_params(("arbitrary", "arbitrary")),
        name="rwkv_prep",
    )(zr, shift0, *ws)


def _rwkv_chunk_kernel(r_ref, lw_ref, k_ref, v_ref, kk_ref, b_ref, lwt_ref, kt_ref, bt_ref, a1_ref, a2_ref,
                       *, C, hb, pg, pc, po):
    ri = lax.broadcasted_iota(jnp.int32, (C, C), 0)
    ci = lax.broadcasted_iota(jnp.int32, (C, C), 1)
    low_incl = jnp.where(ri >= ci, 1.0, 0.0).astype(BF16)
    up_incl = jnp.where(ri <= ci, 1.0, 0.0).astype(BF16)
    strict = ri > ci
    incl = ri >= ci
    eye64 = jnp.where(lax.broadcasted_iota(jnp.int32, (64, 64), 0) == lax.broadcasted_iota(jnp.int32, (64, 64), 1),
                      1.0, 0.0)
    sls = [slice(u * LANES, (u + 1) * LANES) for u in range(hb)]
    each = lambda f, *ls: [f(*a) for a in zip(*ls)]
    lw = [lw_ref[0, :, sl] for sl in sls]
    r = [r_ref[0, :, sl].astype(F32) for sl in sls]
    k = [k_ref[0, :, sl].astype(F32) for sl in sls]
    v = [v_ref[0, :, sl].astype(F32) for sl in sls]
    kk = [kk_ref[0, :, sl].astype(F32) for sl in sls]
    b = [b_ref[0, :, sl].astype(F32) for sl in sls]
    lwt = [lwt_ref[0, sl, :][:64] for sl in sls]
    kt = [kt_ref[0, sl, :][:64].astype(F32) for sl in sls]
    bt = [bt_ref[0, sl, :][:64].astype(F32) for sl in sls]
    cum = each(lambda x: sum(_mm(low_incl, p) for p in _split_bf16(x, 3)), lw)
    cumt = each(lambda x: sum(_mm(p, up_incl) for p in _split_bf16(x, 3)), lwt)
    e_neg = each(lambda c: jnp.exp(-c), cum)
    at = each(lambda kk_, c, l: -kk_ * jnp.exp(c - l), kk, cum, lw)
    rt = each(lambda r_, c: r_ * jnp.exp(c), r, cum)
    g = each(lambda a, r_, b_, k_, e: _mmp(jnp.concatenate([a, r_], axis=0),
                                           jnp.concatenate([b_ * e, k_ * e], axis=0), pg, nt=True),
             at, rt, b, k, e_neg)
    n = each(lambda g_: jnp.where(strict, g_[:C, :C], 0.0), g)
    aak = each(lambda g_: jnp.where(strict, g_[:C, C:], 0.0), g)
    lrb = each(lambda g_: jnp.where(incl, g_[C:, :C], 0.0), g)
    lrk = each(lambda g_: jnp.where(incl, g_[C:, C:], 0.0), g)
    blk8 = lax.shift_right_logical(ri, 3) == lax.shift_right_logical(ci, 3)
    eye = jnp.where(ri == ci, 1.0, 0.0)
    n0 = each(lambda n_: jnp.where(blk8, n_, 0.0), n)
    x = each(lambda n_: eye + n_, n0)
    p = each(lambda n_: _mmp(n_, n_, pc), n0)
    x = each(lambda x_, p_: x_ + _mmp(x_, p_, pc), x, p)
    p = each(lambda p_: _mmp(p_, p_, pc), p)
    x = each(lambda x_, p_: x_ + _mmp(x_, p_, pc), x, p)
    for lv in range(3, int(C).bit_length() - 1):
        off = (lax.shift_right_logical(ri, lv + 1) == lax.shift_right_logical(ci, lv + 1)) & (
            lax.shift_right_logical(ri, lv) != lax.shift_right_logical(ci, lv))
        t = each(lambda n_, x_: _mmp(jnp.where(off, n_, 0.0), x_, pc), n, x)
        x = each(lambda x_, t_: x_ + _mmp(x_, t_, pc), x, t)
    z = each(lambda x_, a, b_: _mmp(x_, jnp.concatenate([a, b_], axis=1), pc), x, aak, at)
    w2v = each(lambda z_, v_: _mmp(z_[:, :C], v_, po), z, v)
    cl = each(lambda c: c[:, C - 1:C], cumt)
    eb = each(lambda c_, ct: jnp.exp(c_ - ct), cl, cumt)
    lhs = each(lambda bt_, kt_, e, lb, lk: jnp.concatenate(
        [jnp.concatenate([bt_ * e, kt_ * e], axis=1), jnp.concatenate([lb, lk], axis=1)], axis=0),
        bt, kt, eb, lrb, lrk)
    left = each(lambda l_, z_: _mmp(l_[:, :C], z_[:, C:], po), lhs, z)
    right = each(lambda l_, w, v_: _mmp(l_, jnp.concatenate([w, v_], axis=0), po), lhs, w2v, v)
    for u in range(hb):
        diag = eye64 * jnp.exp(cl[u])
        a1_ref[0, u, 0] = jnp.concatenate([left[u][:64, :64] + diag, (left[u][64:] + rt[u])[:, :64]], axis=0)
        a2_ref[0, u, 0] = right[u]


def _rwkv_chunks(r, lw, k, v, kk, b, lwt, kt, bt, C, hb, passes=RW_PASSES):
    B, T, _ = r.shape
    nc = T // C
    tspec = pl.BlockSpec((1, C, hb * LANES), lambda bb, h, c: (bb, c, h))
    fspec = pl.BlockSpec((1, hb * LANES, C), lambda bb, h, c: (bb, h, c))
    pg, pc, po = passes
    return pl.pallas_call(
        functools.partial(_rwkv_chunk_kernel, C=C, hb=hb, pg=pg, pc=pc, po=po),
        grid=(B, 8 // hb, nc),
        in_specs=[tspec] * 6 + [fspec] * 3,
        out_specs=[pl.BlockSpec((1, hb, 1, 64 + C, 64), lambda bb, h, c: (bb, h, c, 0, 0)),
                   pl.BlockSpec((1, hb, 1, 64 + C, LANES), lambda bb, h, c: (bb, h, c, 0, 0))],
        out_shape=[jax.ShapeDtypeStruct((B, 8, nc, 64 + C, 64), F32),
                   jax.ShapeDtypeStruct((B, 8, nc, 64 + C, LANES), F32)],
        compiler_params=_params(("arbitrary", "arbitrary", "arbitrary")),
        name="rwkv_chunks",
    )(r, lw, k, v, kk, b, lwt, kt, bt)


def _rwkv_scan_kernel(a1_ref, a2_ref, s0_ref, y_ref, sT_ref, h_ref, *, C, cb):
    c = pl.program_id(1)

    @pl.when(c == 0)
    def _():
        h_ref[...] = s0_ref[0]

    for cc in range(cb):
        for h in range(8):
            res = _mmp(a1_ref[0, h, cc], h_ref[h], 3) + a2_ref[0, h, cc]
            h_ref[h] = res[:64]
            y_ref[0, cc * C:(cc + 1) * C, h * LANES:(h + 1) * LANES] = res[64:]

    @pl.when(c == pl.num_programs(1) - 1)
    def _():
        sT_ref[0] = h_ref[...]


def _rwkv_scan(a1, a2, s0t, C, cb):
    B, _, nc, _, _ = a1.shape
    return pl.pallas_call(
        functools.partial(_rwkv_scan_kernel, C=C, cb=cb),
        grid=(B, nc // cb),
        in_specs=[pl.BlockSpec((1, 8, cb, 64 + C, 64), lambda b, c: (b, 0, c, 0, 0)),
                  pl.BlockSpec((1, 8, cb, 64 + C, LANES), lambda b, c: (b, 0, c, 0, 0)),
                  pl.BlockSpec((1, 8, 64, LANES), lambda b, c: (b, 0, 0, 0))],
        out_specs=[pl.BlockSpec((1, cb * C, HP), lambda b, c: (b, c, 0)),
                   pl.BlockSpec((1, 8, 64, LANES), lambda b, c: (b, 0, 0, 0))],
        out_shape=[jax.ShapeDtypeStruct((B, nc * C, HP), F32),
                   jax.ShapeDtypeStruct((B, 8, 64, LANES), F32)],
        scratch_shapes=[pltpu.VMEM((8, 64, LANES), F32)],
        compiler_params=_params(("arbitrary", "arbitrary")),
        name="rwkv_scan",
    )(a1, a2, s0t)


def _mix_kernel(att_ref, y_ref, g_ref, bg_ref, sga_ref, sgb_ref, x_ref, gt1_ref, sc2_ref, sh2_ref,
                woa_ref, wor_ref, wout_ref, lnw_ref, lnb_ref, gpost_ref, gpre_ref, wr_ref, br_ref,
                x1_ref, h2_ref, gate_ref, eidx_ref, wgt_ref, rel_ref, cnt_out_ref, cnt_ref, *, tm):
    att = jnp.concatenate([att_ref[0, h] for h in range(8)], axis=1)
    lane = lax.broadcasted_iota(jnp.int32, (tm, LANES), 1)
    valid = lane < RWKV_HEAD
    parts = []
    for h in range(8):
        y = y_ref[0, :, h * LANES:(h + 1) * LANES]
        mu = jnp.sum(y, axis=1, keepdims=True) * (1.0 / RWKV_HEAD)
        d = jnp.where(valid, y - mu, 0.0)
        var = jnp.sum(d * d, axis=1, keepdims=True) * (1.0 / RWKV_HEAD)
        parts.append(d * lax.rsqrt(var + GN_EPS))
    yn = jnp.concatenate(parts, axis=1)
    rw = (yn * lnw_ref[...] + lnb_ref[...]) * g_ref[0] + bg_ref[0]
    mix = sga_ref[0] * _mm(att, woa_ref[...]) + sgb_ref[0] * _mm(rw.astype(BF16), wor_ref[...])
    o = _mm(mix.astype(BF16), wout_ref[...])
    x1 = x_ref[0] + gt1_ref[0] * _rms(o, gpost_ref[...])
    x1_ref[0] = x1
    h2 = _rms(x1, gpre_ref[...]) * (1.0 + sc2_ref[0]) + sh2_ref[0]
    h2_ref[0] = h2
    logits = _mmp(h2, wr_ref[...], 3) + br_ref[...]
    el = lax.broadcasted_iota(jnp.int32, logits.shape, 1).astype(F32)
    work = logits
    sel = jnp.zeros(logits.shape, jnp.bool_)
    vmax = None
    firsts = []
    for kk in range(TOP_K):
        mx = jnp.max(work, axis=1, keepdims=True)
        if kk == 0:
            vmax = mx
        first = jnp.min(jnp.where(work == mx, el, float(N_EXPERTS)), axis=1, keepdims=True)
        hit = el == first
        firsts.append(first)
        sel = sel | hit
        work = jnp.where(hit, -jnp.inf, work)
    e = jnp.where(sel, jnp.exp(logits - vmax), 0.0)
    gate = e / jnp.sum(e, axis=1, keepdims=True)
    gate_ref[0] = gate
    @pl.when((pl.program_id(0) == 0) & (pl.program_id(1) == 0))
    def _():
        cnt_ref[...] = jnp.zeros(cnt_ref.shape, F32)

    self = jnp.where(sel, 1.0, 0.0)
    ti = lax.broadcasted_iota(jnp.int32, (tm, tm), 0)
    tj = lax.broadcasted_iota(jnp.int32, (tm, tm), 1)
    before = _mm(jnp.where(tj < ti, 1.0, 0.0).astype(BF16), self.astype(BF16)) + cnt_ref[...]
    cnt_ref[...] = cnt_ref[...] + jnp.sum(self, axis=0, keepdims=True)
    cnt_out_ref[...] = cnt_ref[...]
    ln = lax.broadcasted_iota(jnp.int32, (tm, LANES), 1)
    eo = jnp.zeros((tm, LANES), F32)
    go = jnp.zeros((tm, LANES), F32)
    ro = jnp.zeros((tm, LANES), F32)
    for kk in range(TOP_K):
        hit = el == firsts[kk]
        gk = jnp.sum(jnp.where(hit, gate, 0.0), axis=1, keepdims=True)
        rk = jnp.sum(jnp.where(hit, before, 0.0), axis=1, keepdims=True)
        eo = jnp.where(ln == kk, firsts[kk], eo)
        go = jnp.where(ln == kk, gk, go)
        ro = jnp.where(ln == kk, rk, ro)
    eidx_ref[0] = eo.astype(jnp.int32)
    wgt_ref[0] = go
    rel_ref[0] = ro.astype(jnp.int32)


def _mix(att, y, g, bg, sga, sgb, x, gt1, sc2, sh2, wts, tm):
    B, T, D = x.shape
    tspec = lambda n: pl.BlockSpec((1, tm, n), lambda b, i: (b, i, 0))
    full = lambda a: pl.BlockSpec(a.shape, lambda b, i: (0,) * a.ndim)
    tok = lambda n, dt: jax.ShapeDtypeStruct((B, T, n), dt)
    return pl.pallas_call(
        functools.partial(_mix_kernel, tm=tm),
        grid=(B, T // tm),
        in_specs=[pl.BlockSpec((1, 8, tm, LANES), lambda b, i: (b, 0, i, 0)), tspec(HP), tspec(HP), tspec(HP),
                  tspec(D), tspec(D), tspec(D), _mod_spec(gt1, tm), _mod_spec(sc2, tm), _mod_spec(sh2, tm)]
                 + [full(w) for w in wts],
        out_specs=[tspec(D), tspec(D), tspec(N_EXPERTS), tspec(LANES), tspec(LANES), tspec(LANES),
                   pl.BlockSpec((1, N_EXPERTS), lambda b, i: (0, 0))],
        out_shape=[tok(D, F32), tok(D, F32), tok(N_EXPERTS, F32), tok(LANES, jnp.int32), tok(LANES, F32),
                   tok(LANES, jnp.int32), jax.ShapeDtypeStruct((1, N_EXPERTS), F32)],
        scratch_shapes=[pltpu.VMEM((1, N_EXPERTS), F32)],
        compiler_params=_params(("arbitrary", "arbitrary")),
        name="mix_router",
    )(att, y, g, bg, sga, sgb, x, gt1, sc2, sh2, *wts)


def _expert(xb, wgu, bgu, wd, bd, d_ff):
    wgu = wgu.astype(BF16)
    wd = wd.astype(BF16)
    hgu = _mm(xb, wgu) + bgu
    hg = jnp.minimum(hgu[:, :d_ff], SWIGLU_LIMIT)
    hl = jnp.clip(hgu[:, d_ff:], -SWIGLU_LIMIT, SWIGLU_LIMIT)
    act = hg * (1.0 / (1.0 + jnp.exp(-SWIGLU_ALPHA * hg))) * (hl + 1.0)
    return _mm(act.astype(BF16), wd) + bd


def _sc_mesh():
    return plsc.VectorSubcoreMesh(core_axis_name="core", subcore_axis_name="subcore")


def _sc_dispatch(h, pos_slots, n_rows, q):
    N, D = h.shape
    mesh = _sc_mesh()
    steps = N // (mesh.num_cores * SC_WINDOW)
    assert steps * mesh.num_cores * SC_WINDOW == N

    @pl.kernel(out_type=jax.ShapeDtypeStruct((n_rows, SC_PIECE), h.dtype), mesh=mesh, scratch_types=[])
    def scatter(x_hbm, *refs):
        idx_hbm, o_hbm = refs[:-1], refs[-1]
        base = lax.axis_index("core") * steps

        def body(x_vmem, *i_vmem):
            for iv in i_vmem:
                pltpu.sync_copy(x_vmem, o_hbm.at[iv.at[0]])

        pltpu.emit_pipeline(
            body,
            grid=(steps,),
            in_specs=[pl.BlockSpec((SC_WINDOW, SC_PIECE), index_map=lambda i: (base + i, q))]
                     + [pl.BlockSpec((1, SC_WINDOW), index_map=lambda i: (0, base + i))] * len(idx_hbm),
            out_specs=[],
            core_axis_name="subcore",
            dimension_semantics=(pltpu.PARALLEL,),
        )(x_hbm, *idx_hbm)

    return scatter(h, *pos_slots)


def _sc_gather(table, idx):
    M = idx.shape[1]
    mesh = _sc_mesh()
    steps = M // (mesh.num_cores * SC_WINDOW)
    assert steps * mesh.num_cores * SC_WINDOW == M

    @pl.kernel(out_type=jax.ShapeDtypeStruct((M, SC_PIECE), table.dtype), mesh=mesh)
    def gather(x_hbm, i_hbm, o_hbm):
        base = lax.axis_index("core") * steps

        def body(i_vmem, o_vmem):
            pltpu.sync_copy(x_hbm.at[i_vmem.at[0]], o_vmem)

        pltpu.emit_pipeline(
            body,
            grid=(steps,),
            in_specs=[pl.BlockSpec((1, SC_WINDOW), index_map=lambda i: (0, base + i))],
            out_specs=[pl.BlockSpec((SC_WINDOW, SC_PIECE), index_map=lambda i: (base + i, 0))],
            core_axis_name="subcore",
            dimension_semantics=(pltpu.PARALLEL,),
        )(i_hbm, o_hbm)

    return gather(table, idx)


def _moe_grouped_kernel(te_ref, nt_ref, *refs, d_ff, npiece):
    x_refs, (wgu_ref, bgu_ref, wd_ref, bd_ref) = refs[:npiece], refs[npiece:npiece + 4]
    o_refs, (wgu_b, wd_b) = refs[npiece + 4:2 * npiece + 4], refs[2 * npiece + 4:]
    g = pl.program_id(0)

    @pl.when((g == 0) | (te_ref[g] != te_ref[jnp.maximum(g - 1, 0)]))
    def _():
        wgu_b[...] = wgu_ref[0].astype(BF16)
        wd_b[...] = wd_ref[0].astype(BF16)

    @pl.when(g < nt_ref[0])
    def _():
        xb = jnp.concatenate([r[...] for r in x_refs], axis=1).astype(BF16)
        y = _expert(xb, wgu_b[...], bgu_ref[0], wd_b[...], bd_ref[0], d_ff)
        for q, o_ref in enumerate(o_refs):
            o_ref[...] = y[:, q * SC_PIECE:(q + 1) * SC_PIECE]


def _moe_grouped(xs, te, nt, wgu, bgu, wd, bd, TG):
    NP = xs[0].shape[0]
    E, D, F2 = wgu.shape
    npiece = len(xs)
    wmap = lambda g, te_, nt_: (te_[g], 0, 0)
    xmap = lambda g, te_, nt_: (jnp.minimum(g, nt_[0] - 1), 0)
    pspec = pl.BlockSpec((TG, SC_PIECE), xmap)
    return pl.pallas_call(
        functools.partial(_moe_grouped_kernel, d_ff=F2 // 2, npiece=npiece),
        grid_spec=pltpu.PrefetchScalarGridSpec(
            num_scalar_prefetch=2,
            grid=(NP // TG,),
            in_specs=[pspec] * npiece + [pl.BlockSpec((1, D, F2), wmap), pl.BlockSpec((1, 1, F2), wmap),
                                         pl.BlockSpec((1, F2 // 2, D), wmap), pl.BlockSpec((1, 1, D), wmap)],
            out_specs=[pspec] * npiece,
            scratch_shapes=[pltpu.VMEM((D, F2), BF16), pltpu.VMEM((F2 // 2, D), BF16)]),
        out_shape=[jax.ShapeDtypeStruct((NP, SC_PIECE), F32)] * npiece,
        compiler_params=_params(("arbitrary",)),
        name="moe_grouped",
    )(te, nt, *xs, wgu, bgu, wd, bd)


def _moe_combine_kernel(*refs, npiece):
    y_refs = refs[:TOP_K * npiece]
    w_ref, x1_ref, gt2_ref, gpost_ref, o_ref = refs[TOP_K * npiece:]
    w = w_ref[...]
    acc = None
    for k in range(TOP_K):
        yk = jnp.concatenate([y_refs[k * npiece + q][...] for q in range(npiece)], axis=1)
        term = w[:, k:k + 1] * yk
        acc = term if acc is None else acc + term
    o_ref[...] = x1_ref[...] + gt2_ref[0] * _rms(acc, gpost_ref[...])


def _moe_combine(yslot, wgt, x1, gt2, gpost, tm):
    N, D = x1.shape
    nb = N // tm
    tpb = N // gt2.shape[0]
    npiece = len(yslot)
    yspec = lambda k: pl.BlockSpec((tm, SC_PIECE), lambda i, k=k: (i + k * nb, 0))
    return pl.pallas_call(
        functools.partial(_moe_combine_kernel, npiece=npiece),
        grid=(nb,),
        in_specs=[yspec(k) for k in range(TOP_K) for _ in range(npiece)]
                 + [pl.BlockSpec((tm, LANES), lambda i: (i, 0)), pl.BlockSpec((tm, D), lambda i: (i, 0)),
                    pl.BlockSpec((1, 1, D), lambda i: ((i * tm) // tpb, 0, 0)), pl.BlockSpec((1, D), lambda i: (0, 0))],
        out_specs=pl.BlockSpec((tm, D), lambda i: (i, 0)),
        out_shape=jax.ShapeDtypeStruct((N, D), F32),
        compiler_params=_params(("arbitrary",)),
        name="moe_combine",
    )(*[yslot[q] for _ in range(TOP_K) for q in range(npiece)], wgt, x1, gt2, gpost)


def _moe_kernel(h_ref, gate_ref, x1_ref, gt2_ref, gpost_ref, wgu_ref, bgu_ref, wd_ref, bd_ref, o_ref, acc_ref,
                *, d_ff):
    e = pl.program_id(1)

    @pl.when(e == 0)
    def _():
        acc_ref[...] = jnp.zeros(acc_ref.shape, F32)

    contrib = _expert(h_ref[...].astype(BF16), wgu_ref[0], bgu_ref[0], wd_ref[0], bd_ref[0], d_ff)
    gate = gate_ref[...]
    el = lax.broadcasted_iota(jnp.int32, gate.shape, 1)
    ge = jnp.sum(jnp.where(el == e, gate, 0.0), axis=1, keepdims=True)
    acc_ref[...] += ge * contrib

    @pl.when(e == pl.num_programs(1) - 1)
    def _():
        o_ref[...] = x1_ref[...] + gt2_ref[0] * _rms(acc_ref[...], gpost_ref[...])


def _moe(h2, gate, x1, gt2, gpost, wgu, bgu, wd, bd, tm):
    N, D = h2.shape
    E, _, F2 = wgu.shape
    tspec = lambda n: pl.BlockSpec((tm, n), lambda i, e: (i, 0))
    if gt2.shape[1] == 1:
        tpb = N // gt2.shape[0]
        gspec = pl.BlockSpec((1, 1, D), lambda i, e: ((i * tm) // tpb, 0, 0))
    else:
        gspec = pl.BlockSpec((1, tm, D), lambda i, e: (i, 0, 0))
    return pl.pallas_call(
        functools.partial(_moe_kernel, d_ff=F2 // 2),
        grid=(N // tm, E),
        in_specs=[tspec(D), tspec(E), tspec(D), gspec, pl.BlockSpec((1, D), lambda i, e: (0, 0)),
                  pl.BlockSpec((1, D, F2), lambda i, e: (e, 0, 0)), pl.BlockSpec((1, 1, F2), lambda i, e: (e, 0, 0)),
                  pl.BlockSpec((1, F2 // 2, D), lambda i, e: (e, 0, 0)), pl.BlockSpec((1, 1, D), lambda i, e: (e, 0, 0))],
        out_specs=tspec(D),
        out_shape=jax.ShapeDtypeStruct((N, D), F32),
        scratch_shapes=[pltpu.VMEM((tm, D), F32)],
        compiler_params=_params(("arbitrary", "arbitrary")),
        name="moe",
    )(h2, gate, x1, gt2, gpost, wgu, bgu, wd, bd)


def _per_token(m, T):
    B, _, D = m.shape
    return jnp.broadcast_to(m, (B, T, D)).reshape(1, B * T, D)


def _layer(x, mods, past, s0, shift0, P, cfg):
    B, T, D = x.shape
    sh1, sc1, gt1, sh2, sc2, gt2 = mods
    tm = cfg["tm"]

    q, k, v, kb, vb, iq, ik, ikb, iw = _proj_a(x, sc1, sh1, P["g_pre_mix"], P["w_a"], P["ik_ln_w"], P["ik_ln_b"],
                                              cfg["tm_a"])
    zr, sga, sgb = _proj_b(x, sc1, sh1, P["g_pre_mix"], P["w_b"], tm)

    if past is not None:
        pk, pv, pik = past
        plen = pk.shape[1]
        kb_all = jnp.concatenate([pk.reshape(B, plen, 128).astype(BF16), kb], axis=1)
        vb_all = jnp.concatenate([pv.reshape(B, plen, 128).astype(BF16), vb], axis=1)
        ik_all = jnp.concatenate([jnp.pad(pik, ((0, 0), (0, 0), (0, 64))).astype(BF16), ikb], axis=1)
    else:
        plen = 0
        kb_all, vb_all, ik_all = kb, vb, ikb
    ltot = plen + T
    KB = cfg["KB"]
    lp = -(-ltot // KB) * KB
    if lp != ltot:
        padk = ((0, 0), (0, lp - ltot), (0, 0))
        kb_all, vb_all, ik_all = jnp.pad(kb_all, padk), jnp.pad(vb_all, padk), jnp.pad(ik_all, padk)
    vb1 = jnp.concatenate([vb_all, jnp.ones_like(vb_all)], axis=-1)
    att = _attention(iq, iw, q, ik_all, kb_all, vb1, R=cfg["R"], KB=KB, SB=cfg["SB"], HG=cfg["HG"], ltot=ltot,
                     q_off=plen)

    C = RW_CHUNK
    tp = -(-T // C) * C
    zr_p = zr if tp == T else jnp.pad(zr, ((0, 0), (0, tp - T), (0, 0)))
    shift_pad = _take_cols(shift0.reshape(B, RWKV_COLS), _COLS_ZR).reshape(B, 1, ZRW)
    r, lw, km, vv, kkn, bb, g, bg, lwt, kt, bt = _rwkv_prep(zr_p, shift_pad, P, min(cfg["tm_rw"], tp), T)
    a1, a2 = _rwkv_chunks(r, lw, km, vv, kkn, bb, lwt, kt, bt, C, cfg["hb"])
    s0t = jnp.pad(jnp.swapaxes(s0, 2, 3), ((0, 0), (0, 0), (0, 0), (0, 64)))
    nc = tp // C
    cb = min(cfg["cb"], nc)
    y, sT = _rwkv_scan(a1, a2, s0t, C, cb)
    s_new = jnp.swapaxes(sT[..., :64], 2, 3)
    zlast = zr[:, T - 1]
    inv = np.zeros((RWKV_COLS,), np.int32)
    inv[_COLS_ZR[0][_COLS_ZR[1]]] = np.nonzero(_COLS_ZR[1])[0]
    shift_new = jnp.take(zlast, jnp.asarray(inv), axis=1).reshape(B, 1, RWKV_COLS)
    if tp != T:
        y, g, bg = y[:, :T], g[:, :T], bg[:, :T]

    wts = [P[n] for n in ("w_o_att", "w_o_rwkv", "w_out", "ln_x_w", "ln_x_b", "g_post_mix", "g_pre_ffn",
                          "w_router", "b_router")]
    x1, h2, gate, eidx, wgt, rel, cnt = _mix(att, y, g, bg, sga, sgb, x, gt1, sc2, sh2, wts, cfg["tm_mix"])

    N = B * T
    tmm = min(cfg["tm_moe"], N)
    if cfg["routed"]:
        TG = cfg["TG"]
        ntm = N * TOP_K // TG + N_EXPERTS
        cnt_i = cnt[0].astype(jnp.int32)
        padded = (cnt_i + TG - 1) // TG * TG
        ends = jnp.cumsum(padded)
        off = ends - padded
        nt = ends[-1] // TG
        gi = jnp.minimum(jnp.arange(ntm, dtype=jnp.int32), nt - 1)
        te = jnp.sum((ends // TG)[None, :] <= gi[:, None], axis=1).astype(jnp.int32)
        e4 = eidx.reshape(N, LANES)[:, :TOP_K]
        pos = jnp.take(off, e4) + rel.reshape(N, LANES)[:, :TOP_K]
        pos = jnp.clip(pos, 0, ntm * TG - 1)
        pos_t = pos.T
        npiece = D // SC_PIECE
        h2f = h2.reshape(N, D)
        xs = [_sc_dispatch(h2f, [pos_t[k:k + 1] for k in range(TOP_K)], ntm * TG, q) for q in range(npiece)]
        ys = _moe_grouped(xs, te, nt.reshape(1), P["w_gu"], P["b_gu"], P["w_down"], P["b_down"], TG)
        idx_c = pos_t.reshape(1, TOP_K * N)
        yslot = [_sc_gather(ys[q], idx_c) for q in range(npiece)]
        out = _moe_combine(yslot, wgt.reshape(N, LANES), x1.reshape(N, D), gt2, P["g_post_ffn"], tm)
    else:
        if T % tmm == 0:
            gt2m = gt2
        else:
            gt2m = _per_token(gt2, T).reshape(N // tmm, tmm, D)
        out = _moe(h2.reshape(N, D), gate.reshape(N, N_EXPERTS), x1.reshape(N, D), gt2m, P["g_post_ffn"],
                   P["w_gu"], P["b_gu"], P["w_down"], P["b_down"], tmm)
    return (out.reshape(B, T, D), k.reshape(B, T, ATT_KV_HEADS, HEAD_DIM), v.reshape(B, T, ATT_KV_HEADS, HEAD_DIM),
            ik, s_new, shift_new)


def _prep_weights(l, w_in, ik_ln_w, ik_ln_b, mu_rwkv, w0, w_up, a0, a_up, g_up, k_k, k_a, r_k, ln_x_w, ln_x_b,
                  w_o_att, w_o_rwkv, w_out, w_router, b_router, w_gu, b_gu, w_down, b_down,
                  g_pre_mix, g_post_mix, g_pre_ffn, g_post_ffn):
    row = lambda a: a.reshape(1, -1)
    hp = lambda a: _take_cols(row(a), _HEAD_IDX)
    P = {}
    P["w_a"] = _take_cols(w_in[l], _COLS_A).astype(BF16)
    P["w_b"] = _take_cols(w_in[l], _COLS_B).astype(BF16)
    P["ik_ln_w"] = jnp.pad(row(ik_ln_w[l]), ((0, 0), (0, 64)))
    P["ik_ln_b"] = jnp.pad(row(ik_ln_b[l]), ((0, 0), (0, 64)))
    P["mu"] = _take_cols(row(mu_rwkv[l]), _COLS_ZR)
    P["w0"], P["a0"], P["k_k"], P["k_a"] = hp(w0[l]), hp(a0[l]), hp(k_k[l]), hp(k_a[l])
    P["r_k"] = hp(r_k[l].reshape(-1))
    P["ln_x_w"], P["ln_x_b"] = hp(ln_x_w[l]), hp(ln_x_b[l])
    z64 = jnp.zeros((64, HP), F32)
    P["w_up"] = jnp.concatenate([_take_cols(w_up[l], _HEAD_IDX), z64], axis=0)
    P["a_up"] = jnp.concatenate([z64, _take_cols(a_up[l], _HEAD_IDX)], axis=0)
    P["g_up"] = _take_cols(g_up[l], _HEAD_IDX)
    P["w_o_att"] = _take_rows(w_o_att[l], _ATT_ROW_IDX).astype(BF16)
    P["w_o_rwkv"] = _take_rows(w_o_rwkv[l], _HEAD_IDX).astype(BF16)
    P["w_out"] = w_out[l].astype(BF16)
    P["w_router"] = w_router[l]
    P["b_router"] = row(b_router[l])
    P["w_gu"] = w_gu[l]
    P["b_gu"] = b_gu[l].reshape(N_EXPERTS, 1, -1)
    P["w_down"] = w_down[l]
    P["b_down"] = b_down[l].reshape(N_EXPERTS, 1, -1)
    P["g_pre_mix"], P["g_post_mix"] = row(g_pre_mix[l]), row(g_post_mix[l])
    P["g_pre_ffn"], P["g_post_ffn"] = row(g_pre_ffn[l]), row(g_post_ffn[l])
    return P


CFG_PROMPT = dict(tm=256, tm_a=1024, tm_mix=512, R=256,KB=512, SB=256, HG=2,tm_rw=256, hb=8, cb=2, tm_moe=1024, routed=True, TG=512)
CFG_SAMPLE = dict(tm=32, tm_a=32, tm_mix=32, R=32, KB=512, SB=256, HG=2,tm_rw=128, hb=8, cb=1, tm_moe=512, routed=False)


def kernel(x_prompt, x_sample, c_prompt, c_sample, cache_k, cache_v, cache_idx_k, state_rwkv, state_shift, w_ada, b_ada, g_pre_mix, g_post_mix, g_pre_ffn, g_post_ffn, w_in, ik_ln_w, ik_ln_b, mu_rwkv, w0, w_up, a0, a_up, g_up, k_k, k_a, r_k, ln_x_w, ln_x_b, w_o_att, w_o_rwkv, w_out, w_router, b_router, w_gu, b_gu, w_down, b_down):
    depth = w_in.shape[0]
    bp, tp_, D = x_prompt.shape
    bs, ts, _ = x_sample.shape
    y_p, y_s = x_prompt, x_sample
    st_p = [[] for _ in range(5)]
    st_s = [[] for _ in range(5)]
    nc_all = bp + bs
    npad = -(-nc_all // 8) * 8
    c_all = jnp.pad(jnp.concatenate([c_prompt, c_sample], axis=0), ((0, npad - nc_all), (0, 0)))
    for l in range(depth):
        P = _prep_weights(l, w_in, ik_ln_w, ik_ln_b, mu_rwkv, w0, w_up, a0, a_up, g_up, k_k, k_a, r_k, ln_x_w,
                          ln_x_b, w_o_att, w_o_rwkv, w_out, w_router, b_router, w_gu, b_gu, w_down, b_down,
                          g_pre_mix, g_post_mix, g_pre_ffn, g_post_ffn)
        ada = _ada(c_all, w_ada[l], b_ada[l])
        mods_p = [m[:bp, None, :] for m in jnp.split(ada, 6, axis=-1)]
        mods_s = [m[bp:nc_all, None, :] for m in jnp.split(ada, 6, axis=-1)]
        zero_state = jnp.zeros((bp, RWKV_HEADS, RWKV_HEAD, RWKV_HEAD), F32)
        zero_shift = jnp.zeros((bp, 1, RWKV_COLS), F32)
        outs_p = _layer(y_p, mods_p, None, zero_state, zero_shift, P, CFG_PROMPT)
        outs_s = _layer(y_s, mods_s, (cache_k[l], cache_v[l], cache_idx_k[l]), state_rwkv[l], state_shift[l], P,
                        CFG_SAMPLE)
        y_p, y_s = outs_p[0], outs_s[0]
        for lst, val in zip(st_p, outs_p[1:]):
            lst.append(val)
        for lst, val in zip(st_s, outs_s[1:]):
            lst.append(val)
    sp = [jnp.stack(v, axis=0) for v in st_p]
    ss = [jnp.stack(v, axis=0) for v in st_s]
    return (y_p, y_s, sp[0], sp[1], sp[2], sp[3], sp[4], ss[0], ss[1], ss[2], ss[3], ss[4])
```

```python
import functools

import numpy as np
import jax
import jax.numpy as jnp
from jax import lax
from jax.experimental import pallas as pl
from jax.experimental.pallas import tpu as pltpu
from jax.experimental.pallas import tpu_sc as plsc

F32 = jnp.float32
BF16 = jnp.bfloat16
HI = lax.Precision.HIGHEST

CHUNK = 64
ATT_HEADS = 8
ATT_KV_HEADS = 2
HEAD_DIM = 64
IDX_HEADS = 8
IDX_DIM = 64
IDX_SCALE = (IDX_DIM ** -0.5) * (IDX_HEADS ** -0.5)
TOPK_MAX = 256
RWKV_HEADS = 8
RWKV_HEAD = 64
RWKV_WIDTH = RWKV_HEADS * RWKV_HEAD
DECAY_LORA = 64
AAA_LORA = 64
GATE_LORA = 128
RWKV_COLS = 3 * RWKV_WIDTH + DECAY_LORA + AAA_LORA + GATE_LORA
GN_EPS = 64e-5
L2_EPS = 1e-24
N_EXPERTS = 32
TOP_K = 4
SWIGLU_LIMIT = 7.0
SWIGLU_ALPHA = 1.702
NORM_EPS = 1e-6
LN_EPS = 1e-6

LANES = 128
VMEM_LIMIT = 56 * 1024 * 1024

HP = 8 * LANES
RW_CHUNK = 128
NEG_BIG = -1e30
LOG2E = 1.4426950408889634
KEY_NEG_INF = -2139095041
INT_MIN = -2147483648
SC_WINDOW = 128
SC_PIECE = 256
LO_GROUP = 4
RW_PASSES = (1, 1, 1)


def _params(sem):
    return pltpu.CompilerParams(dimension_semantics=sem, vmem_limit_bytes=VMEM_LIMIT)


def _nt(a, b, precision=None):
    return lax.dot_general(a, b, (((1,), (1,)), ((), ())), precision=precision,
                           preferred_element_type=F32)


def _mm(a, b, precision=None):
    return jnp.dot(a, b, precision=precision, preferred_element_type=F32)


def _split_bf16(x, terms):
    out = []
    for _ in range(terms):
        p = x.astype(BF16)
        out.append(p)
        x = x - p.astype(F32)
    return out


def _mmp(a, b, passes, nt=False):
    dot = _nt if nt else _mm
    if passes == 1:
        return dot(a.astype(BF16), b.astype(BF16))
    ah, al = _split_bf16(a, 2)
    bh, bl = _split_bf16(b, 2)
    return dot(ah, bh) + (dot(ah, bl) + dot(al, bh))


def _rms(x, g):
    return x * lax.rsqrt(jnp.mean(x * x, axis=-1, keepdims=True) + NORM_EPS) * g


def _headpad_idx(seg_off, lane_off_fn=lambda h: 0):
    idx = np.zeros((HP,), np.int32)
    ok = np.zeros((HP,), bool)
    for h in range(8):
        d0 = h * LANES + lane_off_fn(h)
        idx[d0:d0 + 64] = seg_off + h * 64 + np.arange(64)
        ok[d0:d0 + 64] = True
    return idx, ok


def _plain_idx(seg_off, n, width):
    idx = np.zeros((width,), np.int32)
    ok = np.zeros((width,), bool)
    idx[:n] = seg_off + np.arange(n)
    ok[:n] = True
    return idx, ok


def _cat(parts):
    return np.concatenate([p[0] for p in parts]), np.concatenate([p[1] for p in parts])


_O_Q, _O_K, _O_V, _O_IQ, _O_IK, _O_IW, _O_ZR, _O_GA, _O_GB = 0, 512, 640, 768, 1280, 1344, 1352, 3144, 4168
_COLS_A = _cat([_headpad_idx(_O_Q, lambda h: (h // 4) * 64), _plain_idx(_O_K, 128, 128), _plain_idx(_O_V, 128, 128),
                _headpad_idx(_O_IQ), _plain_idx(_O_IK, 64, 128), _plain_idx(_O_IW, 8, 128)])
_COLS_ZR = _cat([_headpad_idx(0), _headpad_idx(512), _headpad_idx(1024), _plain_idx(1536, 128, 128),
                 _plain_idx(1664, 128, 128)])
ZRW = _COLS_ZR[0].shape[0]
_COLS_B = _cat([(_COLS_ZR[0] + _O_ZR, _COLS_ZR[1]), _plain_idx(_O_GA, 1024, 1024), _plain_idx(_O_GB, 1024, 1024)])
NA = _COLS_A[0].shape[0]
NB = _COLS_B[0].shape[0]
_HEAD_IDX = _headpad_idx(0)
_ATT_ROW_IDX = _headpad_idx(0, lambda h: (h // 4) * 64)


def _take_cols(w, cols):
    idx, ok = cols
    return jnp.where(jnp.asarray(ok)[None, :], jnp.take(w, jnp.asarray(idx), axis=1), 0.0)


def _take_rows(w, cols):
    idx, ok = cols
    return jnp.where(jnp.asarray(ok)[:, None], jnp.take(w, jnp.asarray(idx), axis=0), 0.0)


def _mod_spec(m, tm):
    assert m.shape[1] == 1
    return pl.BlockSpec((1, 1, m.shape[2]), lambda b, i: (b, 0, 0))


def _ada_kernel(c_ref, w_ref, b_ref, o_ref):
    c = c_ref[...]
    s = c * (1.0 / (1.0 + jnp.exp(-c)))
    o_ref[...] = _mm(s, w_ref[...], HI) + b_ref[...]


def _ada(c, w, b):
    n, d = c.shape
    nout = w.shape[1]
    bn = 768
    return pl.pallas_call(
        _ada_kernel,
        grid=(nout // bn,),
        in_specs=[pl.BlockSpec((n, d), lambda j: (0, 0)),
                  pl.BlockSpec((d, bn), lambda j: (0, j)),
                  pl.BlockSpec((1, bn), lambda j: (0, j))],
        out_specs=pl.BlockSpec((n, bn), lambda j: (0, j)),
        out_shape=jax.ShapeDtypeStruct((n, nout), F32),
        compiler_params=_params(("arbitrary",)),
        name="ada",
    )(c, w, b.reshape(1, nout))


def _modulated(x_ref, sc_ref, sh_ref, g_ref):
    h = _rms(x_ref[0], g_ref[...])
    return (h * (1.0 + sc_ref[0]) + sh_ref[0]).astype(BF16)


def _proj_a_kernel(x_ref, sc_ref, sh_ref, g_ref, w_ref, lnw_ref, lnb_ref,
                   q_ref, k_ref, v_ref, kb_ref, vb_ref, iq_ref, ik_ref, ikb_ref, iw_ref):
    hb = _modulated(x_ref, sc_ref, sh_ref, g_ref)
    for h in range(8):
        zq = _mm(hb, w_ref[:, h * LANES:(h + 1) * LANES])
        q_ref[0, h] = (zq * (HEAD_DIM ** -0.5 * LOG2E)).astype(BF16)
    kv = _mm(hb, w_ref[:, 1024:1280])
    k = kv[:, :128]
    v = kv[:, 128:]
    k_ref[0] = k
    v_ref[0] = v
    kb_ref[0] = k.astype(BF16)
    vb_ref[0] = v.astype(BF16)
    for h in range(8):
        zi = _mm(hb, w_ref[:, 1280 + h * LANES:1280 + (h + 1) * LANES])
        iq_ref[0, h] = zi.astype(BF16)
    t = _mm(hb, w_ref[:, 2304:2560])
    ik = t[:, :128]
    lane = lax.broadcasted_iota(jnp.int32, ik.shape, 1)
    valid = lane < IDX_DIM
    mu = jnp.sum(ik, axis=-1, keepdims=True) * (1.0 / IDX_DIM)
    d = jnp.where(valid, ik - mu, 0.0)
    var = jnp.sum(d * d, axis=-1, keepdims=True) * (1.0 / IDX_DIM)
    ikn = d * lax.rsqrt(var + LN_EPS) * lnw_ref[...] + lnb_ref[...]
    ik_ref[0] = ikn[:, :IDX_DIM]
    ikb_ref[0] = ikn.astype(BF16)
    iw_ref[0] = t[:, 128:136] * IDX_SCALE


def _proj_a(x, sc, sh, g, w, lnw, lnb, tm):
    B, T, D = x.shape
    nt = T // tm
    tok = lambda n, dt: jax.ShapeDtypeStruct((B, T, n), dt)
    hm = lambda dt: jax.ShapeDtypeStruct((B, 8, T, LANES), dt)
    tspec = lambda n: pl.BlockSpec((1, tm, n), lambda b, i: (b, i, 0))
    hspec = pl.BlockSpec((1, 8, tm, LANES), lambda b, i: (b, 0, i, 0))
    full = lambda a: pl.BlockSpec(a.shape, lambda b, i: (0,) * a.ndim)
    return pl.pallas_call(
        _proj_a_kernel,
        grid=(B, nt),
        in_specs=[tspec(D), _mod_spec(sc, tm), _mod_spec(sh, tm), full(g), full(w), full(lnw), full(lnb)],
        out_specs=[hspec, tspec(128), tspec(128), tspec(128), tspec(128), hspec, tspec(IDX_DIM), tspec(128),
                   tspec(8)],
        out_shape=[hm(BF16), tok(128, F32), tok(128, F32), tok(128, BF16), tok(128, BF16), hm(BF16),
                   tok(IDX_DIM, F32), tok(128, BF16), tok(8, F32)],
        compiler_params=_params(("arbitrary", "arbitrary")),
        name="proj_att",
    )(x, sc, sh, g, w, lnw, lnb)


def _proj_b_kernel(x_ref, sc_ref, sh_ref, g_ref, w_ref, zr_ref, ga_ref, gb_ref):
    hb = _modulated(x_ref, sc_ref, sh_ref, g_ref)
    for j in range(ZRW // 256):
        zr_ref[0, :, j * 256:(j + 1) * 256] = _mm(hb, w_ref[:, j * 256:(j + 1) * 256])
    for j in range(4):
        za = _mm(hb, w_ref[:, ZRW + j * 256:ZRW + (j + 1) * 256])
        ga_ref[0, :, j * 256:(j + 1) * 256] = 1.0 / (1.0 + jnp.exp(-za))
        zb = _mm(hb, w_ref[:, ZRW + 1024 + j * 256:ZRW + 1024 + (j + 1) * 256])
        gb_ref[0, :, j * 256:(j + 1) * 256] = 1.0 / (1.0 + jnp.exp(-zb))


def _proj_b(x, sc, sh, g, w, tm):
    B, T, D = x.shape
    nt = T // tm
    tok = lambda n: jax.ShapeDtypeStruct((B, T, n), F32)
    tspec = lambda n: pl.BlockSpec((1, tm, n), lambda b, i: (b, i, 0))
    full = lambda a: pl.BlockSpec(a.shape, lambda b, i: (0,) * a.ndim)
    return pl.pallas_call(
        _proj_b_kernel,
        grid=(B, nt),
        in_specs=[tspec(D), _mod_spec(sc, tm), _mod_spec(sh, tm), full(g), full(w)],
        out_specs=[tspec(ZRW), tspec(1024), tspec(1024)],
        out_shape=[tok(ZRW), tok(1024), tok(1024)],
        compiler_params=_params(("arbitrary", "arbitrary")),
        name="proj_rwkv",
    )(x, sc, sh, g, w)


def _sortable_to_float(t):
    bits = t ^ (lax.shift_right_arithmetic(t, 31) & 0x7FFFFFFF)
    return lax.bitcast_convert_type(bits, F32)


def _top16(x):
    bits = lax.bitcast_convert_type(x, jnp.int32) & jnp.int32(-65536)
    return lax.bitcast_convert_type(bits, F32).astype(BF16)


def _attn_kernel(iq_ref, iw_ref, q_ref, ik_ref, k_ref, v_ref, o_ref, s_ref, sb_ref, m_ref, acc_ref,
                 *, R, KB, SB, HG, ltot, q_off, topk):
    j = pl.program_id(1)
    q0 = q_off + j * R
    pos = q0 + lax.broadcasted_iota(jnp.int32, (R, 1), 0)
    qchunk = lax.shift_right_logical(pos, 6)
    kend = jnp.minimum(ltot, (lax.shift_right_logical(q0 + R - 1, 6) + 1) * CHUNK)
    nkb = lax.shift_right_logical(kend + KB - 1, KB.bit_length() - 1)
    lane_k = lax.broadcasted_iota(jnp.int32, (R, KB), 1)
    nch = KB // LANES

    iq_all = iq_ref[0].reshape(8 * R, LANES)
    iw = iw_ref[0]
    iwb = [jnp.broadcast_to(iw[:, h:h + 1], (R, SB)) for h in range(8)]
    lane_s = lax.broadcasted_iota(jnp.int32, (R, SB), 1)
    nsb = KB // SB
    key_lim = jnp.minimum((qchunk + 1) * CHUNK, ltot)

    def score_body(kb, carry):
        for u in range(nsb):
            off = pl.multiple_of(kb * KB + u * SB, SB)
            s_all = _mm(iq_all, ik_ref[0, kb * nsb + u])
            acc = jnp.zeros((R, SB), F32)
            for h in range(8):
                acc = acc + jnp.maximum(s_all[h * R:(h + 1) * R], 0.0) * iwb[h]
            kidx = off + lane_s
            sc = jnp.where(kidx < key_lim, acc, -jnp.inf)
            s_ref[kb, :, u * SB:(u + 1) * SB] = sc
            sb_ref[kb, :, u * SB:(u + 1) * SB] = _top16(sc)
        return carry

    lax.fori_loop(0, nkb, score_body, 0)

    def count_ge(ref, cand, dt):
        one, zero = jnp.ones((), dt), jnp.zeros((), dt)

        def body(kb, part):
            blk = ref[kb]
            m = jnp.where(blk >= cand, one, zero)
            for c in range(nch):
                part = part + m[:, c * LANES:(c + 1) * LANES]
            return part
        part = lax.fori_loop(0, nkb, body, jnp.zeros((R, LANES), dt))
        return jnp.sum(part.astype(F32), axis=1, keepdims=True)

    kf = float(topk)
    c0 = count_ge(sb_ref, jnp.zeros((R, 1), BF16), BF16)
    t0 = jnp.where(c0 >= kf, 0, INT_MIN).astype(jnp.int32)
    ex0 = jnp.where(c0 == kf, 1.0, 0.0)

    def accept(c, cand, cnt):
        t, ex = c
        ok = cnt >= kf
        return jnp.where(ok, cand, t), jnp.where(ok & (cnt == kf), 1.0, ex)

    def hi_body(i, c):
        cand = c[0] + lax.shift_left(jnp.int32(1), 30 - i)
        return accept(c, cand, count_ge(sb_ref, _top16(_sortable_to_float(cand)), BF16))

    def lo_pass(i, c):
        cand = c[0] + lax.shift_left(jnp.int32(1), 15 - i)
        return accept(c, cand, count_ge(s_ref, _sortable_to_float(cand), F32))

    def lo_group(c):
        gi, t, ex, _ = c
        tc = (t, ex)
        for jj in range(LO_GROUP):
            tc = lo_pass(gi * LO_GROUP + jj, tc)
        return gi + 1, tc[0], tc[1], (jnp.min(tc[1]) > 0.5).astype(jnp.int32)

    t, ex = lax.fori_loop(0, 15, hi_body, (t0, ex0))
    _, t, _, _ = lax.while_loop(lambda c: (c[0] < 16 // LO_GROUP) & (c[3] == 0), lo_group,
                                (jnp.int32(0), t, ex, (jnp.min(ex) > 0.5).astype(jnp.int32)))
    all_finite = t <= KEY_NEG_INF
    tau = jnp.where(all_finite, -jnp.inf, _sortable_to_float(jnp.maximum(t, KEY_NEG_INF)))

    def count_gt_eq(_):
        def body(kb, carry):
            pg, pe = carry
            for c in range(nch):
                blk = s_ref[kb, :, c * LANES:(c + 1) * LANES]
                pg = pg + jnp.where(blk > tau, 1.0, 0.0)
                pe = pe + jnp.where(blk == tau, 1.0, 0.0)
            return pg, pe
        z = jnp.zeros((R, LANES), F32)
        pg, pe = lax.fori_loop(0, nkb, body, (z, z))
        return jnp.sum(pg, axis=1, keepdims=True), jnp.sum(pe, axis=1, keepdims=True)

    cnt_gt, cnt_eq = count_gt_eq(0)
    need = kf - cnt_gt
    tie = (cnt_eq > need) & jnp.logical_not(all_finite)
    any_tie = jnp.max(jnp.where(tie, 1.0, 0.0)) > 0.0

    def tie_bound():
        def count_eq_below(x):
            def body(kb, part):
                blk = s_ref[kb]
                kidx = kb * KB + lane_k
                m = jnp.where((blk == tau) & (kidx < x), 1.0, 0.0)
                for c in range(nch):
                    part = part + m[:, c * LANES:(c + 1) * LANES]
                return part
            part = lax.fori_loop(0, nkb, body, jnp.zeros((R, LANES), F32))
            return jnp.sum(part, axis=1, keepdims=True)

        nbits = int(ltot).bit_length()

        def body(i, x):
            cand = x + lax.shift_left(jnp.int32(1), nbits - 1 - i)
            ok = count_eq_below(cand) < need
            return jnp.where(ok, cand, x)
        x = lax.fori_loop(0, nbits, body, jnp.zeros((R, 1), jnp.int32))
        return x + 1

    big = jnp.full((R, 1), 1 << 30, jnp.int32)
    bound = lax.cond(any_tie, lambda: jnp.where(tie, tie_bound(), big), lambda: big)

    def bias_body(kb, carry):
        blk = s_ref[kb]
        kidx = kb * KB + lane_k
        sel = (blk > tau) | ((blk == tau) & (kidx < bound))
        sel = sel & (blk > -jnp.inf)
        s_ref[kb] = jnp.where(sel, 0.0, NEG_BIG)
        return carry

    tau_eff = jnp.where(all_finite, -float(np.finfo(np.float32).max), tau)

    def bias_fast(kb, carry):
        s_ref[kb] = jnp.where(s_ref[kb] >= tau_eff, 0.0, NEG_BIG)
        return carry

    @pl.when(any_tie)
    def _():
        lax.fori_loop(0, nkb, bias_body, 0)

    @pl.when(jnp.logical_not(any_tie))
    def _():
        lax.fori_loop(0, nkb, bias_fast, 0)

    q_all = q_ref[0].reshape(8 * R, LANES)
    m_ref[...] = jnp.full(m_ref.shape, NEG_BIG, F32)
    acc_ref[...] = jnp.zeros(acc_ref.shape, F32)

    def att_body(kb, carry):
        for u in range(nsb):
            off = pl.multiple_of(kb * KB + u * SB, SB)
            kblk = k_ref[0, kb * nsb + u]
            vblk = v_ref[0, pl.ds(off, SB), :]
            bias = s_ref[kb, :, u * SB:(u + 1) * SB]
            logits = _mm(q_all, kblk)
            nch = SB // LANES
            for h0 in range(0, 8, HG):
                hs = list(range(h0, h0 + HG))
                lg = [logits[h * R:(h + 1) * R] + bias for h in hs]
                cm = [functools.reduce(jnp.maximum, [l[:, c * LANES:(c + 1) * LANES] for c in range(nch)])
                      for l in lg]
                m_old = [m_ref[h] for h in hs]
                m_new = [jnp.maximum(mo, jnp.max(c, axis=1, keepdims=True)) for mo, c in zip(m_old, cm)]
                alpha = [jnp.exp2(mo - mn) for mo, mn in zip(m_old, m_new)]
                p = [jnp.exp2(l - jnp.concatenate([mn] * nch, axis=1)).astype(BF16) for l, mn in zip(lg, m_new)]
                pv = [_mm(pp, vblk) for pp in p]
                for i, h in enumerate(hs):
                    acc_ref[h] = jnp.concatenate([alpha[i], alpha[i]], axis=1) * acc_ref[h] + pv[i]
                    m_ref[h] = m_new[i]
        return carry

    lax.fori_loop(0, nkb, att_body, 0)
    for h in range(8):
        a = acc_ref[h]
        o_ref[0, h] = (a[:, :LANES] / a[:, LANES:]).astype(BF16)


def _attention(iq, iw, q, ikb, kb, vb1, *, R, KB, SB, HG, ltot, q_off):
    B, _, Sq, _ = q.shape
    Lp = kb.shape[1]
    topk = min(TOPK_MAX, ltot // 4)
    assert Lp % KB == 0 and KB % SB == 0 and KB >= topk and Sq % R == 0
    to_blocks = lambda a: jnp.swapaxes(a.reshape(B, Lp // SB, SB, LANES), 2, 3)
    ikb, kb = to_blocks(ikb), to_blocks(kb)
    hspec = pl.BlockSpec((1, 8, R, LANES), lambda b, i: (b, 0, i, 0))
    kspec = pl.BlockSpec((1, Lp // SB, LANES, SB), lambda b, i: (b, 0, 0, 0))
    vspec = pl.BlockSpec((1, Lp, 2 * LANES), lambda b, i: (b, 0, 0))
    kern = functools.partial(_attn_kernel, R=R, KB=KB, SB=SB, HG=HG, ltot=ltot, q_off=q_off, topk=topk)
    return pl.pallas_call(
        kern,
        grid=(B, Sq // R),
        in_specs=[hspec, pl.BlockSpec((1, R, 8), lambda b, i: (b, i, 0)), hspec, kspec, kspec, vspec],
        out_specs=hspec,
        out_shape=jax.ShapeDtypeStruct((B, 8, Sq, LANES), BF16),
        scratch_shapes=[pltpu.VMEM((Lp // KB, R, KB), F32), pltpu.VMEM((Lp // KB, R, KB), BF16),
                        pltpu.VMEM((8, R, LANES), F32),
                        pltpu.VMEM((8, R, 2 * LANES), F32)],
        compiler_params=_params(("arbitrary", "arbitrary")),
        name="dsa_attention",
    )(iq, iw, q, ikb, kb, vb1)


def _head_sum(x):
    parts = []
    for h in range(8):
        s = jnp.sum(x[:, h * LANES:(h + 1) * LANES], axis=1, keepdims=True)
        parts.append(jnp.broadcast_to(s, (x.shape[0], LANES)))
    return jnp.concatenate(parts, axis=1)


def _rwkv_prep_kernel(z_ref, sh0_ref, mu_ref, w0_ref, wup_ref, a0_ref, aup_ref, gup_ref, kk_ref, ka_ref, rk_ref,
                      r_o, lw_o, k_o, v_o, kkn_o, b_o, g_o, bg_o, lwt_o, kt_o, bt_o, carry_ref,
                      *, tm, t_valid):
    i = pl.program_id(1)

    @pl.when(i == 0)
    def _():
        carry_ref[...] = sh0_ref[0]

    z = z_ref[0]
    row = lax.broadcasted_iota(jnp.int32, (tm, 1), 0)
    prev = jnp.where(row == 0, carry_ref[...], pltpu.roll(z, 1, axis=0))
    carry_ref[...] = z[tm - 1:tm, :]
    zm = z + (prev - z) * mu_ref[...]
    live = i * tm + row < t_valid
    zm = jnp.where(live, zm, 0.0)
    r = zm[:, 0:HP]
    k = zm[:, HP:2 * HP]
    v = zm[:, 2 * HP:3 * HP]
    wa = zm[:, 3 * HP:3 * HP + LANES]
    gd = zm[:, 3 * HP + LANES:3 * HP + 2 * LANES]
    w_raw = w0_ref[...] + _mmp(jnp.tanh(wa), wup_ref[...], 3)
    lw = (-float(np.exp(-0.5))) / (1.0 + jnp.exp(-w_raw))
    lw = jnp.where(live, lw, 0.0)
    a = 1.0 / (1.0 + jnp.exp(-(a0_ref[...] + _mmp(wa, aup_ref[...], 3))))
    g = _mmp(1.0 / (1.0 + jnp.exp(-gd)), gup_ref[...], 3)
    kk = k * kk_ref[...]
    kk = kk * lax.rsqrt(jnp.maximum(_head_sum(kk * kk), L2_EPS))
    k_mod = k * (1.0 + (a - 1.0) * ka_ref[...])
    b = kk * a
    bonus = _head_sum(r * k_mod * rk_ref[...]) * v
    r_o[0] = r.astype(BF16)
    lw_o[0] = lw
    k_o[0] = k_mod.astype(BF16)
    v_o[0] = v.astype(BF16)
    kkn_o[0] = kk.astype(BF16)
    b_o[0] = b.astype(BF16)
    g_o[0] = g
    bg_o[0] = bonus * g
    lwt_o[0] = lw.T
    kt_o[0] = k_mod.T.astype(BF16)
    bt_o[0] = b.T.astype(BF16)


def _rwkv_prep(zr, shift0, prm, tm, t_valid):
    B, T, _ = zr.shape
    tspec = pl.BlockSpec((1, tm, HP), lambda b, i: (b, i, 0))
    fspec = pl.BlockSpec((1, HP, tm), lambda b, i: (b, 0, i))
    full = lambda a: pl.BlockSpec(a.shape, lambda b, i: (0,) * a.ndim)
    tok = jax.ShapeDtypeStruct((B, T, HP), F32)
    feat = jax.ShapeDtypeStruct((B, HP, T), F32)
    tokb = jax.ShapeDtypeStruct((B, T, HP), BF16)
    featb = jax.ShapeDtypeStruct((B, HP, T), BF16)
    names =("mu", "w0", "w_up", "a0", "a_up", "g_up", "k_k", "k_a", "r_k")
    ws = [prm[n] for n in names]
    return pl.pallas_call(
        functools.partial(_rwkv_prep_kernel, tm=tm, t_valid=t_valid),
        grid=(B, T // tm),
        in_specs=[pl.BlockSpec((1, tm, ZRW), lambda b, i: (b, i, 0)),
                  pl.BlockSpec((1, 1, ZRW), lambda b, i: (b, 0, 0))] + [full(w) for w in ws],
        out_specs=[tspec] * 8 + [fspec] * 3,
        out_shape=[tokb, tok, tokb, tokb, tokb, tokb, tok, tok, feat, featb, featb],
        scratch_shapes=[pltpu.VMEM((1, ZRW), F32)],
        compiler_params=_params(("arbitrary", "arbitrary")),
        name="rwkv_prep",
    )(zr, shift0, *ws)


def _rwkv_chunk_kernel(r_ref, lw_ref, k_ref, v_ref, kk_ref, b_ref, lwt_ref, kt_ref, bt_ref, a1_ref, a2_ref,
                       *, C, hb, pg, pc, po):
    ri = lax.broadcasted_iota(jnp.int32, (C, C), 0)
    ci = lax.broadcasted_iota(jnp.int32, (C, C), 1)
    low_incl = jnp.where(ri >= ci, 1.0, 0.0).astype(BF16)
    up_incl = jnp.where(ri <= ci, 1.0, 0.0).astype(BF16)
    strict = ri > ci
    incl = ri >= ci
    eye64 = jnp.where(lax.broadcasted_iota(jnp.int32, (64, 64), 0) == lax.broadcasted_iota(jnp.int32, (64, 64), 1),
                      1.0, 0.0)
    sls = [slice(u * LANES, (u + 1) * LANES) for u in range(hb)]
    each = lambda f, *ls: [f(*a) for a in zip(*ls)]
    lw = [lw_ref[0, :, sl] for sl in sls]
    r = [r_ref[0, :, sl].astype(F32) for sl in sls]
    k = [k_ref[0, :, sl].astype(F32) for sl in sls]
    v = [v_ref[0, :, sl].astype(F32) for sl in sls]
    kk = [kk_ref[0, :, sl].astype(F32) for sl in sls]
    b = [b_ref[0, :, sl].astype(F32) for sl in sls]
    lwt = [lwt_ref[0, sl, :][:64] for sl in sls]
    kt = [kt_ref[0, sl, :][:64].astype(F32) for sl in sls]
    bt = [bt_ref[0, sl, :][:64].astype(F32) for sl in sls]
    cum = each(lambda x: sum(_mm(low_incl, p) for p in _split_bf16(x, 3)), lw)
    cumt = each(lambda x: sum(_mm(p, up_incl) for p in _split_bf16(x, 3)), lwt)
    e_neg = each(lambda c: jnp.exp(-c), cum)
    at = each(lambda kk_, c, l: -kk_ * jnp.exp(c - l), kk, cum, lw)
    rt = each(lambda r_, c: r_ * jnp.exp(c), r, cum)
    g = each(lambda a, r_, b_, k_, e: _mmp(jnp.concatenate([a, r_], axis=0),
                                           jnp.concatenate([b_ * e, k_ * e], axis=0), pg, nt=True),
             at, rt, b, k, e_neg)
    n = each(lambda g_: jnp.where(strict, g_[:C, :C], 0.0), g)
    aak = each(lambda g_: jnp.where(strict, g_[:C, C:], 0.0), g)
    lrb = each(lambda g_: jnp.where(incl, g_[C:, :C], 0.0), g)
    lrk = each(lambda g_: jnp.where(incl, g_[C:, C:], 0.0), g)
    blk8 = lax.shift_right_logical(ri, 3) == lax.shift_right_logical(ci, 3)
    eye = jnp.where(ri == ci, 1.0, 0.0)
    n0 = each(lambda n_: jnp.where(blk8, n_, 0.0), n)
    x = each(lambda n_: eye + n_, n0)
    p = each(lambda n_: _mmp(n_, n_, pc), n0)
    x = each(lambda x_, p_: x_ + _mmp(x_, p_, pc), x, p)
    p = each(lambda p_: _mmp(p_, p_, pc), p)
    x = each(lambda x_, p_: x_ + _mmp(x_, p_, pc), x, p)
    for lv in range(3, int(C).bit_length() - 1):
        off = (lax.shift_right_logical(ri, lv + 1) == lax.shift_right_logical(ci, lv + 1)) & (
            lax.shift_right_logical(ri, lv) != lax.shift_right_logical(ci, lv))
        t = each(lambda n_, x_: _mmp(jnp.where(off, n_, 0.0), x_, pc), n, x)
        x = each(lambda x_, t_: x_ + _mmp(x_, t_, pc), x, t)
    z = each(lambda x_, a, b_: _mmp(x_, jnp.concatenate([a, b_], axis=1), pc), x, aak, at)
    w2v = each(lambda z_, v_: _mmp(z_[:, :C], v_, po), z, v)
    cl = each(lambda c: c[:, C - 1:C], cumt)
    eb = each(lambda c_, ct: jnp.exp(c_ - ct), cl, cumt)
    lhs = each(lambda bt_, kt_, e, lb, lk: jnp.concatenate(
        [jnp.concatenate([bt_ * e, kt_ * e], axis=1), jnp.concatenate([lb, lk], axis=1)], axis=0),
        bt, kt, eb, lrb, lrk)
    left = each(lambda l_, z_: _mmp(l_[:, :C], z_[:, C:], po), lhs, z)
    right = each(lambda l_, w, v_: _mmp(l_, jnp.concatenate([w, v_], axis=0), po), lhs, w2v, v)
    for u in range(hb):
        diag = eye64 * jnp.exp(cl[u])
        a1_ref[0, u, 0] = jnp.concatenate([left[u][:64, :64] + diag, (left[u][64:] + rt[u])[:, :64]], axis=0)
        a2_ref[0, u, 0] = right[u]


def _rwkv_chunks(r, lw, k, v, kk, b, lwt, kt, bt, C, hb, passes=RW_PASSES):
    B, T, _ = r.shape
    nc = T // C
    tspec = pl.BlockSpec((1, C, hb * LANES), lambda bb, h, c: (bb, c, h))
    fspec = pl.BlockSpec((1, hb * LANES, C), lambda bb, h, c: (bb, h, c))
    pg, pc, po = passes
    return pl.pallas_call(
        functools.partial(_rwkv_chunk_kernel, C=C, hb=hb, pg=pg, pc=pc, po=po),
        grid=(B, 8 // hb, nc),
        in_specs=[tspec] * 6 + [fspec] * 3,
        out_specs=[pl.BlockSpec((1, hb, 1, 64 + C, 64), lambda bb, h, c: (bb, h, c, 0, 0)),
                   pl.BlockSpec((1, hb, 1, 64 + C, LANES), lambda bb, h, c: (bb, h, c, 0, 0))],
        out_shape=[jax.ShapeDtypeStruct((B, 8, nc, 64 + C, 64), F32),
                   jax.ShapeDtypeStruct((B, 8, nc, 64 + C, LANES), F32)],
        compiler_params=_params(("arbitrary", "arbitrary", "arbitrary")),
        name="rwkv_chunks",
    )(r, lw, k, v, kk, b, lwt, kt, bt)


def _rwkv_scan_kernel(a1_ref, a2_ref, s0_ref, y_ref, sT_ref, h_ref, *, C, cb):
    c = pl.program_id(1)

    @pl.when(c == 0)
    def _():
        h_ref[...] = s0_ref[0]

    for cc in range(cb):
        for h in range(8):
            res = _mmp(a1_ref[0, h, cc], h_ref[h], 3) + a2_ref[0, h, cc]
            h_ref[h] = res[:64]
            y_ref[0, cc * C:(cc + 1) * C, h * LANES:(h + 1) * LANES] = res[64:]

    @pl.when(c == pl.num_programs(1) - 1)
    def _():
        sT_ref[0] = h_ref[...]


def _rwkv_scan(a1, a2, s0t, C, cb):
    B, _, nc, _, _ = a1.shape
    return pl.pallas_call(
        functools.partial(_rwkv_scan_kernel, C=C, cb=cb),
        grid=(B, nc // cb),
        in_specs=[pl.BlockSpec((1, 8, cb, 64 + C, 64), lambda b, c: (b, 0, c, 0, 0)),
                  pl.BlockSpec((1, 8, cb, 64 + C, LANES), lambda b, c: (b, 0, c, 0, 0)),
                  pl.BlockSpec((1, 8, 64, LANES), lambda b, c: (b, 0, 0, 0))],
        out_specs=[pl.BlockSpec((1, cb * C, HP), lambda b, c: (b, c, 0)),
                   pl.BlockSpec((1, 8, 64, LANES), lambda b, c: (b, 0, 0, 0))],
        out_shape=[jax.ShapeDtypeStruct((B, nc * C, HP), F32),
                   jax.ShapeDtypeStruct((B, 8, 64, LANES), F32)],
        scratch_shapes=[pltpu.VMEM((8, 64, LANES), F32)],
        compiler_params=_params(("arbitrary", "arbitrary")),
        name="rwkv_scan",
    )(a1, a2, s0t)


def _mix_kernel(att_ref, y_ref, g_ref, bg_ref, sga_ref, sgb_ref, x_ref, gt1_ref, sc2_ref, sh2_ref,
                woa_ref, wor_ref, wout_ref, lnw_ref, lnb_ref, gpost_ref, gpre_ref, wr_ref, br_ref,
                x1_ref, h2_ref, gate_ref, eidx_ref, wgt_ref, rel_ref, cnt_out_ref, cnt_ref, *, tm):
    att = jnp.concatenate([att_ref[0, h] for h in range(8)], axis=1)
    lane = lax.broadcasted_iota(jnp.int32, (tm, LANES), 1)
    valid = lane < RWKV_HEAD
    parts = []
    for h in range(8):
        y = y_ref[0, :, h * LANES:(h + 1) * LANES]
        mu = jnp.sum(y, axis=1, keepdims=True) * (1.0 / RWKV_HEAD)
        d = jnp.where(valid, y - mu, 0.0)
        var = jnp.sum(d * d, axis=1, keepdims=True) * (1.0 / RWKV_HEAD)
        parts.append(d * lax.rsqrt(var + GN_EPS))
    yn = jnp.concatenate(parts, axis=1)
    rw = (yn * lnw_ref[...] + lnb_ref[...]) * g_ref[0] + bg_ref[0]
    mix = sga_ref[0] * _mm(att, woa_ref[...]) + sgb_ref[0] * _mm(rw.astype(BF16), wor_ref[...])
    o = _mm(mix.astype(BF16), wout_ref[...])
    x1 = x_ref[0] + gt1_ref[0] * _rms(o, gpost_ref[...])
    x1_ref[0] = x1
    h2 = _rms(x1, gpre_ref[...]) * (1.0 + sc2_ref[0]) + sh2_ref[0]
    h2_ref[0] = h2
    logits = _mmp(h2, wr_ref[...], 3) + br_ref[...]
    el = lax.broadcasted_iota(jnp.int32, logits.shape, 1).astype(F32)
    work = logits
    sel = jnp.zeros(logits.shape, jnp.bool_)
    vmax = None
    firsts = []
    for kk in range(TOP_K):
        mx = jnp.max(work, axis=1, keepdims=True)
        if kk == 0:
            vmax = mx
        first = jnp.min(jnp.where(work == mx, el, float(N_EXPERTS)), axis=1, keepdims=True)
        hit = el == first
        firsts.append(first)
        sel = sel | hit
        work = jnp.where(hit, -jnp.inf, work)
    e = jnp.where(sel, jnp.exp(logits - vmax), 0.0)
    gate = e / jnp.sum(e, axis=1, keepdims=True)
    gate_ref[0] = gate
    @pl.when((pl.program_id(0) == 0) & (pl.program_id(1) == 0))
    def _():
        cnt_ref[...] = jnp.zeros(cnt_ref.shape, F32)

    self = jnp.where(sel, 1.0, 0.0)
    ti = lax.broadcasted_iota(jnp.int32, (tm, tm), 0)
    tj = lax.broadcasted_iota(jnp.int32, (tm, tm), 1)
    before = _mm(jnp.where(tj < ti, 1.0, 0.0).astype(BF16), self.astype(BF16)) + cnt_ref[...]
    cnt_ref[...] = cnt_ref[...] + jnp.sum(self, axis=0, keepdims=True)
    cnt_out_ref[...] = cnt_ref[...]
    ln = lax.broadcasted_iota(jnp.int32, (tm, LANES), 1)
    eo = jnp.zeros((tm, LANES), F32)
    go = jnp.zeros((tm, LANES), F32)
    ro = jnp.zeros((tm, LANES), F32)
    for kk in range(TOP_K):
        hit = el == firsts[kk]
        gk = jnp.sum(jnp.where(hit, gate, 0.0), axis=1, keepdims=True)
        rk = jnp.sum(jnp.where(hit, before, 0.0), axis=1, keepdims=True)
        eo = jnp.where(ln == kk, firsts[kk], eo)
        go = jnp.where(ln == kk, gk, go)
        ro = jnp.where(ln == kk, rk, ro)
    eidx_ref[0] = eo.astype(jnp.int32)
    wgt_ref[0] = go
    rel_ref[0] = ro.astype(jnp.int32)


def _mix(att, y, g, bg, sga, sgb, x, gt1, sc2, sh2, wts, tm):
    B, T, D = x.shape
    tspec = lambda n: pl.BlockSpec((1, tm, n), lambda b, i: (b, i, 0))
    full = lambda a: pl.BlockSpec(a.shape, lambda b, i: (0,) * a.ndim)
    tok = lambda n, dt: jax.ShapeDtypeStruct((B, T, n), dt)
    return pl.pallas_call(
        functools.partial(_mix_kernel, tm=tm),
        grid=(B, T // tm),
        in_specs=[pl.BlockSpec((1, 8, tm, LANES), lambda b, i: (b, 0, i, 0)), tspec(HP), tspec(HP), tspec(HP),
                  tspec(D), tspec(D), tspec(D), _mod_spec(gt1, tm), _mod_spec(sc2, tm), _mod_spec(sh2, tm)]
                 + [full(w) for w in wts],
        out_specs=[tspec(D), tspec(D), tspec(N_EXPERTS), tspec(LANES), tspec(LANES), tspec(LANES),
                   pl.BlockSpec((1, N_EXPERTS), lambda b, i: (0, 0))],
        out_shape=[tok(D, F32), tok(D, F32), tok(N_EXPERTS, F32), tok(LANES, jnp.int32), tok(LANES, F32),
                   tok(LANES, jnp.int32), jax.ShapeDtypeStruct((1, N_EXPERTS), F32)],
        scratch_shapes=[pltpu.VMEM((1, N_EXPERTS), F32)],
        compiler_params=_params(("arbitrary", "arbitrary")),
        name="mix_router",
    )(att, y, g, bg, sga, sgb, x, gt1, sc2, sh2, *wts)


def _expert(xb, wgu, bgu, wd, bd, d_ff):
    wgu = wgu.astype(BF16)
    wd = wd.astype(BF16)
    hgu = _mm(xb, wgu) + bgu
    hg = jnp.minimum(hgu[:, :d_ff], SWIGLU_LIMIT)
    hl = jnp.clip(hgu[:, d_ff:], -SWIGLU_LIMIT, SWIGLU_LIMIT)
    act = hg * (1.0 / (1.0 + jnp.exp(-SWIGLU_ALPHA * hg))) * (hl + 1.0)
    return _mm(act.astype(BF16), wd) + bd


def _sc_mesh():
    return plsc.VectorSubcoreMesh(core_axis_name="core", subcore_axis_name="subcore")


def _sc_dispatch(h, pos_slots, n_rows, q):
    N, D = h.shape
    mesh = _sc_mesh()
    steps = N // (mesh.num_cores * SC_WINDOW)
    assert steps * mesh.num_cores * SC_WINDOW == N

    @pl.kernel(out_type=jax.ShapeDtypeStruct((n_rows, SC_PIECE), h.dtype), mesh=mesh, scratch_types=[])
    def scatter(x_hbm, *refs):
        idx_hbm, o_hbm = refs[:-1], refs[-1]
        base = lax.axis_index("core") * steps

        def body(x_vmem, *i_vmem):
            for iv in i_vmem:
                pltpu.sync_copy(x_vmem, o_hbm.at[iv.at[0]])

        pltpu.emit_pipeline(
            body,
            grid=(steps,),
            in_specs=[pl.BlockSpec((SC_WINDOW, SC_PIECE), index_map=lambda i: (base + i, q))]
                     + [pl.BlockSpec((1, SC_WINDOW), index_map=lambda i: (0, base + i))] * len(idx_hbm),
            out_specs=[],
            core_axis_name="subcore",
            dimension_semantics=(pltpu.PARALLEL,),
        )(x_hbm, *idx_hbm)

    return scatter(h, *pos_slots)


def _sc_gather(table, idx):
    M = idx.shape[1]
    mesh = _sc_mesh()
    steps = M // (mesh.num_cores * SC_WINDOW)
    assert steps * mesh.num_cores * SC_WINDOW == M

    @pl.kernel(out_type=jax.ShapeDtypeStruct((M, SC_PIECE), table.dtype), mesh=mesh)
    def gather(x_hbm, i_hbm, o_hbm):
        base = lax.axis_index("core") * steps

        def body(i_vmem, o_vmem):
            pltpu.sync_copy(x_hbm.at[i_vmem.at[0]], o_vmem)

        pltpu.emit_pipeline(
            body,
            grid=(steps,),
            in_specs=[pl.BlockSpec((1, SC_WINDOW), index_map=lambda i: (0, base + i))],
            out_specs=[pl.BlockSpec((SC_WINDOW, SC_PIECE), index_map=lambda i: (base + i, 0))],
            core_axis_name="subcore",
            dimension_semantics=(pltpu.PARALLEL,),
        )(i_hbm, o_hbm)

    return gather(table, idx)


def _moe_grouped_kernel(te_ref, nt_ref, *refs, d_ff, npiece):
    x_refs, (wgu_ref, bgu_ref, wd_ref, bd_ref) = refs[:npiece], refs[npiece:npiece + 4]
    o_refs, (wgu_b, wd_b) = refs[npiece + 4:2 * npiece + 4], refs[2 * npiece + 4:]
    g = pl.program_id(0)

    @pl.when((g == 0) | (te_ref[g] != te_ref[jnp.maximum(g - 1, 0)]))
    def _():
        wgu_b[...] = wgu_ref[0].astype(BF16)
        wd_b[...] = wd_ref[0].astype(BF16)

    @pl.when(g < nt_ref[0])
    def _():
        xb = jnp.concatenate([r[...] for r in x_refs], axis=1).astype(BF16)
        y = _expert(xb, wgu_b[...], bgu_ref[0], wd_b[...], bd_ref[0], d_ff)
        for q, o_ref in enumerate(o_refs):
            o_ref[...] = y[:, q * SC_PIECE:(q + 1) * SC_PIECE]


def _moe_grouped(xs, te, nt, wgu, bgu, wd, bd, TG):
    NP = xs[0].shape[0]
    E, D, F2 = wgu.shape
    npiece = len(xs)
    wmap = lambda g, te_, nt_: (te_[g], 0, 0)
    xmap = lambda g, te_, nt_: (jnp.minimum(g, nt_[0] - 1), 0)
    pspec = pl.BlockSpec((TG, SC_PIECE), xmap)
    return pl.pallas_call(
        functools.partial(_moe_grouped_kernel, d_ff=F2 // 2, npiece=npiece),
        grid_spec=pltpu.PrefetchScalarGridSpec(
            num_scalar_prefetch=2,
            grid=(NP // TG,),
            in_specs=[pspec] * npiece + [pl.BlockSpec((1, D, F2), wmap), pl.BlockSpec((1, 1, F2), wmap),
                                         pl.BlockSpec((1, F2 // 2, D), wmap), pl.BlockSpec((1, 1, D), wmap)],
            out_specs=[pspec] * npiece,
            scratch_shapes=[pltpu.VMEM((D, F2), BF16), pltpu.VMEM((F2 // 2, D), BF16)]),
        out_shape=[jax.ShapeDtypeStruct((NP, SC_PIECE), F32)] * npiece,
        compiler_params=_params(("arbitrary",)),
        name="moe_grouped",
    )(te, nt, *xs, wgu, bgu, wd, bd)


def _moe_combine_kernel(*refs, npiece):
    y_refs = refs[:TOP_K * npiece]
    w_ref, x1_ref, gt2_ref, gpost_ref, o_ref = refs[TOP_K * npiece:]
    w = w_ref[...]
    acc = None
    for k in range(TOP_K):
        yk = jnp.concatenate([y_refs[k * npiece + q][...] for q in range(npiece)], axis=1)
        term = w[:, k:k + 1] * yk
        acc = term if acc is None else acc + term
    o_ref[...] = x1_ref[...] + gt2_ref[0] * _rms(acc, gpost_ref[...])


def _moe_combine(yslot, wgt, x1, gt2, gpost, tm):
    N, D = x1.shape
    nb = N // tm
    tpb = N // gt2.shape[0]
    npiece = len(yslot)
    yspec = lambda k: pl.BlockSpec((tm, SC_PIECE), lambda i, k=k: (i + k * nb, 0))
    return pl.pallas_call(
        functools.partial(_moe_combine_kernel, npiece=npiece),
        grid=(nb,),
        in_specs=[yspec(k) for k in range(TOP_K) for _ in range(npiece)]
                 + [pl.BlockSpec((tm, LANES), lambda i: (i, 0)), pl.BlockSpec((tm, D), lambda i: (i, 0)),
                    pl.BlockSpec((1, 1, D), lambda i: ((i * tm) // tpb, 0, 0)), pl.BlockSpec((1, D), lambda i: (0, 0))],
        out_specs=pl.BlockSpec((tm, D), lambda i: (i, 0)),
        out_shape=jax.ShapeDtypeStruct((N, D), F32),
        compiler_params=_params(("arbitrary",)),
        name="moe_combine",
    )(*[yslot[q] for _ in range(TOP_K) for q in range(npiece)], wgt, x1, gt2, gpost)


def _moe_kernel(h_ref, gate_ref, x1_ref, gt2_ref, gpost_ref, wgu_ref, bgu_ref, wd_ref, bd_ref, o_ref, acc_ref,
                *, d_ff):
    e = pl.program_id(1)

    @pl.when(e == 0)
    def _():
        acc_ref[...] = jnp.zeros(acc_ref.shape, F32)

    contrib = _expert(h_ref[...].astype(BF16), wgu_ref[0], bgu_ref[0], wd_ref[0], bd_ref[0], d_ff)
    gate = gate_ref[...]
    el = lax.broadcasted_iota(jnp.int32, gate.shape, 1)
    ge = jnp.sum(jnp.where(el == e, gate, 0.0), axis=1, keepdims=True)
    acc_ref[...] += ge * contrib

    @pl.when(e == pl.num_programs(1) - 1)
    def _():
        o_ref[...] = x1_ref[...] + gt2_ref[0] * _rms(acc_ref[...], gpost_ref[...])


def _moe(h2, gate, x1, gt2, gpost, wgu, bgu, wd, bd, tm):
    N, D = h2.shape
    E, _, F2 = wgu.shape
    tspec = lambda n: pl.BlockSpec((tm, n), lambda i, e: (i, 0))
    if gt2.shape[1] == 1:
        tpb = N // gt2.shape[0]
        gspec = pl.BlockSpec((1, 1, D), lambda i, e: ((i * tm) // tpb, 0, 0))
    else:
        gspec = pl.BlockSpec((1, tm, D), lambda i, e: (i, 0, 0))
    return pl.pallas_call(
        functools.partial(_moe_kernel, d_ff=F2 // 2),
        grid=(N // tm, E),
        in_specs=[tspec(D), tspec(E), tspec(D), gspec, pl.BlockSpec((1, D), lambda i, e: (0, 0)),
                  pl.BlockSpec((1, D, F2), lambda i, e: (e, 0, 0)), pl.BlockSpec((1, 1, F2), lambda i, e: (e, 0, 0)),
                  pl.BlockSpec((1, F2 // 2, D), lambda i, e: (e, 0, 0)), pl.BlockSpec((1, 1, D), lambda i, e: (e, 0, 0))],
        out_specs=tspec(D),
        out_shape=jax.ShapeDtypeStruct((N, D), F32),
        scratch_shapes=[pltpu.VMEM((tm, D), F32)],
        compiler_params=_params(("arbitrary", "arbitrary")),
        name="moe",
    )(h2, gate, x1, gt2, gpost, wgu, bgu, wd, bd)


def _per_token(m, T):
    B, _, D = m.shape
    return jnp.broadcast_to(m, (B, T, D)).reshape(1, B * T, D)


def _layer(x, mods, past, s0, shift0, P, cfg):
    B, T, D = x.shape
    sh1, sc1, gt1, sh2, sc2, gt2 = mods
    tm = cfg["tm"]

    q, k, v, kb, vb, iq, ik, ikb, iw = _proj_a(x, sc1, sh1, P["g_pre_mix"], P["w_a"], P["ik_ln_w"], P["ik_ln_b"],
                                              cfg["tm_a"])
    zr, sga, sgb = _proj_b(x, sc1, sh1, P["g_pre_mix"], P["w_b"], tm)

    if past is not None:
        pk, pv, pik = past
        plen = pk.shape[1]
        kb_all = jnp.concatenate([pk.reshape(B, plen, 128).astype(BF16), kb], axis=1)
        vb_all = jnp.concatenate([pv.reshape(B, plen, 128).astype(BF16), vb], axis=1)
        ik_all = jnp.concatenate([jnp.pad(pik, ((0, 0), (0, 0), (0, 64))).astype(BF16), ikb], axis=1)
    else:
        plen = 0
        kb_all, vb_all, ik_all = kb, vb, ikb
    ltot = plen + T
    KB = cfg["KB"]
    lp = -(-ltot // KB) * KB
    if lp != ltot:
        padk = ((0, 0), (0, lp - ltot), (0, 0))
        kb_all, vb_all, ik_all = jnp.pad(kb_all, padk), jnp.pad(vb_all, padk), jnp.pad(ik_all, padk)
    vb1 = jnp.concatenate([vb_all, jnp.ones_like(vb_all)], axis=-1)
    att = _attention(iq, iw, q, ik_all, kb_all, vb1, R=cfg["R"], KB=KB, SB=cfg["SB"], HG=cfg["HG"], ltot=ltot,
                     q_off=plen)

    C = RW_CHUNK
    tp = -(-T // C) * C
    zr_p = zr if tp == T else jnp.pad(zr, ((0, 0), (0, tp - T), (0, 0)))
    shift_pad = _take_cols(shift0.reshape(B, RWKV_COLS), _COLS_ZR).reshape(B, 1, ZRW)
    r, lw, km, vv, kkn, bb, g, bg, lwt, kt, bt = _rwkv_prep(zr_p, shift_pad, P, min(cfg["tm_rw"], tp), T)
    a1, a2 = _rwkv_chunks(r, lw, km, vv, kkn, bb, lwt, kt, bt, C, cfg["hb"])
    s0t = jnp.pad(jnp.swapaxes(s0, 2, 3), ((0, 0), (0, 0), (0, 0), (0, 64)))
    nc = tp // C
    cb = min(cfg["cb"], nc)
    y, sT = _rwkv_scan(a1, a2, s0t, C, cb)
    s_new = jnp.swapaxes(sT[..., :64], 2, 3)
    zlast = zr[:, T - 1]
    inv = np.zeros((RWKV_COLS,), np.int32)
    inv[_COLS_ZR[0][_COLS_ZR[1]]] = np.nonzero(_COLS_ZR[1])[0]
    shift_new = jnp.take(zlast, jnp.asarray(inv), axis=1).reshape(B, 1, RWKV_COLS)
    if tp != T:
        y, g, bg = y[:, :T], g[:, :T], bg[:, :T]

    wts = [P[n] for n in ("w_o_att", "w_o_rwkv", "w_out", "ln_x_w", "ln_x_b", "g_post_mix", "g_pre_ffn",
                          "w_router", "b_router")]
    x1, h2, gate, eidx, wgt, rel, cnt = _mix(att, y, g, bg, sga, sgb, x, gt1, sc2, sh2, wts, cfg["tm_mix"])

    N = B * T
    tmm = min(cfg["tm_moe"], N)
    if cfg["routed"]:
        TG = cfg["TG"]
        ntm = N * TOP_K // TG + N_EXPERTS
        cnt_i = cnt[0].astype(jnp.int32)
        padded = (cnt_i + TG - 1) // TG * TG
        ends = jnp.cumsum(padded)
        off = ends - padded
        nt = ends[-1] // TG
        gi = jnp.minimum(jnp.arange(ntm, dtype=jnp.int32), nt - 1)
        te = jnp.sum((ends // TG)[None, :] <= gi[:, None], axis=1).astype(jnp.int32)
        e4 = eidx.reshape(N, LANES)[:, :TOP_K]
        pos = jnp.take(off, e4) + rel.reshape(N, LANES)[:, :TOP_K]
        pos = jnp.clip(pos, 0, ntm * TG - 1)
        pos_t = pos.T
        npiece = D // SC_PIECE
        h2f = h2.reshape(N, D)
        xs = [_sc_dispatch(h2f, [pos_t[k:k + 1] for k in range(TOP_K)], ntm * TG, q) for q in range(npiece)]
        ys = _moe_grouped(xs, te, nt.reshape(1), P["w_gu"], P["b_gu"], P["w_down"], P["b_down"], TG)
        idx_c = pos_t.reshape(1, TOP_K * N)
        yslot = [_sc_gather(ys[q], idx_c) for q in range(npiece)]
        out = _moe_combine(yslot, wgt.reshape(N, LANES), x1.reshape(N, D), gt2, P["g_post_ffn"], tm)
    else:
        if T % tmm == 0:
            gt2m = gt2
        else:
            gt2m = _per_token(gt2, T).reshape(N // tmm, tmm, D)
        out = _moe(h2.reshape(N, D), gate.reshape(N, N_EXPERTS), x1.reshape(N, D), gt2m, P["g_post_ffn"],
                   P["w_gu"], P["b_gu"], P["w_down"], P["b_down"], tmm)
    return (out.reshape(B, T, D), k.reshape(B, T, ATT_KV_HEADS, HEAD_DIM), v.reshape(B, T, ATT_KV_HEADS, HEAD_DIM),
            ik, s_new, shift_new)


def _prep_weights(l, w_in, ik_ln_w, ik_ln_b, mu_rwkv, w0, w_up, a0, a_up, g_up, k_k, k_a, r_k, ln_x_w, ln_x_b,
                  w_o_att, w_o_rwkv, w_out, w_router, b_router, w_gu, b_gu, w_down, b_down,
                  g_pre_mix, g_post_mix, g_pre_ffn, g_post_ffn):
    row = lambda a: a.reshape(1, -1)
    hp = lambda a: _take_cols(row(a), _HEAD_IDX)
    P = {}
    P["w_a"] = _take_cols(w_in[l], _COLS_A).astype(BF16)
    P["w_b"] = _take_cols(w_in[l], _COLS_B).astype(BF16)
    P["ik_ln_w"] = jnp.pad(row(ik_ln_w[l]), ((0, 0), (0, 64)))
    P["ik_ln_b"] = jnp.pad(row(ik_ln_b[l]), ((0, 0), (0, 64)))
    P["mu"] = _take_cols(row(mu_rwkv[l]), _COLS_ZR)
    P["w0"], P["a0"], P["k_k"], P["k_a"] = hp(w0[l]), hp(a0[l]), hp(k_k[l]), hp(k_a[l])
    P["r_k"] = hp(r_k[l].reshape(-1))
    P["ln_x_w"], P["ln_x_b"] = hp(ln_x_w[l]), hp(ln_x_b[l])
    z64 = jnp.zeros((64, HP), F32)
    P["w_up"] = jnp.concatenate([_take_cols(w_up[l], _HEAD_IDX), z64], axis=0)
    P["a_up"] = jnp.concatenate([z64, _take_cols(a_up[l], _HEAD_IDX)], axis=0)
    P["g_up"] = _take_cols(g_up[l], _HEAD_IDX)
    P["w_o_att"] = _take_rows(w_o_att[l], _ATT_ROW_IDX).astype(BF16)
    P["w_o_rwkv"] = _take_rows(w_o_rwkv[l], _HEAD_IDX).astype(BF16)
    P["w_out"] = w_out[l].astype(BF16)
    P["w_router"] = w_router[l]
    P["b_router"] = row(b_router[l])
    P["w_gu"] = w_gu[l]
    P["b_gu"] = b_gu[l].reshape(N_EXPERTS, 1, -1)
    P["w_down"] = w_down[l]
    P["b_down"] = b_down[l].reshape(N_EXPERTS, 1, -1)
    P["g_pre_mix"], P["g_post_mix"] = row(g_pre_mix[l]), row(g_post_mix[l])
    P["g_pre_ffn"], P["g_post_ffn"] = row(g_pre_ffn[l]), row(g_post_ffn[l])
    return P


CFG_PROMPT = dict(tm=256, tm_a=1024, tm_mix=512, R=256,KB=512, SB=256, HG=2,tm_rw=256, hb=8, cb=2, tm_moe=1024, routed=True, TG=512)
CFG_SAMPLE = dict(tm=32, tm_a=32, tm_mix=32, R=32, KB=512, SB=256, HG=2,tm_rw=128, hb=8, cb=1, tm_moe=512, routed=False)


def kernel(x_prompt, x_sample, c_prompt, c_sample, cache_k, cache_v, cache_idx_k, state_rwkv, state_shift, w_ada, b_ada, g_pre_mix, g_post_mix, g_pre_ffn, g_post_ffn, w_in, ik_ln_w, ik_ln_b, mu_rwkv, w0, w_up, a0, a_up, g_up, k_k, k_a, r_k, ln_x_w, ln_x_b, w_o_att, w_o_rwkv, w_out, w_router, b_router, w_gu, b_gu, w_down, b_down):
    depth = w_in.shape[0]
    bp, tp_, D = x_prompt.shape
    bs, ts, _ = x_sample.shape
    y_p, y_s = x_prompt, x_sample
    st_p = [[] for _ in range(5)]
    st_s = [[] for _ in range(5)]
    nc_all = bp + bs
    npad = -(-nc_all // 8) * 8
    c_all = jnp.pad(jnp.concatenate([c_prompt, c_sample], axis=0), ((0, npad - nc_all), (0, 0)))
    for l in range(depth):
        P = _prep_weights(l, w_in, ik_ln_w, ik_ln_b, mu_rwkv, w0, w_up, a0, a_up, g_up, k_k, k_a, r_k, ln_x_w,
                          ln_x_b, w_o_att, w_o_rwkv, w_out, w_router, b_router, w_gu, b_gu, w_down, b_down,
                          g_pre_mix, g_post_mix, g_pre_ffn, g_post_ffn)
        ada = _ada(c_all, w_ada[l], b_ada[l])
        mods_p = [m[:bp, None, :] for m in jnp.split(ada, 6, axis=-1)]
        mods_s = [m[bp:nc_all, None, :] for m in jnp.split(ada, 6, axis=-1)]
        zero_state = jnp.zeros((bp, RWKV_HEADS, RWKV_HEAD, RWKV_HEAD), F32)
        zero_shift = jnp.zeros((bp, 1, RWKV_COLS), F32)
        outs_p = _layer(y_p, mods_p, None, zero_state, zero_shift, P, CFG_PROMPT)
        outs_s = _layer(y_s, mods_s, (cache_k[l], cache_v[l], cache_idx_k[l]), state_rwkv[l], state_shift[l], P,
                        CFG_SAMPLE)
        y_p, y_s = outs_p[0], outs_s[0]
        for lst, val in zip(st_p, outs_p[1:]):
            lst.append(val)
        for lst, val in zip(st_s, outs_s[1:]):
            lst.append(val)
    sp = [jnp.stack(v, axis=0) for v in st_p]
    ss = [jnp.stack(v, axis=0) for v in st_s]
    return (y_p, y_s, sp[0], sp[1], sp[2], sp[3], sp[4], ss[0], ss[1], ss[2], ss[3], ss[4])
```

```python
import functools

import numpy as np
import jax
import jax.numpy as jnp
from jax import lax
from jax.experimental import pallas as pl
from jax.experimental.pallas import tpu as pltpu
from jax.experimental.pallas import tpu_sc as plsc

F32 = jnp.float32
BF16 = jnp.bfloat16
HI = lax.Precision.HIGHEST

CHUNK = 64
ATT_HEADS = 8
ATT_KV_HEADS = 2
HEAD_DIM = 64
IDX_HEADS = 8
IDX_DIM = 64
IDX_SCALE = (IDX_DIM ** -0.5) * (IDX_HEADS ** -0.5)
TOPK_MAX = 256
RWKV_HEADS = 8
RWKV_HEAD = 64
RWKV_WIDTH = RWKV_HEADS * RWKV_HEAD
DECAY_LORA = 64
AAA_LORA = 64
GATE_LORA = 128
RWKV_COLS = 3 * RWKV_WIDTH + DECAY_LORA + AAA_LORA + GATE_LORA
GN_EPS = 64e-5
L2_EPS = 1e-24
N_EXPERTS = 32
TOP_K = 4
SWIGLU_LIMIT = 7.0
SWIGLU_ALPHA = 1.702
NORM_EPS = 1e-6
LN_EPS = 1e-6

LANES = 128
VMEM_LIMIT = 56 * 1024 * 1024

HP = 8 * LANES
RW_CHUNK = 128
NEG_BIG = -1e30
LOG2E = 1.4426950408889634
KEY_NEG_INF = -2139095041
INT_MIN = -2147483648
SC_WINDOW = 128
SC_PIECE = 256
LO_GROUP = 4
RW_PASSES = (1, 1, 1)


def _params(sem):
    return pltpu.CompilerParams(dimension_semantics=sem, vmem_limit_bytes=VMEM_LIMIT)


def _nt(a, b, precision=None):
    return lax.dot_general(a, b, (((1,), (1,)), ((), ())), precision=precision,
                           preferred_element_type=F32)


def _mm(a, b, precision=None):
    return jnp.dot(a, b, precision=precision, preferred_element_type=F32)


def _split_bf16(x, terms):
    out = []
    for _ in range(terms):
        p = x.astype(BF16)
        out.append(p)
        x = x - p.astype(F32)
    return out


def _mmp(a, b, passes, nt=False):
    dot = _nt if nt else _mm
    if passes == 1:
        return dot(a.astype(BF16), b.astype(BF16))
    ah, al = _split_bf16(a, 2)
    bh, bl = _split_bf16(b, 2)
    return dot(ah, bh) + (dot(ah, bl) + dot(al, bh))


def _rms(x, g):
    return x * lax.rsqrt(jnp.mean(x * x, axis=-1, keepdims=True) + NORM_EPS) * g


def _headpad_idx(seg_off, lane_off_fn=lambda h: 0):
    idx = np.zeros((HP,), np.int32)
    ok = np.zeros((HP,), bool)
    for h in range(8):
        d0 = h * LANES + lane_off_fn(h)
        idx[d0:d0 + 64] = seg_off + h * 64 + np.arange(64)
        ok[d0:d0 + 64] = True
    return idx, ok


def _plain_idx(seg_off, n, width):
    idx = np.zeros((width,), np.int32)
    ok = np.zeros((width,), bool)
    idx[:n] = seg_off + np.arange(n)
    ok[:n] = True
    return idx, ok


def _cat(parts):
    return np.concatenate([p[0] for p in parts]), np.concatenate([p[1] for p in parts])


_O_Q, _O_K, _O_V, _O_IQ, _O_IK, _O_IW, _O_ZR, _O_GA, _O_GB = 0, 512, 640, 768, 1280, 1344, 1352, 3144, 4168
_COLS_A = _cat([_headpad_idx(_O_Q, lambda h: (h // 4) * 64), _plain_idx(_O_K, 128, 128), _plain_idx(_O_V, 128, 128),
                _headpad_idx(_O_IQ), _plain_idx(_O_IK, 64, 128), _plain_idx(_O_IW, 8, 128)])
_COLS_ZR = _cat([_headpad_idx(0), _headpad_idx(512), _headpad_idx(1024), _plain_idx(1536, 128, 128),
                 _plain_idx(1664, 128, 128)])
ZRW = _COLS_ZR[0].shape[0]
_COLS_B = _cat([(_COLS_ZR[0] + _O_ZR, _COLS_ZR[1]), _plain_idx(_O_GA, 1024, 1024), _plain_idx(_O_GB, 1024, 1024)])
NA = _COLS_A[0].shape[0]
NB = _COLS_B[0].shape[0]
_HEAD_IDX = _headpad_idx(0)
_ATT_ROW_IDX = _headpad_idx(0, lambda h: (h // 4) * 64)


def _take_cols(w, cols):
    idx, ok = cols
    return jnp.where(jnp.asarray(ok)[None, :], jnp.take(w, jnp.asarray(idx), axis=1), 0.0)


def _take_rows(w, cols):
    idx, ok = cols
    return jnp.where(jnp.asarray(ok)[:, None], jnp.take(w, jnp.asarray(idx), axis=0), 0.0)


def _mod_spec(m, tm):
    assert m.shape[1] == 1
    return pl.BlockSpec((1, 1, m.shape[2]), lambda b, i: (b, 0, 0))


def _ada_kernel(c_ref, w_ref, b_ref, o_ref):
    c = c_ref[...]
    s = c * (1.0 / (1.0 + jnp.exp(-c)))
    o_ref[...] = _mm(s, w_ref[...], HI) + b_ref[...]


def _ada(c, w, b):
    n, d = c.shape
    nout = w.shape[1]
    bn = 768
    return pl.pallas_call(
        _ada_kernel,
        grid=(nout // bn,),
        in_specs=[pl.BlockSpec((n, d), lambda j: (0, 0)),
                  pl.BlockSpec((d, bn), lambda j: (0, j)),
                  pl.BlockSpec((1, bn), lambda j: (0, j))],
        out_specs=pl.BlockSpec((n, bn), lambda j: (0, j)),
        out_shape=jax.ShapeDtypeStruct((n, nout), F32),
        compiler_params=_params(("arbitrary",)),
        name="ada",
    )(c, w, b.reshape(1, nout))


def _modulated(x_ref, sc_ref, sh_ref, g_ref):
    h = _rms(x_ref[0], g_ref[...])
    return (h * (1.0 + sc_ref[0]) + sh_ref[0]).astype(BF16)


def _proj_a_kernel(x_ref, sc_ref, sh_ref, g_ref, w_ref, lnw_ref, lnb_ref,
                   q_ref, k_ref, v_ref, kb_ref, vb_ref, iq_ref, ik_ref, ikb_ref, iw_ref):
    hb = _modulated(x_ref, sc_ref, sh_ref, g_ref)
    for h in range(8):
        zq = _mm(hb, w_ref[:, h * LANES:(h + 1) * LANES])
        q_ref[0, h] = (zq * (HEAD_DIM ** -0.5 * LOG2E)).astype(BF16)
    kv = _mm(hb, w_ref[:, 1024:1280])
    k = kv[:, :128]
    v = kv[:, 128:]
    k_ref[0] = k
    v_ref[0] = v
    kb_ref[0] = k.astype(BF16)
    vb_ref[0] = v.astype(BF16)
    for h in range(8):
        zi = _mm(hb, w_ref[:, 1280 + h * LANES:1280 + (h + 1) * LANES])
        iq_ref[0, h] = zi.astype(BF16)
    t = _mm(hb, w_ref[:, 2304:2560])
    ik = t[:, :128]
    lane = lax.broadcasted_iota(jnp.int32, ik.shape, 1)
    valid = lane < IDX_DIM
    mu = jnp.sum(ik, axis=-1, keepdims=True) * (1.0 / IDX_DIM)
    d = jnp.where(valid, ik - mu, 0.0)
    var = jnp.sum(d * d, axis=-1, keepdims=True) * (1.0 / IDX_DIM)
    ikn = d * lax.rsqrt(var + LN_EPS) * lnw_ref[...] + lnb_ref[...]
    ik_ref[0] = ikn[:, :IDX_DIM]
    ikb_ref[0] = ikn.astype(BF16)
    iw_ref[0] = t[:, 128:136] * IDX_SCALE


def _proj_a(x, sc, sh, g, w, lnw, lnb, tm):
    B, T, D = x.shape
    nt = T // tm
    tok = lambda n, dt: jax.ShapeDtypeStruct((B, T, n), dt)
    hm = lambda dt: jax.ShapeDtypeStruct((B, 8, T, LANES), dt)
    tspec = lambda n: pl.BlockSpec((1, tm, n), lambda b, i: (b, i, 0))
    hspec = pl.BlockSpec((1, 8, tm, LANES), lambda b, i: (b, 0, i, 0))
    full = lambda a: pl.BlockSpec(a.shape, lambda b, i: (0,) * a.ndim)
    return pl.pallas_call(
        _proj_a_kernel,
        grid=(B, nt),
        in_specs=[tspec(D), _mod_spec(sc, tm), _mod_spec(sh, tm), full(g), full(w), full(lnw), full(lnb)],
        out_specs=[hspec, tspec(128), tspec(128), tspec(128), tspec(128), hspec, tspec(IDX_DIM), tspec(128),
                   tspec(8)],
        out_shape=[hm(BF16), tok(128, F32), tok(128, F32), tok(128, BF16), tok(128, BF16), hm(BF16),
                   tok(IDX_DIM, F32), tok(128, BF16), tok(8, F32)],
        compiler_params=_params(("arbitrary", "arbitrary")),
        name="proj_att",
    )(x, sc, sh, g, w, lnw, lnb)


def _proj_b_kernel(x_ref, sc_ref, sh_ref, g_ref, w_ref, zr_ref, ga_ref, gb_ref):
    hb = _modulated(x_ref, sc_ref, sh_ref, g_ref)
    for j in range(ZRW // 256):
        zr_ref[0, :, j * 256:(j + 1) * 256] = _mm(hb, w_ref[:, j * 256:(j + 1) * 256])
    for j in range(4):
        za = _mm(hb, w_ref[:, ZRW + j * 256:ZRW + (j + 1) * 256])
        ga_ref[0, :, j * 256:(j + 1) * 256] = 1.0 / (1.0 + jnp.exp(-za))
        zb = _mm(hb, w_ref[:, ZRW + 1024 + j * 256:ZRW + 1024 + (j + 1) * 256])
        gb_ref[0, :, j * 256:(j + 1) * 256] = 1.0 / (1.0 + jnp.exp(-zb))


def _proj_b(x, sc, sh, g, w, tm):
    B, T, D = x.shape
    nt = T // tm
    tok = lambda n: jax.ShapeDtypeStruct((B, T, n), F32)
    tspec = lambda n: pl.BlockSpec((1, tm, n), lambda b, i: (b, i, 0))
    full = lambda a: pl.BlockSpec(a.shape, lambda b, i: (0,) * a.ndim)
    return pl.pallas_call(
        _proj_b_kernel,
        grid=(B, nt),
        in_specs=[tspec(D), _mod_spec(sc, tm), _mod_spec(sh, tm), full(g), full(w)],
        out_specs=[tspec(ZRW), tspec(1024), tspec(1024)],
        out_shape=[tok(ZRW), tok(1024), tok(1024)],
        compiler_params=_params(("arbitrary", "arbitrary")),
        name="proj_rwkv",
    )(x, sc, sh, g, w)


def _sortable_to_float(t):
    bits = t ^ (lax.shift_right_arithmetic(t, 31) & 0x7FFFFFFF)
    return lax.bitcast_convert_type(bits, F32)


def _top16(x):
    bits = lax.bitcast_convert_type(x, jnp.int32) & jnp.int32(-65536)
    return lax.bitcast_convert_type(bits, F32).astype(BF16)


def _attn_kernel(iq_ref, iw_ref, q_ref, ik_ref, k_ref, v_ref, o_ref, s_ref, sb_ref, m_ref, acc_ref,
                 *, R, KB, SB, HG, ltot, q_off, topk):
    j = pl.program_id(1)
    q0 = q_off + j * R
    pos = q0 + lax.broadcasted_iota(jnp.int32, (R, 1), 0)
    qchunk = lax.shift_right_logical(pos, 6)
    kend = jnp.minimum(ltot, (lax.shift_right_logical(q0 + R - 1, 6) + 1) * CHUNK)
    nkb = lax.shift_right_logical(kend + KB - 1, KB.bit_length() - 1)
    lane_k = lax.broadcasted_iota(jnp.int32, (R, KB), 1)
    nch = KB // LANES

    iq_all = iq_ref[0].reshape(8 * R, LANES)
    iw = iw_ref[0]
    iwb = [jnp.broadcast_to(iw[:, h:h + 1], (R, SB)) for h in range(8)]
    lane_s = lax.broadcasted_iota(jnp.int32, (R, SB), 1)
    nsb = KB // SB
    key_lim = jnp.minimum((qchunk + 1) * CHUNK, ltot)

    def score_body(kb, carry):
        for u in range(nsb):
            off = pl.multiple_of(kb * KB + u * SB, SB)
            s_all = _mm(iq_all, ik_ref[0, kb * nsb + u])
            acc = jnp.zeros((R, SB), F32)
            for h in range(8):
                acc = acc + jnp.maximum(s_all[h * R:(h + 1) * R], 0.0) * iwb[h]
            kidx = off + lane_s
            sc = jnp.where(kidx < key_lim, acc, -jnp.inf)
            s_ref[kb, :, u * SB:(u + 1) * SB] = sc
            sb_ref[kb, :, u * SB:(u + 1) * SB] = _top16(sc)
        return carry

    lax.fori_loop(0, nkb, score_body, 0)

    def count_ge(ref, cand, dt):
        one, zero = jnp.ones((), dt), jnp.zeros((), dt)

        def body(kb, part):
            if dt == BF16:
                for c in range(nch):
                    part = part + jnp.where(ref[kb, :, c * LANES:(c + 1) * LANES] >= cand, one, zero)
                return part
            blk = ref[kb]
            m = jnp.where(blk >= cand, one, zero)
            for c in range(nch):
                part = part + m[:, c * LANES:(c + 1) * LANES]
            return part
        part = lax.fori_loop(0, nkb, body, jnp.zeros((R, LANES), dt))
        return jnp.sum(part.astype(F32), axis=1, keepdims=True)

    kf = float(topk)
    c0 = count_ge(sb_ref, jnp.zeros((R, 1), BF16), BF16)
    t0 = jnp.where(c0 >= kf, 0, INT_MIN).astype(jnp.int32)
    ex0 = jnp.where(c0 == kf, 1.0, 0.0)

    def accept(c, cand, cnt):
        t, ex = c
        ok = cnt >= kf
        return jnp.where(ok, cand, t), jnp.where(ok & (cnt == kf), 1.0, ex)

    def hi_body(i, c):
        cand = c[0] + lax.shift_left(jnp.int32(1), 30 - i)
        return accept(c, cand, count_ge(sb_ref, _top16(_sortable_to_float(cand)), BF16))

    def lo_pass(i, c):
        cand = c[0] + lax.shift_left(jnp.int32(1), 15 - i)
        return accept(c, cand, count_ge(s_ref, _sortable_to_float(cand), F32))

    def lo_group(c):
        gi, t, ex, _ = c
        tc = (t, ex)
        for jj in range(LO_GROUP):
            tc = lo_pass(gi * LO_GROUP + jj, tc)
        return gi + 1, tc[0], tc[1], (jnp.min(tc[1]) > 0.5).astype(jnp.int32)

    t, ex = lax.fori_loop(0, 15, hi_body, (t0, ex0))
    _, t, _, _ = lax.while_loop(lambda c: (c[0] < 16 // LO_GROUP) & (c[3] == 0), lo_group,
                                (jnp.int32(0), t, ex, (jnp.min(ex) > 0.5).astype(jnp.int32)))
    all_finite = t <= KEY_NEG_INF
    tau = jnp.where(all_finite, -jnp.inf, _sortable_to_float(jnp.maximum(t, KEY_NEG_INF)))

    def count_gt_eq(_):
        def body(kb, carry):
            pg, pe = carry
            for c in range(nch):
                blk = s_ref[kb, :, c * LANES:(c + 1) * LANES]
                pg = pg + jnp.where(blk > tau, 1.0, 0.0)
                pe = pe + jnp.where(blk == tau, 1.0, 0.0)
            return pg, pe
        z = jnp.zeros((R, LANES), F32)
        pg, pe = lax.fori_loop(0, nkb, body, (z, z))
        return jnp.sum(pg, axis=1, keepdims=True), jnp.sum(pe, axis=1, keepdims=True)

    cnt_gt, cnt_eq = count_gt_eq(0)
    need = kf - cnt_gt
    tie = (cnt_eq > need) & jnp.logical_not(all_finite)
    any_tie = jnp.max(jnp.where(tie, 1.0, 0.0)) > 0.0

    def tie_bound():
        def count_eq_below(x):
            def body(kb, part):
                blk = s_ref[kb]
                kidx = kb * KB + lane_k
                m = jnp.where((blk == tau) & (kidx < x), 1.0, 0.0)
                for c in range(nch):
                    part = part + m[:, c * LANES:(c + 1) * LANES]
                return part
            part = lax.fori_loop(0, nkb, body, jnp.zeros((R, LANES), F32))
            return jnp.sum(part, axis=1, keepdims=True)

        nbits = int(ltot).bit_length()

        def body(i, x):
            cand = x + lax.shift_left(jnp.int32(1), nbits - 1 - i)
            ok = count_eq_below(cand) < need
            return jnp.where(ok, cand, x)
        x = lax.fori_loop(0, nbits, body, jnp.zeros((R, 1), jnp.int32))
        return x + 1

    big = jnp.full((R, 1), 1 << 30, jnp.int32)
    bound = lax.cond(any_tie, lambda: jnp.where(tie, tie_bound(), big), lambda: big)

    def bias_body(kb, carry):
        blk = s_ref[kb]
        kidx = kb * KB + lane_k
        sel = (blk > tau) | ((blk == tau) & (kidx < bound))
        sel = sel & (blk > -jnp.inf)
        s_ref[kb] = jnp.where(sel, 0.0, NEG_BIG)
        return carry

    tau_eff = jnp.where(all_finite, -float(np.finfo(np.float32).max), tau)

    def bias_fast(kb, carry):
        s_ref[kb] = jnp.where(s_ref[kb] >= tau_eff, 0.0, NEG_BIG)
        return carry

    @pl.when(any_tie)
    def _():
        lax.fori_loop(0, nkb, bias_body, 0)

    @pl.when(jnp.logical_not(any_tie))
    def _():
        lax.fori_loop(0, nkb, bias_fast, 0)

    q_all = q_ref[0].reshape(8 * R, LANES)
    m_ref[...] = jnp.full(m_ref.shape, NEG_BIG, F32)
    acc_ref[...] = jnp.zeros(acc_ref.shape, F32)

    def att_body(kb, carry):
        for u in range(nsb):
            off = pl.multiple_of(kb * KB + u * SB, SB)
            kblk = k_ref[0, kb * nsb + u]
            vblk = v_ref[0, pl.ds(off, SB), :]
            bias = s_ref[kb, :, u * SB:(u + 1) * SB]
            logits = _mm(q_all, kblk)
            nch = SB // LANES
            for h0 in range(0, 8, HG):
                hs = list(range(h0, h0 + HG))
                lg = [logits[h * R:(h + 1) * R] + bias for h in hs]
                cm = [functools.reduce(jnp.maximum, [l[:, c * LANES:(c + 1) * LANES] for c in range(nch)])
                      for l in lg]
                m_old = [m_ref[h] for h in hs]
                m_new = [jnp.maximum(mo, jnp.max(c, axis=1, keepdims=True)) for mo, c in zip(m_old, cm)]
                alpha = [jnp.exp2(mo - mn) for mo, mn in zip(m_old, m_new)]
                p = [jnp.exp2(l - jnp.concatenate([mn] * nch, axis=1)).astype(BF16) for l, mn in zip(lg, m_new)]
                pv = [_mm(pp, vblk) for pp in p]
                for i, h in enumerate(hs):
                    acc_ref[h] = jnp.concatenate([alpha[i], alpha[i]], axis=1) * acc_ref[h] + pv[i]
                    m_ref[h] = m_new[i]
        return carry

    lax.fori_loop(0, nkb, att_body, 0)
    for h in range(8):
        a = acc_ref[h]
        o_ref[0, h] = (a[:, :LANES] / a[:, LANES:]).astype(BF16)


def _attention(iq, iw, q, ikb, kb, vb1, *, R, KB, SB, HG, ltot, q_off):
    B, _, Sq, _ = q.shape
    Lp = kb.shape[1]
    topk = min(TOPK_MAX, ltot // 4)
    assert Lp % KB == 0 and KB % SB == 0 and KB >= topk and Sq % R == 0
    to_blocks = lambda a: jnp.swapaxes(a.reshape(B, Lp // SB, SB, LANES), 2, 3)
    ikb, kb = to_blocks(ikb), to_blocks(kb)
    hspec = pl.BlockSpec((1, 8, R, LANES), lambda b, i: (b, 0, i, 0))
    kspec = pl.BlockSpec((1, Lp // SB, LANES, SB), lambda b, i: (b, 0, 0, 0))
    vspec = pl.BlockSpec((1, Lp, 2 * LANES), lambda b, i: (b, 0, 0))
    kern = functools.partial(_attn_kernel, R=R, KB=KB, SB=SB, HG=HG, ltot=ltot, q_off=q_off, topk=topk)
    return pl.pallas_call(
        kern,
        grid=(B, Sq // R),
        in_specs=[hspec, pl.BlockSpec((1, R, 8), lambda b, i: (b, i, 0)), hspec, kspec, kspec, vspec],
        out_specs=hspec,
        out_shape=jax.ShapeDtypeStruct((B, 8, Sq, LANES), BF16),
        scratch_shapes=[pltpu.VMEM((Lp // KB, R, KB), F32), pltpu.VMEM((Lp // KB, R, KB), BF16),
                        pltpu.VMEM((8, R, LANES), F32),
                        pltpu.VMEM((8, R, 2 * LANES), F32)],
        compiler_params=_params(("arbitrary", "arbitrary")),
        name="dsa_attention",
    )(iq, iw, q, ikb, kb, vb1)


def _head_sum(x):
    parts = []
    for h in range(8):
        s = jnp.sum(x[:, h * LANES:(h + 1) * LANES], axis=1, keepdims=True)
        parts.append(jnp.broadcast_to(s, (x.shape[0], LANES)))
    return jnp.concatenate(parts, axis=1)


def _rwkv_prep_kernel(z_ref, sh0_ref, mu_ref, w0_ref, wup_ref, a0_ref, aup_ref, gup_ref, kk_ref, ka_ref, rk_ref,
                      r_o, lw_o, k_o, v_o, kkn_o, b_o, g_o, bg_o, lwt_o, kt_o, bt_o, carry_ref,
                      *, tm, t_valid):
    i = pl.program_id(1)

    @pl.when(i == 0)
    def _():
        carry_ref[...] = sh0_ref[0]

    z = z_ref[0]
    row = lax.broadcasted_iota(jnp.int32, (tm, 1), 0)
    prev = jnp.where(row == 0, carry_ref[...], pltpu.roll(z, 1, axis=0))
    carry_ref[...] = z[tm - 1:tm, :]
    zm = z + (prev - z) * mu_ref[...]
    live = i * tm + row < t_valid
    zm = jnp.where(live, zm, 0.0)
    r = zm[:, 0:HP]
    k = zm[:, HP:2 * HP]
    v = zm[:, 2 * HP:3 * HP]
    wa = zm[:, 3 * HP:3 * HP + LANES]
    gd = zm[:, 3 * HP + LANES:3 * HP + 2 * LANES]
    w_raw = w0_ref[...] + _mmp(jnp.tanh(wa), wup_ref[...], 3)
    lw = (-float(np.exp(-0.5))) / (1.0 + jnp.exp(-w_raw))
    lw = jnp.where(live, lw, 0.0)
    a = 1.0 / (1.0 + jnp.exp(-(a0_ref[...] + _mmp(wa, aup_ref[...], 3))))
    g = _mmp(1.0 / (1.0 + jnp.exp(-gd)), gup_ref[...], 3)
    kk = k * kk_ref[...]
    kk = kk * lax.rsqrt(jnp.maximum(_head_sum(kk * kk), L2_EPS))
    k_mod = k * (1.0 + (a - 1.0) * ka_ref[...])
    b = kk * a
    bonus = _head_sum(r * k_mod * rk_ref[...]) * v
    r_o[0] = r.astype(BF16)
    lw_o[0] = lw
    k_o[0] = k_mod.astype(BF16)
    v_o[0] = v.astype(BF16)
    kkn_o[0] = kk.astype(BF16)
    b_o[0] = b.astype(BF16)
    g_o[0] = g
    bg_o[0] = bonus * g
    lwt_o[0] = lw.T
    kt_o[0] = k_mod.T.astype(BF16)
    bt_o[0] = b.T.astype(BF16)


def _rwkv_prep(zr, shift0, prm, tm, t_valid):
    B, T, _ = zr.shape
    tspec = pl.BlockSpec((1, tm, HP), lambda b, i: (b, i, 0))
    fspec = pl.BlockSpec((1, HP, tm), lambda b, i: (b, 0, i))
    full = lambda a: pl.BlockSpec(a.shape, lambda b, i: (0,) * a.ndim)
    tok = jax.ShapeDtypeStruct((B, T, HP), F32)
    feat = jax.ShapeDtypeStruct((B, HP, T), F32)
    tokb = jax.ShapeDtypeStruct((B, T, HP), BF16)
    featb = jax.ShapeDtypeStruct((B, HP, T), BF16)
    names =("mu", "w0", "w_up", "a0", "a_up", "g_up", "k_k", "k_a", "r_k")
    ws = [prm[n] for n in names]
    return pl.pallas_call(
        functools.partial(_rwkv_prep_kernel, tm=tm, t_valid=t_valid),
        grid=(B, T // tm),
        in_specs=[pl.BlockSpec((1, tm, ZRW), lambda b, i: (b, i, 0)),
                  pl.BlockSpec((1, 1, ZRW), lambda b, i: (b, 0, 0))] + [full(w) for w in ws],
        out_specs=[tspec] * 8 + [fspec] * 3,
        out_shape=[tokb, tok, tokb, tokb, tokb, tokb, tok, tok, feat, featb, featb],
        scratch_shapes=[pltpu.VMEM((1, ZRW), F32)],
        compiler_params=_params(("arbitrary", "arbitrary")),
        name="rwkv_prep",
    )(zr, shift0, *ws)


def _rwkv_chunk_kernel(r_ref, lw_ref, k_ref, v_ref, kk_ref, b_ref, lwt_ref, kt_ref, bt_ref, a1_ref, a2_ref,
                       *, C, hb, pg, pc, po):
    ri = lax.broadcasted_iota(jnp.int32, (C, C), 0)
    ci = lax.broadcasted_iota(jnp.int32, (C, C), 1)
    low_incl = jnp.where(ri >= ci, 1.0, 0.0).astype(BF16)
    up_incl = jnp.where(ri <= ci, 1.0, 0.0).astype(BF16)
    strict = ri > ci
    incl = ri >= ci
    eye64 = jnp.where(lax.broadcasted_iota(jnp.int32, (64, 64), 0) == lax.broadcasted_iota(jnp.int32, (64, 64), 1),
                      1.0, 0.0)
    sls = [slice(u * LANES, (u + 1) * LANES) for u in range(hb)]
    each = lambda f, *ls: [f(*a) for a in zip(*ls)]
    lw = [lw_ref[0, :, sl] for sl in sls]
    r = [r_ref[0, :, sl].astype(F32) for sl in sls]
    k = [k_ref[0, :, sl].astype(F32) for sl in sls]
    v = [v_ref[0, :, sl].astype(F32) for sl in sls]
    kk = [kk_ref[0, :, sl].astype(F32) for sl in sls]
    b = [b_ref[0, :, sl].astype(F32) for sl in sls]
    lwt = [lwt_ref[0, sl, :][:64] for sl in sls]
    kt = [kt_ref[0, sl, :][:64].astype(F32) for sl in sls]
    bt = [bt_ref[0, sl, :][:64].astype(F32) for sl in sls]
    cum = each(lambda x: sum(_mm(low_incl, p) for p in _split_bf16(x, 3)), lw)
    cumt = each(lambda x: sum(_mm(p, up_incl) for p in _split_bf16(x, 3)), lwt)
    e_neg = each(lambda c: jnp.exp(-c), cum)
    at = each(lambda kk_, c, l: -kk_ * jnp.exp(c - l), kk, cum, lw)
    rt = each(lambda r_, c: r_ * jnp.exp(c), r, cum)
    g = each(lambda a, r_, b_, k_, e: _mmp(jnp.concatenate([a, r_], axis=0),
                                           jnp.concatenate([b_ * e, k_ * e], axis=0), pg, nt=True),
             at, rt, b, k, e_neg)
    n = each(lambda g_: jnp.where(strict, g_[:C, :C], 0.0), g)
    aak = each(lambda g_: jnp.where(strict, g_[:C, C:], 0.0), g)
    lrb = each(lambda g_: jnp.where(incl, g_[C:, :C], 0.0), g)
    lrk = each(lambda g_: jnp.where(incl, g_[C:, C:], 0.0), g)
    blk8 = lax.shift_right_logical(ri, 3) == lax.shift_right_logical(ci, 3)
    eye = jnp.where(ri == ci, 1.0, 0.0)
    n0 = each(lambda n_: jnp.where(blk8, n_, 0.0), n)
    x = each(lambda n_: eye + n_, n0)
    p = each(lambda n_: _mmp(n_, n_, pc), n0)
    x = each(lambda x_, p_: x_ + _mmp(x_, p_, pc), x, p)
    p = each(lambda p_: _mmp(p_, p_, pc), p)
    x = each(lambda x_, p_: x_ + _mmp(x_, p_, pc), x, p)
    for lv in range(3, int(C).bit_length() - 1):
        off = (lax.shift_right_logical(ri, lv + 1) == lax.shift_right_logical(ci, lv + 1)) & (
            lax.shift_right_logical(ri, lv) != lax.shift_right_logical(ci, lv))
        t = each(lambda n_, x_: _mmp(jnp.where(off, n_, 0.0), x_, pc), n, x)
        x = each(lambda x_, t_: x_ + _mmp(x_, t_, pc), x, t)
    z = each(lambda x_, a, b_: _mmp(x_, jnp.concatenate([a, b_], axis=1), pc), x, aak, at)
    w2v = each(lambda z_, v_: _mmp(z_[:, :C], v_, po), z, v)
    cl = each(lambda c: c[:, C - 1:C], cumt)
    eb = each(lambda c_, ct: jnp.exp(c_ - ct), cl, cumt)
    lhs = each(lambda bt_, kt_, e, lb, lk: jnp.concatenate(
        [jnp.concatenate([bt_ * e, kt_ * e], axis=1), jnp.concatenate([lb, lk], axis=1)], axis=0),
        bt, kt, eb, lrb, lrk)
    left = each(lambda l_, z_: _mmp(l_[:, :C], z_[:, C:], po), lhs, z)
    right = each(lambda l_, w, v_: _mmp(l_, jnp.concatenate([w, v_], axis=0), po), lhs, w2v, v)
    for u in range(hb):
        diag = eye64 * jnp.exp(cl[u])
        a1_ref[0, u, 0] = jnp.concatenate([left[u][:64, :64] + diag, (left[u][64:] + rt[u])[:, :64]], axis=0)
        a2_ref[0, u, 0] = right[u]


def _rwkv_chunks(r, lw, k, v, kk, b, lwt, kt, bt, C, hb, passes=RW_PASSES):
    B, T, _ = r.shape
    nc = T // C
    tspec = pl.BlockSpec((1, C, hb * LANES), lambda bb, h, c: (bb, c, h))
    fspec = pl.BlockSpec((1, hb * LANES, C), lambda bb, h, c: (bb, h, c))
    pg, pc, po = passes
    return pl.pallas_call(
        functools.partial(_rwkv_chunk_kernel, C=C, hb=hb, pg=pg, pc=pc, po=po),
        grid=(B, 8 // hb, nc),
        in_specs=[tspec] * 6 + [fspec] * 3,
        out_specs=[pl.BlockSpec((1, hb, 1, 64 + C, 64), lambda bb, h, c: (bb, h, c, 0, 0)),
                   pl.BlockSpec((1, hb, 1, 64 + C, LANES), lambda bb, h, c: (bb, h, c, 0, 0))],
        out_shape=[jax.ShapeDtypeStruct((B, 8, nc, 64 + C, 64), F32),
                   jax.ShapeDtypeStruct((B, 8, nc, 64 + C, LANES), F32)],
        compiler_params=_params(("arbitrary", "arbitrary", "arbitrary")),
        name="rwkv_chunks",
    )(r, lw, k, v, kk, b, lwt, kt, bt)


def _rwkv_scan_kernel(a1_ref, a2_ref, s0_ref, y_ref, sT_ref, h_ref, *, C, cb):
    c = pl.program_id(1)

    @pl.when(c == 0)
    def _():
        h_ref[...] = s0_ref[0]

    for cc in range(cb):
        for h in range(8):
            res = _mmp(a1_ref[0, h, cc], h_ref[h], 3) + a2_ref[0, h, cc]
            h_ref[h] = res[:64]
            y_ref[0, cc * C:(cc + 1) * C, h * LANES:(h + 1) * LANES] = res[64:]

    @pl.when(c == pl.num_programs(1) - 1)
    def _():
        sT_ref[0] = h_ref[...]


def _rwkv_scan(a1, a2, s0t, C, cb):
    B, _, nc, _, _ = a1.shape
    return pl.pallas_call(
        functools.partial(_rwkv_scan_kernel, C=C, cb=cb),
        grid=(B, nc // cb),
        in_specs=[pl.BlockSpec((1, 8, cb, 64 + C, 64), lambda b, c: (b, 0, c, 0, 0)),
                  pl.BlockSpec((1, 8, cb, 64 + C, LANES), lambda b, c: (b, 0, c, 0, 0)),
                  pl.BlockSpec((1, 8, 64, LANES), lambda b, c: (b, 0, 0, 0))],
        out_specs=[pl.BlockSpec((1, cb * C, HP), lambda b, c: (b, c, 0)),
                   pl.BlockSpec((1, 8, 64, LANES), lambda b, c: (b, 0, 0, 0))],
        out_shape=[jax.ShapeDtypeStruct((B, nc * C, HP), F32),
                   jax.ShapeDtypeStruct((B, 8, 64, LANES), F32)],
        scratch_shapes=[pltpu.VMEM((8, 64, LANES), F32)],
        compiler_params=_params(("arbitrary", "arbitrary")),
        name="rwkv_scan",
    )(a1, a2, s0t)


def _mix_kernel(att_ref, y_ref, g_ref, bg_ref, sga_ref, sgb_ref, x_ref, gt1_ref, sc2_ref, sh2_ref,
                woa_ref, wor_ref, wout_ref, lnw_ref, lnb_ref, gpost_ref, gpre_ref, wr_ref, br_ref,
                x1_ref, h2_ref, gate_ref, eidx_ref, wgt_ref, rel_ref, cnt_out_ref, cnt_ref, *, tm):
    att = jnp.concatenate([att_ref[0, h] for h in range(8)], axis=1)
    lane = lax.broadcasted_iota(jnp.int32, (tm, LANES), 1)
    valid = lane < RWKV_HEAD
    parts = []
    for h in range(8):
        y = y_ref[0, :, h * LANES:(h + 1) * LANES]
        mu = jnp.sum(y, axis=1, keepdims=True) * (1.0 / RWKV_HEAD)
        d = jnp.where(valid, y - mu, 0.0)
        var = jnp.sum(d * d, axis=1, keepdims=True) * (1.0 / RWKV_HEAD)
        parts.append(d * lax.rsqrt(var + GN_EPS))
    yn = jnp.concatenate(parts, axis=1)
    rw = (yn * lnw_ref[...] + lnb_ref[...]) * g_ref[0] + bg_ref[0]
    mix = sga_ref[0] * _mm(att, woa_ref[...]) + sgb_ref[0] * _mm(rw.astype(BF16), wor_ref[...])
    o = _mm(mix.astype(BF16), wout_ref[...])
    x1 = x_ref[0] + gt1_ref[0] * _rms(o, gpost_ref[...])
    x1_ref[0] = x1
    h2 = _rms(x1, gpre_ref[...]) * (1.0 + sc2_ref[0]) + sh2_ref[0]
    h2_ref[0] = h2
    logits = _mmp(h2, wr_ref[...], 3) + br_ref[...]
    el = lax.broadcasted_iota(jnp.int32, logits.shape, 1).astype(F32)
    work = logits
    sel = jnp.zeros(logits.shape, jnp.bool_)
    vmax = None
    firsts = []
    for kk in range(TOP_K):
        mx = jnp.max(work, axis=1, keepdims=True)
        if kk == 0:
            vmax = mx
        first = jnp.min(jnp.where(work == mx, el, float(N_EXPERTS)), axis=1, keepdims=True)
        hit = el == first
        firsts.append(first)
        sel = sel | hit
        work = jnp.where(hit, -jnp.inf, work)
    e = jnp.where(sel, jnp.exp(logits - vmax), 0.0)
    gate = e / jnp.sum(e, axis=1, keepdims=True)
    gate_ref[0] = gate
    @pl.when((pl.program_id(0) == 0) & (pl.program_id(1) == 0))
    def _():
        cnt_ref[...] = jnp.zeros(cnt_ref.shape, F32)

    self = jnp.where(sel, 1.0, 0.0)
    ti = lax.broadcasted_iota(jnp.int32, (tm, tm), 0)
    tj = lax.broadcasted_iota(jnp.int32, (tm, tm), 1)
    before = _mm(jnp.where(tj < ti, 1.0, 0.0).astype(BF16), self.astype(BF16)) + cnt_ref[...]
    cnt_ref[...] = cnt_ref[...] + jnp.sum(self, axis=0, keepdims=True)
    cnt_out_ref[...] = cnt_ref[...]
    ln = lax.broadcasted_iota(jnp.int32, (tm, LANES), 1)
    eo = jnp.zeros((tm, LANES), F32)
    go = jnp.zeros((tm, LANES), F32)
    ro = jnp.zeros((tm, LANES), F32)
    for kk in range(TOP_K):
        hit = el == firsts[kk]
        gk = jnp.sum(jnp.where(hit, gate, 0.0), axis=1, keepdims=True)
        rk = jnp.sum(jnp.where(hit, before, 0.0), axis=1, keepdims=True)
        eo = jnp.where(ln == kk, firsts[kk], eo)
        go = jnp.where(ln == kk, gk, go)
        ro = jnp.where(ln == kk, rk, ro)
    eidx_ref[0] = eo.astype(jnp.int32)
    wgt_ref[0] = go
    rel_ref[0] = ro.astype(jnp.int32)


def _mix(att, y, g, bg, sga, sgb, x, gt1, sc2, sh2, wts, tm):
    B, T, D = x.shape
    tspec = lambda n: pl.BlockSpec((1, tm, n), lambda b, i: (b, i, 0))
    full = lambda a: pl.BlockSpec(a.shape, lambda b, i: (0,) * a.ndim)
    tok = lambda n, dt: jax.ShapeDtypeStruct((B, T, n), dt)
    return pl.pallas_call(
        functools.partial(_mix_kernel, tm=tm),
        grid=(B, T // tm),
        in_specs=[pl.BlockSpec((1, 8, tm, LANES), lambda b, i: (b, 0, i, 0)), tspec(HP), tspec(HP), tspec(HP),
                  tspec(D), tspec(D), tspec(D), _mod_spec(gt1, tm), _mod_spec(sc2, tm), _mod_spec(sh2, tm)]
                 + [full(w) for w in wts],
        out_specs=[tspec(D), tspec(D), tspec(N_EXPERTS), tspec(LANES), tspec(LANES), tspec(LANES),
                   pl.BlockSpec((1, N_EXPERTS), lambda b, i: (0, 0))],
        out_shape=[tok(D, F32), tok(D, F32), tok(N_EXPERTS, F32), tok(LANES, jnp.int32), tok(LANES, F32),
                   tok(LANES, jnp.int32), jax.ShapeDtypeStruct((1, N_EXPERTS), F32)],
        scratch_shapes=[pltpu.VMEM((1, N_EXPERTS), F32)],
        compiler_params=_params(("arbitrary", "arbitrary")),
        name="mix_router",
    )(att, y, g, bg, sga, sgb, x, gt1, sc2, sh2, *wts)


def _expert(xb, wgu, bgu, wd, bd, d_ff):
    wgu = wgu.astype(BF16)
    wd = wd.astype(BF16)
    hgu = _mm(xb, wgu) + bgu
    hg = jnp.minimum(hgu[:, :d_ff], SWIGLU_LIMIT)
    hl = jnp.clip(hgu[:, d_ff:], -SWIGLU_LIMIT, SWIGLU_LIMIT)
    act = hg * (1.0 / (1.0 + jnp.exp(-SWIGLU_ALPHA * hg))) * (hl + 1.0)
    return _mm(act.astype(BF16), wd) + bd


def _sc_mesh():
    return plsc.VectorSubcoreMesh(core_axis_name="core", subcore_axis_name="subcore")


def _sc_dispatch(h, pos_slots, n_rows, q):
    N, D = h.shape
    mesh = _sc_mesh()
    steps = N // (mesh.num_cores * SC_WINDOW)
    assert steps * mesh.num_cores * SC_WINDOW == N

    @pl.kernel(out_type=jax.ShapeDtypeStruct((n_rows, SC_PIECE), h.dtype), mesh=mesh, scratch_types=[])
    def scatter(x_hbm, *refs):
        idx_hbm, o_hbm = refs[:-1], refs[-1]
        base = lax.axis_index("core") * steps

        def body(x_vmem, *i_vmem):
            for iv in i_vmem:
                pltpu.sync_copy(x_vmem, o_hbm.at[iv.at[0]])

        pltpu.emit_pipeline(
            body,
            grid=(steps,),
            in_specs=[pl.BlockSpec((SC_WINDOW, SC_PIECE), index_map=lambda i: (base + i, q))]
                     + [pl.BlockSpec((1, SC_WINDOW), index_map=lambda i: (0, base + i))] * len(idx_hbm),
            out_specs=[],
            core_axis_name="subcore",
            dimension_semantics=(pltpu.PARALLEL,),
        )(x_hbm, *idx_hbm)

    return scatter(h, *pos_slots)


def _sc_gather(table, idx):
    M = idx.shape[1]
    mesh = _sc_mesh()
    steps = M // (mesh.num_cores * SC_WINDOW)
    assert steps * mesh.num_cores * SC_WINDOW == M

    @pl.kernel(out_type=jax.ShapeDtypeStruct((M, SC_PIECE), table.dtype), mesh=mesh)
    def gather(x_hbm, i_hbm, o_hbm):
        base = lax.axis_index("core") * steps

        def body(i_vmem, o_vmem):
            pltpu.sync_copy(x_hbm.at[i_vmem.at[0]], o_vmem)

        pltpu.emit_pipeline(
            body,
            grid=(steps,),
            in_specs=[pl.BlockSpec((1, SC_WINDOW), index_map=lambda i: (0, base + i))],
            out_specs=[pl.BlockSpec((SC_WINDOW, SC_PIECE), index_map=lambda i: (base + i, 0))],
            core_axis_name="subcore",
            dimension_semantics=(pltpu.PARALLEL,),
        )(i_hbm, o_hbm)

    return gather(table, idx)


def _moe_grouped_kernel(te_ref, nt_ref, *refs, d_ff, npiece):
    x_refs, (wgu_ref, bgu_ref, wd_ref, bd_ref) = refs[:npiece], refs[npiece:npiece + 4]
    o_refs, (wgu_b, wd_b) = refs[npiece + 4:2 * npiece + 4], refs[2 * npiece + 4:]
    g = pl.program_id(0)

    @pl.when((g == 0) | (te_ref[g] != te_ref[jnp.maximum(g - 1, 0)]))
    def _():
        wgu_b[...] = wgu_ref[0].astype(BF16)
        wd_b[...] = wd_ref[0].astype(BF16)

    @pl.when(g < nt_ref[0])
    def _():
        xb = jnp.concatenate([r[...] for r in x_refs], axis=1).astype(BF16)
        y = _expert(xb, wgu_b[...], bgu_ref[0], wd_b[...], bd_ref[0], d_ff)
        for q, o_ref in enumerate(o_refs):
            o_ref[...] = y[:, q * SC_PIECE:(q + 1) * SC_PIECE]


def _moe_grouped(xs, te, nt, wgu, bgu, wd, bd, TG):
    NP = xs[0].shape[0]
    E, D, F2 = wgu.shape
    npiece = len(xs)
    wmap = lambda g, te_, nt_: (te_[g], 0, 0)
    xmap = lambda g, te_, nt_: (jnp.minimum(g, nt_[0] - 1), 0)
    pspec = pl.BlockSpec((TG, SC_PIECE), xmap)
    return pl.pallas_call(
        functools.partial(_moe_grouped_kernel, d_ff=F2 // 2, npiece=npiece),
        grid_spec=pltpu.PrefetchScalarGridSpec(
            num_scalar_prefetch=2,
            grid=(NP // TG,),
            in_specs=[pspec] * npiece + [pl.BlockSpec((1, D, F2), wmap), pl.BlockSpec((1, 1, F2), wmap),
                                         pl.BlockSpec((1, F2 // 2, D), wmap), pl.BlockSpec((1, 1, D), wmap)],
            out_specs=[pspec] * npiece,
            scratch_shapes=[pltpu.VMEM((D, F2), BF16), pltpu.VMEM((F2 // 2, D), BF16)]),
        out_shape=[jax.ShapeDtypeStruct((NP, SC_PIECE), F32)] * npiece,
        compiler_params=_params(("arbitrary",)),
        name="moe_grouped",
    )(te, nt, *xs, wgu, bgu, wd, bd)


def _moe_combine_kernel(*refs, npiece):
    y_refs = refs[:TOP_K * npiece]
    w_ref, x1_ref, gt2_ref, gpost_ref, o_ref = refs[TOP_K * npiece:]
    w = w_ref[...]
    acc = None
    for k in range(TOP_K):
        yk = jnp.concatenate([y_refs[k * npiece + q][...] for q in range(npiece)], axis=1)
        term = w[:, k:k + 1] * yk
        acc = term if acc is None else acc + term
    o_ref[...] = x1_ref[...] + gt2_ref[0] * _rms(acc, gpost_ref[...])


def _moe_combine(yslot, wgt, x1, gt2, gpost, tm):
    N, D = x1.shape
    nb = N // tm
    tpb = N // gt2.shape[0]
    npiece = len(yslot)
    yspec = lambda k: pl.BlockSpec((tm, SC_PIECE), lambda i, k=k: (i + k * nb, 0))
    return pl.pallas_call(
        functools.partial(_moe_combine_kernel, npiece=npiece),
        grid=(nb,),
        in_specs=[yspec(k) for k in range(TOP_K) for _ in range(npiece)]
                 + [pl.BlockSpec((tm, LANES), lambda i: (i, 0)), pl.BlockSpec((tm, D), lambda i: (i, 0)),
                    pl.BlockSpec((1, 1, D), lambda i: ((i * tm) // tpb, 0, 0)), pl.BlockSpec((1, D), lambda i: (0, 0))],
        out_specs=pl.BlockSpec((tm, D), lambda i: (i, 0)),
        out_shape=jax.ShapeDtypeStruct((N, D), F32),
        compiler_params=_params(("arbitrary",)),
        name="moe_combine",
    )(*[yslot[q] for _ in range(TOP_K) for q in range(npiece)], wgt, x1, gt2, gpost)


def _moe_kernel(h_ref, gate_ref, x1_ref, gt2_ref, gpost_ref, wgu_ref, bgu_ref, wd_ref, bd_ref, o_ref, acc_ref,
                *, d_ff):
    e = pl.program_id(1)

    @pl.when(e == 0)
    def _():
        acc_ref[...] = jnp.zeros(acc_ref.shape, F32)

    contrib = _expert(h_ref[...].astype(BF16), wgu_ref[0], bgu_ref[0], wd_ref[0], bd_ref[0], d_ff)
    gate = gate_ref[...]
    el = lax.broadcasted_iota(jnp.int32, gate.shape, 1)
    ge = jnp.sum(jnp.where(el == e, gate, 0.0), axis=1, keepdims=True)
    acc_ref[...] += ge * contrib

    @pl.when(e == pl.num_programs(1) - 1)
    def _():
        o_ref[...] = x1_ref[...] + gt2_ref[0] * _rms(acc_ref[...], gpost_ref[...])


def _moe(h2, gate, x1, gt2, gpost, wgu, bgu, wd, bd, tm):
    N, D = h2.shape
    E, _, F2 = wgu.shape
    tspec = lambda n: pl.BlockSpec((tm, n), lambda i, e: (i, 0))
    if gt2.shape[1] == 1:
        tpb = N // gt2.shape[0]
        gspec = pl.BlockSpec((1, 1, D), lambda i, e: ((i * tm) // tpb, 0, 0))
    else:
        gspec = pl.BlockSpec((1, tm, D), lambda i, e: (i, 0, 0))
    return pl.pallas_call(
        functools.partial(_moe_kernel, d_ff=F2 // 2),
        grid=(N // tm, E),
        in_specs=[tspec(D), tspec(E), tspec(D), gspec, pl.BlockSpec((1, D), lambda i, e: (0, 0)),
                  pl.BlockSpec((1, D, F2), lambda i, e: (e, 0, 0)), pl.BlockSpec((1, 1, F2), lambda i, e: (e, 0, 0)),
                  pl.BlockSpec((1, F2 // 2, D), lambda i, e: (e, 0, 0)), pl.BlockSpec((1, 1, D), lambda i, e: (e, 0, 0))],
        out_specs=tspec(D),
        out_shape=jax.ShapeDtypeStruct((N, D), F32),
        scratch_shapes=[pltpu.VMEM((tm, D), F32)],
        compiler_params=_params(("arbitrary", "arbitrary")),
        name="moe",
    )(h2, gate, x1, gt2, gpost, wgu, bgu, wd, bd)


def _per_token(m, T):
    B, _, D = m.shape
    return jnp.broadcast_to(m, (B, T, D)).reshape(1, B * T, D)


def _layer(x, mods, past, s0, shift0, P, cfg):
    B, T, D = x.shape
    sh1, sc1, gt1, sh2, sc2, gt2 = mods
    tm = cfg["tm"]

    q, k, v, kb, vb, iq, ik, ikb, iw = _proj_a(x, sc1, sh1, P["g_pre_mix"], P["w_a"], P["ik_ln_w"], P["ik_ln_b"],
                                              cfg["tm_a"])
    zr, sga, sgb = _proj_b(x, sc1, sh1, P["g_pre_mix"], P["w_b"], tm)

    if past is not None:
        pk, pv, pik = past
        plen = pk.shape[1]
        kb_all = jnp.concatenate([pk.reshape(B, plen, 128).astype(BF16), kb], axis=1)
        vb_all = jnp.concatenate([pv.reshape(B, plen, 128).astype(BF16), vb], axis=1)
        ik_all = jnp.concatenate([jnp.pad(pik, ((0, 0), (0, 0), (0, 64))).astype(BF16), ikb], axis=1)
    else:
        plen = 0
        kb_all, vb_all, ik_all = kb, vb, ikb
    ltot = plen + T
    KB = cfg["KB"]
    lp = -(-ltot // KB) * KB
    if lp != ltot:
        padk = ((0, 0), (0, lp - ltot), (0, 0))
        kb_all, vb_all, ik_all = jnp.pad(kb_all, padk), jnp.pad(vb_all, padk), jnp.pad(ik_all, padk)
    vb1 = jnp.concatenate([vb_all, jnp.ones_like(vb_all)], axis=-1)
    att = _attention(iq, iw, q, ik_all, kb_all, vb1, R=cfg["R"], KB=KB, SB=cfg["SB"], HG=cfg["HG"], ltot=ltot,
                     q_off=plen)

    C = RW_CHUNK
    tp = -(-T // C) * C
    zr_p = zr if tp == T else jnp.pad(zr, ((0, 0), (0, tp - T), (0, 0)))
    shift_pad = _take_cols(shift0.reshape(B, RWKV_COLS), _COLS_ZR).reshape(B, 1, ZRW)
    r, lw, km, vv, kkn, bb, g, bg, lwt, kt, bt = _rwkv_prep(zr_p, shift_pad, P, min(cfg["tm_rw"], tp), T)
    a1, a2 = _rwkv_chunks(r, lw, km, vv, kkn, bb, lwt, kt, bt, C, cfg["hb"])
    s0t = jnp.pad(jnp.swapaxes(s0, 2, 3), ((0, 0), (0, 0), (0, 0), (0, 64)))
    nc = tp // C
    cb = min(cfg["cb"], nc)
    y, sT = _rwkv_scan(a1, a2, s0t, C, cb)
    s_new = jnp.swapaxes(sT[..., :64], 2, 3)
    zlast = zr[:, T - 1]
    inv = np.zeros((RWKV_COLS,), np.int32)
    inv[_COLS_ZR[0][_COLS_ZR[1]]] = np.nonzero(_COLS_ZR[1])[0]
    shift_new = jnp.take(zlast, jnp.asarray(inv), axis=1).reshape(B, 1, RWKV_COLS)
    if tp != T:
        y, g, bg = y[:, :T], g[:, :T], bg[:, :T]

    wts = [P[n] for n in ("w_o_att", "w_o_rwkv", "w_out", "ln_x_w", "ln_x_b", "g_post_mix", "g_pre_ffn",
                          "w_router", "b_router")]
    x1, h2, gate, eidx, wgt, rel, cnt = _mix(att, y, g, bg, sga, sgb, x, gt1, sc2, sh2, wts, cfg["tm_mix"])

    N = B * T
    tmm = min(cfg["tm_moe"], N)
    if cfg["routed"]:
        TG = cfg["TG"]
        ntm = N * TOP_K // TG + N_EXPERTS
        cnt_i = cnt[0].astype(jnp.int32)
        padded = (cnt_i + TG - 1) // TG * TG
        ends = jnp.cumsum(padded)
        off = ends - padded
        nt = ends[-1] // TG
        gi = jnp.minimum(jnp.arange(ntm, dtype=jnp.int32), nt - 1)
        te = jnp.sum((ends // TG)[None, :] <= gi[:, None], axis=1).astype(jnp.int32)
        e4 = eidx.reshape(N, LANES)[:, :TOP_K]
        pos = jnp.take(off, e4) + rel.reshape(N, LANES)[:, :TOP_K]
        pos = jnp.clip(pos, 0, ntm * TG - 1)
        pos_t = pos.T
        npiece = D // SC_PIECE
        h2f = h2.reshape(N, D)
        xs = [_sc_dispatch(h2f, [pos_t[k:k + 1] for k in range(TOP_K)], ntm * TG, q) for q in range(npiece)]
        ys = _moe_grouped(xs, te, nt.reshape(1), P["w_gu"], P["b_gu"], P["w_down"], P["b_down"], TG)
        idx_c = pos_t.reshape(1, TOP_K * N)
        yslot = [_sc_gather(ys[q], idx_c) for q in range(npiece)]
        out = _moe_combine(yslot, wgt.reshape(N, LANES), x1.reshape(N, D), gt2, P["g_post_ffn"], tm)
    else:
        if T % tmm == 0:
            gt2m = gt2
        else:
            gt2m = _per_token(gt2, T).reshape(N // tmm, tmm, D)
        out = _moe(h2.reshape(N, D), gate.reshape(N, N_EXPERTS), x1.reshape(N, D), gt2m, P["g_post_ffn"],
                   P["w_gu"], P["b_gu"], P["w_down"], P["b_down"], tmm)
    return (out.reshape(B, T, D), k.reshape(B, T, ATT_KV_HEADS, HEAD_DIM), v.reshape(B, T, ATT_KV_HEADS, HEAD_DIM),
            ik, s_new, shift_new)


def _prep_weights(l, w_in, ik_ln_w, ik_ln_b, mu_rwkv, w0, w_up, a0, a_up, g_up, k_k, k_a, r_k, ln_x_w, ln_x_b,
                  w_o_att, w_o_rwkv, w_out, w_router, b_router, w_gu, b_gu, w_down, b_down,
                  g_pre_mix, g_post_mix, g_pre_ffn, g_post_ffn):
    row = lambda a: a.reshape(1, -1)
    hp = lambda a: _take_cols(row(a), _HEAD_IDX)
    P = {}
    P["w_a"] = _take_cols(w_in[l], _COLS_A).astype(BF16)
    P["w_b"] = _take_cols(w_in[l], _COLS_B).astype(BF16)
    P["ik_ln_w"] = jnp.pad(row(ik_ln_w[l]), ((0, 0), (0, 64)))
    P["ik_ln_b"] = jnp.pad(row(ik_ln_b[l]), ((0, 0), (0, 64)))
    P["mu"] = _take_cols(row(mu_rwkv[l]), _COLS_ZR)
    P["w0"], P["a0"], P["k_k"], P["k_a"] = hp(w0[l]), hp(a0[l]), hp(k_k[l]), hp(k_a[l])
    P["r_k"] = hp(r_k[l].reshape(-1))
    P["ln_x_w"], P["ln_x_b"] = hp(ln_x_w[l]), hp(ln_x_b[l])
    z64 = jnp.zeros((64, HP), F32)
    P["w_up"] = jnp.concatenate([_take_cols(w_up[l], _HEAD_IDX), z64], axis=0)
    P["a_up"] = jnp.concatenate([z64, _take_cols(a_up[l], _HEAD_IDX)], axis=0)
    P["g_up"] = _take_cols(g_up[l], _HEAD_IDX)
    P["w_o_att"] = _take_rows(w_o_att[l], _ATT_ROW_IDX).astype(BF16)
    P["w_o_rwkv"] = _take_rows(w_o_rwkv[l], _HEAD_IDX).astype(BF16)
    P["w_out"] = w_out[l].astype(BF16)
    P["w_router"] = w_router[l]
    P["b_router"] = row(b_router[l])
    P["w_gu"] = w_gu[l]
    P["b_gu"] = b_gu[l].reshape(N_EXPERTS, 1, -1)
    P["w_down"] = w_down[l]
    P["b_down"] = b_down[l].reshape(N_EXPERTS, 1, -1)
    P["g_pre_mix"], P["g_post_mix"] = row(g_pre_mix[l]), row(g_post_mix[l])
    P["g_pre_ffn"], P["g_post_ffn"] = row(g_pre_ffn[l]), row(g_post_ffn[l])
    return P


CFG_PROMPT = dict(tm=256, tm_a=1024, tm_mix=512, R=256,KB=512, SB=256, HG=2,tm_rw=256, hb=8, cb=2, tm_moe=1024, routed=True, TG=512)
CFG_SAMPLE = dict(tm=32, tm_a=32, tm_mix=32, R=32, KB=512, SB=256, HG=2,tm_rw=128, hb=8, cb=1, tm_moe=512, routed=False)


def kernel(x_prompt, x_sample, c_prompt, c_sample, cache_k, cache_v, cache_idx_k, state_rwkv, state_shift, w_ada, b_ada, g_pre_mix, g_post_mix, g_pre_ffn, g_post_ffn, w_in, ik_ln_w, ik_ln_b, mu_rwkv, w0, w_up, a0, a_up, g_up, k_k, k_a, r_k, ln_x_w, ln_x_b, w_o_att, w_o_rwkv, w_out, w_router, b_router, w_gu, b_gu, w_down, b_down):
    depth = w_in.shape[0]
    bp, tp_, D = x_prompt.shape
    bs, ts, _ = x_sample.shape
    y_p, y_s = x_prompt, x_sample
    st_p = [[] for _ in range(5)]
    st_s = [[] for _ in range(5)]
    nc_all = bp + bs
    npad = -(-nc_all // 8) * 8
    c_all = jnp.pad(jnp.concatenate([c_prompt, c_sample], axis=0), ((0, npad - nc_all), (0, 0)))
    for l in range(depth):
        P = _prep_weights(l, w_in, ik_ln_w, ik_ln_b, mu_rwkv, w0, w_up, a0, a_up, g_up, k_k, k_a, r_k, ln_x_w,
                          ln_x_b, w_o_att, w_o_rwkv, w_out, w_router, b_router, w_gu, b_gu, w_down, b_down,
                          g_pre_mix, g_post_mix, g_pre_ffn, g_post_ffn)
        ada = _ada(c_all, w_ada[l], b_ada[l])
        mods_p = [m[:bp, None, :] for m in jnp.split(ada, 6, axis=-1)]
        mods_s = [m[bp:nc_all, None, :] for m in jnp.split(ada, 6, axis=-1)]
        zero_state = jnp.zeros((bp, RWKV_HEADS, RWKV_HEAD, RWKV_HEAD), F32)
        zero_shift = jnp.zeros((bp, 1, RWKV_COLS), F32)
        outs_p = _layer(y_p, mods_p, None, zero_state, zero_shift, P, CFG_PROMPT)
        outs_s = _layer(y_s, mods_s, (cache_k[l], cache_v[l], cache_idx_k[l]), state_rwkv[l], state_shift[l], P,
                        CFG_SAMPLE)
        y_p, y_s = outs_p[0], outs_s[0]
        for lst, val in zip(st_p, outs_p[1:]):
            lst.append(val)
        for lst, val in zip(st_s, outs_s[1:]):
            lst.append(val)
    sp = [jnp.stack(v, axis=0) for v in st_p]
    ss = [jnp.stack(v, axis=0) for v in st_s]
    return (y_p, y_s, sp[0], sp[1], sp[2], sp[3], sp[4], ss[0], ss[1], ss[2], ss[3], ss[4])
```
